```python
import math
import jax, jax.numpy as jnp
from jax import lax
import numpy as np

D_MODEL = 2048
BATCH = 8
SEQ = 4096
DEPTH = 1

CHUNK = 64
Q_BLOCK = 128
HEAD_DIM = 128
SB_HEADS = 8
DN_HEADS = 8
SB_WIDTH = SB_HEADS * HEAD_DIM
DN_WIDTH = DN_HEADS * HEAD_DIM
MIX_WIDTH = SB_WIDTH + DN_WIDTH
SHORT_CONV = 4
FFN_CONV = 3
D_FF = 5632
IN_COLS = 3 * SB_WIDTH + 4 * DN_WIDTH + 2 * DN_HEADS
EPS = 1e-6

kernel_name = "sb_gdn_hybrid_convffn_block"


def rmsnorm(x, gain):
    xf = x.astype(jnp.float32)
    y = xf * lax.rsqrt(jnp.mean(xf * xf, axis=-1, keepdims=True) + EPS)
    return (y * gain.astype(jnp.float32)).astype(x.dtype)


def l2norm(x):
    xf = x.astype(jnp.float32)
    return xf * lax.rsqrt(jnp.sum(xf * xf, axis=-1, keepdims=True) + EPS)


def causal_dwconv(x, w):
    K = w.shape[0]
    T = x.shape[1]
    xp = jnp.pad(x, ((0, 0), (K - 1, 0), (0, 0)))
    out = xp[:, 0:T] * w[0]
    for j in range(1, K):
        out = out + xp[:, j:j + T] * w[j]
    return out


def _heads(t, n):
    B, T, _ = t.shape
    return t.reshape(B, T, n, HEAD_DIM).transpose(0, 2, 1, 3)


def stick_breaking_attention(q, k, v):
    B, H, T, Dh = q.shape
    scale = Dh ** -0.5
    outs = []
    for blk in range(T // Q_BLOCK):
        q0 = blk * Q_BLOCK
        kend = q0 + Q_BLOCK
        z = jnp.einsum('bhqd,bhkd->bhqk', q[:, :, q0:kend], k[:, :, :kend]).astype(jnp.float32) * scale
        t_idx = q0 + jnp.arange(Q_BLOCK)[:, None]
        s_idx = jnp.arange(kend)[None, :]
        valid = s_idx < t_idx
        log_beta = jax.nn.log_sigmoid(z)
        log_1m = jnp.where(valid, jax.nn.log_sigmoid(-z), 0.0)
        later = lax.cumsum(log_1m, axis=3, reverse=True) - log_1m
        att = jnp.where(valid, jnp.exp(log_beta + later), 0.0)
        outs.append(jnp.einsum('bhqk,bhkd->bhqd', att.astype(v.dtype), v[:, :, :kend]))
    return jnp.concatenate(outs, axis=2)


def gated_delta_rule(q, k, v, g, beta):
    out_dtype = v.dtype
    B, H, T, Dk = q.shape
    Dv = v.shape[-1]
    C = CHUNK
    N = T // C
    q = q.astype(jnp.float32) * (Dk ** -0.5)
    k = k.astype(jnp.float32)
    v = v.astype(jnp.float32)
    beta = beta.astype(jnp.float32)
    q = q.reshape(B, H, N, C, Dk)
    k = k.reshape(B, H, N, C, Dk)
    v = v.reshape(B, H, N, C, Dv)
    beta = beta.reshape(B, H, N, C)
    g = jnp.cumsum(g.astype(jnp.float32).reshape(B, H, N, C), axis=-1)

    causal = jnp.tril(jnp.ones((C, C), dtype=bool))
    strict = jnp.tril(jnp.ones((C, C), dtype=bool), k=-1)
    diff = g[..., :, None] - g[..., None, :]
    decay = jnp.where(causal, jnp.exp(jnp.where(causal, diff, 0.0)), 0.0)

    k_beta = k * beta[..., None]
    v_beta = v * beta[..., None]
    L = jnp.where(strict, jnp.einsum('bhncd,bhnmd->bhncm', k_beta, k) * decay, 0.0)
    eye = jnp.eye(C, dtype=jnp.float32)
    T_mat = lax.linalg.triangular_solve(eye + L, jnp.broadcast_to(eye, L.shape),
                                        left_side=True, lower=True, unit_diagonal=True)
    u = jnp.einsum('bhncm,bhnmd->bhncd', T_mat, v_beta)
    w = jnp.einsum('bhncm,bhnmd->bhncd', T_mat, k_beta * jnp.exp(g)[..., None])
    qk_intra = jnp.where(causal, jnp.einsum('bhncd,bhnmd->bhncm', q, k) * decay, 0.0)
    g_last = g[..., -1]
    k_to_end = k * jnp.exp(g_last[..., None] - g)[..., None]
    q_decay = q * jnp.exp(g)[..., None]

    def step(S, xs):
        u_i, w_i, qa_i, qd_i, kt_i, gl_i = xs
        v_new = u_i - jnp.einsum('bhcd,bhde->bhce', w_i, S)
        o = jnp.einsum('bhcd,bhde->bhce', qd_i, S) + jnp.einsum('bhcm,bhme->bhce', qa_i, v_new)
        S = S * jnp.exp(gl_i)[..., None, None] + jnp.einsum('bhcd,bhce->bhde', kt_i, v_new)
        return S, o

    to_scan = lambda t: jnp.moveaxis(t, 2, 0)
    xs = (to_scan(u), to_scan(w), to_scan(qk_intra), to_scan(q_decay), to_scan(k_to_end),
          jnp.moveaxis(g_last, 2, 0))
    S0 = jnp.zeros((B, H, Dk, Dv), dtype=jnp.float32)
    _, o = lax.scan(step, S0, xs)
    o = jnp.moveaxis(o, 0, 2).reshape(B, H, T, Dv)
    return o.astype(out_dtype)


def token_mixer(xn, w_in, sb_out_gain, dn_conv_w, dn_a_log, dn_dt_bias, dn_out_gain, w_out):
    B, T, _ = xn.shape
    proj = jnp.einsum('btd,dc->btc', xn, w_in)
    sizes = (SB_WIDTH, SB_WIDTH, SB_WIDTH, 3 * DN_WIDTH, DN_WIDTH, DN_HEADS, DN_HEADS)
    offs = np.cumsum(sizes)[:-1].tolist()
    sb_q, sb_k, sb_v, dn_qkv, dn_z, dn_b, dn_a = jnp.split(proj, offs, axis=-1)

    o_sb = stick_breaking_attention(_heads(sb_q, SB_HEADS), _heads(sb_k, SB_HEADS), _heads(sb_v, SB_HEADS))
    o_sb = rmsnorm(o_sb.transpose(0, 2, 1, 3), sb_out_gain).reshape(B, T, SB_WIDTH)

    dn_qkv = jax.nn.silu(causal_dwconv(dn_qkv, dn_conv_w))
    dq, dk, dv = jnp.split(dn_qkv, 3, axis=-1)
    q = l2norm(_heads(dq, DN_HEADS))
    k = l2norm(_heads(dk, DN_HEADS))
    v = _heads(dv, DN_HEADS)
    beta = jax.nn.sigmoid(dn_b.astype(jnp.float32)).transpose(0, 2, 1)
    g = -(jnp.exp(dn_a_log.astype(jnp.float32)) *
          jax.nn.softplus(dn_a.astype(jnp.float32) + dn_dt_bias.astype(jnp.float32))).transpose(0, 2, 1)
    o_dn = gated_delta_rule(q, k, v, g, beta).transpose(0, 2, 1, 3)
    o_dn = rmsnorm(o_dn, dn_out_gain) * jax.nn.silu(dn_z.reshape(B, T, DN_HEADS, HEAD_DIM))
    o_dn = o_dn.reshape(B, T, DN_WIDTH).astype(xn.dtype)

    mix = jnp.concatenate([o_sb.astype(xn.dtype), o_dn], axis=-1)
    return jnp.einsum('btc,cd->btd', mix, w_out)


def conv_ffn(xn, w_up, ffn_conv_w, ffn_conv_b, w_down):
    h = jnp.einsum('btd,df->btf', xn, w_up)
    h = causal_dwconv(h, ffn_conv_w) + ffn_conv_b
    gate, val = jnp.split(h, 2, axis=-1)
    return jnp.einsum('btf,fd->btd', jax.nn.gelu(gate, approximate=True) * val, w_down)


def _fwd_setup_inputs(seed: int = 0) -> dict:
    key = jax.random.key(seed)
    ks = jax.random.split(key, 20)
    f32 = jnp.float32
    nrm = lambda k, shape, s: jax.random.normal(k, shape, f32) * s
    gain = lambda k, shape: 1.0 + 0.05 * jax.random.normal(k, shape, f32)
    dt = jnp.exp(jax.random.uniform(ks[5], (DEPTH, DN_HEADS), f32, math.log(1e-3), math.log(1e-1)))
    return {
        "x": jax.random.normal(ks[0], (BATCH, SEQ, D_MODEL), f32),
        "w_in": nrm(ks[1], (DEPTH, D_MODEL, IN_COLS), D_MODEL ** -0.5),
        "sb_out_gain": gain(ks[2], (DEPTH, HEAD_DIM)),
        "dn_conv_w": nrm(ks[3], (DEPTH, SHORT_CONV, 3 * DN_WIDTH), SHORT_CONV ** -0.5),
        "dn_a_log": jnp.log(jax.random.uniform(ks[4], (DEPTH, DN_HEADS), f32, 1.0, 16.0)),
        "dn_dt_bias": dt + jnp.log(-jnp.expm1(-dt)),
        "dn_out_gain": gain(ks[6], (DEPTH, HEAD_DIM)),
        "w_out": nrm(ks[7], (DEPTH, MIX_WIDTH, D_MODEL), MIX_WIDTH ** -0.5),
        "ln_mix_pre": gain(ks[8], (DEPTH, D_MODEL)),
        "ln_mix_post": gain(ks[9], (DEPTH, D_MODEL)),
        "w_up": nrm(ks[10], (DEPTH, D_MODEL, 2 * D_FF), D_MODEL ** -0.5),
        "ffn_conv_w": nrm(ks[11], (DEPTH, FFN_CONV, 2 * D_FF), FFN_CONV ** -0.5),
        "ffn_conv_b": nrm(ks[12], (DEPTH, 2 * D_FF), 0.01),
        "w_down": nrm(ks[13], (DEPTH, D_FF, D_MODEL), D_FF ** -0.5),
        "ln_ffn_pre": gain(ks[14], (DEPTH, D_MODEL)),
        "ln_ffn_post": gain(ks[15], (DEPTH, D_MODEL)),
    }


def _fwd_reference(x, w_in, sb_out_gain, dn_conv_w, dn_a_log, dn_dt_bias, dn_out_gain, w_out,
              ln_mix_pre, ln_mix_post, w_up, ffn_conv_w, ffn_conv_b, w_down, ln_ffn_pre, ln_ffn_post):
    h = x
    for l in range(DEPTH):
        m = token_mixer(rmsnorm(h, ln_mix_pre[l]), w_in[l], sb_out_gain[l], dn_conv_w[l],
                        dn_a_log[l], dn_dt_bias[l], dn_out_gain[l], w_out[l])
        h = h + rmsnorm(m, ln_mix_post[l])
        f = conv_ffn(rmsnorm(h, ln_ffn_pre[l]), w_up[l], ffn_conv_w[l], ffn_conv_b[l], w_down[l])
        h = h + rmsnorm(f, ln_ffn_post[l])
    return h


import jax as _jax
import jax.numpy as _jnp

TWIN_FORMAT = 'train_step'
FWD_PARAMS = ['x', 'w_in', 'sb_out_gain', 'dn_conv_w', 'dn_a_log', 'dn_dt_bias', 'dn_out_gain', 'w_out', 'ln_mix_pre', 'ln_mix_post', 'w_up', 'ffn_conv_w', 'ffn_conv_b', 'w_down', 'ln_ffn_pre', 'ln_ffn_post']
TWIN_WEIGHTS = ['w_in', 'sb_out_gain', 'dn_conv_w', 'dn_a_log', 'dn_dt_bias', 'dn_out_gain', 'w_out', 'ln_mix_pre', 'ln_mix_post', 'w_up', 'ffn_conv_w', 'ffn_conv_b', 'w_down', 'ln_ffn_pre', 'ln_ffn_post']
TWIN_DIFF_INPUT = 'x'
TWIN_INPUTS = ['x', 'w_in', 'sb_out_gain', 'dn_conv_w', 'dn_a_log', 'dn_dt_bias', 'dn_out_gain', 'w_out', 'ln_mix_pre', 'ln_mix_post', 'w_up', 'ffn_conv_w', 'ffn_conv_b', 'w_down', 'ln_ffn_pre', 'ln_ffn_post', 'loss_target', 'm_w_in', 'm_sb_out_gain', 'm_dn_conv_w', 'm_dn_a_log', 'm_dn_dt_bias', 'm_dn_out_gain', 'm_w_out', 'm_ln_mix_pre', 'm_ln_mix_post', 'm_w_up', 'm_ffn_conv_w', 'm_ffn_conv_b', 'm_w_down', 'm_ln_ffn_pre', 'm_ln_ffn_post', 'v_w_in', 'v_sb_out_gain', 'v_dn_conv_w', 'v_dn_a_log', 'v_dn_dt_bias', 'v_dn_out_gain', 'v_w_out', 'v_ln_mix_pre', 'v_ln_mix_post', 'v_w_up', 'v_ffn_conv_w', 'v_ffn_conv_b', 'v_w_down', 'v_ln_ffn_pre', 'v_ln_ffn_post']
TWIN_OUTPUTS = ['loss', 'grad_x', 'grad_w_in', 'grad_sb_out_gain', 'grad_dn_conv_w', 'grad_dn_a_log', 'grad_dn_dt_bias', 'grad_dn_out_gain', 'grad_w_out', 'grad_ln_mix_pre', 'grad_ln_mix_post', 'grad_w_up', 'grad_ffn_conv_w', 'grad_ffn_conv_b', 'grad_w_down', 'grad_ln_ffn_pre', 'grad_ln_ffn_post', 'delta_w_in', 'delta_sb_out_gain', 'delta_dn_conv_w', 'delta_dn_a_log', 'delta_dn_dt_bias', 'delta_dn_out_gain', 'delta_w_out', 'delta_ln_mix_pre', 'delta_ln_mix_post', 'delta_w_up', 'delta_ffn_conv_w', 'delta_ffn_conv_b', 'delta_w_down', 'delta_ln_ffn_pre', 'delta_ln_ffn_post', 'new_m_w_in', 'new_m_sb_out_gain', 'new_m_dn_conv_w', 'new_m_dn_a_log', 'new_m_dn_dt_bias', 'new_m_dn_out_gain', 'new_m_w_out', 'new_m_ln_mix_pre', 'new_m_ln_mix_post', 'new_m_w_up', 'new_m_ffn_conv_w', 'new_m_ffn_conv_b', 'new_m_w_down', 'new_m_ln_ffn_pre', 'new_m_ln_ffn_post', 'new_v_w_in', 'new_v_sb_out_gain', 'new_v_dn_conv_w', 'new_v_dn_a_log', 'new_v_dn_dt_bias', 'new_v_dn_out_gain', 'new_v_w_out', 'new_v_ln_mix_pre', 'new_v_ln_mix_post', 'new_v_w_up', 'new_v_ffn_conv_w', 'new_v_ffn_conv_b', 'new_v_w_down', 'new_v_ln_ffn_pre', 'new_v_ln_ffn_post']
TWIN_LEAF_KINDS = {'loss': 'loss', 'grad_x': 'grad_x', 'grad_w_in': 'grad_w', 'grad_sb_out_gain': 'grad_w', 'grad_dn_conv_w': 'grad_w', 'grad_dn_a_log': 'grad_w', 'grad_dn_dt_bias': 'grad_w', 'grad_dn_out_gain': 'grad_w', 'grad_w_out': 'grad_w', 'grad_ln_mix_pre': 'grad_w', 'grad_ln_mix_post': 'grad_w', 'grad_w_up': 'grad_w', 'grad_ffn_conv_w': 'grad_w', 'grad_ffn_conv_b': 'grad_w', 'grad_w_down': 'grad_w', 'grad_ln_ffn_pre': 'grad_w', 'grad_ln_ffn_post': 'grad_w', 'delta_w_in': 'delta_w', 'delta_sb_out_gain': 'delta_w', 'delta_dn_conv_w': 'delta_w', 'delta_dn_a_log': 'delta_w', 'delta_dn_dt_bias': 'delta_w', 'delta_dn_out_gain': 'delta_w', 'delta_w_out': 'delta_w', 'delta_ln_mix_pre': 'delta_w', 'delta_ln_mix_post': 'delta_w', 'delta_w_up': 'delta_w', 'delta_ffn_conv_w': 'delta_w', 'delta_ffn_conv_b': 'delta_w', 'delta_w_down': 'delta_w', 'delta_ln_ffn_pre': 'delta_w', 'delta_ln_ffn_post': 'delta_w', 'new_m_w_in': 'new_m', 'new_m_sb_out_gain': 'new_m', 'new_m_dn_conv_w': 'new_m', 'new_m_dn_a_log': 'new_m', 'new_m_dn_dt_bias': 'new_m', 'new_m_dn_out_gain': 'new_m', 'new_m_w_out': 'new_m', 'new_m_ln_mix_pre': 'new_m', 'new_m_ln_mix_post': 'new_m', 'new_m_w_up': 'new_m', 'new_m_ffn_conv_w': 'new_m', 'new_m_ffn_conv_b': 'new_m', 'new_m_w_down': 'new_m', 'new_m_ln_ffn_pre': 'new_m', 'new_m_ln_ffn_post': 'new_m', 'new_v_w_in': 'new_v', 'new_v_sb_out_gain': 'new_v', 'new_v_dn_conv_w': 'new_v', 'new_v_dn_a_log': 'new_v', 'new_v_dn_dt_bias': 'new_v', 'new_v_dn_out_gain': 'new_v', 'new_v_w_out': 'new_v', 'new_v_ln_mix_pre': 'new_v', 'new_v_ln_mix_post': 'new_v', 'new_v_w_up': 'new_v', 'new_v_ffn_conv_w': 'new_v', 'new_v_ffn_conv_b': 'new_v', 'new_v_w_down': 'new_v', 'new_v_ln_ffn_pre': 'new_v', 'new_v_ln_ffn_post': 'new_v'}


def _forward(args):
    return _fwd_reference(*[args[k] for k in FWD_PARAMS])


def _output_shape():
    def fwd():
        inp = _fwd_setup_inputs(0)
        return _fwd_reference(*[inp[k] for k in FWD_PARAMS])
    out = _jax.eval_shape(fwd)
    return out.shape, out.dtype

N_MICROBATCH = 1
ADAM_LR = 0.001
ADAM_B1 = 0.9
ADAM_B2 = 0.999
ADAM_EPS = 1e-08
ADAM_WD = 0.01
ADAM_STEP = 10
PER_EXAMPLE_BATCH_AXIS = {'x': 0, 'loss_target': 0}
SHARED_INPUTS = []
_WEIGHT_DTYPES = {'w_in': _jnp.float32, 'sb_out_gain': _jnp.float32, 'dn_conv_w': _jnp.float32, 'dn_a_log': _jnp.float32, 'dn_dt_bias': _jnp.float32, 'dn_out_gain': _jnp.float32, 'w_out': _jnp.float32, 'ln_mix_pre': _jnp.float32, 'ln_mix_post': _jnp.float32, 'w_up': _jnp.float32, 'ffn_conv_w': _jnp.float32, 'ffn_conv_b': _jnp.float32, 'w_down': _jnp.float32, 'ln_ffn_pre': _jnp.float32, 'ln_ffn_post': _jnp.float32}
MOMENT_SCALE = {'w_in': 1.646480e-01, 'sb_out_gain': 8.458036e-01, 'dn_conv_w': 1.278106e-01, 'dn_a_log': 1.232418e+00, 'dn_dt_bias': 1.162788e+00, 'dn_out_gain': 6.006952e-01, 'w_out': 2.609484e-01, 'ln_mix_pre': 3.164819e-01, 'ln_mix_post': 1.598960e+01, 'w_up': 1.096403e-01, 'ffn_conv_w': 1.164140e-01, 'ffn_conv_b': 3.293766e-01, 'w_down': 2.050698e-01, 'ln_ffn_pre': 2.558815e-01, 'ln_ffn_post': 1.599562e+01}


def _to_microbatches(a, axis):
    t = _jnp.moveaxis(a, axis, 0)
    t = t.reshape((N_MICROBATCH, t.shape[0] // N_MICROBATCH) + t.shape[1:])
    return _jnp.moveaxis(t, 1, axis + 1)


def setup_inputs(seed: int = 0) -> dict:
    inp = _fwd_setup_inputs(seed)
    key = _jax.random.fold_in(_jax.random.key(seed), 7919)
    shape, _ = _output_shape()
    out = dict(inp)
    out["loss_target"] = _jax.random.normal(_jax.random.fold_in(key, 0), shape, _jnp.float32)
    for i, name in enumerate(TWIN_WEIGHTS):
        w = inp[name].astype(_jnp.float32)
        if MOMENT_SCALE is None:
            s = _jnp.sqrt(_jnp.mean(_jnp.square(w)) + 1e-30)
        else:
            s = MOMENT_SCALE[name]
        km, kv = _jax.random.split(_jax.random.fold_in(key, i + 1))
        out[name] = w
        out["m_" + name] = s * _jax.random.normal(km, w.shape, _jnp.float32)
        out["v_" + name] = (s * s) * _jax.random.uniform(kv, w.shape, _jnp.float32, 0.5, 1.5)
    if N_MICROBATCH > 1:
        for name, axis in PER_EXAMPLE_BATCH_AXIS.items():
            out[name] = _to_microbatches(out[name], axis)
    return {'x': out['x'], 'w_in': out['w_in'], 'sb_out_gain': out['sb_out_gain'], 'dn_conv_w': out['dn_conv_w'], 'dn_a_log': out['dn_a_log'], 'dn_dt_bias': out['dn_dt_bias'], 'dn_out_gain': out['dn_out_gain'], 'w_out': out['w_out'], 'ln_mix_pre': out['ln_mix_pre'], 'ln_mix_post': out['ln_mix_post'], 'w_up': out['w_up'], 'ffn_conv_w': out['ffn_conv_w'], 'ffn_conv_b': out['ffn_conv_b'], 'w_down': out['w_down'], 'ln_ffn_pre': out['ln_ffn_pre'], 'ln_ffn_post': out['ln_ffn_post'], 'loss_target': out['loss_target'], 'm_w_in': out['m_w_in'], 'm_sb_out_gain': out['m_sb_out_gain'], 'm_dn_conv_w': out['m_dn_conv_w'], 'm_dn_a_log': out['m_dn_a_log'], 'm_dn_dt_bias': out['m_dn_dt_bias'], 'm_dn_out_gain': out['m_dn_out_gain'], 'm_w_out': out['m_w_out'], 'm_ln_mix_pre': out['m_ln_mix_pre'], 'm_ln_mix_post': out['m_ln_mix_post'], 'm_w_up': out['m_w_up'], 'm_ffn_conv_w': out['m_ffn_conv_w'], 'm_ffn_conv_b': out['m_ffn_conv_b'], 'm_w_down': out['m_w_down'], 'm_ln_ffn_pre': out['m_ln_ffn_pre'], 'm_ln_ffn_post': out['m_ln_ffn_post'], 'v_w_in': out['v_w_in'], 'v_sb_out_gain': out['v_sb_out_gain'], 'v_dn_conv_w': out['v_dn_conv_w'], 'v_dn_a_log': out['v_dn_a_log'], 'v_dn_dt_bias': out['v_dn_dt_bias'], 'v_dn_out_gain': out['v_dn_out_gain'], 'v_w_out': out['v_w_out'], 'v_ln_mix_pre': out['v_ln_mix_pre'], 'v_ln_mix_post': out['v_ln_mix_post'], 'v_w_up': out['v_w_up'], 'v_ffn_conv_w': out['v_ffn_conv_w'], 'v_ffn_conv_b': out['v_ffn_conv_b'], 'v_w_down': out['v_w_down'], 'v_ln_ffn_pre': out['v_ln_ffn_pre'], 'v_ln_ffn_post': out['v_ln_ffn_post']}


def _loss(weights, diff, rest, loss_target):
    with _jax.named_scope("forward"):
        args = {**rest, TWIN_DIFF_INPUT: diff, **{k: w.astype(_WEIGHT_DTYPES[k]) for k, w in weights.items()}}
        y = _forward(args)
    with _jax.named_scope("loss_head"):
        err = _jnp.square(y.astype(_jnp.float32) - loss_target)
        return 0.5 * _jnp.sum(_jnp.mean(err, axis=-1)) if err.ndim else 0.5 * err


def _adamw(w, g, m, v):
    m = ADAM_B1 * m + (1.0 - ADAM_B1) * g
    v = ADAM_B2 * v + (1.0 - ADAM_B2) * _jnp.square(g)
    m_hat = m / (1.0 - ADAM_B1 ** ADAM_STEP)
    v_hat = v / (1.0 - ADAM_B2 ** ADAM_STEP)
    delta = -ADAM_LR * (m_hat / (_jnp.sqrt(v_hat) + ADAM_EPS) + ADAM_WD * w)
    return delta, m, v


def reference(x, w_in, sb_out_gain, dn_conv_w, dn_a_log, dn_dt_bias, dn_out_gain, w_out, ln_mix_pre, ln_mix_post, w_up, ffn_conv_w, ffn_conv_b, w_down, ln_ffn_pre, ln_ffn_post, loss_target, m_w_in, m_sb_out_gain, m_dn_conv_w, m_dn_a_log, m_dn_dt_bias, m_dn_out_gain, m_w_out, m_ln_mix_pre, m_ln_mix_post, m_w_up, m_ffn_conv_w, m_ffn_conv_b, m_w_down, m_ln_ffn_pre, m_ln_ffn_post, v_w_in, v_sb_out_gain, v_dn_conv_w, v_dn_a_log, v_dn_dt_bias, v_dn_out_gain, v_w_out, v_ln_mix_pre, v_ln_mix_post, v_w_up, v_ffn_conv_w, v_ffn_conv_b, v_w_down, v_ln_ffn_pre, v_ln_ffn_post):
    given = dict(x=x, w_in=w_in, sb_out_gain=sb_out_gain, dn_conv_w=dn_conv_w, dn_a_log=dn_a_log, dn_dt_bias=dn_dt_bias, dn_out_gain=dn_out_gain, w_out=w_out, ln_mix_pre=ln_mix_pre, ln_mix_post=ln_mix_post, w_up=w_up, ffn_conv_w=ffn_conv_w, ffn_conv_b=ffn_conv_b, w_down=w_down, ln_ffn_pre=ln_ffn_pre, ln_ffn_post=ln_ffn_post, loss_target=loss_target, m_w_in=m_w_in, m_sb_out_gain=m_sb_out_gain, m_dn_conv_w=m_dn_conv_w, m_dn_a_log=m_dn_a_log, m_dn_dt_bias=m_dn_dt_bias, m_dn_out_gain=m_dn_out_gain, m_w_out=m_w_out, m_ln_mix_pre=m_ln_mix_pre, m_ln_mix_post=m_ln_mix_post, m_w_up=m_w_up, m_ffn_conv_w=m_ffn_conv_w, m_ffn_conv_b=m_ffn_conv_b, m_w_down=m_w_down, m_ln_ffn_pre=m_ln_ffn_pre, m_ln_ffn_post=m_ln_ffn_post, v_w_in=v_w_in, v_sb_out_gain=v_sb_out_gain, v_dn_conv_w=v_dn_conv_w, v_dn_a_log=v_dn_a_log, v_dn_dt_bias=v_dn_dt_bias, v_dn_out_gain=v_dn_out_gain, v_w_out=v_w_out, v_ln_mix_pre=v_ln_mix_pre, v_ln_mix_post=v_ln_mix_post, v_w_up=v_w_up, v_ffn_conv_w=v_ffn_conv_w, v_ffn_conv_b=v_ffn_conv_b, v_w_down=v_w_down, v_ln_ffn_pre=v_ln_ffn_pre, v_ln_ffn_post=v_ln_ffn_post)
    weights = {n: given[n] for n in TWIN_WEIGHTS}
    shared = {n: given[n] for n in SHARED_INPUTS}
    per_example = {n: given[n] for n in ['x']}
    grad_fn = _jax.value_and_grad(_loss, argnums=(0, 1))

    def one_microbatch(ex, loss_target):
        ex = dict(ex)
        diff = ex.pop(TWIN_DIFF_INPUT)
        return grad_fn(weights, diff, {**shared, **ex}, loss_target)

    if N_MICROBATCH == 1:
        loss, (grad_w, grad_x) = one_microbatch(per_example, given["loss_target"])
    else:
        def body(carry, xs):
            loss_sum, grad_sum = carry
            l_k, (gw_k, gx_k) = one_microbatch(xs[0], xs[1])
            with _jax.named_scope("update"):
                return (loss_sum + l_k, _jax.tree.map(_jnp.add, grad_sum, gw_k)), gx_k

        init = (_jnp.zeros((), _jnp.float32), _jax.tree.map(_jnp.zeros_like, weights))
        (loss, grad_w), grad_x = _jax.lax.scan(body, init, (per_example, given["loss_target"]))
    with _jax.named_scope("update"):
        delta_w, new_m, new_v = {}, {}, {}
        for n in TWIN_WEIGHTS:
            delta_w[n], new_m[n], new_v[n] = _adamw(weights[n], grad_w[n], given["m_" + n], given["v_" + n])
    return (loss, grad_x, *[grad_w[n] for n in TWIN_WEIGHTS], *[delta_w[n] for n in TWIN_WEIGHTS],
            *[new_m[n] for n in TWIN_WEIGHTS], *[new_v[n] for n in TWIN_WEIGHTS])
```

```python
import functools

import numpy as np
import jax
import jax.numpy as jnp
from jax import lax
from jax.experimental import pallas as pl
from jax.experimental.pallas import tpu as pltpu

F32 = jnp.float32
BF16 = jnp.bfloat16
HEAD_DIM = 128
CHUNK = 64
ROWS = 2 * CHUNK
EPS = 1e-6
LANES = 128
HALO = 8
VMEM_LIMIT = 48 * 1024 * 1024
ADAM_LR, ADAM_B1, ADAM_B2, ADAM_EPS, ADAM_WD, ADAM_STEP = 0.001, 0.9, 0.999, 1e-08, 0.01, 10
MESH = pl.DeviceIdType.MESH
HIGHEST = lax.Precision.HIGHEST

NN = (((1,), (0,)), ((), ()))
NT = (((1,), (1,)), ((), ()))
TN = (((0,), (0,)), ((), ()))


def _params(n_axes):
    return pltpu.CompilerParams(dimension_semantics=("arbitrary",) * n_axes, vmem_limit_bytes=VMEM_LIMIT)


def _bdot(a, b, dims=NN):
    return lax.dot_general(a.astype(BF16), b.astype(BF16), dims, preferred_element_type=F32)


def _split3(a):
    hi = a.astype(BF16)
    r1 = a - hi.astype(F32)
    mid = r1.astype(BF16)
    lo = (r1 - mid.astype(F32)).astype(BF16)
    return hi, mid, lo


def _dot3(a, sel, dims=NN):
    return sum(lax.dot_general(p, sel, dims, preferred_element_type=F32) for p in _split3(a))


def _dot3r(sel, a, dims=NN):
    return sum(lax.dot_general(sel, p, dims, preferred_element_type=F32) for p in _split3(a))


def _iota2(n, m):
    return lax.broadcasted_iota(jnp.int32, (n, m), 0), lax.broadcasted_iota(jnp.int32, (n, m), 1)


def _sigmoid(x):
    return 1.0 / (1.0 + jnp.exp(-x))


def _softplus(x):
    return jnp.maximum(x, 0.0) + jnp.log(1.0 + jnp.exp(-jnp.abs(x)))


def _fit(n, target):
    best = None
    for t in range(LANES, min(n, target) + 1, LANES):
        if n % t == 0:
            best = t
    assert best is not None, (n, target)
    return best


def _mm(parts, mode, out_dtype, tm, tn, name):
    dims = {"nn": NN, "nt": NT, "tn": TN}[mode]
    a0, b0, _ = parts[0]
    M = a0.shape[1] if mode == "tn" else a0.shape[0]
    N = b0.shape[0] if mode == "nt" else b0.shape[1]
    tm, tn = _fit(M, tm), _fit(N, tn)
    offs, nks = [], []
    off = 0
    parts = [(a, b, _fit(a.shape[0] if mode == "tn" else a.shape[1], tk)) for a, b, tk in parts]
    for a, b, tk in parts:
        K = a.shape[0] if mode == "tn" else a.shape[1]
        assert K % tk == 0, (name, K, tk)
        offs.append(off)
        nks.append(K // tk)
        off += K // tk
    nk_total = off

    def a_spec(p):
        o, n, tk = offs[p], nks[p], parts[p][2]
        kk = lambda k: jnp.clip(k - o, 0, n - 1)
        if mode == "tn":
            return pl.BlockSpec((tk, tm), lambda i, j, k: (kk(k), i))
        return pl.BlockSpec((tm, tk), lambda i, j, k: (i, kk(k)))

    def b_spec(p):
        o, n, tk = offs[p], nks[p], parts[p][2]
        kk = lambda k: jnp.clip(k - o, 0, n - 1)
        if mode == "nt":
            return pl.BlockSpec((tn, tk), lambda i, j, k: (j, kk(k)))
        return pl.BlockSpec((tk, tn), lambda i, j, k: (kk(k), j))

    n_parts = len(parts)

    def body(*refs):
        a_refs, b_refs = refs[:n_parts], refs[n_parts:2 * n_parts]
        o_ref = refs[2 * n_parts]
        if nk_total == 1:
            o_ref[...] = _bdot(a_refs[0][...], b_refs[0][...], dims).astype(out_dtype)
            return
        acc = refs[2 * n_parts + 1]
        k = pl.program_id(2)

        @pl.when(k == 0)
        def _():
            acc[...] = jnp.zeros_like(acc)

        for p in range(n_parts):
            @pl.when((k >= offs[p]) & (k < offs[p] + nks[p]))
            def _(p=p):
                acc[...] += _bdot(a_refs[p][...], b_refs[p][...], dims)

        @pl.when(k == nk_total - 1)
        def _():
            o_ref[...] = acc[...].astype(out_dtype)

    return pl.pallas_call(
        body, name=name, grid=(M // tm, N // tn, nk_total),
        in_specs=[a_spec(p) for p in range(n_parts)] + [b_spec(p) for p in range(n_parts)],
        out_specs=pl.BlockSpec((tm, tn), lambda i, j, k: (i, j)),
        out_shape=jax.ShapeDtypeStruct((M, N), out_dtype),
        scratch_shapes=[] if nk_total == 1 else [pltpu.VMEM((tm, tn), F32)],
        compiler_params=_params(3),
    )(*[p[0] for p in parts], *[p[1] for p in parts])


def _rms_fwd(x, gain, resid, out_dtype, bt, name):
    T, D = x.shape
    row = pl.BlockSpec((bt, D), lambda i: (i, 0))
    vec = pl.BlockSpec((1, D), lambda i: (0, 0))

    def body(*refs):
        x_ref, g_ref = refs[0], refs[1]
        o_ref = refs[-1]
        xv = x_ref[...]
        y = xv * lax.rsqrt(jnp.mean(xv * xv, axis=-1, keepdims=True) + EPS) * g_ref[...]
        if resid is not None:
            y = refs[2][...] + y
        o_ref[...] = y.astype(out_dtype)

    ins = [x, gain] + ([resid] if resid is not None else [])
    return pl.pallas_call(
        body, name=name, grid=(T // bt,),
        in_specs=[row, vec] + ([row] if resid is not None else []),
        out_specs=row, out_shape=jax.ShapeDtypeStruct((T, D), out_dtype), compiler_params=_params(1),
    )(*ins)


def _rms_bwd_math(xv, g, dy):
    r = lax.rsqrt(jnp.mean(xv * xv, axis=-1, keepdims=True) + EPS)
    n = xv * r
    gy = dy * g
    dx = r * (gy - n * jnp.mean(gy * n, axis=-1, keepdims=True))
    return dx, dy * n


def _rms_bwd(x, gain, dy, resid, out_dtype, bt, name):
    T, D = x.shape
    row = pl.BlockSpec((bt, D), lambda i: (i, 0))
    vec = pl.BlockSpec((1, D), lambda i: (0, 0))

    def body(*refs):
        x_ref, g_ref, dy_ref = refs[0], refs[1], refs[2]
        dx_ref, dg_ref = refs[-2], refs[-1]
        dx, dgp = _rms_bwd_math(x_ref[...], g_ref[...], dy_ref[...].astype(F32))
        if resid is not None:
            dx = refs[3][...] + dx
        dx_ref[...] = dx.astype(out_dtype)

        @pl.when(pl.program_id(0) == 0)
        def _():
            dg_ref[...] = jnp.zeros_like(dg_ref)

        dg_ref[...] += jnp.sum(dgp, axis=0, keepdims=True)

    ins = [x, gain, dy] + ([resid] if resid is not None else [])
    return pl.pallas_call(
        body, name=name, grid=(T // bt,),
        in_specs=[row, vec, row] + ([row] if resid is not None else []),
        out_specs=(row, vec),
        out_shape=(jax.ShapeDtypeStruct((T, D), out_dtype), jax.ShapeDtypeStruct((1, D), F32)),
        compiler_params=_params(1),
    )(*ins)


def _loss_head(f, gain, h, target, bt, name):
    T, D = f.shape
    row = pl.BlockSpec((bt, D), lambda i: (i, 0))
    vec = pl.BlockSpec((1, D), lambda i: (0, 0))

    def body(f_ref, g_ref, h_ref, t_ref, dy_ref, df_ref, dg_ref, sq_ref):
        fv, g = f_ref[...], g_ref[...]
        r = lax.rsqrt(jnp.mean(fv * fv, axis=-1, keepdims=True) + EPS)
        n = fv * r
        err = (h_ref[...] + n * g) - t_ref[...]
        dy = err * (1.0 / D)
        gy = dy * g
        df = r * (gy - n * jnp.mean(gy * n, axis=-1, keepdims=True))
        dy_ref[...] = dy
        df_ref[...] = df.astype(BF16)

        @pl.when(pl.program_id(0) == 0)
        def _():
            dg_ref[...] = jnp.zeros_like(dg_ref)
            sq_ref[...] = jnp.zeros_like(sq_ref)

        dg_ref[...] += jnp.sum(dy * n, axis=0, keepdims=True)
        sq_ref[...] += jnp.sum(err * err, axis=0, keepdims=True)

    return pl.pallas_call(
        body, name=name, grid=(T // bt,), in_specs=[row, vec, row, row], out_specs=(row, row, vec, vec),
        out_shape=(jax.ShapeDtypeStruct((T, D), F32), jax.ShapeDtypeStruct((T, D), BF16),
                   jax.ShapeDtypeStruct((1, D), F32), jax.ShapeDtypeStruct((1, D), F32)),
        compiler_params=_params(1),
    )(f, gain, h, target)


def _sb_logits(q, k, kb, i, blk, row, col):
    z = lax.dot_general(q, k, NT, preferred_element_type=F32) * (HEAD_DIM ** -0.5)
    valid = (col + kb * blk) < (row + i * blk)
    sp = jnp.log(1.0 + jnp.exp(-jnp.abs(z)))
    lb = jnp.minimum(z, 0.0) - sp
    l1 = jnp.where(valid, -(jnp.maximum(z, 0.0) + sp), 0.0)
    return valid, lb, l1


def _sb_fwd(qkv, gain, blk, name):
    T, W = qkv.shape[0], qkv.shape[1] // 3
    H = W // HEAD_DIM

    def body(q_ref, k_ref, v_ref, g_ref, o_ref, mix_ref, lt_ref):
        i = pl.program_id(1)
        q = q_ref[...]
        row, col = _iota2(blk, blk)
        after = (row > col).astype(BF16)

        def step(jj, carry):
            run, acc = carry
            kb = i - jj
            ks = pl.ds(pl.multiple_of(kb * blk, blk), blk)
            valid, lb, l1 = _sb_logits(q, k_ref[ks, :], kb, i, blk, row, col)
            later = _dot3(l1, after) + run
            att = jnp.where(valid, jnp.exp(lb + later), 0.0)
            acc = acc + _bdot(att, v_ref[ks, :])
            return run + jnp.sum(l1, axis=1, keepdims=True), acc

        run, o = lax.fori_loop(0, i + 1, step, (jnp.zeros((blk, 1), F32), jnp.zeros((blk, HEAD_DIM), F32)))
        o_ref[...] = o
        r = lax.rsqrt(jnp.mean(o * o, axis=-1, keepdims=True) + EPS)
        mix_ref[...] = (o * r * g_ref[...]).astype(BF16)
        lt_ref[...] = jnp.broadcast_to(run, (blk, HEAD_DIM))

    qb = pl.BlockSpec((blk, HEAD_DIM), lambda h, i: (i, h))
    return pl.pallas_call(
        body, name=name, grid=(H, T // blk),
        in_specs=[qb, pl.BlockSpec((T, HEAD_DIM), lambda h, i: (0, H + h)),
                  pl.BlockSpec((T, HEAD_DIM), lambda h, i: (0, 2 * H + h)),
                  pl.BlockSpec((1, HEAD_DIM), lambda h, i: (0, 0))],
        out_specs=(qb, qb, qb),
        out_shape=(jax.ShapeDtypeStruct((T, W), F32), jax.ShapeDtypeStruct((T, W), BF16),
                   jax.ShapeDtypeStruct((T, W), F32)),
        compiler_params=_params(2),
    )(qkv, qkv, qkv, gain)


def _headnorm_bwd(o, gain, dmix, bt, name):
    T, W = o.shape
    H = W // HEAD_DIM
    blk = pl.BlockSpec((bt, HEAD_DIM), lambda i, h: (i, h))
    vec = pl.BlockSpec((1, HEAD_DIM), lambda i, h: (0, 0))

    def body(o_ref, g_ref, d_ref, do_ref, dg_ref):
        do, dgp = _rms_bwd_math(o_ref[...], g_ref[...], d_ref[...])
        do_ref[...] = do

        @pl.when((pl.program_id(0) == 0) & (pl.program_id(1) == 0))
        def _():
            dg_ref[...] = jnp.zeros_like(dg_ref)

        dg_ref[...] += jnp.sum(dgp, axis=0, keepdims=True)

    return pl.pallas_call(
        body, name=name, grid=(T // bt, H), in_specs=[blk, vec, blk], out_specs=(blk, vec),
        out_shape=(jax.ShapeDtypeStruct((T, W), F32), jax.ShapeDtypeStruct((1, HEAD_DIM), F32)),
        compiler_params=_params(2),
    )(o, gain, dmix)


def _sb_bwd(qkv, do, lt, blk, name):
    T, W = qkv.shape[0], qkv.shape[1] // 3
    H = W // HEAD_DIM
    scale = HEAD_DIM ** -0.5

    def body(q_ref, k_ref, v_ref, do_ref, lt_ref, dq_ref, dk_ref, dv_ref):
        i = pl.program_id(1)

        @pl.when(i == 0)
        def _():
            dk_ref[...] = jnp.zeros_like(dk_ref)
            dv_ref[...] = jnp.zeros_like(dv_ref)

        q = q_ref[...]
        dob = do_ref[...].astype(BF16)
        total = lt_ref[...][:, :1]
        row, col = _iota2(blk, blk)
        upto = (row <= col).astype(BF16)
        before = (row < col).astype(BF16)

        def step(kb, carry):
            seen, psum, dq = carry
            ks = pl.ds(pl.multiple_of(kb * blk, blk), blk)
            k, v = k_ref[ks, :], v_ref[ks, :]
            valid, lb, l1 = _sb_logits(q, k, kb, i, blk, row, col)
            later = total - seen - _dot3(l1, upto)
            att = jnp.where(valid, jnp.exp(lb + later), 0.0)
            p = att * lax.dot_general(dob, v, NT, preferred_element_type=F32)
            c = psum + _dot3(p, before)
            sig = jnp.exp(lb)
            dz = (jnp.where(valid, p * (1.0 - sig) - c * sig, 0.0) * scale).astype(BF16)
            dq = dq + jnp.dot(dz, k, preferred_element_type=F32)
            dk_ref[ks, :] += lax.dot_general(dz, q, TN, preferred_element_type=F32)
            dv_ref[ks, :] += lax.dot_general(att.astype(BF16), dob, TN, preferred_element_type=F32)
            return seen + jnp.sum(l1, axis=1, keepdims=True), psum + jnp.sum(p, axis=1, keepdims=True), dq

        zero = jnp.zeros((blk, 1), F32)
        _, _, dq = lax.fori_loop(0, i + 1, step, (zero, zero, jnp.zeros((blk, HEAD_DIM), F32)))
        dq_ref[...] = dq

    qb = pl.BlockSpec((blk, HEAD_DIM), lambda h, i: (i, h))
    head = pl.BlockSpec((T, HEAD_DIM), lambda h, i: (0, h))
    out = jax.ShapeDtypeStruct((T, W), F32)
    return pl.pallas_call(
        body, name=name, grid=(H, T // blk),
        in_specs=[qb, pl.BlockSpec((T, HEAD_DIM), lambda h, i: (0, H + h)),
                  pl.BlockSpec((T, HEAD_DIM), lambda h, i: (0, 2 * H + h)), qb, qb],
        out_specs=(qb, head, head), out_shape=(out, out, out), compiler_params=_params(2),
    )(qkv, qkv, qkv, do, lt)


def _expanders(H):
    lane = np.arange(H * HEAD_DIM) // HEAD_DIM
    eb = np.zeros((LANES, H * HEAD_DIM), np.float32)
    eg = np.zeros((LANES, H * HEAD_DIM), np.float32)
    eb[lane, np.arange(H * HEAD_DIM)] = 1.0
    eg[H + lane, np.arange(H * HEAD_DIM)] = 1.0
    sb = np.zeros((H * HEAD_DIM, LANES), np.float32)
    sg = np.zeros((H * HEAD_DIM, LANES), np.float32)
    sb[np.arange(H) * HEAD_DIM, np.arange(H)] = 1.0
    sg[np.arange(H) * HEAD_DIM, H + np.arange(H)] = 1.0
    return [jnp.asarray(m, BF16) for m in (eb, eg, sb, sg)]


def _conv_taps(ext_ref, w, n_out, lead):
    K = w.shape[0]
    out = None
    for j in range(K):
        term = ext_ref[pl.ds(lead - (K - 1) + j, n_out), :] * w[j:j + 1, :]
        out = term if out is None else out + term
    return out


def _l2_heads(s, H, fn):
    return jnp.concatenate([fn(s[:, h * HEAD_DIM:(h + 1) * HEAD_DIM]) for h in range(H)], axis=1)


def _dn_pre_fwd(pdn, pba, conv_w, a_log, dt_bias, bt, name):
    T, W = pdn.shape[0], pdn.shape[1] // 4
    H = W // HEAD_DIM
    eb, eg, _, _ = _expanders(H)
    nb = T // bt

    def body(x_ref, prev_ref, ba_ref, w_ref, al_ref, dt_ref, eb_ref, eg_ref,
             q_ref, k_ref, v_ref, bx_ref, gx_ref, ext):
        i = pl.program_id(0)
        ext[pl.ds(0, HALO), :] = jnp.where(i > 0, prev_ref[...], 0.0)
        ext[pl.ds(HALO, bt), :] = x_ref[...]
        c = _conv_taps(ext, w_ref[...], bt, HALO)
        s = c * _sigmoid(c)
        q_ref[...] = _l2_heads(s[:, :W], H, lambda t: t * lax.rsqrt(jnp.sum(t * t, axis=-1, keepdims=True) + EPS)
                               * (HEAD_DIM ** -0.5))
        k_ref[...] = _l2_heads(s[:, W:2 * W], H, lambda t: t * lax.rsqrt(jnp.sum(t * t, axis=-1, keepdims=True) + EPS))
        v_ref[...] = s[:, 2 * W:]
        ba = ba_ref[...]
        beta = _sigmoid(ba)
        graw = -jnp.exp(al_ref[...]) * _softplus(ba + dt_ref[...])
        row, col = _iota2(bt, bt)
        tri = ((row // CHUNK == col // CHUNK) & (row >= col)).astype(BF16)
        gcum = _dot3r(tri, graw)
        bx_ref[...] = _dot3(beta, eb_ref[...])
        gx_ref[...] = _dot3(gcum, eg_ref[...])

    C = 3 * W
    rowb = lambda w: pl.BlockSpec((bt, w), lambda i: (i, 0))
    full = lambda a: pl.BlockSpec(a.shape, lambda i: (0,) * a.ndim)
    out = jax.ShapeDtypeStruct((T, W), F32)
    return pl.pallas_call(
        body, name=name, grid=(nb,),
        in_specs=[rowb(C), pl.BlockSpec((HALO, C), lambda i: (jnp.maximum(i * (bt // HALO) - 1, 0), 0)),
                  rowb(LANES), full(conv_w), full(a_log), full(dt_bias), full(eb), full(eg)],
        out_specs=(rowb(W),) * 5, out_shape=(out,) * 5,
        scratch_shapes=[pltpu.VMEM((bt + HALO, C), F32)], compiler_params=_params(1),
    )(pdn, pdn, pba, conv_w, a_log, dt_bias, eb, eg)


def _dn_pre_bwd(pdn, pba, conv_w, a_log, dt_bias, dq, dk, dv, dbx, dgx, bt, name):
    T, W = pdn.shape[0], pdn.shape[1] // 4
    H = W // HEAD_DIM
    C = 3 * W
    K = conv_w.shape[0]
    _, _, sb, sg = _expanders(H)
    nb = T // bt
    n_ext = bt + HALO

    def body(x_ref, prev_ref, next_ref, ba_ref, w_ref, al_ref, dt_ref, sb_ref, sg_ref,
             dq_ref, dqn_ref, dk_ref, dkn_ref, dv_ref, dvn_ref, dbx_ref, dgx_ref,
             dx_ref, dba_ref, dw_ref, dal_ref, ddt_ref, ext, dext, dcext):
        i = pl.program_id(0)
        last = i == nb - 1
        ext[pl.ds(0, HALO), :] = jnp.where(i > 0, prev_ref[...], 0.0)
        ext[pl.ds(HALO, bt), :] = x_ref[...]
        ext[pl.ds(HALO + bt, HALO), :] = jnp.where(last, 0.0, next_ref[...])
        dext[pl.ds(0, bt), pl.ds(0, W)] = dq_ref[...]
        dext[pl.ds(0, bt), pl.ds(W, W)] = dk_ref[...]
        dext[pl.ds(0, bt), pl.ds(2 * W, W)] = dv_ref[...]
        dext[pl.ds(bt, HALO), pl.ds(0, W)] = jnp.where(last, 0.0, dqn_ref[...])
        dext[pl.ds(bt, HALO), pl.ds(W, W)] = jnp.where(last, 0.0, dkn_ref[...])
        dext[pl.ds(bt, HALO), pl.ds(2 * W, W)] = jnp.where(last, 0.0, dvn_ref[...])
        w = w_ref[...]
        c = _conv_taps(ext, w, n_ext, HALO)
        sg_c = _sigmoid(c)
        s = c * sg_c
        d = dext[...]

        def l2_bwd(scale):
            def fn(pair):
                t, dt = pair
                r = lax.rsqrt(jnp.sum(t * t, axis=-1, keepdims=True) + EPS)
                return scale * r * (dt - t * (r * r) * jnp.sum(t * dt, axis=-1, keepdims=True))
            return fn

        def heads(lo, fn):
            return jnp.concatenate(
                [fn((s[:, lo + h * HEAD_DIM:lo + (h + 1) * HEAD_DIM], d[:, lo + h * HEAD_DIM:lo + (h + 1) * HEAD_DIM]))
                 for h in range(H)], axis=1)

        ds = jnp.concatenate([heads(0, l2_bwd(HEAD_DIM ** -0.5)), heads(W, l2_bwd(1.0)), d[:, 2 * W:]], axis=1)
        dc = ds * (sg_c * (1.0 + c * (1.0 - sg_c)))
        dcext[...] = dc
        dx = None
        for j in range(K):
            term = dcext[pl.ds(K - 1 - j, bt), :] * w[j:j + 1, :]
            dx = term if dx is None else dx + term
        dx_ref[...] = dx.astype(BF16)

        @pl.when(i == 0)
        def _():
            dw_ref[...] = jnp.zeros_like(dw_ref)
            dal_ref[...] = jnp.zeros_like(dal_ref)
            ddt_ref[...] = jnp.zeros_like(ddt_ref)

        dcb = dc[:bt]
        dw_ref[...] += jnp.concatenate(
            [jnp.sum(dcb * ext[pl.ds(HALO - (K - 1) + j, bt), :], axis=0, keepdims=True) for j in range(K)], axis=0)

        ba = ba_ref[...]
        beta = _sigmoid(ba)
        al, dtb = al_ref[...], dt_ref[...]
        dbeta = _dot3(dbx_ref[...], sb_ref[...])
        dg = _dot3(dgx_ref[...], sg_ref[...])
        sp = _softplus(ba + dtb)
        da = dg * (-jnp.exp(al)) * _sigmoid(ba + dtb)
        dba_ref[...] = dbeta * beta * (1.0 - beta) + da
        dal_ref[...] += jnp.sum(dg * (-jnp.exp(al)) * sp, axis=0, keepdims=True)
        ddt_ref[...] += jnp.sum(da, axis=0, keepdims=True)

    rowb = lambda w: pl.BlockSpec((bt, w), lambda i: (i, 0))
    full = lambda a: pl.BlockSpec(a.shape, lambda i: (0,) * a.ndim)
    nxt = lambda w: pl.BlockSpec((HALO, w), lambda i: (jnp.minimum((i + 1) * (bt // HALO), T // HALO - 1), 0))
    vec = pl.BlockSpec((1, LANES), lambda i: (0, 0))
    return pl.pallas_call(
        body, name=name, grid=(nb,),
        in_specs=[rowb(C), pl.BlockSpec((HALO, C), lambda i: (jnp.maximum(i * (bt // HALO) - 1, 0), 0)), nxt(C),
                  rowb(LANES), full(conv_w), full(a_log), full(dt_bias), full(sb), full(sg),
                  rowb(W), nxt(W), rowb(W), nxt(W), rowb(W), nxt(W), rowb(W), rowb(W)],
        out_specs=(rowb(C), rowb(LANES), pl.BlockSpec((K, C), lambda i: (0, 0)), vec, vec),
        out_shape=(jax.ShapeDtypeStruct((T, C), BF16), jax.ShapeDtypeStruct((T, LANES), F32),
                   jax.ShapeDtypeStruct((K, C), F32), jax.ShapeDtypeStruct((1, LANES), F32),
                   jax.ShapeDtypeStruct((1, LANES), F32)),
        scratch_shapes=[pltpu.VMEM((bt + 2 * HALO, C), F32), pltpu.VMEM((n_ext, C), F32), pltpu.VMEM((n_ext, C), F32)],
        compiler_params=_params(1),
    )(pdn, pdn, pdn, pba, conv_w, a_log, dt_bias, sb, sg, dq, dq, dk, dk, dv, dv, dbx, dgx)


def _dn_local(q, k, v, beta, g):
    row, col = _iota2(ROWS, ROWS)
    same = (row // CHUNK) == (col // CHUNK)
    causal = same & (row >= col)
    strict = same & (row > col)
    eye = (row == col).astype(F32)
    last_of = (col == (row // CHUNK) * CHUNK + (CHUNK - 1)).astype(BF16)
    eg = jnp.exp(g)
    decay = jnp.where(causal, jnp.exp(jnp.where(causal, g - g.T, 0.0)), 0.0)
    kb = k * beta
    vb = v * beta
    kk = _bdot(kb, k, NT)
    low = jnp.where(strict, kk * decay, 0.0)
    neg = -low
    tm = eye + neg
    pw = neg
    for _ in range(5):
        pw = jnp.dot(pw, pw, precision=HIGHEST, preferred_element_type=F32)
        tm = tm + jnp.dot(tm, pw, precision=HIGHEST, preferred_element_type=F32)
    kbg = kb * eg
    u = _bdot(tm, vb)
    w = _bdot(tm, kbg)
    qk = _bdot(q, k, NT)
    qa = jnp.where(causal, qk * decay, 0.0)
    glast = _dot3r(last_of, g)
    e2 = jnp.exp(glast - g)
    return dict(row=row, col=col, same=same, causal=causal, strict=strict, eg=eg, decay=decay, kb=kb, vb=vb, kk=kk,
                tm=tm, kbg=kbg, u=u, w=w, qk=qk, qa=qa, glast=glast, e2=e2, kte=k * e2, qd=q * eg)


def _dn_core_fwd(q, k, v, bx, gx, z, gain, name):
    T, W = q.shape
    H = W // HEAD_DIM
    nb = T // ROWS

    def body(q_ref, k_ref, v_ref, b_ref, g_ref, z_ref, gain_ref, o_ref, mix_ref, ss_ref, state):
        @pl.when(pl.program_id(1) == 0)
        def _():
            state[...] = jnp.zeros_like(state)

        L = _dn_local(q_ref[...], k_ref[...], v_ref[...], b_ref[...], g_ref[...])
        s = state[...]
        vns, qds = [], []
        for c in range(2):
            rows = slice(c * CHUNK, (c + 1) * CHUNK)
            ss_ref[0, c] = s
            vn = L["u"][rows] - _bdot(L["w"][rows], s)
            qds.append(_bdot(L["qd"][rows], s))
            vns.append(vn)
            s = s * jnp.exp(L["glast"][c * CHUNK:c * CHUNK + 1, :]) + _bdot(L["kte"][rows], vn, TN)
        state[...] = s
        o = jnp.concatenate(qds, axis=0) + _bdot(L["qa"], jnp.concatenate(vns, axis=0))
        o_ref[...] = o
        zz = z_ref[...]
        r = lax.rsqrt(jnp.mean(o * o, axis=-1, keepdims=True) + EPS)
        mix_ref[...] = ((o * r * gain_ref[...]) * (zz * _sigmoid(zz))).astype(BF16)

    blk = pl.BlockSpec((ROWS, HEAD_DIM), lambda h, b: (b, h))
    zblk = pl.BlockSpec((ROWS, HEAD_DIM), lambda h, b: (b, 3 * H + h))
    return pl.pallas_call(
        body, name=name, grid=(H, nb),
        in_specs=[blk] * 5 + [zblk, pl.BlockSpec((1, HEAD_DIM), lambda h, b: (0, 0))],
        out_specs=(blk, blk, pl.BlockSpec((1, 2, HEAD_DIM, HEAD_DIM), lambda h, b: (h, b, 0, 0))),
        out_shape=(jax.ShapeDtypeStruct((T, W), F32), jax.ShapeDtypeStruct((T, W), BF16),
                   jax.ShapeDtypeStruct((H, T // CHUNK, HEAD_DIM, HEAD_DIM), F32)),
        scratch_shapes=[pltpu.VMEM((HEAD_DIM, HEAD_DIM), F32)], compiler_params=_params(2),
    )(q, k, v, bx, gx, z, gain)


def _dn_core_bwd(q, k, v, bx, gx, z, gain, o, dmix, ss, dmix_col0, name):
    T, W = q.shape
    H = W // HEAD_DIM
    nb = T // ROWS
    c0 = dmix_col0 // HEAD_DIM

    def body(q_ref, k_ref, v_ref, b_ref, g_ref, z_ref, gain_ref, o_ref, dm_ref, ss_ref,
             dq_ref, dk_ref, dv_ref, dbx_ref, dgx_ref, dz_ref, dgain_ref, dstate):
        @pl.when(pl.program_id(1) == 0)
        def _():
            dstate[...] = jnp.zeros_like(dstate)
            dgain_ref[...] = jnp.zeros_like(dgain_ref)

        qv, kv, vv, beta, g = q_ref[...], k_ref[...], v_ref[...], b_ref[...], g_ref[...]
        gain_v = gain_ref[...]
        ov, zz, dm = o_ref[...], z_ref[...], dm_ref[...]
        r = lax.rsqrt(jnp.mean(ov * ov, axis=-1, keepdims=True) + EPS)
        n = ov * r
        sgz = _sigmoid(zz)
        d_on = dm * (zz * sgz)
        dz_ref[...] = dm * (n * gain_v) * (sgz * (1.0 + zz * (1.0 - sgz)))
        dgain_ref[0] += jnp.sum(d_on * n, axis=0, keepdims=True)
        gy = d_on * gain_v
        do = r * (gy - n * jnp.mean(gy * n, axis=-1, keepdims=True))

        L = _dn_local(qv, kv, vv, beta, g)
        row, causal, strict = L["row"], L["causal"], L["strict"]
        u, w, qa, qd, kte, tm = L["u"], L["w"], L["qa"], L["qd"], L["kte"], L["tm"]
        s_in = [ss_ref[0, 0], ss_ref[0, 1]]
        vn = [u[c * CHUNK:(c + 1) * CHUNK] - _bdot(w[c * CHUNK:(c + 1) * CHUNK], s_in[c]) for c in range(2)]
        vn_all = jnp.concatenate(vn, axis=0)
        qat_do = _bdot(qa, do, TN)
        d_qa = jnp.where(causal, _bdot(do, vn_all, NT), 0.0)
        ds = dstate[...]
        d_vn, d_kte, d_qd, d_w, d_gl = [None] * 2, [None] * 2, [None] * 2, [None] * 2, [None] * 2
        for c in (1, 0):
            rows = slice(c * CHUNK, (c + 1) * CHUNK)
            egl = jnp.exp(L["glast"][c * CHUNK:c * CHUNK + 1, :])
            d_vn[c] = qat_do[rows] + _bdot(kte[rows], ds)
            d_kte[c] = _bdot(vn[c], ds, NT)
            d_gl[c] = jnp.sum(jnp.sum(ds * s_in[c], axis=1, keepdims=True), axis=0, keepdims=True) * egl
            d_qd[c] = _bdot(do[rows], s_in[c], NT)
            d_w[c] = -_bdot(d_vn[c], s_in[c], NT)
            ds = ds * egl + _bdot(qd[rows], do[rows], TN) - _bdot(w[rows], d_vn[c], TN)
        dstate[...] = ds
        d_u = jnp.concatenate(d_vn, axis=0)
        d_w = jnp.concatenate(d_w, axis=0)
        d_qd = jnp.concatenate(d_qd, axis=0)
        d_kte = jnp.concatenate(d_kte, axis=0)

        d_tm = _bdot(d_u, L["vb"], NT) + _bdot(d_w, L["kbg"], NT)
        d_vb = _bdot(tm, d_u, TN)
        d_kbg = _bdot(tm, d_w, TN)
        d_low = jnp.where(strict, -_bdot(_bdot(tm, d_tm, TN), tm, NT), 0.0)
        decay = L["decay"]
        d_kk = d_low * decay
        d_qk = d_qa * decay
        d_decay = d_low * L["kk"] + d_qa * L["qk"]
        eg, e2 = L["eg"], L["e2"]
        d_kb = _bdot(d_kk, kv) + d_kbg * eg
        dk_ref[...] = _bdot(d_kk, L["kb"], TN) + _bdot(d_qk, qv, TN) + d_kb * beta + d_kte * e2
        dq_ref[...] = _bdot(d_qk, kv) + d_qd * eg
        dv_ref[...] = d_vb * beta
        rsum = lambda t: jnp.sum(t, axis=-1, keepdims=True)
        dbx_ref[...] = jnp.broadcast_to(rsum(d_kb * kv) + rsum(d_vb * vv), (ROWS, HEAD_DIM))
        d_eg = rsum(d_kbg * L["kb"]) + rsum(d_qd * qv)
        t2 = rsum(d_kte * kv) * e2
        ed = d_decay * decay
        d_g = d_eg * eg - t2 + rsum(ed) - rsum(ed.T)
        col = L["col"]
        chunk_sum = L["same"].astype(BF16)
        is_last = (row % CHUNK) == (CHUNK - 1)
        d_glast = _dot3r(chunk_sum, t2) + jnp.where(row < CHUNK, d_gl[0], d_gl[1])
        d_g = d_g + jnp.where(is_last, d_glast, 0.0)
        suffix = (L["same"] & (col >= row)).astype(BF16)
        dgx_ref[...] = _dot3r(suffix, d_g)

    rev = lambda b: nb - 1 - b
    blk = pl.BlockSpec((ROWS, HEAD_DIM), lambda h, b: (rev(b), h))
    zblk = pl.BlockSpec((ROWS, HEAD_DIM), lambda h, b: (rev(b), 3 * H + h))
    dmblk = pl.BlockSpec((ROWS, HEAD_DIM), lambda h, b: (rev(b), c0 + h))
    out = jax.ShapeDtypeStruct((T, W), F32)
    return pl.pallas_call(
        body, name=name, grid=(H, nb),
        in_specs=[blk] * 5 + [zblk, pl.BlockSpec((1, HEAD_DIM), lambda h, b: (0, 0)), blk, dmblk,
                              pl.BlockSpec((1, 2, HEAD_DIM, HEAD_DIM), lambda h, b: (h, rev(b), 0, 0))],
        out_specs=(blk,) * 6 + (pl.BlockSpec((1, 1, HEAD_DIM), lambda h, b: (h, 0, 0)),),
        out_shape=(out,) * 6 + (jax.ShapeDtypeStruct((H, 1, HEAD_DIM), F32),),
        scratch_shapes=[pltpu.VMEM((HEAD_DIM, HEAD_DIM), F32)], compiler_params=_params(2),
    )(q, k, v, bx, gx, z, gain, o, dmix, ss)


_GELU_C = 0.7978845608028654
_GELU_A = 0.044715


def _gelu(x):
    t = jnp.tanh(_GELU_C * (x + _GELU_A * (x * x * x)))
    return 0.5 * x * (1.0 + t), t


def _ffn_mid_fwd(up, conv_w, conv_b, bt, bc, name):
    T, F = up.shape[0], up.shape[1] // 2
    nc = F // bc

    def body(g_ref, gp_ref, v_ref, vp_ref, wg_ref, wv_ref, bg_ref, bv_ref, o_ref, gext, vext):
        i = pl.program_id(0)
        for ext, cur, prev in ((gext, g_ref, gp_ref), (vext, v_ref, vp_ref)):
            ext[pl.ds(0, HALO), :] = jnp.where(i > 0, prev[...], 0.0)
            ext[pl.ds(HALO, bt), :] = cur[...]
        gate = _conv_taps(gext, wg_ref[...], bt, HALO) + bg_ref[...]
        val = _conv_taps(vext, wv_ref[...], bt, HALO) + bv_ref[...]
        o_ref[...] = (_gelu(gate)[0] * val).astype(BF16)

    K = conv_w.shape[0]
    prev = lambda i: jnp.maximum(i * (bt // HALO) - 1, 0)
    return pl.pallas_call(
        body, name=name, grid=(T // bt, nc),
        in_specs=[pl.BlockSpec((bt, bc), lambda i, j: (i, j)), pl.BlockSpec((HALO, bc), lambda i, j: (prev(i), j)),
                  pl.BlockSpec((bt, bc), lambda i, j: (i, nc + j)),
                  pl.BlockSpec((HALO, bc), lambda i, j: (prev(i), nc + j)),
                  pl.BlockSpec((K, bc), lambda i, j: (0, j)), pl.BlockSpec((K, bc), lambda i, j: (0, nc + j)),
                  pl.BlockSpec((1, bc), lambda i, j: (0, j)), pl.BlockSpec((1, bc), lambda i, j: (0, nc + j))],
        out_specs=pl.BlockSpec((bt, bc), lambda i, j: (i, j)),
        out_shape=jax.ShapeDtypeStruct((T, F), BF16),
        scratch_shapes=[pltpu.VMEM((bt + HALO, bc), F32)] * 2, compiler_params=_params(2),
    )(up, up, up, up, conv_w, conv_w, conv_b, conv_b)


def _ffn_mid_bwd(up, conv_w, conv_b, da, bt, bc, name):
    T, F = up.shape[0], up.shape[1] // 2
    nc = F // bc
    K = conv_w.shape[0]
    nb = T // bt
    n_ext = bt + HALO

    def body(g_ref, gp_ref, gn_ref, v_ref, vp_ref, vn_ref, da_ref, dan_ref, wg_ref, wv_ref, bg_ref, bv_ref,
             dg_ref, dv_ref, dwg_ref, dwv_ref, dbg_ref, dbv_ref, gext, vext, dgext, dvext):
        i = pl.program_id(1)
        last = i == nb - 1
        for ext, cur, prev, nxt in ((gext, g_ref, gp_ref, gn_ref), (vext, v_ref, vp_ref, vn_ref)):
            ext[pl.ds(0, HALO), :] = jnp.where(i > 0, prev[...], 0.0)
            ext[pl.ds(HALO, bt), :] = cur[...]
            ext[pl.ds(HALO + bt, HALO), :] = jnp.where(last, 0.0, nxt[...])
        wg, wv = wg_ref[...], wv_ref[...]
        gate = _conv_taps(gext, wg, n_ext, HALO) + bg_ref[...]
        val = _conv_taps(vext, wv, n_ext, HALO) + bv_ref[...]
        dact = jnp.concatenate([da_ref[...], jnp.where(last, 0.0, dan_ref[...])], axis=0)
        ge, t = _gelu(gate)
        dgelu = 0.5 * (1.0 + t) + 0.5 * gate * (1.0 - t * t) * (_GELU_C * (1.0 + 3.0 * _GELU_A * (gate * gate)))
        dgext[...] = dact * val * dgelu
        dvext[...] = dact * ge

        @pl.when(i == 0)
        def _():
            for ref in (dwg_ref, dwv_ref, dbg_ref, dbv_ref):
                ref[...] = jnp.zeros_like(ref)

        for dext, ext, w, dx_ref, dw_ref, db_ref in ((dgext, gext, wg, dg_ref, dwg_ref, dbg_ref),
                                                     (dvext, vext, wv, dv_ref, dwv_ref, dbv_ref)):
            dx = None
            for j in range(K):
                term = dext[pl.ds(K - 1 - j, bt), :] * w[j:j + 1, :]
                dx = term if dx is None else dx + term
            dx_ref[...] = dx.astype(BF16)
            dcur = dext[pl.ds(0, bt), :]
            dw_ref[...] += jnp.concatenate(
                [jnp.sum(dcur * ext[pl.ds(HALO - (K - 1) + j, bt), :], axis=0, keepdims=True) for j in range(K)], axis=0)
            db_ref[...] += jnp.sum(dcur, axis=0, keepdims=True)

    prev = lambda i: jnp.maximum(i * (bt // HALO) - 1, 0)
    nxt = lambda i: jnp.minimum((i + 1) * (bt // HALO), T // HALO - 1)
    cur_g = pl.BlockSpec((bt, bc), lambda j, i: (i, j))
    cur_v = pl.BlockSpec((bt, bc), lambda j, i: (i, nc + j))
    outs = pl.pallas_call(
        body, name=name, grid=(nc, nb),
        in_specs=[cur_g, pl.BlockSpec((HALO, bc), lambda j, i: (prev(i), j)),
                  pl.BlockSpec((HALO, bc), lambda j, i: (nxt(i), j)),
                  cur_v, pl.BlockSpec((HALO, bc), lambda j, i: (prev(i), nc + j)),
                  pl.BlockSpec((HALO, bc), lambda j, i: (nxt(i), nc + j)),
                  cur_g, pl.BlockSpec((HALO, bc), lambda j, i: (nxt(i), j)),
                  pl.BlockSpec((K, bc), lambda j, i: (0, j)), pl.BlockSpec((K, bc), lambda j, i: (0, nc + j)),
                  pl.BlockSpec((1, bc), lambda j, i: (0, j)), pl.BlockSpec((1, bc), lambda j, i: (0, nc + j))],
        out_specs=(cur_g, cur_g, pl.BlockSpec((K, bc), lambda j, i: (0, j)), pl.BlockSpec((K, bc), lambda j, i: (0, j)),
                   pl.BlockSpec((1, bc), lambda j, i: (0, j)), pl.BlockSpec((1, bc), lambda j, i: (0, j))),
        out_shape=(jax.ShapeDtypeStruct((T, F), BF16), jax.ShapeDtypeStruct((T, F), BF16),
                   jax.ShapeDtypeStruct((K, F), F32), jax.ShapeDtypeStruct((K, F), F32),
                   jax.ShapeDtypeStruct((1, F), F32), jax.ShapeDtypeStruct((1, F), F32)),
        scratch_shapes=[pltpu.VMEM((bt + 2 * HALO, bc), F32)] * 2 + [pltpu.VMEM((n_ext, bc), F32)] * 2,
        compiler_params=_params(2),
    )(up, up, up, up, up, up, da, da, conv_w, conv_w, conv_b, conv_b)
    return outs


def _adamw(w, g, m, v, bt, name):
    R, C = w.shape
    bt = _fit_rows(R, bt)
    blk = pl.BlockSpec((bt, C), lambda i: (i, 0))

    def body(w_ref, g_ref, m_ref, v_ref, d_ref, m2_ref, v2_ref):
        gv = g_ref[...]
        m2 = ADAM_B1 * m_ref[...] + (1.0 - ADAM_B1) * gv
        v2 = ADAM_B2 * v_ref[...] + (1.0 - ADAM_B2) * (gv * gv)
        m_hat = m2 / (1.0 - ADAM_B1 ** ADAM_STEP)
        v_hat = v2 / (1.0 - ADAM_B2 ** ADAM_STEP)
        d_ref[...] = -ADAM_LR * (m_hat / (jnp.sqrt(v_hat) + ADAM_EPS) + ADAM_WD * w_ref[...])
        m2_ref[...] = m2
        v2_ref[...] = v2

    out = jax.ShapeDtypeStruct((R, C), F32)
    return pl.pallas_call(body, name=name, grid=(R // bt,), in_specs=[blk] * 4, out_specs=(blk,) * 3,
                          out_shape=(out,) * 3, compiler_params=_params(1))(w, g, m, v)


def _place():
    x, y, c = lax.axis_index("x"), lax.axis_index("y"), lax.axis_index("c")
    chips = [(1 - x, y), (x, 1 - y), (1 - x, 1 - y)]
    return x, y, c, chips


_HBM = pl.BlockSpec(memory_space=pltpu.HBM)


def _gather_xy(bufs, name):
    n = len(bufs)
    halves = [b.shape[0] // 2 for b in bufs]

    def body(*refs):
        ins, outs = refs[:n], refs[n:2 * n]
        send, recv, local = refs[2 * n:]
        x, y, c, chips = _place()
        me = 2 * x + y
        copies, forwards = [], []
        for b in range(n):
            h = halves[b]
            mine = pl.ds(c * h, h)
            own = pltpu.make_async_copy(ins[b], outs[b].at[me], local.at[b])
            own.start()
            copies.append(own)
            for j, (px, py) in enumerate(chips):
                cp = pltpu.make_async_remote_copy(
                    src_ref=ins[b].at[mine], dst_ref=outs[b].at[me, mine], send_sem=send.at[b, j],
                    recv_sem=recv.at[b, j], device_id=(px, py, c), device_id_type=MESH)
                cp.start()
                copies.append(cp)
        for b in range(n):
            h = halves[b]
            mine = pl.ds(c * h, h)
            for j, (px, py) in enumerate(chips):
                src = 2 * px + py
                landed = pltpu.make_async_remote_copy(
                    src_ref=ins[b].at[mine], dst_ref=outs[b].at[src, mine], send_sem=send.at[b, j],
                    recv_sem=recv.at[b, j], device_id=(px, py, c), device_id_type=MESH)
                landed.wait_recv()
                fw = pltpu.make_async_remote_copy(
                    src_ref=outs[b].at[src, mine], dst_ref=outs[b].at[src, mine], send_sem=send.at[b, 3 + j],
                    recv_sem=recv.at[b, 3 + j], device_id=(x, y, 1 - c), device_id_type=MESH)
                fw.start()
                forwards.append(fw)
        for b in range(n):
            h = halves[b]
            theirs = pl.ds((1 - c) * h, h)
            for j, (px, py) in enumerate(chips):
                src = 2 * px + py
                pltpu.make_async_remote_copy(
                    src_ref=outs[b].at[src, theirs], dst_ref=outs[b].at[src, theirs], send_sem=send.at[b, 3 + j],
                    recv_sem=recv.at[b, 3 + j], device_id=(x, y, 1 - c), device_id_type=MESH).wait_recv()
        for b in range(n):
            copies[b * 4].wait()
            for j in range(3):
                copies[b * 4 + 1 + j].wait_send()
        for fw in forwards:
            fw.wait_send()

    return pl.pallas_call(
        body, name=name, in_specs=[_HBM] * n, out_specs=[_HBM] * n,
        out_shape=[jax.ShapeDtypeStruct((4,) + b.shape, b.dtype) for b in bufs],
        scratch_shapes=[pltpu.SemaphoreType.DMA((n, 6)), pltpu.SemaphoreType.DMA((n, 6)), pltpu.SemaphoreType.DMA((n,))],
        compiler_params=pltpu.CompilerParams(has_side_effects=True),
    )(*bufs)


def _swap_half_c(buf, name):
    n, h = buf.shape[0], buf.shape[1] // 2

    def body(in_ref, out_ref, send, recv):
        x, y, c, _ = _place()
        cp = pltpu.make_async_remote_copy(
            src_ref=in_ref.at[:, pl.ds((1 - c) * h, h)], dst_ref=out_ref, send_sem=send, recv_sem=recv,
            device_id=(x, y, 1 - c), device_id_type=MESH)
        cp.start()
        cp.wait()

    return pl.pallas_call(
        body, name=name, in_specs=[_HBM], out_specs=_HBM,
        out_shape=jax.ShapeDtypeStruct((n, h, buf.shape[2]), buf.dtype),
        scratch_shapes=[pltpu.SemaphoreType.DMA, pltpu.SemaphoreType.DMA],
        compiler_params=pltpu.CompilerParams(has_side_effects=True),
    )(buf)


def _scatter_xy(buf, name):
    h = buf.shape[1]

    def body(in_ref, out_ref, send, recv):
        x, y, c, chips = _place()
        cps = []
        for j, (px, py) in enumerate(chips):
            cp = pltpu.make_async_remote_copy(
                src_ref=in_ref.at[2 * px + py], dst_ref=out_ref.at[j], send_sem=send.at[j], recv_sem=recv.at[j],
                device_id=(px, py, c), device_id_type=MESH)
            cp.start()
            cps.append(cp)
        for cp in cps:
            cp.wait()

    return pl.pallas_call(
        body, name=name, in_specs=[_HBM], out_specs=_HBM,
        out_shape=jax.ShapeDtypeStruct((3, h, buf.shape[2]), buf.dtype),
        scratch_shapes=[pltpu.SemaphoreType.DMA((3,)), pltpu.SemaphoreType.DMA((3,))],
        compiler_params=pltpu.CompilerParams(has_side_effects=True),
    )(buf)


def _join_c(half, name):
    h = half.shape[0]

    def body(in_ref, out_ref, send, recv, local):
        x, y, c, _ = _place()
        own = pltpu.make_async_copy(in_ref, out_ref.at[c], local)
        own.start()
        cp = pltpu.make_async_remote_copy(
            src_ref=in_ref, dst_ref=out_ref.at[c], send_sem=send, recv_sem=recv,
            device_id=(x, y, 1 - c), device_id_type=MESH)
        cp.start()
        cp.wait()
        own.wait()

    return pl.pallas_call(
        body, name=name, in_specs=[_HBM], out_specs=_HBM,
        out_shape=jax.ShapeDtypeStruct((2, h, half.shape[1]), half.dtype),
        scratch_shapes=[pltpu.SemaphoreType.DMA, pltpu.SemaphoreType.DMA, pltpu.SemaphoreType.DMA],
        compiler_params=pltpu.CompilerParams(has_side_effects=True),
    )(half)


def _add_half(buf, other, c, bt, name):
    n, _, h, lanes = buf.shape
    bt = _fit_rows(h, bt)

    def body(c_ref, a_ref, b_ref, o_ref):
        o_ref[...] = a_ref[0] + b_ref[...]

    return pl.pallas_call(
        body, name=name,
        grid_spec=pltpu.PrefetchScalarGridSpec(
            num_scalar_prefetch=1, grid=(n, h // bt),
            in_specs=[pl.BlockSpec((1, 1, bt, lanes), lambda k, i, c_ref: (k, c_ref[0], i, 0)),
                      pl.BlockSpec((1, bt, lanes), lambda k, i, c_ref: (k, i, 0))],
            out_specs=pl.BlockSpec((1, bt, lanes), lambda k, i, c_ref: (k, i, 0))),
        out_shape=jax.ShapeDtypeStruct((n, h, lanes), buf.dtype), compiler_params=_params(2),
    )(c, buf, other)


def _add_four(own, me, others, bt, name):
    _, h, lanes = own.shape
    bt = _fit_rows(h, bt)

    def body(me_ref, a_ref, b_ref, o_ref):
        o_ref[...] = ((a_ref[0] + b_ref[0]) + b_ref[1]) + b_ref[2]

    return pl.pallas_call(
        body, name=name,
        grid_spec=pltpu.PrefetchScalarGridSpec(
            num_scalar_prefetch=1, grid=(h // bt,),
            in_specs=[pl.BlockSpec((1, bt, lanes), lambda i, me_ref: (me_ref[0], i, 0)),
                      pl.BlockSpec((3, bt, lanes), lambda i, me_ref: (0, i, 0))],
            out_specs=pl.BlockSpec((bt, lanes), lambda i, me_ref: (i, 0))),
        out_shape=jax.ShapeDtypeStruct((h, lanes), own.dtype), compiler_params=_params(1),
    )(me, own, others)


def _fit_rows(n, target):
    best = None
    for t in range(HALO, min(n, target) + 1, HALO):
        if n % t == 0:
            best = t
    assert best is not None, (n, target)
    return best


def _allreduce_small(buf, name):
    R, lanes = buf.shape

    def body(in_ref, out_ref, land, send, recv):
        x, y, c, _ = _place()
        me = 4 * x + 2 * y + c
        land[me] = in_ref[...]
        cps = []
        for r in range(1, 8):
            px, py, pc = x ^ (r >> 2), y ^ ((r >> 1) & 1), c ^ (r & 1)
            cp = pltpu.make_async_remote_copy(
                src_ref=in_ref, dst_ref=land.at[me], send_sem=send.at[r - 1], recv_sem=recv.at[me],
                device_id=(px, py, pc), device_id_type=MESH)
            cp.start()
            cps.append(cp)
        for r in range(1, 8):
            peer = 4 * (x ^ (r >> 2)) + 2 * (y ^ ((r >> 1) & 1)) + (c ^ (r & 1))
            pltpu.make_async_remote_copy(
                src_ref=in_ref, dst_ref=land.at[peer], send_sem=send.at[r - 1], recv_sem=recv.at[peer],
                device_id=(x, y, c), device_id_type=MESH).wait_recv()
        for cp in cps:
            cp.wait_send()
        acc = land[0]
        for d in range(1, 8):
            acc = acc + land[d]
        out_ref[...] = acc

    vm = pl.BlockSpec(memory_space=pltpu.VMEM)
    return pl.pallas_call(
        body, name=name, in_specs=[vm], out_specs=vm, out_shape=jax.ShapeDtypeStruct((R, lanes), buf.dtype),
        scratch_shapes=[pltpu.VMEM((8, R, lanes), buf.dtype), pltpu.SemaphoreType.DMA((7,)), pltpu.SemaphoreType.DMA((8,))],
        compiler_params=pltpu.CompilerParams(has_side_effects=True, vmem_limit_bytes=VMEM_LIMIT),
    )(buf)


ROW_BLOCK = 256
SB_BLOCK = 256
MM_TM, MM_TN, MM_TK = 1024, 512, 512
FFN_COLS = 512


def _lane_pad(vec, start):
    return jnp.pad(vec, ((0, 0), (start, LANES - start - vec.shape[1])))


def _local_step(x, target, wt):
    T, D = x.shape
    W = D // 2
    H = W // HEAD_DIM
    F = wt["w_down"].shape[0]
    bt = min(ROW_BLOCK, T)
    blk = min(SB_BLOCK, T)
    w_in = wt["w_in"]
    w_sb, w_dn = w_in[:, :3 * W], w_in[:, 3 * W:7 * W]
    w_ba = jnp.pad(w_in[:, 7 * W:], ((0, 0), (0, LANES - 2 * H)))
    w_out, w_up, w_down = wt["w_out"], wt["w_up"], wt["w_down"]
    a_log, dt_bias = _lane_pad(wt["dn_a_log"], H), _lane_pad(wt["dn_dt_bias"], H)
    mm = functools.partial(_mm, tm=MM_TM, tn=MM_TN)

    xn = _rms_fwd(x, wt["ln_mix_pre"], None, BF16, bt, "rms_mix_pre")
    psb = mm([(xn, w_sb, D)], "nn", BF16, name="proj_sb")
    pdn = mm([(xn, w_dn, D)], "nn", F32, name="proj_dn")
    pba = mm([(xn, w_ba, D)], "nn", F32, name="proj_ba")
    o_sb, mix_sb, lt = _sb_fwd(psb, wt["sb_out_gain"], blk, "sb_fwd")
    qn, kn, vv, bx, gx = _dn_pre_fwd(pdn, pba, wt["dn_conv_w"], a_log, dt_bias, bt, "dn_pre_fwd")
    o_dn, mix_dn, ss = _dn_core_fwd(qn, kn, vv, bx, gx, pdn, wt["dn_out_gain"], "dn_core_fwd")
    m = mm([(mix_sb, w_out[:W], MM_TK), (mix_dn, w_out[W:], MM_TK)], "nn", F32, name="out_proj")
    h = _rms_fwd(m, wt["ln_mix_post"], x, F32, bt, "rms_mix_post")
    hn = _rms_fwd(h, wt["ln_ffn_pre"], None, BF16, bt, "rms_ffn_pre")
    up = mm([(hn, w_up, D)], "nn", F32, name="ffn_up")
    bc = min(FFN_COLS, F)
    act = _ffn_mid_fwd(up, wt["ffn_conv_w"], wt["ffn_conv_b"], bt, bc, "ffn_mid_fwd")
    f = mm([(act, w_down, MM_TK)], "nn", F32, name="ffn_down")
    dy, df, g_ffn_post, sq = _loss_head(f, wt["ln_ffn_post"], h, target, bt, "loss_head")
    loss = 0.5 * jnp.sum(sq) / D

    da = mm([(df, w_down, D)], "nt", F32, name="d_act")
    g_w_down = mm([(act, df, MM_TK)], "tn", F32, name="g_w_down")
    dug, duv, dwg, dwv, dbg, dbv = _ffn_mid_bwd(up, wt["ffn_conv_w"], wt["ffn_conv_b"], da, bt, bc, "ffn_mid_bwd")
    dhn = mm([(dug, w_up[:, :F], MM_TK), (duv, w_up[:, F:], MM_TK)], "nt", F32, name="d_hn")
    g_w_up = jnp.concatenate([mm([(hn, dug, MM_TK)], "tn", F32, name="g_w_up_gate"),
                              mm([(hn, duv, MM_TK)], "tn", F32, name="g_w_up_val")], axis=1)
    dh, g_ffn_pre = _rms_bwd(h, wt["ln_ffn_pre"], dhn, dy, F32, bt, "rms_ffn_pre_bwd")
    dm, g_mix_post = _rms_bwd(m, wt["ln_mix_post"], dh, None, BF16, bt, "rms_mix_post_bwd")
    dmix = mm([(dm, w_out, D)], "nt", F32, name="d_mix")
    g_w_out = jnp.concatenate([mm([(mix_sb, dm, MM_TK)], "tn", F32, name="g_w_out_sb"),
                               mm([(mix_dn, dm, MM_TK)], "tn", F32, name="g_w_out_dn")], axis=0)
    do_sb, g_sb_gain = _headnorm_bwd(o_sb, wt["sb_out_gain"], dmix, bt, "sb_norm_bwd")
    dq, dk, dv = _sb_bwd(psb, do_sb, lt, blk, "sb_bwd")
    ddq, ddk, ddv, dbx, dgx, dz, g_dn_gain = _dn_core_bwd(qn, kn, vv, bx, gx, pdn, wt["dn_out_gain"], o_dn, dmix, ss,
                                                         W, "dn_core_bwd")
    dconv, dba, g_dn_conv, g_a_log, g_dt_bias = _dn_pre_bwd(pdn, pba, wt["dn_conv_w"], a_log, dt_bias,
                                                            ddq, ddk, ddv, dbx, dgx, bt, "dn_pre_bwd")
    pieces = [(dq, w_in[:, :W]), (dk, w_in[:, W:2 * W]), (dv, w_in[:, 2 * W:3 * W]), (dconv, w_in[:, 3 * W:6 * W]),
              (dz, w_in[:, 6 * W:7 * W]), (dba, w_ba)]
    dxn = mm([(d, wp, MM_TK) for d, wp in pieces], "nt", F32, name="d_xn")
    g_w_in = jnp.concatenate([mm([(xn, d, MM_TK)], "tn", F32, name=f"g_w_in_{i}") for i, (d, _) in enumerate(pieces)],
                             axis=1)[:, :7 * W + 2 * H]
    dx, g_mix_pre = _rms_bwd(x, wt["ln_mix_pre"], dxn, dh, F32, bt, "rms_mix_pre_bwd")

    grads = dict(
        w_in=g_w_in, sb_out_gain=g_sb_gain, dn_conv_w=g_dn_conv, dn_a_log=g_a_log[:, H:2 * H],
        dn_dt_bias=g_dt_bias[:, H:2 * H], dn_out_gain=jnp.sum(g_dn_gain, axis=0), w_out=g_w_out,
        ln_mix_pre=g_mix_pre, ln_mix_post=g_mix_post, w_up=g_w_up,
        ffn_conv_w=jnp.concatenate([dwg, dwv], axis=1), ffn_conv_b=jnp.concatenate([dbg, dbv], axis=1),
        w_down=g_w_down, ln_ffn_pre=g_ffn_pre, ln_ffn_post=g_ffn_post)
    return loss, dx, grads


WEIGHTS = ("w_in", "sb_out_gain", "dn_conv_w", "dn_a_log", "dn_dt_bias", "dn_out_gain", "w_out", "ln_mix_pre",
           "ln_mix_post", "w_up", "ffn_conv_w", "ffn_conv_b", "w_down", "ln_ffn_pre", "ln_ffn_post")
MATRICES = {"w_in": 1, "w_out": 0, "w_up": 1, "w_down": 0}
CONV_SHARDED = ("dn_conv_w", "ffn_conv_w")
SMALL = tuple(n for n in WEIGHTS if n not in MATRICES)
N_CHIPS = 4
ROW_QUANTUM = 32
ADD_ROWS = 2048
ADAM_ROWS = 256


def _pack(arrs, quantum):
    rows, layout, off = [], [], 0
    for a in arrs:
        n = int(np.prod(a.shape))
        r = -(-n // LANES)
        r = -(-r // HALO) * HALO
        rows.append(jnp.pad(a.reshape(-1), (0, r * LANES - n)).reshape(r, LANES))
        layout.append((off, r, n, a.shape))
        off += r
    total = -(-off // quantum) * quantum
    if total > off:
        rows.append(jnp.zeros((total - off, LANES), rows[0].dtype))
    return jnp.concatenate(rows, axis=0), layout


def _unpack(packed, layout):
    return [packed[off:off + r].reshape(-1)[:n].reshape(shape) for off, r, n, shape in layout]


def kernel(x, w_in, sb_out_gain, dn_conv_w, dn_a_log, dn_dt_bias, dn_out_gain, w_out, ln_mix_pre, ln_mix_post, w_up, ffn_conv_w, ffn_conv_b, w_down, ln_ffn_pre, ln_ffn_post, loss_target, m_w_in, m_sb_out_gain, m_dn_conv_w, m_dn_a_log, m_dn_dt_bias, m_dn_out_gain, m_w_out, m_ln_mix_pre, m_ln_mix_post, m_w_up, m_ffn_conv_w, m_ffn_conv_b, m_w_down, m_ln_ffn_pre, m_ln_ffn_post, v_w_in, v_sb_out_gain, v_dn_conv_w, v_dn_a_log, v_dn_dt_bias, v_dn_out_gain, v_w_out, v_ln_mix_pre, v_ln_mix_post, v_w_up, v_ffn_conv_w, v_ffn_conv_b, v_w_down, v_ln_ffn_pre, v_ln_ffn_post):
    given = dict(locals())
    wl = {n: given[n][0] for n in WEIGHTS}
    ml = {n: given["m_" + n][0] for n in WEIGHTS}
    vl = {n: given["v_" + n][0] for n in WEIGHTS}
    for d in (wl, ml, vl):
        for n in SMALL:
            if d[n].ndim == 1:
                d[n] = d[n][None]
    cx, cy, cc = lax.axis_index("x"), lax.axis_index("y"), lax.axis_index("c")
    chip = 2 * cx + cy

    mats, mat_layout = _pack([wl[n].astype(BF16) for n in MATRICES], ROW_QUANTUM)
    taps, tap_layout = _pack([wl[n] for n in CONV_SHARDED], ROW_QUANTUM)
    all_mats, all_taps = _gather_xy([mats, taps], "gather_weights")
    wt = {n: wl[n] for n in SMALL}
    for i, n in enumerate(MATRICES):
        wt[n] = jnp.concatenate([_unpack(all_mats[k], mat_layout)[i] for k in range(N_CHIPS)], axis=MATRICES[n])
    for i, n in enumerate(CONV_SHARDED):
        wt[n] = jnp.concatenate([_unpack(all_taps[k], tap_layout)[i] for k in range(N_CHIPS)], axis=1)

    loss, dx, grads = _local_step(x[0], loss_target[0], wt)
    loss = lax.psum(loss, ("x", "y", "c"))

    def shard_of(n, k):
        g, axis = grads[n], MATRICES[n]
        size = g.shape[axis] // N_CHIPS
        return lax.slice_in_dim(g, k * size, (k + 1) * size, axis=axis)

    packed = [_pack([shard_of(n, k) for n in MATRICES], ROW_QUANTUM) for k in range(N_CHIPS)]
    glayout = packed[0][1]
    gp = jnp.stack([p[0] for p in packed])
    half = gp.shape[1] // 2
    from_sibling = _swap_half_c(gp, "grad_swap_cores")
    chip_sum = _add_half(gp.reshape(N_CHIPS, 2, half, LANES), from_sibling, cc.reshape(1), ADD_ROWS, "grad_add_cores")
    from_chips = _scatter_xy(chip_sum, "grad_scatter_chips")
    reduced_half = _add_four(chip_sum, chip.reshape(1), from_chips, ADD_ROWS, "grad_add_chips")
    reduced = _join_c(reduced_half, "grad_join_cores").reshape(2 * half, LANES)
    gl = dict(zip(MATRICES, _unpack(reduced, glayout)))

    small, small_layout = _pack([grads[n] for n in SMALL], HALO)
    small = _allreduce_small(small, "grad_allreduce_small")
    for n, g in zip(SMALL, _unpack(small, small_layout)):
        if n in CONV_SHARDED:
            size = g.shape[1] // N_CHIPS
            g = lax.dynamic_slice_in_dim(g, chip * size, size, axis=1)
        gl[n] = g

    delta, new_m, new_v = {}, {}, {}
    for n in MATRICES:
        delta[n], new_m[n], new_v[n] = _adamw(wl[n], gl[n], ml[n], vl[n], ADAM_ROWS, "adamw_" + n)
    packs = [_pack([d[n] for n in SMALL], HALO) for d in (wl, gl, ml, vl)]
    outs = _adamw(*[p[0] for p in packs], ADAM_ROWS, "adamw_small")
    for res, o in zip((delta, new_m, new_v), outs):
        res.update(zip(SMALL, _unpack(o, packs[0][1])))

    shaped = lambda d: [d[n].reshape(given[n].shape) for n in WEIGHTS]
    return (loss, dx[None], *shaped(gl), *shaped(delta), *shaped(new_m), *shaped(new_v))
```

```python
import functools

import numpy as np
import jax
import jax.numpy as jnp
from jax import lax
from jax.experimental import pallas as pl
from jax.experimental.pallas import tpu as pltpu

F32 = jnp.float32
BF16 = jnp.bfloat16
HEAD_DIM = 128
CHUNK = 64
ROWS = 2 * CHUNK
EPS = 1e-6
LANES = 128
HALO = 8
VMEM_LIMIT = 48 * 1024 * 1024
ADAM_LR, ADAM_B1, ADAM_B2, ADAM_EPS, ADAM_WD, ADAM_STEP = 0.001, 0.9, 0.999, 1e-08, 0.01, 10
MESH = pl.DeviceIdType.MESH
HIGHEST = lax.Precision.HIGHEST

NN = (((1,), (0,)), ((), ()))
NT = (((1,), (1,)), ((), ()))
TN = (((0,), (0,)), ((), ()))


def _params(n_axes):
    return pltpu.CompilerParams(dimension_semantics=("arbitrary",) * n_axes, vmem_limit_bytes=VMEM_LIMIT)


def _bdot(a, b, dims=NN):
    return lax.dot_general(a.astype(BF16), b.astype(BF16), dims, preferred_element_type=F32)


def _split3(a):
    hi = a.astype(BF16)
    r1 = a - hi.astype(F32)
    mid = r1.astype(BF16)
    lo = (r1 - mid.astype(F32)).astype(BF16)
    return hi, mid, lo


def _dot3(a, sel, dims=NN):
    return sum(lax.dot_general(p, sel, dims, preferred_element_type=F32) for p in _split3(a))


def _dot3r(sel, a, dims=NN):
    return sum(lax.dot_general(sel, p, dims, preferred_element_type=F32) for p in _split3(a))


def _iota2(n, m):
    return lax.broadcasted_iota(jnp.int32, (n, m), 0), lax.broadcasted_iota(jnp.int32, (n, m), 1)


def _sigmoid(x):
    return 1.0 / (1.0 + jnp.exp(-x))


def _softplus(x):
    return jnp.maximum(x, 0.0) + jnp.log(1.0 + jnp.exp(-jnp.abs(x)))


def _fit(values, target):
    values = [v for v in (values if isinstance(values, (list, tuple)) else [values]) if v]
    best = None
    for t in range(LANES, min(min(values), target) + 1, LANES):
        if all(v % t == 0 for v in values):
            best = t
    assert best is not None, (values, target)
    return best


def _mm(parts, mode, out_dtype, tm, tn, name, n_window=None, out_shard=None, into=None):
    dims = {"nn": NN, "nt": NT, "tn": TN}[mode]
    a0, b0 = parts[0][0], parts[0][1]
    b3 = b0.ndim == 3
    shard_c = b0.shape[2] if b3 else None
    M = a0.shape[1] if mode == "tn" else a0.shape[0]
    if mode == "nt":
        n_full = b0.shape[1] if b3 else b0.shape[0]
    else:
        n_full = b0.shape[0] * b0.shape[2] if b3 else b0.shape[1]
    n0, N = n_window if n_window is not None else (0, n_full)
    out_n0 = into[1] if into is not None else 0
    tm = _fit(M, tm)
    tn = _fit([N, n0, out_n0, out_shard, shard_c if mode != "nt" else None], tn)
    specs_a, specs_b, offs, nks = [], [], [], []
    off = 0
    for a, b, tk, a_k0, b_k0 in parts:
        K = a.shape[0] if mode == "tn" else a.shape[1]
        tk = _fit([K, a_k0, b_k0, shard_c if mode == "nt" else None], tk)
        nk = K // tk
        kk = lambda k, o=off, n=nk: jnp.clip(k - o, 0, n - 1)
        ao, bo, no = a_k0 // tk, b_k0 // tk, n0 // tn
        if mode == "tn":
            specs_a.append(pl.BlockSpec((tk, tm), lambda i, j, k, kk=kk, ao=ao: (kk(k) + ao, i)))
        else:
            specs_a.append(pl.BlockSpec((tm, tk), lambda i, j, k, kk=kk, ao=ao: (i, kk(k) + ao)))
        if mode == "nt":
            if b3:
                per = shard_c // tk
                specs_b.append(pl.BlockSpec((None, tn, tk), lambda i, j, k, kk=kk, bo=bo, per=per:
                                            ((kk(k) + bo) // per, j, (kk(k) + bo) % per)))
            else:
                specs_b.append(pl.BlockSpec((tn, tk), lambda i, j, k, kk=kk, bo=bo: (j, kk(k) + bo)))
        else:
            if b3:
                per = shard_c // tn
                specs_b.append(pl.BlockSpec((None, tk, tn), lambda i, j, k, kk=kk, bo=bo, no=no, per=per:
                                            ((j + no) // per, kk(k) + bo, (j + no) % per)))
            else:
                specs_b.append(pl.BlockSpec((tk, tn), lambda i, j, k, kk=kk, bo=bo, no=no: (kk(k) + bo, j + no)))
        offs.append(off)
        nks.append(nk)
        off += nk
    nk_total = off
    n_parts = len(parts)

    def body(*refs):
        a_refs, b_refs = refs[:n_parts], refs[n_parts:2 * n_parts]
        o_ref = refs[2 * n_parts + (1 if into is not None else 0)]
        if nk_total == 1:
            o_ref[...] = _bdot(a_refs[0][...], b_refs[0][...], dims).astype(out_dtype)
            return
        acc = refs[-1]
        k = pl.program_id(2)

        @pl.when(k == 0)
        def _():
            acc[...] = jnp.zeros_like(acc)

        for p in range(n_parts):
            @pl.when((k >= offs[p]) & (k < offs[p] + nks[p]))
            def _(p=p):
                acc[...] += _bdot(a_refs[p][...], b_refs[p][...], dims)

        @pl.when(k == nk_total - 1)
        def _():
            o_ref[...] = acc[...].astype(out_dtype)

    jo = out_n0 // tn
    if out_shard is not None:
        per_o = out_shard // tn
        out_spec = pl.BlockSpec((None, tm, tn), lambda i, j, k: ((j + jo) // per_o, i, (j + jo) % per_o))
        out_shape = jax.ShapeDtypeStruct((N // out_shard, M, out_shard), out_dtype)
    else:
        out_spec = pl.BlockSpec((tm, tn), lambda i, j, k: (i, j + jo))
        out_shape = jax.ShapeDtypeStruct((M, N), out_dtype)
    ins = [p[0] for p in parts] + [p[1] for p in parts]
    in_specs = specs_a + specs_b
    aliases = {}
    if into is not None:
        out_shape = jax.ShapeDtypeStruct(into[0].shape, into[0].dtype)
        aliases = {len(ins): 0}
        ins.append(into[0])
        in_specs.append(pl.BlockSpec(memory_space=pl.ANY))
    return pl.pallas_call(
        body, name=name, grid=(M // tm, N // tn, nk_total), in_specs=in_specs, out_specs=out_spec, out_shape=out_shape,
        scratch_shapes=[] if nk_total == 1 else [pltpu.VMEM((tm, tn), F32)],
        input_output_aliases=aliases, compiler_params=_params(3),
    )(*ins)


def _rms_fwd(x, gain, resid, out_dtype, bt, name):
    T, D = x.shape
    row = pl.BlockSpec((bt, D), lambda i: (i, 0))
    vec = pl.BlockSpec((1, D), lambda i: (0, 0))

    def body(*refs):
        x_ref, g_ref = refs[0], refs[1]
        o_ref = refs[-1]
        xv = x_ref[...]
        y = xv * lax.rsqrt(jnp.mean(xv * xv, axis=-1, keepdims=True) + EPS) * g_ref[...]
        if resid is not None:
            y = refs[2][...] + y
        o_ref[...] = y.astype(out_dtype)

    ins = [x, gain] + ([resid] if resid is not None else [])
    return pl.pallas_call(
        body, name=name, grid=(T // bt,),
        in_specs=[row, vec] + ([row] if resid is not None else []),
        out_specs=row, out_shape=jax.ShapeDtypeStruct((T, D), out_dtype), compiler_params=_params(1),
    )(*ins)


def _rms_bwd_math(xv, g, dy):
    r = lax.rsqrt(jnp.mean(xv * xv, axis=-1, keepdims=True) + EPS)
    n = xv * r
    gy = dy * g
    dx = r * (gy - n * jnp.mean(gy * n, axis=-1, keepdims=True))
    return dx, dy * n


def _rms_bwd(x, gain, dy, resid, out_dtype, bt, name):
    T, D = x.shape
    row = pl.BlockSpec((bt, D), lambda i: (i, 0))
    vec = pl.BlockSpec((1, D), lambda i: (0, 0))

    def body(*refs):
        x_ref, g_ref, dy_ref = refs[0], refs[1], refs[2]
        dx_ref, dg_ref = refs[-2], refs[-1]
        dx, dgp = _rms_bwd_math(x_ref[...], g_ref[...], dy_ref[...].astype(F32))
        if resid is not None:
            dx = refs[3][...] + dx
        dx_ref[...] = dx.astype(out_dtype)

        @pl.when(pl.program_id(0) == 0)
        def _():
            dg_ref[...] = jnp.zeros_like(dg_ref)

        dg_ref[...] += jnp.sum(dgp, axis=0, keepdims=True)

    ins = [x, gain, dy] + ([resid] if resid is not None else [])
    return pl.pallas_call(
        body, name=name, grid=(T // bt,),
        in_specs=[row, vec, row] + ([row] if resid is not None else []),
        out_specs=(row, vec),
        out_shape=(jax.ShapeDtypeStruct((T, D), out_dtype), jax.ShapeDtypeStruct((1, D), F32)),
        compiler_params=_params(1),
    )(*ins)


def _loss_head(f, gain, h, target, bt, name):
    T, D = f.shape
    row = pl.BlockSpec((bt, D), lambda i: (i, 0))
    vec = pl.BlockSpec((1, D), lambda i: (0, 0))

    def body(f_ref, g_ref, h_ref, t_ref, dy_ref, df_ref, dg_ref, sq_ref):
        fv, g = f_ref[...], g_ref[...]
        r = lax.rsqrt(jnp.mean(fv * fv, axis=-1, keepdims=True) + EPS)
        n = fv * r
        err = (h_ref[...] + n * g) - t_ref[...]
        dy = err * (1.0 / D)
        gy = dy * g
        df = r * (gy - n * jnp.mean(gy * n, axis=-1, keepdims=True))
        dy_ref[...] = dy
        df_ref[...] = df.astype(BF16)

        @pl.when(pl.program_id(0) == 0)
        def _():
            dg_ref[...] = jnp.zeros_like(dg_ref)
            sq_ref[...] = jnp.zeros_like(sq_ref)

        dg_ref[...] += jnp.sum(dy * n, axis=0, keepdims=True)
        sq_ref[...] += jnp.sum(err * err, axis=0, keepdims=True)

    return pl.pallas_call(
        body, name=name, grid=(T // bt,), in_specs=[row, vec, row, row], out_specs=(row, row, vec, vec),
        out_shape=(jax.ShapeDtypeStruct((T, D), F32), jax.ShapeDtypeStruct((T, D), BF16),
                   jax.ShapeDtypeStruct((1, D), F32), jax.ShapeDtypeStruct((1, D), F32)),
        compiler_params=_params(1),
    )(f, gain, h, target)


def _sb_logits(q, k, valid):
    z = lax.dot_general(q, k, NT, preferred_element_type=F32) * (HEAD_DIM ** -0.5)
    sp = jnp.log(1.0 + jnp.exp(-jnp.abs(z)))
    lb = jnp.minimum(z, 0.0) - sp
    l1 = -(jnp.maximum(z, 0.0) + sp)
    return lb, (l1 if valid is None else jnp.where(valid, l1, 0.0))


def _masked(valid, x):
    return x if valid is None else jnp.where(valid, x, 0.0)


def _dot2(a, sel):
    hi = a.astype(BF16)
    lo = (a - hi.astype(F32)).astype(BF16)
    return jnp.dot(hi, sel, preferred_element_type=F32) + jnp.dot(lo, sel, preferred_element_type=F32)


def _sb_fwd(qkv, gain, blk, name):
    T, W = qkv.shape[0], qkv.shape[1] // 3
    H = W // HEAD_DIM

    def body(q_ref, k_ref, v_ref, g_ref, o_ref, mix_ref, lt_ref):
        i = pl.program_id(1)
        q = q_ref[...]
        row, col = _iota2(blk, blk)
        after = (row > col).astype(BF16)

        def step(kb, carry, valid):
            run, acc = carry
            ks = pl.ds(pl.multiple_of(kb * blk, blk), blk)
            lb, l1 = _sb_logits(q, k_ref[ks, :], valid)
            att = _masked(valid, jnp.exp(lb + _dot2(l1, after) + run))
            acc = acc + _bdot(att, v_ref[ks, :])
            return run + jnp.sum(l1, axis=1, keepdims=True), acc

        carry = step(i, (jnp.zeros((blk, 1), F32), jnp.zeros((blk, HEAD_DIM), F32)), col < row)
        run, o = lax.fori_loop(0, i, lambda jj, c: step(i - 1 - jj, c, None), carry)
        o_ref[...] = o
        r = lax.rsqrt(jnp.mean(o * o, axis=-1, keepdims=True) + EPS)
        mix_ref[...] = (o * r * g_ref[...]).astype(BF16)
        lt_ref[...] = jnp.broadcast_to(run, (blk, HEAD_DIM))

    qb = pl.BlockSpec((blk, HEAD_DIM), lambda h, i: (i, h))
    return pl.pallas_call(
        body, name=name, grid=(H, T // blk),
        in_specs=[qb, pl.BlockSpec((T, HEAD_DIM), lambda h, i: (0, H + h)),
                  pl.BlockSpec((T, HEAD_DIM), lambda h, i: (0, 2 * H + h)),
                  pl.BlockSpec((1, HEAD_DIM), lambda h, i: (0, 0))],
        out_specs=(qb, qb, qb),
        out_shape=(jax.ShapeDtypeStruct((T, W), F32), jax.ShapeDtypeStruct((T, W), BF16),
                   jax.ShapeDtypeStruct((T, W), F32)),
        compiler_params=_params(2),
    )(qkv, qkv, qkv, gain)


def _headnorm_bwd(o, gain, dmix, bt, name):
    T, W = o.shape
    H = W // HEAD_DIM
    blk = pl.BlockSpec((bt, HEAD_DIM), lambda i, h: (i, h))
    vec = pl.BlockSpec((1, HEAD_DIM), lambda i, h: (0, 0))

    def body(o_ref, g_ref, d_ref, do_ref, dg_ref):
        do, dgp = _rms_bwd_math(o_ref[...], g_ref[...], d_ref[...])
        do_ref[...] = do

        @pl.when((pl.program_id(0) == 0) & (pl.program_id(1) == 0))
        def _():
            dg_ref[...] = jnp.zeros_like(dg_ref)

        dg_ref[...] += jnp.sum(dgp, axis=0, keepdims=True)

    return pl.pallas_call(
        body, name=name, grid=(T // bt, H), in_specs=[blk, vec, blk], out_specs=(blk, vec),
        out_shape=(jax.ShapeDtypeStruct((T, W), F32), jax.ShapeDtypeStruct((1, HEAD_DIM), F32)),
        compiler_params=_params(2),
    )(o, gain, dmix)


def _sb_bwd(qkv, do, lt, blk, name):
    T, W = qkv.shape[0], qkv.shape[1] // 3
    H = W // HEAD_DIM
    scale = HEAD_DIM ** -0.5

    def body(q_ref, k_ref, v_ref, do_ref, lt_ref, dq_ref, dk_ref, dv_ref):
        i = pl.program_id(1)

        @pl.when(i == 0)
        def _():
            dk_ref[...] = jnp.zeros_like(dk_ref)
            dv_ref[...] = jnp.zeros_like(dv_ref)

        q = q_ref[...]
        dob = do_ref[...].astype(BF16)
        total = lt_ref[...][:, :1]
        row, col = _iota2(blk, blk)
        upto = (row <= col).astype(BF16)
        before = (row < col).astype(BF16)

        def step(kb, carry, valid):
            seen, psum, dq = carry
            ks = pl.ds(pl.multiple_of(kb * blk, blk), blk)
            k, v = k_ref[ks, :], v_ref[ks, :]
            lb, l1 = _sb_logits(q, k, valid)
            later = total - seen - _dot2(l1, upto)
            att = _masked(valid, jnp.exp(lb + later))
            p = att * lax.dot_general(dob, v, NT, preferred_element_type=F32)
            c = psum + _dot2(p, before)
            sig = jnp.exp(lb)
            dz = (_masked(valid, p * (1.0 - sig) - c * sig) * scale).astype(BF16)
            dq = dq + jnp.dot(dz, k, preferred_element_type=F32)
            dk_ref[ks, :] += lax.dot_general(dz, q, TN, preferred_element_type=F32)
            dv_ref[ks, :] += lax.dot_general(att.astype(BF16), dob, TN, preferred_element_type=F32)
            return seen + jnp.sum(l1, axis=1, keepdims=True), psum + jnp.sum(p, axis=1, keepdims=True), dq

        zero = jnp.zeros((blk, 1), F32)
        carry = lax.fori_loop(0, i, lambda kb, c: step(kb, c, None), (zero, zero, jnp.zeros((blk, HEAD_DIM), F32)))
        dq_ref[...] = step(i, carry, col < row)[2]

    qb = pl.BlockSpec((blk, HEAD_DIM), lambda h, i: (i, h))
    head = pl.BlockSpec((T, HEAD_DIM), lambda h, i: (0, h))
    out = jax.ShapeDtypeStruct((T, W), F32)
    return pl.pallas_call(
        body, name=name, grid=(H, T // blk),
        in_specs=[qb, pl.BlockSpec((T, HEAD_DIM), lambda h, i: (0, H + h)),
                  pl.BlockSpec((T, HEAD_DIM), lambda h, i: (0, 2 * H + h)), qb, qb],
        out_specs=(qb, head, head), out_shape=(out, out, out), compiler_params=_params(2),
    )(qkv, qkv, qkv, do, lt)


def _expanders(H):
    lane = np.arange(H * HEAD_DIM) // HEAD_DIM
    eb = np.zeros((LANES, H * HEAD_DIM), np.float32)
    eg = np.zeros((LANES, H * HEAD_DIM), np.float32)
    eb[lane, np.arange(H * HEAD_DIM)] = 1.0
    eg[H + lane, np.arange(H * HEAD_DIM)] = 1.0
    sb = np.zeros((H * HEAD_DIM, LANES), np.float32)
    sg = np.zeros((H * HEAD_DIM, LANES), np.float32)
    sb[np.arange(H) * HEAD_DIM, np.arange(H)] = 1.0
    sg[np.arange(H) * HEAD_DIM, H + np.arange(H)] = 1.0
    return [jnp.asarray(m, BF16) for m in (eb, eg, sb, sg)]


def _conv_taps(ext_ref, w, n_out, lead):
    K = w.shape[0]
    out = None
    for j in range(K):
        term = ext_ref[pl.ds(lead - (K - 1) + j, n_out), :] * w[j:j + 1, :]
        out = term if out is None else out + term
    return out


def _l2_heads(s, H, fn):
    return jnp.concatenate([fn(s[:, h * HEAD_DIM:(h + 1) * HEAD_DIM]) for h in range(H)], axis=1)


def _dn_pre_fwd(pdn, pba, conv_w, a_log, dt_bias, bt, name):
    T, W = pdn.shape[0], pdn.shape[1] // 4
    H = W // HEAD_DIM
    eb, eg, _, _ = _expanders(H)
    nb = T // bt

    def body(x_ref, prev_ref, ba_ref, w_ref, al_ref, dt_ref, eb_ref, eg_ref,
             q_ref, k_ref, v_ref, bx_ref, gx_ref, ext):
        i = pl.program_id(0)
        ext[pl.ds(0, HALO), :] = jnp.where(i > 0, prev_ref[...], 0.0)
        ext[pl.ds(HALO, bt), :] = x_ref[...]
        c = _conv_taps(ext, w_ref[...], bt, HALO)
        s = c * _sigmoid(c)
        q_ref[...] = _l2_heads(s[:, :W], H, lambda t: t * lax.rsqrt(jnp.sum(t * t, axis=-1, keepdims=True) + EPS)
                               * (HEAD_DIM ** -0.5))
        k_ref[...] = _l2_heads(s[:, W:2 * W], H, lambda t: t * lax.rsqrt(jnp.sum(t * t, axis=-1, keepdims=True) + EPS))
        v_ref[...] = s[:, 2 * W:]
        ba = ba_ref[...]
        beta = _sigmoid(ba)
        graw = -jnp.exp(al_ref[...]) * _softplus(ba + dt_ref[...])
        row, col = _iota2(bt, bt)
        tri = ((row // CHUNK == col // CHUNK) & (row >= col)).astype(BF16)
        gcum = _dot3r(tri, graw)
        bx_ref[...] = _dot3(beta, eb_ref[...])
        gx_ref[...] = _dot3(gcum, eg_ref[...])

    C = 3 * W
    rowb = lambda w: pl.BlockSpec((bt, w), lambda i: (i, 0))
    full = lambda a: pl.BlockSpec(a.shape, lambda i: (0,) * a.ndim)
    out = jax.ShapeDtypeStruct((T, W), F32)
    return pl.pallas_call(
        body, name=name, grid=(nb,),
        in_specs=[rowb(C), pl.BlockSpec((HALO, C), lambda i: (jnp.maximum(i * (bt // HALO) - 1, 0), 0)),
                  rowb(LANES), full(conv_w), full(a_log), full(dt_bias), full(eb), full(eg)],
        out_specs=(rowb(W),) * 5, out_shape=(out,) * 5,
        scratch_shapes=[pltpu.VMEM((bt + HALO, C), F32)], compiler_params=_params(1),
    )(pdn, pdn, pba, conv_w, a_log, dt_bias, eb, eg)


def _dn_pre_bwd(pdn, pba, conv_w, a_log, dt_bias, dq, dk, dv, dbx, dgx, bt, name):
    T, W = pdn.shape[0], pdn.shape[1] // 4
    H = W // HEAD_DIM
    C = 3 * W
    K = conv_w.shape[0]
    _, _, sb, sg = _expanders(H)
    nb = T // bt
    n_ext = bt + HALO

    def body(x_ref, prev_ref, next_ref, ba_ref, w_ref, al_ref, dt_ref, sb_ref, sg_ref,
             dq_ref, dqn_ref, dk_ref, dkn_ref, dv_ref, dvn_ref, dbx_ref, dgx_ref,
             dx_ref, dba_ref, dw_ref, dal_ref, ddt_ref, ext, dext, dcext):
        i = pl.program_id(0)
        last = i == nb - 1
        ext[pl.ds(0, HALO), :] = jnp.where(i > 0, prev_ref[...], 0.0)
        ext[pl.ds(HALO, bt), :] = x_ref[...]
        ext[pl.ds(HALO + bt, HALO), :] = jnp.where(last, 0.0, next_ref[...])
        dext[pl.ds(0, bt), pl.ds(0, W)] = dq_ref[...]
        dext[pl.ds(0, bt), pl.ds(W, W)] = dk_ref[...]
        dext[pl.ds(0, bt), pl.ds(2 * W, W)] = dv_ref[...]
        dext[pl.ds(bt, HALO), pl.ds(0, W)] = jnp.where(last, 0.0, dqn_ref[...])
        dext[pl.ds(bt, HALO), pl.ds(W, W)] = jnp.where(last, 0.0, dkn_ref[...])
        dext[pl.ds(bt, HALO), pl.ds(2 * W, W)] = jnp.where(last, 0.0, dvn_ref[...])
        w = w_ref[...]
        c = _conv_taps(ext, w, n_ext, HALO)
        sg_c = _sigmoid(c)
        s = c * sg_c
        d = dext[...]

        def l2_bwd(scale):
            def fn(pair):
                t, dt = pair
                r = lax.rsqrt(jnp.sum(t * t, axis=-1, keepdims=True) + EPS)
                return scale * r * (dt - t * (r * r) * jnp.sum(t * dt, axis=-1, keepdims=True))
            return fn

        def heads(lo, fn):
            return jnp.concatenate(
                [fn((s[:, lo + h * HEAD_DIM:lo + (h + 1) * HEAD_DIM], d[:, lo + h * HEAD_DIM:lo + (h + 1) * HEAD_DIM]))
                 for h in range(H)], axis=1)

        ds = jnp.concatenate([heads(0, l2_bwd(HEAD_DIM ** -0.5)), heads(W, l2_bwd(1.0)), d[:, 2 * W:]], axis=1)
        dc = ds * (sg_c * (1.0 + c * (1.0 - sg_c)))
        dcext[...] = dc
        dx = None
        for j in range(K):
            term = dcext[pl.ds(K - 1 - j, bt), :] * w[j:j + 1, :]
            dx = term if dx is None else dx + term
        dx_ref[...] = dx.astype(BF16)

        @pl.when(i == 0)
        def _():
            dw_ref[...] = jnp.zeros_like(dw_ref)
            dal_ref[...] = jnp.zeros_like(dal_ref)
            ddt_ref[...] = jnp.zeros_like(ddt_ref)

        dcb = dc[:bt]
        dw_ref[...] += jnp.concatenate(
            [jnp.sum(dcb * ext[pl.ds(HALO - (K - 1) + j, bt), :], axis=0, keepdims=True) for j in range(K)], axis=0)

        ba = ba_ref[...]
        beta = _sigmoid(ba)
        al, dtb = al_ref[...], dt_ref[...]
        dbeta = _dot3(dbx_ref[...], sb_ref[...])
        dg = _dot3(dgx_ref[...], sg_ref[...])
        sp = _softplus(ba + dtb)
        da = dg * (-jnp.exp(al)) * _sigmoid(ba + dtb)
        dba_ref[...] = dbeta * beta * (1.0 - beta) + da
        dal_ref[...] += jnp.sum(dg * (-jnp.exp(al)) * sp, axis=0, keepdims=True)
        ddt_ref[...] += jnp.sum(da, axis=0, keepdims=True)

    rowb = lambda w: pl.BlockSpec((bt, w), lambda i: (i, 0))
    full = lambda a: pl.BlockSpec(a.shape, lambda i: (0,) * a.ndim)
    nxt = lambda w: pl.BlockSpec((HALO, w), lambda i: (jnp.minimum((i + 1) * (bt // HALO), T // HALO - 1), 0))
    vec = pl.BlockSpec((1, LANES), lambda i: (0, 0))
    return pl.pallas_call(
        body, name=name, grid=(nb,),
        in_specs=[rowb(C), pl.BlockSpec((HALO, C), lambda i: (jnp.maximum(i * (bt // HALO) - 1, 0), 0)), nxt(C),
                  rowb(LANES), full(conv_w), full(a_log), full(dt_bias), full(sb), full(sg),
                  rowb(W), nxt(W), rowb(W), nxt(W), rowb(W), nxt(W), rowb(W), rowb(W)],
        out_specs=(rowb(C), rowb(LANES), pl.BlockSpec((K, C), lambda i: (0, 0)), vec, vec),
        out_shape=(jax.ShapeDtypeStruct((T, C), BF16), jax.ShapeDtypeStruct((T, LANES), F32),
                   jax.ShapeDtypeStruct((K, C), F32), jax.ShapeDtypeStruct((1, LANES), F32),
                   jax.ShapeDtypeStruct((1, LANES), F32)),
        scratch_shapes=[pltpu.VMEM((bt + 2 * HALO, C), F32), pltpu.VMEM((n_ext, C), F32), pltpu.VMEM((n_ext, C), F32)],
        compiler_params=_params(1),
    )(pdn, pdn, pdn, pba, conv_w, a_log, dt_bias, sb, sg, dq, dq, dk, dk, dv, dv, dbx, dgx)


def _dn_local(q, k, v, beta, g):
    row, col = _iota2(ROWS, ROWS)
    same = (row // CHUNK) == (col // CHUNK)
    causal = same & (row >= col)
    strict = same & (row > col)
    eye = (row == col).astype(F32)
    last_of = (col == (row // CHUNK) * CHUNK + (CHUNK - 1)).astype(BF16)
    eg = jnp.exp(g)
    decay = jnp.where(causal, jnp.exp(jnp.where(causal, g - g.T, 0.0)), 0.0)
    kb = k * beta
    vb = v * beta
    kk = _bdot(kb, k, NT)
    low = jnp.where(strict, kk * decay, 0.0)
    neg = -low
    tm = eye + neg
    pw = neg
    for _ in range(5):
        pw = jnp.dot(pw, pw, precision=HIGHEST, preferred_element_type=F32)
        tm = tm + jnp.dot(tm, pw, precision=HIGHEST, preferred_element_type=F32)
    kbg = kb * eg
    u = _bdot(tm, vb)
    w = _bdot(tm, kbg)
    qk = _bdot(q, k, NT)
    qa = jnp.where(causal, qk * decay, 0.0)
    glast = _dot3r(last_of, g)
    e2 = jnp.exp(glast - g)
    return dict(row=row, col=col, same=same, causal=causal, strict=strict, eg=eg, decay=decay, kb=kb, vb=vb, kk=kk,
                tm=tm, kbg=kbg, u=u, w=w, qk=qk, qa=qa, glast=glast, e2=e2, kte=k * e2, qd=q * eg)


def _dn_core_fwd(q, k, v, bx, gx, z, gain, name):
    T, W = q.shape
    H = W // HEAD_DIM
    nb = T // ROWS

    def body(q_ref, k_ref, v_ref, b_ref, g_ref, z_ref, gain_ref, o_ref, mix_ref, ss_ref, state):
        @pl.when(pl.program_id(1) == 0)
        def _():
            state[...] = jnp.zeros_like(state)

        L = _dn_local(q_ref[...], k_ref[...], v_ref[...], b_ref[...], g_ref[...])
        s = state[...]
        vns, qds = [], []
        for c in range(2):
            rows = slice(c * CHUNK, (c + 1) * CHUNK)
            ss_ref[0, c] = s
            vn = L["u"][rows] - _bdot(L["w"][rows], s)
            qds.append(_bdot(L["qd"][rows], s))
            vns.append(vn)
            s = s * jnp.exp(L["glast"][c * CHUNK:c * CHUNK + 1, :]) + _bdot(L["kte"][rows], vn, TN)
        state[...] = s
        o = jnp.concatenate(qds, axis=0) + _bdot(L["qa"], jnp.concatenate(vns, axis=0))
        o_ref[...] = o
        zz = z_ref[...]
        r = lax.rsqrt(jnp.mean(o * o, axis=-1, keepdims=True) + EPS)
        mix_ref[...] = ((o * r * gain_ref[...]) * (zz * _sigmoid(zz))).astype(BF16)

    blk = pl.BlockSpec((ROWS, HEAD_DIM), lambda h, b: (b, h))
    zblk = pl.BlockSpec((ROWS, HEAD_DIM), lambda h, b: (b, 3 * H + h))
    return pl.pallas_call(
        body, name=name, grid=(H, nb),
        in_specs=[blk] * 5 + [zblk, pl.BlockSpec((1, HEAD_DIM), lambda h, b: (0, 0))],
        out_specs=(blk, blk, pl.BlockSpec((1, 2, HEAD_DIM, HEAD_DIM), lambda h, b: (h, b, 0, 0))),
        out_shape=(jax.ShapeDtypeStruct((T, W), F32), jax.ShapeDtypeStruct((T, W), BF16),
                   jax.ShapeDtypeStruct((H, T // CHUNK, HEAD_DIM, HEAD_DIM), F32)),
        scratch_shapes=[pltpu.VMEM((HEAD_DIM, HEAD_DIM), F32)], compiler_params=_params(2),
    )(q, k, v, bx, gx, z, gain)


def _dn_core_bwd(q, k, v, bx, gx, z, gain, o, dmix, ss, dmix_col0, name):
    T, W = q.shape
    H = W // HEAD_DIM
    nb = T // ROWS
    c0 = dmix_col0 // HEAD_DIM

    def body(q_ref, k_ref, v_ref, b_ref, g_ref, z_ref, gain_ref, o_ref, dm_ref, ss_ref,
             dq_ref, dk_ref, dv_ref, dbx_ref, dgx_ref, dz_ref, dgain_ref, dstate):
        @pl.when(pl.program_id(1) == 0)
        def _():
            dstate[...] = jnp.zeros_like(dstate)
            dgain_ref[...] = jnp.zeros_like(dgain_ref)

        qv, kv, vv, beta, g = q_ref[...], k_ref[...], v_ref[...], b_ref[...], g_ref[...]
        gain_v = gain_ref[...]
        ov, zz, dm = o_ref[...], z_ref[...], dm_ref[...]
        r = lax.rsqrt(jnp.mean(ov * ov, axis=-1, keepdims=True) + EPS)
        n = ov * r
        sgz = _sigmoid(zz)
        d_on = dm * (zz * sgz)
        dz_ref[...] = dm * (n * gain_v) * (sgz * (1.0 + zz * (1.0 - sgz)))
        dgain_ref[0] += jnp.sum(d_on * n, axis=0, keepdims=True)
        gy = d_on * gain_v
        do = r * (gy - n * jnp.mean(gy * n, axis=-1, keepdims=True))

        L = _dn_local(qv, kv, vv, beta, g)
        row, causal, strict = L["row"], L["causal"], L["strict"]
        u, w, qa, qd, kte, tm = L["u"], L["w"], L["qa"], L["qd"], L["kte"], L["tm"]
        s_in = [ss_ref[0, 0], ss_ref[0, 1]]
        vn = [u[c * CHUNK:(c + 1) * CHUNK] - _bdot(w[c * CHUNK:(c + 1) * CHUNK], s_in[c]) for c in range(2)]
        vn_all = jnp.concatenate(vn, axis=0)
        qat_do = _bdot(qa, do, TN)
        d_qa = jnp.where(causal, _bdot(do, vn_all, NT), 0.0)
        ds = dstate[...]
        d_vn, d_kte, d_qd, d_w, d_gl = [None] * 2, [None] * 2, [None] * 2, [None] * 2, [None] * 2
        for c in (1, 0):
            rows = slice(c * CHUNK, (c + 1) * CHUNK)
            egl = jnp.exp(L["glast"][c * CHUNK:c * CHUNK + 1, :])
            d_vn[c] = qat_do[rows] + _bdot(kte[rows], ds)
            d_kte[c] = _bdot(vn[c], ds, NT)
            d_gl[c] = jnp.sum(jnp.sum(ds * s_in[c], axis=1, keepdims=True), axis=0, keepdims=True) * egl
            d_qd[c] = _bdot(do[rows], s_in[c], NT)
            d_w[c] = -_bdot(d_vn[c], s_in[c], NT)
            ds = ds * egl + _bdot(qd[rows], do[rows], TN) - _bdot(w[rows], d_vn[c], TN)
        dstate[...] = ds
        d_u = jnp.concatenate(d_vn, axis=0)
        d_w = jnp.concatenate(d_w, axis=0)
        d_qd = jnp.concatenate(d_qd, axis=0)
        d_kte = jnp.concatenate(d_kte, axis=0)

        d_tm = _bdot(d_u, L["vb"], NT) + _bdot(d_w, L["kbg"], NT)
        d_vb = _bdot(tm, d_u, TN)
        d_kbg = _bdot(tm, d_w, TN)
        d_low = jnp.where(strict, -_bdot(_bdot(tm, d_tm, TN), tm, NT), 0.0)
        decay = L["decay"]
        d_kk = d_low * decay
        d_qk = d_qa * decay
        d_decay = d_low * L["kk"] + d_qa * L["qk"]
        eg, e2 = L["eg"], L["e2"]
        d_kb = _bdot(d_kk, kv) + d_kbg * eg
        dk_ref[...] = _bdot(d_kk, L["kb"], TN) + _bdot(d_qk, qv, TN) + d_kb * beta + d_kte * e2
        dq_ref[...] = _bdot(d_qk, kv) + d_qd * eg
        dv_ref[...] = d_vb * beta
        rsum = lambda t: jnp.sum(t, axis=-1, keepdims=True)
        dbx_ref[...] = jnp.broadcast_to(rsum(d_kb * kv) + rsum(d_vb * vv), (ROWS, HEAD_DIM))
        d_eg = rsum(d_kbg * L["kb"]) + rsum(d_qd * qv)
        t2 = rsum(d_kte * kv) * e2
        ed = d_decay * decay
        d_g = d_eg * eg - t2 + rsum(ed) - rsum(ed.T)
        col = L["col"]
        chunk_sum = L["same"].astype(BF16)
        is_last = (row % CHUNK) == (CHUNK - 1)
        d_glast = _dot3r(chunk_sum, t2) + jnp.where(row < CHUNK, d_gl[0], d_gl[1])
        d_g = d_g + jnp.where(is_last, d_glast, 0.0)
        suffix = (L["same"] & (col >= row)).astype(BF16)
        dgx_ref[...] = _dot3r(suffix, d_g)

    rev = lambda b: nb - 1 - b
    blk = pl.BlockSpec((ROWS, HEAD_DIM), lambda h, b: (rev(b), h))
    zblk = pl.BlockSpec((ROWS, HEAD_DIM), lambda h, b: (rev(b), 3 * H + h))
    dmblk = pl.BlockSpec((ROWS, HEAD_DIM), lambda h, b: (rev(b), c0 + h))
    out = jax.ShapeDtypeStruct((T, W), F32)
    return pl.pallas_call(
        body, name=name, grid=(H, nb),
        in_specs=[blk] * 5 + [zblk, pl.BlockSpec((1, HEAD_DIM), lambda h, b: (0, 0)), blk, dmblk,
                              pl.BlockSpec((1, 2, HEAD_DIM, HEAD_DIM), lambda h, b: (h, rev(b), 0, 0))],
        out_specs=(blk,) * 6 + (pl.BlockSpec((1, 1, HEAD_DIM), lambda h, b: (h, 0, 0)),),
        out_shape=(out,) * 6 + (jax.ShapeDtypeStruct((H, 1, HEAD_DIM), F32),),
        scratch_shapes=[pltpu.VMEM((HEAD_DIM, HEAD_DIM), F32)], compiler_params=_params(2),
    )(q, k, v, bx, gx, z, gain, o, dmix, ss)


_GELU_C = 0.7978845608028654
_GELU_A = 0.044715


def _gelu(x):
    t = jnp.tanh(_GELU_C * (x + _GELU_A * (x * x * x)))
    return 0.5 * x * (1.0 + t), t


def _ffn_mid_fwd(up, conv_w, conv_b, bt, bc, name):
    T, F = up.shape[0], up.shape[1] // 2
    nc = F // bc

    def body(g_ref, gp_ref, v_ref, vp_ref, wg_ref, wv_ref, bg_ref, bv_ref, o_ref, gext, vext):
        i = pl.program_id(0)
        for ext, cur, prev in ((gext, g_ref, gp_ref), (vext, v_ref, vp_ref)):
            ext[pl.ds(0, HALO), :] = jnp.where(i > 0, prev[...], 0.0)
            ext[pl.ds(HALO, bt), :] = cur[...]
        gate = _conv_taps(gext, wg_ref[...], bt, HALO) + bg_ref[...]
        val = _conv_taps(vext, wv_ref[...], bt, HALO) + bv_ref[...]
        o_ref[...] = (_gelu(gate)[0] * val).astype(BF16)

    K = conv_w.shape[0]
    prev = lambda i: jnp.maximum(i * (bt // HALO) - 1, 0)
    return pl.pallas_call(
        body, name=name, grid=(T // bt, nc),
        in_specs=[pl.BlockSpec((bt, bc), lambda i, j: (i, j)), pl.BlockSpec((HALO, bc), lambda i, j: (prev(i), j)),
                  pl.BlockSpec((bt, bc), lambda i, j: (i, nc + j)),
                  pl.BlockSpec((HALO, bc), lambda i, j: (prev(i), nc + j)),
                  pl.BlockSpec((K, bc), lambda i, j: (0, j)), pl.BlockSpec((K, bc), lambda i, j: (0, nc + j)),
                  pl.BlockSpec((1, bc), lambda i, j: (0, j)), pl.BlockSpec((1, bc), lambda i, j: (0, nc + j))],
        out_specs=pl.BlockSpec((bt, bc), lambda i, j: (i, j)),
        out_shape=jax.ShapeDtypeStruct((T, F), BF16),
        scratch_shapes=[pltpu.VMEM((bt + HALO, bc), F32)] * 2, compiler_params=_params(2),
    )(up, up, up, up, conv_w, conv_w, conv_b, conv_b)


def _ffn_mid_bwd(up, conv_w, conv_b, da, bt, bc, name):
    T, F = up.shape[0], up.shape[1] // 2
    nc = F // bc
    K = conv_w.shape[0]
    nb = T // bt
    n_ext = bt + HALO

    def body(g_ref, gp_ref, gn_ref, v_ref, vp_ref, vn_ref, da_ref, dan_ref, wg_ref, wv_ref, bg_ref, bv_ref,
             dg_ref, dv_ref, dwg_ref, dwv_ref, dbg_ref, dbv_ref, gext, vext, dgext, dvext):
        i = pl.program_id(1)
        last = i == nb - 1
        for ext, cur, prev, nxt in ((gext, g_ref, gp_ref, gn_ref), (vext, v_ref, vp_ref, vn_ref)):
            ext[pl.ds(0, HALO), :] = jnp.where(i > 0, prev[...], 0.0)
            ext[pl.ds(HALO, bt), :] = cur[...]
            ext[pl.ds(HALO + bt, HALO), :] = jnp.where(last, 0.0, nxt[...])
        wg, wv = wg_ref[...], wv_ref[...]
        gate = _conv_taps(gext, wg, n_ext, HALO) + bg_ref[...]
        val = _conv_taps(vext, wv, n_ext, HALO) + bv_ref[...]
        dact = jnp.concatenate([da_ref[...], jnp.where(last, 0.0, dan_ref[...])], axis=0)
        ge, t = _gelu(gate)
        dgelu = 0.5 * (1.0 + t) + 0.5 * gate * (1.0 - t * t) * (_GELU_C * (1.0 + 3.0 * _GELU_A * (gate * gate)))
        dgext[...] = dact * val * dgelu
        dvext[...] = dact * ge

        @pl.when(i == 0)
        def _():
            for ref in (dwg_ref, dwv_ref, dbg_ref, dbv_ref):
                ref[...] = jnp.zeros_like(ref)

        for dext, ext, w, dx_ref, dw_ref, db_ref in ((dgext, gext, wg, dg_ref, dwg_ref, dbg_ref),
                                                     (dvext, vext, wv, dv_ref, dwv_ref, dbv_ref)):
            dx = None
            for j in range(K):
                term = dext[pl.ds(K - 1 - j, bt), :] * w[j:j + 1, :]
                dx = term if dx is None else dx + term
            dx_ref[...] = dx.astype(BF16)
            dcur = dext[pl.ds(0, bt), :]
            dw_ref[...] += jnp.concatenate(
                [jnp.sum(dcur * ext[pl.ds(HALO - (K - 1) + j, bt), :], axis=0, keepdims=True) for j in range(K)], axis=0)
            db_ref[...] += jnp.sum(dcur, axis=0, keepdims=True)

    prev = lambda i: jnp.maximum(i * (bt // HALO) - 1, 0)
    nxt = lambda i: jnp.minimum((i + 1) * (bt // HALO), T // HALO - 1)
    cur_g = pl.BlockSpec((bt, bc), lambda j, i: (i, j))
    cur_v = pl.BlockSpec((bt, bc), lambda j, i: (i, nc + j))
    outs = pl.pallas_call(
        body, name=name, grid=(nc, nb),
        in_specs=[cur_g, pl.BlockSpec((HALO, bc), lambda j, i: (prev(i), j)),
                  pl.BlockSpec((HALO, bc), lambda j, i: (nxt(i), j)),
                  cur_v, pl.BlockSpec((HALO, bc), lambda j, i: (prev(i), nc + j)),
                  pl.BlockSpec((HALO, bc), lambda j, i: (nxt(i), nc + j)),
                  cur_g, pl.BlockSpec((HALO, bc), lambda j, i: (nxt(i), j)),
                  pl.BlockSpec((K, bc), lambda j, i: (0, j)), pl.BlockSpec((K, bc), lambda j, i: (0, nc + j)),
                  pl.BlockSpec((1, bc), lambda j, i: (0, j)), pl.BlockSpec((1, bc), lambda j, i: (0, nc + j))],
        out_specs=(cur_g, cur_g, pl.BlockSpec((K, bc), lambda j, i: (0, j)), pl.BlockSpec((K, bc), lambda j, i: (0, j)),
                   pl.BlockSpec((1, bc), lambda j, i: (0, j)), pl.BlockSpec((1, bc), lambda j, i: (0, j))),
        out_shape=(jax.ShapeDtypeStruct((T, F), BF16), jax.ShapeDtypeStruct((T, F), BF16),
                   jax.ShapeDtypeStruct((K, F), F32), jax.ShapeDtypeStruct((K, F), F32),
                   jax.ShapeDtypeStruct((1, F), F32), jax.ShapeDtypeStruct((1, F), F32)),
        scratch_shapes=[pltpu.VMEM((bt + 2 * HALO, bc), F32)] * 2 + [pltpu.VMEM((n_ext, bc), F32)] * 2,
        compiler_params=_params(2),
    )(up, up, up, up, up, up, da, da, conv_w, conv_w, conv_b, conv_b)
    return outs


def _adamw(w, g, m, v, bt, name):
    R, C = w.shape
    bt = _fit_rows(R, bt)
    blk = pl.BlockSpec((bt, C), lambda i: (i, 0))

    def body(w_ref, g_ref, m_ref, v_ref, d_ref, m2_ref, v2_ref):
        gv = g_ref[...]
        m2 = ADAM_B1 * m_ref[...] + (1.0 - ADAM_B1) * gv
        v2 = ADAM_B2 * v_ref[...] + (1.0 - ADAM_B2) * (gv * gv)
        m_hat = m2 / (1.0 - ADAM_B1 ** ADAM_STEP)
        v_hat = v2 / (1.0 - ADAM_B2 ** ADAM_STEP)
        d_ref[...] = -ADAM_LR * (m_hat / (jnp.sqrt(v_hat) + ADAM_EPS) + ADAM_WD * w_ref[...])
        m2_ref[...] = m2
        v2_ref[...] = v2

    out = jax.ShapeDtypeStruct((R, C), F32)
    return pl.pallas_call(body, name=name, grid=(R // bt,), in_specs=[blk] * 4, out_specs=(blk,) * 3,
                          out_shape=(out,) * 3, compiler_params=_params(1))(w, g, m, v)


def _place():
    x, y, c = lax.axis_index("x"), lax.axis_index("y"), lax.axis_index("c")
    chips = [(1 - x, y), (x, 1 - y), (1 - x, 1 - y)]
    return x, y, c, chips


_HBM = pl.BlockSpec(memory_space=pltpu.HBM)


def _gather_xy(bufs, name):
    n = len(bufs)
    halves = [b.shape[0] // 2 for b in bufs]

    def body(*refs):
        ins, outs = refs[:n], refs[n:2 * n]
        send, recv, local = refs[2 * n:]
        x, y, c, chips = _place()
        me = 2 * x + y
        copies, forwards = [], []
        for b in range(n):
            h = halves[b]
            mine = pl.ds(c * h, h)
            own = pltpu.make_async_copy(ins[b], outs[b].at[me], local.at[b])
            own.start()
            copies.append(own)
            for j, (px, py) in enumerate(chips):
                cp = pltpu.make_async_remote_copy(
                    src_ref=ins[b].at[mine], dst_ref=outs[b].at[me, mine], send_sem=send.at[b, j],
                    recv_sem=recv.at[b, j], device_id=(px, py, c), device_id_type=MESH)
                cp.start()
                copies.append(cp)
        for b in range(n):
            h = halves[b]
            mine = pl.ds(c * h, h)
            for j, (px, py) in enumerate(chips):
                src = 2 * px + py
                landed = pltpu.make_async_remote_copy(
                    src_ref=ins[b].at[mine], dst_ref=outs[b].at[src, mine], send_sem=send.at[b, j],
                    recv_sem=recv.at[b, j], device_id=(px, py, c), device_id_type=MESH)
                landed.wait_recv()
                fw = pltpu.make_async_remote_copy(
                    src_ref=outs[b].at[src, mine], dst_ref=outs[b].at[src, mine], send_sem=send.at[b, 3 + j],
                    recv_sem=recv.at[b, 3 + j], device_id=(x, y, 1 - c), device_id_type=MESH)
                fw.start()
                forwards.append(fw)
        for b in range(n):
            h = halves[b]
            theirs = pl.ds((1 - c) * h, h)
            for j, (px, py) in enumerate(chips):
                src = 2 * px + py
                pltpu.make_async_remote_copy(
                    src_ref=outs[b].at[src, theirs], dst_ref=outs[b].at[src, theirs], send_sem=send.at[b, 3 + j],
                    recv_sem=recv.at[b, 3 + j], device_id=(x, y, 1 - c), device_id_type=MESH).wait_recv()
        for b in range(n):
            copies[b * 4].wait()
            for j in range(3):
                copies[b * 4 + 1 + j].wait_send()
        for fw in forwards:
            fw.wait_send()

    return pl.pallas_call(
        body, name=name, in_specs=[_HBM] * n, out_specs=[_HBM] * n,
        out_shape=[jax.ShapeDtypeStruct((4,) + b.shape, b.dtype) for b in bufs],
        scratch_shapes=[pltpu.SemaphoreType.DMA((n, 6)), pltpu.SemaphoreType.DMA((n, 6)), pltpu.SemaphoreType.DMA((n,))],
        compiler_params=pltpu.CompilerParams(has_side_effects=True),
    )(*bufs)


def _swap_half_c(buf, name):
    n, h = buf.shape[0], buf.shape[1] // 2

    def body(in_ref, out_ref, send, recv):
        x, y, c, _ = _place()
        cp = pltpu.make_async_remote_copy(
            src_ref=in_ref.at[:, pl.ds((1 - c) * h, h)], dst_ref=out_ref, send_sem=send, recv_sem=recv,
            device_id=(x, y, 1 - c), device_id_type=MESH)
        cp.start()
        cp.wait()

    return pl.pallas_call(
        body, name=name, in_specs=[_HBM], out_specs=_HBM,
        out_shape=jax.ShapeDtypeStruct((n, h, buf.shape[2]), buf.dtype),
        scratch_shapes=[pltpu.SemaphoreType.DMA, pltpu.SemaphoreType.DMA],
        compiler_params=pltpu.CompilerParams(has_side_effects=True),
    )(buf)


def _scatter_xy(buf, name):
    h = buf.shape[1]

    def body(in_ref, out_ref, send, recv):
        x, y, c, chips = _place()
        cps = []
        for j, (px, py) in enumerate(chips):
            cp = pltpu.make_async_remote_copy(
                src_ref=in_ref.at[2 * px + py], dst_ref=out_ref.at[j], send_sem=send.at[j], recv_sem=recv.at[j],
                device_id=(px, py, c), device_id_type=MESH)
            cp.start()
            cps.append(cp)
        for cp in cps:
            cp.wait()

    return pl.pallas_call(
        body, name=name, in_specs=[_HBM], out_specs=_HBM,
        out_shape=jax.ShapeDtypeStruct((3, h, buf.shape[2]), buf.dtype),
        scratch_shapes=[pltpu.SemaphoreType.DMA((3,)), pltpu.SemaphoreType.DMA((3,))],
        compiler_params=pltpu.CompilerParams(has_side_effects=True),
    )(buf)


def _join_c(half, name):
    h = half.shape[0]

    def body(in_ref, out_ref, send, recv, local):
        x, y, c, _ = _place()
        own = pltpu.make_async_copy(in_ref, out_ref.at[c], local)
        own.start()
        cp = pltpu.make_async_remote_copy(
            src_ref=in_ref, dst_ref=out_ref.at[c], send_sem=send, recv_sem=recv,
            device_id=(x, y, 1 - c), device_id_type=MESH)
        cp.start()
        cp.wait()
        own.wait()

    return pl.pallas_call(
        body, name=name, in_specs=[_HBM], out_specs=_HBM,
        out_shape=jax.ShapeDtypeStruct((2, h, half.shape[1]), half.dtype),
        scratch_shapes=[pltpu.SemaphoreType.DMA, pltpu.SemaphoreType.DMA, pltpu.SemaphoreType.DMA],
        compiler_params=pltpu.CompilerParams(has_side_effects=True),
    )(half)


def _add_half(buf, other, c, bt, name):
    n, _, h, lanes = buf.shape
    bt = _fit_rows(h, bt)

    def body(c_ref, a_ref, b_ref, o_ref):
        o_ref[...] = a_ref[0] + b_ref[...]

    return pl.pallas_call(
        body, name=name,
        grid_spec=pltpu.PrefetchScalarGridSpec(
            num_scalar_prefetch=1, grid=(n, h // bt),
            in_specs=[pl.BlockSpec((1, 1, bt, lanes), lambda k, i, c_ref: (k, c_ref[0], i, 0)),
                      pl.BlockSpec((1, bt, lanes), lambda k, i, c_ref: (k, i, 0))],
            out_specs=pl.BlockSpec((1, bt, lanes), lambda k, i, c_ref: (k, i, 0))),
        out_shape=jax.ShapeDtypeStruct((n, h, lanes), buf.dtype), compiler_params=_params(2),
    )(c, buf, other)


def _add_four(own, me, others, bt, name):
    _, h, lanes = own.shape
    bt = _fit_rows(h, bt)

    def body(me_ref, a_ref, b_ref, o_ref):
        o_ref[...] = ((a_ref[0] + b_ref[0]) + b_ref[1]) + b_ref[2]

    return pl.pallas_call(
        body, name=name,
        grid_spec=pltpu.PrefetchScalarGridSpec(
            num_scalar_prefetch=1, grid=(h // bt,),
            in_specs=[pl.BlockSpec((1, bt, lanes), lambda i, me_ref: (me_ref[0], i, 0)),
                      pl.BlockSpec((3, bt, lanes), lambda i, me_ref: (0, i, 0))],
            out_specs=pl.BlockSpec((bt, lanes), lambda i, me_ref: (i, 0))),
        out_shape=jax.ShapeDtypeStruct((h, lanes), own.dtype), compiler_params=_params(1),
    )(me, own, others)


def _gather_chips(bufs, split, name):
    n = len(bufs)

    def body(*refs):
        ins, outs = refs[:n], refs[n:2 * n]
        send, recv, local = refs[2 * n:]
        x, y, c, chips = _place()
        me = 2 * x + y

        def rows(b, core):
            h = bufs[b].shape[0] // 2
            return pl.ds(core * h, h) if split[b] else pl.ds(0, bufs[b].shape[0])

        def over_ici(b, j, block):
            px, py = chips[j]
            return pltpu.make_async_remote_copy(
                src_ref=ins[b].at[rows(b, c)], dst_ref=outs[b].at[block, rows(b, c)], send_sem=send.at[b, j],
                recv_sem=recv.at[b, j], device_id=(px, py, c), device_id_type=MESH)

        def over_d2d(b, j, block, core):
            return pltpu.make_async_remote_copy(
                src_ref=outs[b].at[block, rows(b, core)], dst_ref=outs[b].at[block, rows(b, core)],
                send_sem=send.at[b, 3 + j], recv_sem=recv.at[b, 3 + j], device_id=(x, y, 1 - c), device_id_type=MESH)

        started = []
        for b in range(n):
            own = pltpu.make_async_copy(ins[b], outs[b].at[me], local.at[b])
            own.start()
            started.append(own.wait)
            for j in range(3):
                cp = over_ici(b, j, me)
                cp.start()
                started.append(cp.wait_send)
        for b in range(n):
            for j, (px, py) in enumerate(chips):
                over_ici(b, j, 2 * px + py).wait_recv()
                if split[b]:
                    fw = over_d2d(b, j, 2 * px + py, c)
                    fw.start()
                    started.append(fw.wait_send)
        for b in range(n):
            if split[b]:
                for j, (px, py) in enumerate(chips):
                    over_d2d(b, j, 2 * px + py, 1 - c).wait_recv()
        for wait in started:
            wait()

    return pl.pallas_call(
        body, name=name, in_specs=[_HBM] * n, out_specs=[_HBM] * n,
        out_shape=[jax.ShapeDtypeStruct((4,) + b.shape, b.dtype) for b in bufs],
        scratch_shapes=[pltpu.SemaphoreType.DMA((n, 6)), pltpu.SemaphoreType.DMA((n, 6)), pltpu.SemaphoreType.DMA((n,))],
        compiler_params=pltpu.CompilerParams(has_side_effects=True),
    )(*bufs)


def _swap_halves(bufs, name):
    n = len(bufs)

    def body(*refs):
        ins, outs = refs[:n], refs[n:2 * n]
        send, recv = refs[2 * n:]
        x, y, c, _ = _place()
        cps = []
        for b in range(n):
            h = bufs[b].shape[1] // 2
            cp = pltpu.make_async_remote_copy(
                src_ref=ins[b].at[:, pl.ds((1 - c) * h, h)], dst_ref=outs[b], send_sem=send.at[b], recv_sem=recv.at[b],
                device_id=(x, y, 1 - c), device_id_type=MESH)
            cp.start()
            cps.append(cp)
        for cp in cps:
            cp.wait()

    return pl.pallas_call(
        body, name=name, in_specs=[_HBM] * n, out_specs=[_HBM] * n,
        out_shape=[jax.ShapeDtypeStruct((b.shape[0], b.shape[1] // 2, b.shape[2]), b.dtype) for b in bufs],
        scratch_shapes=[pltpu.SemaphoreType.DMA((n,)), pltpu.SemaphoreType.DMA((n,))],
        compiler_params=pltpu.CompilerParams(has_side_effects=True),
    )(*bufs)


def _scatter_chips(bufs, name):
    n = len(bufs)

    def body(*refs):
        ins, outs = refs[:n], refs[n:2 * n]
        send, recv = refs[2 * n:]
        x, y, c, chips = _place()
        cps = []
        for b in range(n):
            for j, (px, py) in enumerate(chips):
                cp = pltpu.make_async_remote_copy(
                    src_ref=ins[b].at[2 * px + py], dst_ref=outs[b].at[j], send_sem=send.at[b, j],
                    recv_sem=recv.at[b, j], device_id=(px, py, c), device_id_type=MESH)
                cp.start()
                cps.append(cp)
        for cp in cps:
            cp.wait()

    return pl.pallas_call(
        body, name=name, in_specs=[_HBM] * n, out_specs=[_HBM] * n,
        out_shape=[jax.ShapeDtypeStruct((3,) + b.shape[1:], b.dtype) for b in bufs],
        scratch_shapes=[pltpu.SemaphoreType.DMA((n, 3)), pltpu.SemaphoreType.DMA((n, 3))],
        compiler_params=pltpu.CompilerParams(has_side_effects=True),
    )(*bufs)


def _join_halves(halves, name):
    n = len(halves)

    def body(*refs):
        ins, outs = refs[:n], refs[n:2 * n]
        send, recv, local = refs[2 * n:]
        x, y, c, _ = _place()
        waits = []
        for b in range(n):
            own = pltpu.make_async_copy(ins[b], outs[b].at[c], local.at[b])
            own.start()
            cp = pltpu.make_async_remote_copy(
                src_ref=ins[b], dst_ref=outs[b].at[c], send_sem=send.at[b], recv_sem=recv.at[b],
                device_id=(x, y, 1 - c), device_id_type=MESH)
            cp.start()
            waits += [cp.wait, own.wait]
        for wait in waits:
            wait()

    return pl.pallas_call(
        body, name=name, in_specs=[_HBM] * n, out_specs=[_HBM] * n,
        out_shape=[jax.ShapeDtypeStruct((2,) + b.shape, b.dtype) for b in halves],
        scratch_shapes=[pltpu.SemaphoreType.DMA((n,)), pltpu.SemaphoreType.DMA((n,)), pltpu.SemaphoreType.DMA((n,))],
        compiler_params=pltpu.CompilerParams(has_side_effects=True),
    )(*halves)


def _add_cores(buf, other, place, own_only, out_dtype, bt, name):
    n, _, h, cols = buf.shape
    bt = _fit_rows(h, bt)
    row = (lambda k, s: s[1]) if own_only else (lambda k, s: k)

    def body(s_ref, a_ref, b_ref, o_ref):
        o_ref[...] = (a_ref[...] + b_ref[...]).astype(out_dtype)

    return pl.pallas_call(
        body, name=name,
        grid_spec=pltpu.PrefetchScalarGridSpec(
            num_scalar_prefetch=1, grid=(1 if own_only else n, h // bt),
            in_specs=[pl.BlockSpec((None, None, bt, cols), lambda k, i, s: (row(k, s), s[0], i, 0)),
                      pl.BlockSpec((None, bt, cols), lambda k, i, s: (row(k, s), i, 0))],
            out_specs=(pl.BlockSpec((bt, cols), lambda k, i, s: (i, 0)) if own_only
                       else pl.BlockSpec((None, bt, cols), lambda k, i, s: (k, i, 0)))),
        out_shape=jax.ShapeDtypeStruct((h, cols) if own_only else (n, h, cols), out_dtype),
        compiler_params=_params(2),
    )(place, buf, other)


def _add_chips(own, others, bt, name):
    h, cols = own.shape
    bt = _fit_rows(h, bt)

    def body(a_ref, b_ref, o_ref):
        o_ref[...] = ((a_ref[...] + b_ref[0].astype(F32)) + b_ref[1].astype(F32)) + b_ref[2].astype(F32)

    return pl.pallas_call(
        body, name=name, grid=(h // bt,),
        in_specs=[pl.BlockSpec((bt, cols), lambda i: (i, 0)), pl.BlockSpec((3, bt, cols), lambda i: (0, i, 0))],
        out_specs=pl.BlockSpec((bt, cols), lambda i: (i, 0)),
        out_shape=jax.ShapeDtypeStruct((h, cols), F32), compiler_params=_params(1),
    )(own, others)


def _fit_rows(n, target):
    for q in (2 * HALO, HALO):
        for t in range(min(n, target) // q * q, 0, -q):
            if n % t == 0:
                return t
    best = None
    for t in range(HALO, min(n, target) + 1, HALO):
        if n % t == 0:
            best = t
    assert best is not None, (n, target)
    return best


def _allreduce_small(buf, name):
    R, lanes = buf.shape

    def body(in_ref, out_ref, land, send, recv):
        x, y, c, _ = _place()
        me = 4 * x + 2 * y + c
        land[me] = in_ref[...]
        cps = []
        for r in range(1, 8):
            px, py, pc = x ^ (r >> 2), y ^ ((r >> 1) & 1), c ^ (r & 1)
            cp = pltpu.make_async_remote_copy(
                src_ref=in_ref, dst_ref=land.at[me], send_sem=send.at[r - 1], recv_sem=recv.at[me],
                device_id=(px, py, pc), device_id_type=MESH)
            cp.start()
            cps.append(cp)
        for r in range(1, 8):
            peer = 4 * (x ^ (r >> 2)) + 2 * (y ^ ((r >> 1) & 1)) + (c ^ (r & 1))
            pltpu.make_async_remote_copy(
                src_ref=in_ref, dst_ref=land.at[peer], send_sem=send.at[r - 1], recv_sem=recv.at[peer],
                device_id=(x, y, c), device_id_type=MESH).wait_recv()
        for cp in cps:
            cp.wait_send()
        acc = land[0]
        for d in range(1, 8):
            acc = acc + land[d]
        out_ref[...] = acc

    vm = pl.BlockSpec(memory_space=pltpu.VMEM)
    return pl.pallas_call(
        body, name=name, in_specs=[vm], out_specs=vm, out_shape=jax.ShapeDtypeStruct((R, lanes), buf.dtype),
        scratch_shapes=[pltpu.VMEM((8, R, lanes), buf.dtype), pltpu.SemaphoreType.DMA((7,)), pltpu.SemaphoreType.DMA((8,))],
        compiler_params=pltpu.CompilerParams(has_side_effects=True, vmem_limit_bytes=VMEM_LIMIT),
    )(buf)


ROW_BLOCK = 256
SB_BLOCK = 256
MM_TM, MM_TN, MM_TK = 1024, 512, 512
FFN_COLS = 512


def _lane_pad(vec, start):
    return jnp.pad(vec, ((0, 0), (start, LANES - start - vec.shape[1])))


def _local_step(x, target, wt):
    T, D = x.shape
    W = D // 2
    H = W // HEAD_DIM
    F = wt["w_down"].shape[0]
    bt = min(ROW_BLOCK, T)
    blk = min(SB_BLOCK, T)
    w_in = wt["w_in"]
    w_sb, w_dn = w_in[:, :3 * W], w_in[:, 3 * W:7 * W]
    w_ba = jnp.pad(w_in[:, 7 * W:], ((0, 0), (0, LANES - 2 * H)))
    w_out, w_up, w_down = wt["w_out"], wt["w_up"], wt["w_down"]
    a_log, dt_bias = _lane_pad(wt["dn_a_log"], H), _lane_pad(wt["dn_dt_bias"], H)
    mm = functools.partial(_mm, tm=MM_TM, tn=MM_TN)

    xn = _rms_fwd(x, wt["ln_mix_pre"], None, BF16, bt, "rms_mix_pre")
    psb = mm([(xn, w_sb, D)], "nn", BF16, name="proj_sb")
    pdn = mm([(xn, w_dn, D)], "nn", F32, name="proj_dn")
    pba = mm([(xn, w_ba, D)], "nn", F32, name="proj_ba")
    o_sb, mix_sb, lt = _sb_fwd(psb, wt["sb_out_gain"], blk, "sb_fwd")
    qn, kn, vv, bx, gx = _dn_pre_fwd(pdn, pba, wt["dn_conv_w"], a_log, dt_bias, bt, "dn_pre_fwd")
    o_dn, mix_dn, ss = _dn_core_fwd(qn, kn, vv, bx, gx, pdn, wt["dn_out_gain"], "dn_core_fwd")
    m = mm([(mix_sb, w_out[:W], MM_TK), (mix_dn, w_out[W:], MM_TK)], "nn", F32, name="out_proj")
    h = _rms_fwd(m, wt["ln_mix_post"], x, F32, bt, "rms_mix_post")
    hn = _rms_fwd(h, wt["ln_ffn_pre"], None, BF16, bt, "rms_ffn_pre")
    up = mm([(hn, w_up, D)], "nn", F32, name="ffn_up")
    bc = min(FFN_COLS, F)
    act = _ffn_mid_fwd(up, wt["ffn_conv_w"], wt["ffn_conv_b"], bt, bc, "ffn_mid_fwd")
    f = mm([(act, w_down, MM_TK)], "nn", F32, name="ffn_down")
    dy, df, g_ffn_post, sq = _loss_head(f, wt["ln_ffn_post"], h, target, bt, "loss_head")
    loss = 0.5 * jnp.sum(sq) / D

    da = mm([(df, w_down, D)], "nt", F32, name="d_act")
    g_w_down = mm([(act, df, MM_TK)], "tn", F32, name="g_w_down")
    dug, duv, dwg, dwv, dbg, dbv = _ffn_mid_bwd(up, wt["ffn_conv_w"], wt["ffn_conv_b"], da, bt, bc, "ffn_mid_bwd")
    dhn = mm([(dug, w_up[:, :F], MM_TK), (duv, w_up[:, F:], MM_TK)], "nt", F32, name="d_hn")
    g_w_up = jnp.concatenate([mm([(hn, dug, MM_TK)], "tn", F32, name="g_w_up_gate"),
                              mm([(hn, duv, MM_TK)], "tn", F32, name="g_w_up_val")], axis=1)
    dh, g_ffn_pre = _rms_bwd(h, wt["ln_ffn_pre"], dhn, dy, F32, bt, "rms_ffn_pre_bwd")
    dm, g_mix_post = _rms_bwd(m, wt["ln_mix_post"], dh, None, BF16, bt, "rms_mix_post_bwd")
    dmix = mm([(dm, w_out, D)], "nt", F32, name="d_mix")
    g_w_out = jnp.concatenate([mm([(mix_sb, dm, MM_TK)], "tn", F32, name="g_w_out_sb"),
                               mm([(mix_dn, dm, MM_TK)], "tn", F32, name="g_w_out_dn")], axis=0)
    do_sb, g_sb_gain = _headnorm_bwd(o_sb, wt["sb_out_gain"], dmix, bt, "sb_norm_bwd")
    dq, dk, dv = _sb_bwd(psb, do_sb, lt, blk, "sb_bwd")
    ddq, ddk, ddv, dbx, dgx, dz, g_dn_gain = _dn_core_bwd(qn, kn, vv, bx, gx, pdn, wt["dn_out_gain"], o_dn, dmix, ss,
                                                         W, "dn_core_bwd")
    dconv, dba, g_dn_conv, g_a_log, g_dt_bias = _dn_pre_bwd(pdn, pba, wt["dn_conv_w"], a_log, dt_bias,
                                                            ddq, ddk, ddv, dbx, dgx, bt, "dn_pre_bwd")
    pieces = [(dq, w_in[:, :W]), (dk, w_in[:, W:2 * W]), (dv, w_in[:, 2 * W:3 * W]), (dconv, w_in[:, 3 * W:6 * W]),
              (dz, w_in[:, 6 * W:7 * W]), (dba, w_ba)]
    dxn = mm([(d, wp, MM_TK) for d, wp in pieces], "nt", F32, name="d_xn")
    g_w_in = jnp.concatenate([mm([(xn, d, MM_TK)], "tn", F32, name=f"g_w_in_{i}") for i, (d, _) in enumerate(pieces)],
                             axis=1)[:, :7 * W + 2 * H]
    dx, g_mix_pre = _rms_bwd(x, wt["ln_mix_pre"], dxn, dh, F32, bt, "rms_mix_pre_bwd")

    grads = dict(
        w_in=g_w_in, sb_out_gain=g_sb_gain, dn_conv_w=g_dn_conv, dn_a_log=g_a_log[:, H:2 * H],
        dn_dt_bias=g_dt_bias[:, H:2 * H], dn_out_gain=jnp.sum(g_dn_gain, axis=0), w_out=g_w_out,
        ln_mix_pre=g_mix_pre, ln_mix_post=g_mix_post, w_up=g_w_up,
        ffn_conv_w=jnp.concatenate([dwg, dwv], axis=1), ffn_conv_b=jnp.concatenate([dbg, dbv], axis=1),
        w_down=g_w_down, ln_ffn_pre=g_ffn_pre, ln_ffn_post=g_ffn_post)
    return loss, dx, grads


WEIGHTS = ("w_in", "sb_out_gain", "dn_conv_w", "dn_a_log", "dn_dt_bias", "dn_out_gain", "w_out", "ln_mix_pre",
           "ln_mix_post", "w_up", "ffn_conv_w", "ffn_conv_b", "w_down", "ln_ffn_pre", "ln_ffn_post")
MATRICES = {"w_in": 1, "w_out": 0, "w_up": 1, "w_down": 0}
CONV_SHARDED = ("dn_conv_w", "ffn_conv_w")
SMALL = tuple(n for n in WEIGHTS if n not in MATRICES)
N_CHIPS = 4
ROW_QUANTUM = 32
ADD_ROWS = 2048
ADAM_ROWS = 256


def _pack(arrs, quantum):
    rows, layout, off = [], [], 0
    for a in arrs:
        n = int(np.prod(a.shape))
        r = -(-n // LANES)
        r = -(-r // HALO) * HALO
        rows.append(jnp.pad(a.reshape(-1), (0, r * LANES - n)).reshape(r, LANES))
        layout.append((off, r, n, a.shape))
        off += r
    total = -(-off // quantum) * quantum
    if total > off:
        rows.append(jnp.zeros((total - off, LANES), rows[0].dtype))
    return jnp.concatenate(rows, axis=0), layout


def _unpack(packed, layout):
    return [packed[off:off + r].reshape(-1)[:n].reshape(shape) for off, r, n, shape in layout]


def _kernel_packed(x, w_in, sb_out_gain, dn_conv_w, dn_a_log, dn_dt_bias, dn_out_gain, w_out, ln_mix_pre, ln_mix_post, w_up, ffn_conv_w, ffn_conv_b, w_down, ln_ffn_pre, ln_ffn_post, loss_target, m_w_in, m_sb_out_gain, m_dn_conv_w, m_dn_a_log, m_dn_dt_bias, m_dn_out_gain, m_w_out, m_ln_mix_pre, m_ln_mix_post, m_w_up, m_ffn_conv_w, m_ffn_conv_b, m_w_down, m_ln_ffn_pre, m_ln_ffn_post, v_w_in, v_sb_out_gain, v_dn_conv_w, v_dn_a_log, v_dn_dt_bias, v_dn_out_gain, v_w_out, v_ln_mix_pre, v_ln_mix_post, v_w_up, v_ffn_conv_w, v_ffn_conv_b, v_w_down, v_ln_ffn_pre, v_ln_ffn_post):
    given = dict(locals())
    wl = {n: given[n][0] for n in WEIGHTS}
    ml = {n: given["m_" + n][0] for n in WEIGHTS}
    vl = {n: given["v_" + n][0] for n in WEIGHTS}
    for d in (wl, ml, vl):
        for n in SMALL:
            if d[n].ndim == 1:
                d[n] = d[n][None]
    cx, cy, cc = lax.axis_index("x"), lax.axis_index("y"), lax.axis_index("c")
    chip = 2 * cx + cy

    mats, mat_layout = _pack([wl[n].astype(BF16) for n in MATRICES], ROW_QUANTUM)
    taps, tap_layout = _pack([wl[n] for n in CONV_SHARDED], ROW_QUANTUM)
    all_mats, all_taps = _gather_xy([mats, taps], "gather_weights")
    wt = {n: wl[n] for n in SMALL}
    for i, n in enumerate(MATRICES):
        wt[n] = jnp.concatenate([_unpack(all_mats[k], mat_layout)[i] for k in range(N_CHIPS)], axis=MATRICES[n])
    for i, n in enumerate(CONV_SHARDED):
        wt[n] = jnp.concatenate([_unpack(all_taps[k], tap_layout)[i] for k in range(N_CHIPS)], axis=1)

    loss, dx, grads = _local_step(x[0], loss_target[0], wt)
    loss = lax.psum(loss, ("x", "y", "c"))

    def shard_of(n, k):
        g, axis = grads[n], MATRICES[n]
        size = g.shape[axis] // N_CHIPS
        return lax.slice_in_dim(g, k * size, (k + 1) * size, axis=axis)

    packed = [_pack([shard_of(n, k) for n in MATRICES], ROW_QUANTUM) for k in range(N_CHIPS)]
    glayout = packed[0][1]
    gp = jnp.stack([p[0] for p in packed])
    half = gp.shape[1] // 2
    from_sibling = _swap_half_c(gp, "grad_swap_cores")
    chip_sum = _add_half(gp.reshape(N_CHIPS, 2, half, LANES), from_sibling, cc.reshape(1), ADD_ROWS, "grad_add_cores")
    from_chips = _scatter_xy(chip_sum, "grad_scatter_chips")
    reduced_half = _add_four(chip_sum, chip.reshape(1), from_chips, ADD_ROWS, "grad_add_chips")
    reduced = _join_c(reduced_half, "grad_join_cores").reshape(2 * half, LANES)
    gl = dict(zip(MATRICES, _unpack(reduced, glayout)))

    small, small_layout = _pack([grads[n] for n in SMALL], HALO)
    small = _allreduce_small(small, "grad_allreduce_small")
    for n, g in zip(SMALL, _unpack(small, small_layout)):
        if n in CONV_SHARDED:
            size = g.shape[1] // N_CHIPS
            g = lax.dynamic_slice_in_dim(g, chip * size, size, axis=1)
        gl[n] = g

    delta, new_m, new_v = {}, {}, {}
    for n in MATRICES:
        delta[n], new_m[n], new_v[n] = _adamw(wl[n], gl[n], ml[n], vl[n], ADAM_ROWS, "adamw_" + n)
    packs = [_pack([d[n] for n in SMALL], HALO) for d in (wl, gl, ml, vl)]
    outs = _adamw(*[p[0] for p in packs], ADAM_ROWS, "adamw_small")
    for res, o in zip((delta, new_m, new_v), outs):
        res.update(zip(SMALL, _unpack(o, packs[0][1])))

    shaped = lambda d: [d[n].reshape(given[n].shape) for n in WEIGHTS]
    return (loss, dx[None], *shaped(gl), *shaped(delta), *shaped(new_m), *shaped(new_v))


UP_TILE = 1408
PAIR_ROWS = 256


def _step(x, target, wt):
    T, D = x.shape
    W = D // 2
    H = W // HEAD_DIM
    F = wt["w_down"].shape[0]
    bt = min(ROW_BLOCK, T)
    blk = min(SB_BLOCK, T)
    w_in, w_out, w_up, w_down = wt["w_in"], wt["w_out"], wt["w_up"], wt["w_down"]
    a_log, dt_bias = _lane_pad(wt["dn_a_log"], H), _lane_pad(wt["dn_dt_bias"], H)
    mm = functools.partial(_mm, tm=MM_TM, tn=MM_TN)
    mm_up = functools.partial(_mm, tm=MM_TM, tn=UP_TILE)
    one = lambda a, b, tk=MM_TK: [(a, b, tk, 0, 0)]

    xn = _rms_fwd(x, wt["ln_mix_pre"], None, BF16, bt, "rms_mix_pre")
    psb = mm(one(xn, w_in, D), "nn", BF16, name="proj_sb", n_window=(0, 3 * W))
    pdn = mm(one(xn, w_in, D), "nn", F32, name="proj_dn", n_window=(3 * W, 4 * W))
    pba = mm(one(xn, w_in, D), "nn", F32, name="proj_ba", n_window=(7 * W, LANES))
    o_sb, mix_sb, lt = _sb_fwd(psb, wt["sb_out_gain"], blk, "sb_fwd")
    qn, kn, vv, bx, gx = _dn_pre_fwd(pdn, pba, wt["dn_conv_w"], a_log, dt_bias, bt, "dn_pre_fwd")
    o_dn, mix_dn, ss = _dn_core_fwd(qn, kn, vv, bx, gx, pdn, wt["dn_out_gain"], "dn_core_fwd")
    m = mm([(mix_sb, w_out, MM_TK, 0, 0), (mix_dn, w_out, MM_TK, 0, W)], "nn", F32, name="out_proj")
    h = _rms_fwd(m, wt["ln_mix_post"], x, F32, bt, "rms_mix_post")
    hn = _rms_fwd(h, wt["ln_ffn_pre"], None, BF16, bt, "rms_ffn_pre")
    up = mm_up(one(hn, w_up, D), "nn", F32, name="ffn_up")
    bc = min(FFN_COLS, F)
    act = _ffn_mid_fwd(up, wt["ffn_conv_w"], wt["ffn_conv_b"], bt, bc, "ffn_mid_fwd")
    f = mm(one(act, w_down), "nn", F32, name="ffn_down")
    dy, df, g_ffn_post, sq = _loss_head(f, wt["ln_ffn_post"], h, target, bt, "loss_head")
    loss = 0.5 * jnp.sum(sq) / D

    da = mm(one(df, w_down, D), "nt", F32, name="d_act")
    g_w_down = mm(one(act, df), "tn", F32, name="g_w_down")
    dug, duv, dwg, dwv, dbg, dbv = _ffn_mid_bwd(up, wt["ffn_conv_w"], wt["ffn_conv_b"], da, bt, bc, "ffn_mid_bwd")
    dhn = mm([(dug, w_up, UP_TILE, 0, 0), (duv, w_up, UP_TILE, 0, F)], "nt", F32, name="d_hn")
    shard = w_up.shape[2]
    g_w_up = mm_up(one(hn, dug), "tn", F32, name="g_w_up_gate", out_shard=shard, into=(lax.empty(w_up.shape, F32), 0))
    g_w_up = mm_up(one(hn, duv), "tn", F32, name="g_w_up_val", out_shard=shard, into=(g_w_up, F))
    dh, g_ffn_pre = _rms_bwd(h, wt["ln_ffn_pre"], dhn, dy, F32, bt, "rms_ffn_pre_bwd")
    dm, g_mix_post = _rms_bwd(m, wt["ln_mix_post"], dh, None, BF16, bt, "rms_mix_post_bwd")
    dmix = mm(one(dm, w_out, D), "nt", F32, name="d_mix")
    g_w_out = jnp.concatenate([mm(one(mix_sb, dm), "tn", F32, name="g_w_out_sb"),
                               mm(one(mix_dn, dm), "tn", F32, name="g_w_out_dn")], axis=0)
    do_sb, g_sb_gain = _headnorm_bwd(o_sb, wt["sb_out_gain"], dmix, bt, "sb_norm_bwd")
    dq, dk, dv = _sb_bwd(psb, do_sb, lt, blk, "sb_bwd")
    ddq, ddk, ddv, dbx, dgx, dz, g_dn_gain = _dn_core_bwd(qn, kn, vv, bx, gx, pdn, wt["dn_out_gain"], o_dn, dmix, ss,
                                                         W, "dn_core_bwd")
    dconv, dba, g_dn_conv, g_a_log, g_dt_bias = _dn_pre_bwd(pdn, pba, wt["dn_conv_w"], a_log, dt_bias,
                                                            ddq, ddk, ddv, dbx, dgx, bt, "dn_pre_bwd")
    pieces = [(dq, 0), (dk, W), (dv, 2 * W), (dconv, 3 * W), (dz, 6 * W), (dba, 7 * W)]
    dxn = mm([(d, w_in, MM_TK, 0, k0) for d, k0 in pieces], "nt", F32, name="d_xn")
    g_w_in = [mm(one(xn, d), "tn", F32, name=f"g_w_in_{i}") for i, (d, _) in enumerate(pieces)]
    g_w_in[-1] = g_w_in[-1][:, :2 * H]
    dx, g_mix_pre = _rms_bwd(x, wt["ln_mix_pre"], dxn, dh, F32, bt, "rms_mix_pre_bwd")

    grads = dict(
        w_in=g_w_in, sb_out_gain=g_sb_gain, dn_conv_w=g_dn_conv, dn_a_log=g_a_log[:, H:2 * H],
        dn_dt_bias=g_dt_bias[:, H:2 * H], dn_out_gain=jnp.sum(g_dn_gain, axis=0), w_out=g_w_out,
        ln_mix_pre=g_mix_pre, ln_mix_post=g_mix_post, w_up=g_w_up,
        ffn_conv_w=jnp.concatenate([dwg, dwv], axis=1), ffn_conv_b=jnp.concatenate([dbg, dbv], axis=1),
        w_down=g_w_down, ln_ffn_pre=g_ffn_pre, ln_ffn_post=g_ffn_post)
    return loss, dx, grads


def kernel(x, w_in, sb_out_gain, dn_conv_w, dn_a_log, dn_dt_bias, dn_out_gain, w_out, ln_mix_pre, ln_mix_post, w_up, ffn_conv_w, ffn_conv_b, w_down, ln_ffn_pre, ln_ffn_post, loss_target, m_w_in, m_sb_out_gain, m_dn_conv_w, m_dn_a_log, m_dn_dt_bias, m_dn_out_gain, m_w_out, m_ln_mix_pre, m_ln_mix_post, m_w_up, m_ffn_conv_w, m_ffn_conv_b, m_w_down, m_ln_ffn_pre, m_ln_ffn_post, v_w_in, v_sb_out_gain, v_dn_conv_w, v_dn_a_log, v_dn_dt_bias, v_dn_out_gain, v_w_out, v_ln_mix_pre, v_ln_mix_post, v_w_up, v_ffn_conv_w, v_ffn_conv_b, v_w_down, v_ln_ffn_pre, v_ln_ffn_post):
    given = dict(locals())
    wl = {n: given[n][0] for n in WEIGHTS}
    ml = {n: given["m_" + n][0] for n in WEIGHTS}
    vl = {n: given["v_" + n][0] for n in WEIGHTS}
    for d in (wl, ml, vl):
        for n in SMALL:
            if d[n].ndim == 1:
                d[n] = d[n][None]
    cx, cy, cc = lax.axis_index("x"), lax.axis_index("y"), lax.axis_index("c")
    chip = 2 * cx + cy
    D = x.shape[2]
    W = D // 2

    order = list(MATRICES) + list(CONV_SHARDED)
    got = dict(zip(order, _gather_chips([wl[n].astype(BF16) for n in MATRICES] + [wl[n] for n in CONV_SHARDED],
                                        [True] * len(MATRICES) + [False] * len(CONV_SHARDED), "gather_weights")))
    columns = lambda g: g.transpose(1, 0, 2).reshape(g.shape[1], N_CHIPS * g.shape[2])
    wt = {n: wl[n] for n in SMALL}
    w_in_all = columns(got["w_in"])
    wt["w_in"] = jnp.pad(w_in_all, ((0, 0), (0, 7 * W + LANES - w_in_all.shape[1])))
    wt["w_out"] = got["w_out"].reshape(-1, D)
    wt["w_up"] = got["w_up"]
    wt["w_down"] = got["w_down"].reshape(-1, D)
    for n in CONV_SHARDED:
        wt[n] = columns(got[n])

    loss, dx, grads = _step(x[0], loss_target[0], wt)
    loss = lax.psum(loss, ("x", "y", "c"))

    g_in = jnp.concatenate(grads["w_in"], axis=1)
    shares = [g_in.reshape(D, N_CHIPS, -1).transpose(1, 0, 2), grads["w_out"].reshape(N_CHIPS, -1, D), grads["w_up"],
              grads["w_down"].reshape(N_CHIPS, -1, D)]
    place = jnp.stack([cc, chip]).astype(jnp.int32)
    from_sibling = _swap_halves(shares, "grad_swap_cores")
    halves = [s.reshape(N_CHIPS, 2, s.shape[1] // 2, s.shape[2]) for s in shares]
    names = list(MATRICES)
    to_chips = [_add_cores(hv, fs, place, False, BF16, PAIR_ROWS, "grad_add_cores_" + n)
                for hv, fs, n in zip(halves, from_sibling, names)]
    own = [_add_cores(hv, fs, place, True, F32, PAIR_ROWS, "grad_add_cores_own_" + n)
           for hv, fs, n in zip(halves, from_sibling, names)]
    from_chips = _scatter_chips(to_chips, "grad_scatter_chips")
    reduced = [_add_chips(o, fc, PAIR_ROWS, "grad_add_chips_" + n) for o, fc, n in zip(own, from_chips, names)]
    joined = _join_halves(reduced, "grad_join_cores")
    gl = {n: j.reshape(wl[n].shape) for n, j in zip(names, joined)}

    small, small_layout = _pack([grads[n] for n in SMALL], HALO)
    small = _allreduce_small(small, "grad_allreduce_small")
    for n, g in zip(SMALL, _unpack(small, small_layout)):
        if n in CONV_SHARDED:
            size = g.shape[1] // N_CHIPS
            g = lax.dynamic_slice_in_dim(g, chip * size, size, axis=1)
        gl[n] = g

    delta, new_m, new_v = {}, {}, {}
    for n in MATRICES:
        delta[n], new_m[n], new_v[n] = _adamw(wl[n], gl[n], ml[n], vl[n], ADAM_ROWS, "adamw_" + n)
    packs = [_pack([d[n] for n in SMALL], HALO) for d in (wl, gl, ml, vl)]
    outs = _adamw(*[p[0] for p in packs], ADAM_ROWS, "adamw_small")
    for res, o in zip((delta, new_m, new_v), outs):
        res.update(zip(SMALL, _unpack(o, packs[0][1])))

    shaped = lambda d: [d[n].reshape(given[n].shape) for n in WEIGHTS]
    return (loss, dx[None], *shaped(gl), *shaped(delta), *shaped(new_m), *shaped(new_v))
```

```python
import functools

import numpy as np
import jax
import jax.numpy as jnp
from jax import lax
from jax.experimental import pallas as pl
from jax.experimental.pallas import tpu as pltpu

F32 = jnp.float32
BF16 = jnp.bfloat16
HEAD_DIM = 128
CHUNK = 64
ROWS = 2 * CHUNK
EPS = 1e-6
LANES = 128
HALO = 8
VMEM_LIMIT = 48 * 1024 * 1024
ADAM_LR, ADAM_B1, ADAM_B2, ADAM_EPS, ADAM_WD, ADAM_STEP = 0.001, 0.9, 0.999, 1e-08, 0.01, 10
MESH = pl.DeviceIdType.MESH
HIGHEST = lax.Precision.HIGHEST

NN = (((1,), (0,)), ((), ()))
NT = (((1,), (1,)), ((), ()))
TN = (((0,), (0,)), ((), ()))


def _params(n_axes):
    return pltpu.CompilerParams(dimension_semantics=("arbitrary",) * n_axes, vmem_limit_bytes=VMEM_LIMIT)


def _bdot(a, b, dims=NN):
    return lax.dot_general(a.astype(BF16), b.astype(BF16), dims, preferred_element_type=F32)


def _split3(a):
    hi = a.astype(BF16)
    r1 = a - hi.astype(F32)
    mid = r1.astype(BF16)
    lo = (r1 - mid.astype(F32)).astype(BF16)
    return hi, mid, lo


def _dot3(a, sel, dims=NN):
    return sum(lax.dot_general(p, sel, dims, preferred_element_type=F32) for p in _split3(a))


def _dot3r(sel, a, dims=NN):
    return sum(lax.dot_general(sel, p, dims, preferred_element_type=F32) for p in _split3(a))


def _iota2(n, m):
    return lax.broadcasted_iota(jnp.int32, (n, m), 0), lax.broadcasted_iota(jnp.int32, (n, m), 1)


def _sigmoid(x):
    return 1.0 / (1.0 + jnp.exp(-x))


def _softplus(x):
    return jnp.maximum(x, 0.0) + jnp.log(1.0 + jnp.exp(-jnp.abs(x)))


def _fit(values, target):
    values = [v for v in (values if isinstance(values, (list, tuple)) else [values]) if v]
    best = None
    for t in range(LANES, min(min(values), target) + 1, LANES):
        if all(v % t == 0 for v in values):
            best = t
    assert best is not None, (values, target)
    return best


def _mm(parts, mode, out_dtype, tm, tn, name, n_window=None, out_shard=None, into=None, comm=None):
    dims = {"nn": NN, "nt": NT, "tn": TN}[mode]
    a0, b0 = parts[0][0], parts[0][1]
    b3 = b0.ndim == 3
    shard_c = b0.shape[2] if b3 else None
    M = a0.shape[1] if mode == "tn" else a0.shape[0]
    if mode == "nt":
        n_full = b0.shape[1] if b3 else b0.shape[0]
    else:
        n_full = b0.shape[0] * b0.shape[2] if b3 else b0.shape[1]
    n0, N = n_window if n_window is not None else (0, n_full)
    out_n0 = into[1] if into is not None else 0
    tm = _fit(M, tm)
    tn = _fit([N, n0, out_n0, out_shard, shard_c if mode != "nt" else None], tn)
    specs_a, specs_b, offs, nks = [], [], [], []
    off = 0
    for a, b, tk, a_k0, b_k0 in parts:
        K = a.shape[0] if mode == "tn" else a.shape[1]
        tk = _fit([K, a_k0, b_k0, shard_c if mode == "nt" else None], tk)
        nk = K // tk
        kk = lambda k, o=off, n=nk: jnp.clip(k - o, 0, n - 1)
        ao, bo, no = a_k0 // tk, b_k0 // tk, n0 // tn
        if mode == "tn":
            specs_a.append(pl.BlockSpec((tk, tm), lambda i, j, k, kk=kk, ao=ao: (kk(k) + ao, i)))
        else:
            specs_a.append(pl.BlockSpec((tm, tk), lambda i, j, k, kk=kk, ao=ao: (i, kk(k) + ao)))
        if mode == "nt":
            if b3:
                per = shard_c // tk
                specs_b.append(pl.BlockSpec((None, tn, tk), lambda i, j, k, kk=kk, bo=bo, per=per:
                                            ((kk(k) + bo) // per, j, (kk(k) + bo) % per)))
            else:
                specs_b.append(pl.BlockSpec((tn, tk), lambda i, j, k, kk=kk, bo=bo: (j, kk(k) + bo)))
        else:
            if b3:
                per = shard_c // tn
                specs_b.append(pl.BlockSpec((None, tk, tn), lambda i, j, k, kk=kk, bo=bo, no=no, per=per:
                                            ((j + no) // per, kk(k) + bo, (j + no) % per)))
            else:
                specs_b.append(pl.BlockSpec((tk, tn), lambda i, j, k, kk=kk, bo=bo, no=no: (kk(k) + bo, j + no)))
        offs.append(off)
        nks.append(nk)
        off += nk
    nk_total = off
    n_parts = len(parts)

    comm = comm if comm is not None else _Comm()
    grid = (M // tm, N // tn, nk_total)
    n_in = 2 * n_parts + (1 if into is not None else 0)

    def body(*refs):
        ins, (o_ref,), scratch, (first, mid, last) = comm.split(refs, n_in, 1, 0 if nk_total == 1 else 1)
        a_refs, b_refs = ins[:n_parts], ins[n_parts:2 * n_parts]
        at = lambda step: functools.reduce(lambda x, y: x & y, [pl.program_id(d) == step[d] for d in range(3)])
        pl.when(at((0, 0, 0)))(first)
        pl.when(at((grid[0] // 2, 0, 0)))(mid)
        if nk_total == 1:
            o_ref[...] = _bdot(a_refs[0][...], b_refs[0][...], dims).astype(out_dtype)
        else:
            acc = scratch[0]
            k = pl.program_id(2)

            @pl.when(k == 0)
            def _():
                acc[...] = jnp.zeros_like(acc)

            for p in range(n_parts):
                @pl.when((k >= offs[p]) & (k < offs[p] + nks[p]))
                def _(p=p):
                    acc[...] += _bdot(a_refs[p][...], b_refs[p][...], dims)

            @pl.when(k == nk_total - 1)
            def _():
                o_ref[...] = acc[...].astype(out_dtype)
        pl.when(at(tuple(g - 1 for g in grid)))(last)

    jo = out_n0 // tn
    if out_shard is not None:
        per_o = out_shard // tn
        out_spec = pl.BlockSpec((None, tm, tn), lambda i, j, k: ((j + jo) // per_o, i, (j + jo) % per_o))
        out_shape = jax.ShapeDtypeStruct((N // out_shard, M, out_shard), out_dtype)
    else:
        out_spec = pl.BlockSpec((tm, tn), lambda i, j, k: (i, j + jo))
        out_shape = jax.ShapeDtypeStruct((M, N), out_dtype)
    ins = [p[0] for p in parts] + [p[1] for p in parts]
    in_specs = specs_a + specs_b
    aliases = {}
    if into is not None:
        out_shape = jax.ShapeDtypeStruct(into[0].shape, into[0].dtype)
        aliases = {len(ins): 0}
        ins.append(into[0])
        in_specs.append(pl.BlockSpec(memory_space=pl.ANY))
    (out,), carried = comm.call(body, name, grid, in_specs, (out_spec,), (out_shape,),
                                [] if nk_total == 1 else [pltpu.VMEM((tm, tn), F32)], ins, aliases)
    return (out, carried) if comm.phases is not None else out


def _rms_fwd(x, gain, resid, out_dtype, bt, name):
    T, D = x.shape
    row = pl.BlockSpec((bt, D), lambda i: (i, 0))
    vec = pl.BlockSpec((1, D), lambda i: (0, 0))

    def body(*refs):
        x_ref, g_ref = refs[0], refs[1]
        o_ref = refs[-1]
        xv = x_ref[...]
        y = xv * lax.rsqrt(jnp.mean(xv * xv, axis=-1, keepdims=True) + EPS) * g_ref[...]
        if resid is not None:
            y = refs[2][...] + y
        o_ref[...] = y.astype(out_dtype)

    ins = [x, gain] + ([resid] if resid is not None else [])
    return pl.pallas_call(
        body, name=name, grid=(T // bt,),
        in_specs=[row, vec] + ([row] if resid is not None else []),
        out_specs=row, out_shape=jax.ShapeDtypeStruct((T, D), out_dtype), compiler_params=_params(1),
    )(*ins)


def _rms_bwd_math(xv, g, dy):
    r = lax.rsqrt(jnp.mean(xv * xv, axis=-1, keepdims=True) + EPS)
    n = xv * r
    gy = dy * g
    dx = r * (gy - n * jnp.mean(gy * n, axis=-1, keepdims=True))
    return dx, dy * n


def _rms_bwd(x, gain, dy, resid, out_dtype, bt, name):
    T, D = x.shape
    row = pl.BlockSpec((bt, D), lambda i: (i, 0))
    vec = pl.BlockSpec((1, D), lambda i: (0, 0))

    def body(*refs):
        x_ref, g_ref, dy_ref = refs[0], refs[1], refs[2]
        dx_ref, dg_ref = refs[-2], refs[-1]
        dx, dgp = _rms_bwd_math(x_ref[...], g_ref[...], dy_ref[...].astype(F32))
        if resid is not None:
            dx = refs[3][...] + dx
        dx_ref[...] = dx.astype(out_dtype)

        @pl.when(pl.program_id(0) == 0)
        def _():
            dg_ref[...] = jnp.zeros_like(dg_ref)

        dg_ref[...] += jnp.sum(dgp, axis=0, keepdims=True)

    ins = [x, gain, dy] + ([resid] if resid is not None else [])
    return pl.pallas_call(
        body, name=name, grid=(T // bt,),
        in_specs=[row, vec, row] + ([row] if resid is not None else []),
        out_specs=(row, vec),
        out_shape=(jax.ShapeDtypeStruct((T, D), out_dtype), jax.ShapeDtypeStruct((1, D), F32)),
        compiler_params=_params(1),
    )(*ins)


def _loss_head(f, gain, h, target, bt, name):
    T, D = f.shape
    row = pl.BlockSpec((bt, D), lambda i: (i, 0))
    vec = pl.BlockSpec((1, D), lambda i: (0, 0))

    def body(f_ref, g_ref, h_ref, t_ref, dy_ref, df_ref, dg_ref, sq_ref):
        fv, g = f_ref[...], g_ref[...]
        r = lax.rsqrt(jnp.mean(fv * fv, axis=-1, keepdims=True) + EPS)
        n = fv * r
        err = (h_ref[...] + n * g) - t_ref[...]
        dy = err * (1.0 / D)
        gy = dy * g
        df = r * (gy - n * jnp.mean(gy * n, axis=-1, keepdims=True))
        dy_ref[...] = dy
        df_ref[...] = df.astype(BF16)

        @pl.when(pl.program_id(0) == 0)
        def _():
            dg_ref[...] = jnp.zeros_like(dg_ref)
            sq_ref[...] = jnp.zeros_like(sq_ref)

        dg_ref[...] += jnp.sum(dy * n, axis=0, keepdims=True)
        sq_ref[...] += jnp.sum(err * err, axis=0, keepdims=True)

    return pl.pallas_call(
        body, name=name, grid=(T // bt,), in_specs=[row, vec, row, row], out_specs=(row, row, vec, vec),
        out_shape=(jax.ShapeDtypeStruct((T, D), F32), jax.ShapeDtypeStruct((T, D), BF16),
                   jax.ShapeDtypeStruct((1, D), F32), jax.ShapeDtypeStruct((1, D), F32)),
        compiler_params=_params(1),
    )(f, gain, h, target)


def _sb_logits(q, k, valid):
    z = lax.dot_general(q, k, NT, preferred_element_type=F32) * (HEAD_DIM ** -0.5)
    sp = jnp.log(1.0 + jnp.exp(-jnp.abs(z)))
    lb = jnp.minimum(z, 0.0) - sp
    l1 = -(jnp.maximum(z, 0.0) + sp)
    return lb, (l1 if valid is None else jnp.where(valid, l1, 0.0))


def _masked(valid, x):
    return x if valid is None else jnp.where(valid, x, 0.0)


def _heads_per_step(n_heads):
    return 2 if n_heads % 2 == 0 else 1


def _dot2(a, sel):
    hi = a.astype(BF16)
    lo = (a - hi.astype(F32)).astype(BF16)
    return jnp.dot(hi, sel, preferred_element_type=F32) + jnp.dot(lo, sel, preferred_element_type=F32)


class _Comm:
    def __init__(self, arrays=(), plan=((), (), None)):
        self.arrays = list(arrays)
        self.out_shapes, self.sems, self.phases = list(plan[0]), list(plan[1]), plan[2]

    def split(self, refs, n_in, n_out, n_scratch):
        a, o = len(self.arrays), len(self.out_shapes)
        cuts = np.cumsum([0, n_in, a, n_out, o, n_scratch])
        ins, cin, outs, cout, scratch = (refs[cuts[t]:cuts[t + 1]] for t in range(5))
        if self.phases is None:
            return ins, outs, scratch, (lambda: None,) * 3
        return ins, outs, scratch, self.phases(cin, cout, refs[cuts[5]:])

    def call(self, body, name, grid, in_specs, out_specs, out_shape, scratch_shapes, operands, aliases=None):
        outs = pl.pallas_call(
            body, name=name, grid=grid, in_specs=list(in_specs) + [_HBM] * len(self.arrays),
            out_specs=tuple(out_specs) + (_HBM,) * len(self.out_shapes),
            out_shape=tuple(out_shape) + tuple(self.out_shapes),
            scratch_shapes=list(scratch_shapes) + self.sems, input_output_aliases=aliases or {},
            compiler_params=pltpu.CompilerParams(dimension_semantics=("arbitrary",) * len(grid),
                                                 vmem_limit_bytes=VMEM_LIMIT, has_side_effects=self.phases is not None),
        )(*operands, *self.arrays)
        return outs[:len(out_shape)], outs[len(out_shape):]


def _sb_fwd(qkv, gain, blk, name, comm):
    T, W = qkv.shape[0], qkv.shape[1] // 3
    H = W // HEAD_DIM
    nq = T // blk
    hp = _heads_per_step(H)
    ng = H // hp
    lanes = [slice(t * HEAD_DIM, (t + 1) * HEAD_DIM) for t in range(hp)]

    def body(*refs):
        (q_ref, k_ref, v_ref, g_ref), (o_ref, mix_ref, lt_ref), _, (first, mid, last) = comm.split(refs, 4, 3, 0)
        h, i = pl.program_id(0), pl.program_id(1)
        pl.when((h == 0) & (i == 0))(first)
        pl.when((h == ng // 2) & (i == 0))(mid)
        q = [q_ref[:, ln] for ln in lanes]
        row, col = _iota2(blk, blk)
        after = (row > col).astype(BF16)

        def step(kb, carry, valid):
            ks = pl.ds(pl.multiple_of(kb * blk, blk), blk)
            out = []
            for t, (run, acc) in enumerate(carry):
                lb, l1 = _sb_logits(q[t], k_ref[ks, lanes[t]], valid)
                att = _masked(valid, jnp.exp(lb + _dot2(l1, after) + run))
                out.append((run + jnp.sum(l1, axis=1, keepdims=True), acc + _bdot(att, v_ref[ks, lanes[t]])))
            return tuple(out)

        zero = (jnp.zeros((blk, 1), F32), jnp.zeros((blk, HEAD_DIM), F32))
        carry = lax.fori_loop(0, i, lambda jj, c: step(i - 1 - jj, c, None), step(i, (zero,) * hp, col < row))
        for t, (run, o) in enumerate(carry):
            o_ref[:, lanes[t]] = o
            r = lax.rsqrt(jnp.mean(o * o, axis=-1, keepdims=True) + EPS)
            mix_ref[:, lanes[t]] = (o * r * g_ref[...]).astype(BF16)
            lt_ref[:, lanes[t]] = jnp.broadcast_to(run, (blk, HEAD_DIM))
        pl.when((h == ng - 1) & (i == nq - 1))(last)

    wide = hp * HEAD_DIM
    qb = pl.BlockSpec((blk, wide), lambda h, i: (i, h))
    return comm.call(
        body, name, (ng, nq),
        [qb, pl.BlockSpec((T, wide), lambda h, i: (0, ng + h)),
         pl.BlockSpec((T, wide), lambda h, i: (0, 2 * ng + h)), pl.BlockSpec((1, HEAD_DIM), lambda h, i: (0, 0))],
        (qb, qb, qb),
        (jax.ShapeDtypeStruct((T, W), F32), jax.ShapeDtypeStruct((T, W), BF16), jax.ShapeDtypeStruct((T, W), F32)),
        [], (qkv, qkv, qkv, gain))


def _headnorm_bwd(o, gain, dmix, bt, name, comm):
    T, W = o.shape
    H = W // HEAD_DIM
    nt = T // bt
    blk = pl.BlockSpec((bt, HEAD_DIM), lambda i, h: (i, h))
    vec = pl.BlockSpec((1, HEAD_DIM), lambda i, h: (0, 0))

    def body(*refs):
        (o_ref, g_ref, d_ref), (do_ref, dg_ref), _, (first, mid, last) = comm.split(refs, 3, 2, 0)
        i, h = pl.program_id(0), pl.program_id(1)
        pl.when((i == 0) & (h == 0))(first)
        pl.when((i == nt // 2) & (h == 0))(mid)
        do, dgp = _rms_bwd_math(o_ref[...], g_ref[...], d_ref[...])
        do_ref[...] = do

        @pl.when((i == 0) & (h == 0))
        def _():
            dg_ref[...] = jnp.zeros_like(dg_ref)

        dg_ref[...] += jnp.sum(dgp, axis=0, keepdims=True)
        pl.when((i == nt - 1) & (h == H - 1))(last)

    return comm.call(body, name, (nt, H), [blk, vec, blk], (blk, vec),
                     (jax.ShapeDtypeStruct((T, W), F32), jax.ShapeDtypeStruct((1, HEAD_DIM), F32)), [], (o, gain, dmix))


def _sb_bwd(qkv, do, lt, blk, name, comm):
    T, W = qkv.shape[0], qkv.shape[1] // 3
    H = W // HEAD_DIM
    nq = T // blk
    scale = HEAD_DIM ** -0.5
    hp = _heads_per_step(H)
    ng = H // hp
    lanes = [slice(t * HEAD_DIM, (t + 1) * HEAD_DIM) for t in range(hp)]

    def body(*refs):
        (q_ref, k_ref, v_ref, do_ref, lt_ref), (dq_ref, dk_ref, dv_ref), _, (first, mid, last) = comm.split(refs, 5, 3, 0)
        h, i = pl.program_id(0), pl.program_id(1)
        pl.when((h == 0) & (i == 0))(first)
        pl.when((h == ng // 2) & (i == 0))(mid)

        @pl.when(i == 0)
        def _():
            dk_ref[...] = jnp.zeros_like(dk_ref)
            dv_ref[...] = jnp.zeros_like(dv_ref)

        q = [q_ref[:, ln] for ln in lanes]
        dob = [do_ref[:, ln].astype(BF16) for ln in lanes]
        total = [lt_ref[:, ln][:, :1] for ln in lanes]
        row, col = _iota2(blk, blk)
        upto = (row <= col).astype(BF16)
        before = (row < col).astype(BF16)

        def step(kb, carry, valid):
            ks = pl.ds(pl.multiple_of(kb * blk, blk), blk)
            out = []
            for t, (seen, psum, dq) in enumerate(carry):
                k, v = k_ref[ks, lanes[t]], v_ref[ks, lanes[t]]
                lb, l1 = _sb_logits(q[t], k, valid)
                later = total[t] - seen - _dot2(l1, upto)
                att = _masked(valid, jnp.exp(lb + later))
                p = att * lax.dot_general(dob[t], v, NT, preferred_element_type=F32)
                c = psum + _dot2(p, before)
                sig = jnp.exp(lb)
                dz = (_masked(valid, p * (1.0 - sig) - c * sig) * scale).astype(BF16)
                dq = dq + jnp.dot(dz, k, preferred_element_type=F32)
                dk_ref[ks, lanes[t]] += lax.dot_general(dz, q[t], TN, preferred_element_type=F32)
                dv_ref[ks, lanes[t]] += lax.dot_general(att.astype(BF16), dob[t], TN, preferred_element_type=F32)
                out.append((seen + jnp.sum(l1, axis=1, keepdims=True), psum + jnp.sum(p, axis=1, keepdims=True), dq))
            return tuple(out)

        zero = jnp.zeros((blk, 1), F32)
        start = ((zero, zero, jnp.zeros((blk, HEAD_DIM), F32)),) * hp
        carry = step(i, lax.fori_loop(0, i, lambda kb, c: step(kb, c, None), start), col < row)
        for t in range(hp):
            dq_ref[:, lanes[t]] = carry[t][2]
        pl.when((h == ng - 1) & (i == nq - 1))(last)

    wide = hp * HEAD_DIM
    qb = pl.BlockSpec((blk, wide), lambda h, i: (i, h))
    head = pl.BlockSpec((T, wide), lambda h, i: (0, h))
    out = jax.ShapeDtypeStruct((T, W), F32)
    return comm.call(
        body, name, (ng, nq),
        [qb, pl.BlockSpec((T, wide), lambda h, i: (0, ng + h)),
         pl.BlockSpec((T, wide), lambda h, i: (0, 2 * ng + h)), qb, qb],
        (qb, head, head), (out, out, out), [], (qkv, qkv, qkv, do, lt))


def _expanders(H):
    lane = np.arange(H * HEAD_DIM) // HEAD_DIM
    eb = np.zeros((LANES, H * HEAD_DIM), np.float32)
    eg = np.zeros((LANES, H * HEAD_DIM), np.float32)
    eb[lane, np.arange(H * HEAD_DIM)] = 1.0
    eg[H + lane, np.arange(H * HEAD_DIM)] = 1.0
    sb = np.zeros((H * HEAD_DIM, LANES), np.float32)
    sg = np.zeros((H * HEAD_DIM, LANES), np.float32)
    sb[np.arange(H) * HEAD_DIM, np.arange(H)] = 1.0
    sg[np.arange(H) * HEAD_DIM, H + np.arange(H)] = 1.0
    return [jnp.asarray(m, BF16) for m in (eb, eg, sb, sg)]


def _conv_taps(ext_ref, w, n_out, lead):
    K = w.shape[0]
    out = None
    for j in range(K):
        term = ext_ref[pl.ds(lead - (K - 1) + j, n_out), :] * w[j:j + 1, :]
        out = term if out is None else out + term
    return out


def _l2_heads(s, H, fn):
    return jnp.concatenate([fn(s[:, h * HEAD_DIM:(h + 1) * HEAD_DIM]) for h in range(H)], axis=1)


def _dn_pre_fwd(pdn, pba, conv_w, a_log, dt_bias, bt, name):
    T, W = pdn.shape[0], pdn.shape[1] // 4
    H = W // HEAD_DIM
    eb, eg, _, _ = _expanders(H)
    nb = T // bt

    def body(x_ref, prev_ref, ba_ref, w_ref, al_ref, dt_ref, eb_ref, eg_ref,
             q_ref, k_ref, v_ref, bx_ref, gx_ref, ext):
        i = pl.program_id(0)
        ext[pl.ds(0, HALO), :] = jnp.where(i > 0, prev_ref[...], 0.0)
        ext[pl.ds(HALO, bt), :] = x_ref[...]
        c = _conv_taps(ext, w_ref[...], bt, HALO)
        s = c * _sigmoid(c)
        q_ref[...] = _l2_heads(s[:, :W], H, lambda t: t * lax.rsqrt(jnp.sum(t * t, axis=-1, keepdims=True) + EPS)
                               * (HEAD_DIM ** -0.5))
        k_ref[...] = _l2_heads(s[:, W:2 * W], H, lambda t: t * lax.rsqrt(jnp.sum(t * t, axis=-1, keepdims=True) + EPS))
        v_ref[...] = s[:, 2 * W:]
        ba = ba_ref[...]
        beta = _sigmoid(ba)
        graw = -jnp.exp(al_ref[...]) * _softplus(ba + dt_ref[...])
        row, col = _iota2(bt, bt)
        tri = ((row // CHUNK == col // CHUNK) & (row >= col)).astype(BF16)
        gcum = _dot3r(tri, graw)
        bx_ref[...] = _dot3(beta, eb_ref[...])
        gx_ref[...] = _dot3(gcum, eg_ref[...])

    C = 3 * W
    rowb = lambda w: pl.BlockSpec((bt, w), lambda i: (i, 0))
    full = lambda a: pl.BlockSpec(a.shape, lambda i: (0,) * a.ndim)
    out = jax.ShapeDtypeStruct((T, W), F32)
    return pl.pallas_call(
        body, name=name, grid=(nb,),
        in_specs=[rowb(C), pl.BlockSpec((HALO, C), lambda i: (jnp.maximum(i * (bt // HALO) - 1, 0), 0)),
                  rowb(LANES), full(conv_w), full(a_log), full(dt_bias), full(eb), full(eg)],
        out_specs=(rowb(W),) * 5, out_shape=(out,) * 5,
        scratch_shapes=[pltpu.VMEM((bt + HALO, C), F32)], compiler_params=_params(1),
    )(pdn, pdn, pba, conv_w, a_log, dt_bias, eb, eg)


def _dn_pre_bwd(pdn, pba, conv_w, a_log, dt_bias, dq, dk, dv, dbx, dgx, bt, name):
    T, W = pdn.shape[0], pdn.shape[1] // 4
    H = W // HEAD_DIM
    C = 3 * W
    K = conv_w.shape[0]
    _, _, sb, sg = _expanders(H)
    nb = T // bt
    n_ext = bt + HALO

    def body(x_ref, prev_ref, next_ref, ba_ref, w_ref, al_ref, dt_ref, sb_ref, sg_ref,
             dq_ref, dqn_ref, dk_ref, dkn_ref, dv_ref, dvn_ref, dbx_ref, dgx_ref,
             dx_ref, dba_ref, dw_ref, dal_ref, ddt_ref, ext, dext, dcext):
        i = pl.program_id(0)
        last = i == nb - 1
        ext[pl.ds(0, HALO), :] = jnp.where(i > 0, prev_ref[...], 0.0)
        ext[pl.ds(HALO, bt), :] = x_ref[...]
        ext[pl.ds(HALO + bt, HALO), :] = jnp.where(last, 0.0, next_ref[...])
        dext[pl.ds(0, bt), pl.ds(0, W)] = dq_ref[...]
        dext[pl.ds(0, bt), pl.ds(W, W)] = dk_ref[...]
        dext[pl.ds(0, bt), pl.ds(2 * W, W)] = dv_ref[...]
        dext[pl.ds(bt, HALO), pl.ds(0, W)] = jnp.where(last, 0.0, dqn_ref[...])
        dext[pl.ds(bt, HALO), pl.ds(W, W)] = jnp.where(last, 0.0, dkn_ref[...])
        dext[pl.ds(bt, HALO), pl.ds(2 * W, W)] = jnp.where(last, 0.0, dvn_ref[...])
        w = w_ref[...]
        c = _conv_taps(ext, w, n_ext, HALO)
        sg_c = _sigmoid(c)
        s = c * sg_c
        d = dext[...]

        def l2_bwd(scale):
            def fn(pair):
                t, dt = pair
                r = lax.rsqrt(jnp.sum(t * t, axis=-1, keepdims=True) + EPS)
                return scale * r * (dt - t * (r * r) * jnp.sum(t * dt, axis=-1, keepdims=True))
            return fn

        def heads(lo, fn):
            return jnp.concatenate(
                [fn((s[:, lo + h * HEAD_DIM:lo + (h + 1) * HEAD_DIM], d[:, lo + h * HEAD_DIM:lo + (h + 1) * HEAD_DIM]))
                 for h in range(H)], axis=1)

        ds = jnp.concatenate([heads(0, l2_bwd(HEAD_DIM ** -0.5)), heads(W, l2_bwd(1.0)), d[:, 2 * W:]], axis=1)
        dc = ds * (sg_c * (1.0 + c * (1.0 - sg_c)))
        dcext[...] = dc
        dx = None
        for j in range(K):
            term = dcext[pl.ds(K - 1 - j, bt), :] * w[j:j + 1, :]
            dx = term if dx is None else dx + term
        dx_ref[...] = dx.astype(BF16)

        @pl.when(i == 0)
        def _():
            dw_ref[...] = jnp.zeros_like(dw_ref)
            dal_ref[...] = jnp.zeros_like(dal_ref)
            ddt_ref[...] = jnp.zeros_like(ddt_ref)

        dcb = dc[:bt]
        dw_ref[...] += jnp.concatenate(
            [jnp.sum(dcb * ext[pl.ds(HALO - (K - 1) + j, bt), :], axis=0, keepdims=True) for j in range(K)], axis=0)

        ba = ba_ref[...]
        beta = _sigmoid(ba)
        al, dtb = al_ref[...], dt_ref[...]
        dbeta = _dot3(dbx_ref[...], sb_ref[...])
        dg = _dot3(dgx_ref[...], sg_ref[...])
        sp = _softplus(ba + dtb)
        da = dg * (-jnp.exp(al)) * _sigmoid(ba + dtb)
        dba_ref[...] = dbeta * beta * (1.0 - beta) + da
        dal_ref[...] += jnp.sum(dg * (-jnp.exp(al)) * sp, axis=0, keepdims=True)
        ddt_ref[...] += jnp.sum(da, axis=0, keepdims=True)

    rowb = lambda w: pl.BlockSpec((bt, w), lambda i: (i, 0))
    full = lambda a: pl.BlockSpec(a.shape, lambda i: (0,) * a.ndim)
    nxt = lambda w: pl.BlockSpec((HALO, w), lambda i: (jnp.minimum((i + 1) * (bt // HALO), T // HALO - 1), 0))
    vec = pl.BlockSpec((1, LANES), lambda i: (0, 0))
    return pl.pallas_call(
        body, name=name, grid=(nb,),
        in_specs=[rowb(C), pl.BlockSpec((HALO, C), lambda i: (jnp.maximum(i * (bt // HALO) - 1, 0), 0)), nxt(C),
                  rowb(LANES), full(conv_w), full(a_log), full(dt_bias), full(sb), full(sg),
                  rowb(W), nxt(W), rowb(W), nxt(W), rowb(W), nxt(W), rowb(W), rowb(W)],
        out_specs=(rowb(C), rowb(LANES), pl.BlockSpec((K, C), lambda i: (0, 0)), vec, vec),
        out_shape=(jax.ShapeDtypeStruct((T, C), BF16), jax.ShapeDtypeStruct((T, LANES), F32),
                   jax.ShapeDtypeStruct((K, C), F32), jax.ShapeDtypeStruct((1, LANES), F32),
                   jax.ShapeDtypeStruct((1, LANES), F32)),
        scratch_shapes=[pltpu.VMEM((bt + 2 * HALO, C), F32), pltpu.VMEM((n_ext, C), F32), pltpu.VMEM((n_ext, C), F32)],
        compiler_params=_params(1),
    )(pdn, pdn, pdn, pba, conv_w, a_log, dt_bias, sb, sg, dq, dq, dk, dk, dv, dv, dbx, dgx)


def _dot_split(a, b):
    a_hi, b_hi = a.astype(BF16), b.astype(BF16)
    a_lo, b_lo = (a - a_hi.astype(F32)).astype(BF16), (b - b_hi.astype(F32)).astype(BF16)
    dot = functools.partial(jnp.dot, preferred_element_type=F32)
    return dot(a_hi, b_hi) + (dot(a_hi, b_lo) + dot(a_lo, b_hi))


def _dn_local(q, k, v, beta, g, tm=None):
    row, col = _iota2(ROWS, ROWS)
    same = (row // CHUNK) == (col // CHUNK)
    causal = same & (row >= col)
    strict = same & (row > col)
    eye = (row == col).astype(F32)
    last_of = (col == (row // CHUNK) * CHUNK + (CHUNK - 1)).astype(BF16)
    eg = jnp.exp(g)
    decay = jnp.where(causal, jnp.exp(jnp.where(causal, g - g.T, 0.0)), 0.0)
    kb = k * beta
    vb = v * beta
    kk = _bdot(kb, k, NT)
    low = jnp.where(strict, kk * decay, 0.0)
    if tm is None:
        pw = -low
        tm = eye + pw
        for _ in range(5):
            pw = _dot_split(pw, pw)
            tm = tm + _dot_split(tm, pw)
    kbg = kb * eg
    u = _bdot(tm, vb)
    w = _bdot(tm, kbg)
    qk = _bdot(q, k, NT)
    qa = jnp.where(causal, qk * decay, 0.0)
    glast = _dot3r(last_of, g)
    e2 = jnp.exp(glast - g)
    return dict(row=row, col=col, same=same, causal=causal, strict=strict, eg=eg, decay=decay, kb=kb, vb=vb, kk=kk,
                tm=tm, kbg=kbg, u=u, w=w, qk=qk, qa=qa, glast=glast, e2=e2, kte=k * e2, qd=q * eg)


def _dn_core_fwd(q, k, v, bx, gx, z, gain, name):
    T, W = q.shape
    H = W // HEAD_DIM
    nb = T // ROWS
    hp = _heads_per_step(H)
    ng = H // hp

    def body(q_ref, k_ref, v_ref, b_ref, g_ref, z_ref, gain_ref, o_ref, mix_ref, ss_ref, tm_ref, state):
        @pl.when(pl.program_id(1) == 0)
        def _():
            state[...] = jnp.zeros_like(state)

        for t in range(hp):
            ln = slice(t * HEAD_DIM, (t + 1) * HEAD_DIM)
            L = _dn_local(q_ref[:, ln], k_ref[:, ln], v_ref[:, ln], b_ref[:, ln], g_ref[:, ln])
            s = state[t]
            vns, qds = [], []
            for c in range(2):
                rows = slice(c * CHUNK, (c + 1) * CHUNK)
                ss_ref[t, c] = s
                vn = L["u"][rows] - _bdot(L["w"][rows], s)
                qds.append(_bdot(L["qd"][rows], s))
                vns.append(vn)
                s = s * jnp.exp(L["glast"][c * CHUNK:c * CHUNK + 1, :]) + _bdot(L["kte"][rows], vn, TN)
            state[t] = s
            tm_ref[:, ln] = L["tm"]
            o = jnp.concatenate(qds, axis=0) + _bdot(L["qa"], jnp.concatenate(vns, axis=0))
            o_ref[:, ln] = o
            zz = z_ref[:, ln]
            r = lax.rsqrt(jnp.mean(o * o, axis=-1, keepdims=True) + EPS)
            mix_ref[:, ln] = ((o * r * gain_ref[...]) * (zz * _sigmoid(zz))).astype(BF16)

    wide = hp * HEAD_DIM
    blk = pl.BlockSpec((ROWS, wide), lambda h, b: (b, h))
    zblk = pl.BlockSpec((ROWS, wide), lambda h, b: (b, 3 * ng + h))
    return pl.pallas_call(
        body, name=name, grid=(ng, nb),
        in_specs=[blk] * 5 + [zblk, pl.BlockSpec((1, HEAD_DIM), lambda h, b: (0, 0))],
        out_specs=(blk, blk, pl.BlockSpec((hp, 2, HEAD_DIM, HEAD_DIM), lambda h, b: (h, b, 0, 0)), blk),
        out_shape=(jax.ShapeDtypeStruct((T, W), F32), jax.ShapeDtypeStruct((T, W), BF16),
                   jax.ShapeDtypeStruct((H, T // CHUNK, HEAD_DIM, HEAD_DIM), F32), jax.ShapeDtypeStruct((T, W), F32)),
        scratch_shapes=[pltpu.VMEM((hp, HEAD_DIM, HEAD_DIM), F32)], compiler_params=_params(2),
    )(q, k, v, bx, gx, z, gain)


def _dn_core_bwd(q, k, v, bx, gx, z, gain, o, dmix, ss, tms, dmix_col0, name):
    T, W = q.shape
    H = W // HEAD_DIM
    nb = T // ROWS
    hp = _heads_per_step(H)
    ng = H // hp
    wide = hp * HEAD_DIM
    c0 = dmix_col0 // wide

    def body(q_ref, k_ref, v_ref, b_ref, g_ref, z_ref, gain_ref, o_ref, dm_ref, ss_ref, tm_ref,
             dq_ref, dk_ref, dv_ref, dbx_ref, dgx_ref, dz_ref, dgain_ref, dstate):
        @pl.when(pl.program_id(1) == 0)
        def _():
            dstate[...] = jnp.zeros_like(dstate)
            dgain_ref[...] = jnp.zeros_like(dgain_ref)

        refs = (q_ref, k_ref, v_ref, b_ref, g_ref, z_ref, gain_ref, o_ref, dm_ref, ss_ref, tm_ref,
                dq_ref, dk_ref, dv_ref, dbx_ref, dgx_ref, dz_ref, dgain_ref, dstate)
        for t in range(hp):
            one_head(t, *refs)

    def one_head(t, q_ref, k_ref, v_ref, b_ref, g_ref, z_ref, gain_ref, o_ref, dm_ref, ss_ref, tm_ref,
                 dq_ref, dk_ref, dv_ref, dbx_ref, dgx_ref, dz_ref, dgain_ref, dstate):
        ln = slice(t * HEAD_DIM, (t + 1) * HEAD_DIM)
        qv, kv, vv, beta, g = q_ref[:, ln], k_ref[:, ln], v_ref[:, ln], b_ref[:, ln], g_ref[:, ln]
        gain_v = gain_ref[...]
        ov, zz, dm = o_ref[:, ln], z_ref[:, ln], dm_ref[:, ln]
        r = lax.rsqrt(jnp.mean(ov * ov, axis=-1, keepdims=True) + EPS)
        n = ov * r
        sgz = _sigmoid(zz)
        d_on = dm * (zz * sgz)
        dz_ref[:, ln] = dm * (n * gain_v) * (sgz * (1.0 + zz * (1.0 - sgz)))
        dgain_ref[t] += jnp.sum(d_on * n, axis=0, keepdims=True)
        gy = d_on * gain_v
        do = r * (gy - n * jnp.mean(gy * n, axis=-1, keepdims=True))

        L = _dn_local(qv, kv, vv, beta, g, tm_ref[:, ln])
        row, causal, strict = L["row"], L["causal"], L["strict"]
        u, w, qa, qd, kte, tm = L["u"], L["w"], L["qa"], L["qd"], L["kte"], L["tm"]
        s_in = [ss_ref[t, 0], ss_ref[t, 1]]
        vn = [u[c * CHUNK:(c + 1) * CHUNK] - _bdot(w[c * CHUNK:(c + 1) * CHUNK], s_in[c]) for c in range(2)]
        vn_all = jnp.concatenate(vn, axis=0)
        qat_do = _bdot(qa, do, TN)
        d_qa = jnp.where(causal, _bdot(do, vn_all, NT), 0.0)
        ds = dstate[t]
        d_vn, d_kte, d_qd, d_w, d_gl = [None] * 2, [None] * 2, [None] * 2, [None] * 2, [None] * 2
        for c in (1, 0):
            rows = slice(c * CHUNK, (c + 1) * CHUNK)
            egl = jnp.exp(L["glast"][c * CHUNK:c * CHUNK + 1, :])
            d_vn[c] = qat_do[rows] + _bdot(kte[rows], ds)
            d_kte[c] = _bdot(vn[c], ds, NT)
            d_gl[c] = jnp.sum(jnp.sum(ds * s_in[c], axis=1, keepdims=True), axis=0, keepdims=True) * egl
            d_qd[c] = _bdot(do[rows], s_in[c], NT)
            d_w[c] = -_bdot(d_vn[c], s_in[c], NT)
            ds = ds * egl + _bdot(qd[rows], do[rows], TN) - _bdot(w[rows], d_vn[c], TN)
        dstate[t] = ds
        d_u = jnp.concatenate(d_vn, axis=0)
        d_w = jnp.concatenate(d_w, axis=0)
        d_qd = jnp.concatenate(d_qd, axis=0)
        d_kte = jnp.concatenate(d_kte, axis=0)

        d_tm = _bdot(d_u, L["vb"], NT) + _bdot(d_w, L["kbg"], NT)
        d_vb = _bdot(tm, d_u, TN)
        d_kbg = _bdot(tm, d_w, TN)
        d_low = jnp.where(strict, -_bdot(_bdot(tm, d_tm, TN), tm, NT), 0.0)
        decay = L["decay"]
        d_kk = d_low * decay
        d_qk = d_qa * decay
        d_decay = d_low * L["kk"] + d_qa * L["qk"]
        eg, e2 = L["eg"], L["e2"]
        d_kb = _bdot(d_kk, kv) + d_kbg * eg
        dk_ref[:, ln] = _bdot(d_kk, L["kb"], TN) + _bdot(d_qk, qv, TN) + d_kb * beta + d_kte * e2
        dq_ref[:, ln] = _bdot(d_qk, kv) + d_qd * eg
        dv_ref[:, ln] = d_vb * beta
        rsum = lambda a: jnp.sum(a, axis=-1, keepdims=True)
        dbx_ref[:, ln] = jnp.broadcast_to(rsum(d_kb * kv) + rsum(d_vb * vv), (ROWS, HEAD_DIM))
        d_eg = rsum(d_kbg * L["kb"]) + rsum(d_qd * qv)
        t2 = rsum(d_kte * kv) * e2
        ed = d_decay * decay
        d_g = d_eg * eg - t2 + rsum(ed) - rsum(ed.T)
        col = L["col"]
        chunk_sum = L["same"].astype(BF16)
        is_last = (row % CHUNK) == (CHUNK - 1)
        d_glast = _dot3r(chunk_sum, t2) + jnp.where(row < CHUNK, d_gl[0], d_gl[1])
        d_g = d_g + jnp.where(is_last, d_glast, 0.0)
        suffix = (L["same"] & (col >= row)).astype(BF16)
        dgx_ref[:, ln] = _dot3r(suffix, d_g)

    rev = lambda b: nb - 1 - b
    blk = pl.BlockSpec((ROWS, wide), lambda h, b: (rev(b), h))
    zblk = pl.BlockSpec((ROWS, wide), lambda h, b: (rev(b), 3 * ng + h))
    dmblk = pl.BlockSpec((ROWS, wide), lambda h, b: (rev(b), c0 + h))
    out = jax.ShapeDtypeStruct((T, W), F32)
    return pl.pallas_call(
        body, name=name, grid=(ng, nb),
        in_specs=[blk] * 5 + [zblk, pl.BlockSpec((1, HEAD_DIM), lambda h, b: (0, 0)), blk, dmblk,
                              pl.BlockSpec((hp, 2, HEAD_DIM, HEAD_DIM), lambda h, b: (h, rev(b), 0, 0)), blk],
        out_specs=(blk,) * 6 + (pl.BlockSpec((hp, 1, HEAD_DIM), lambda h, b: (h, 0, 0)),),
        out_shape=(out,) * 6 + (jax.ShapeDtypeStruct((H, 1, HEAD_DIM), F32),),
        scratch_shapes=[pltpu.VMEM((hp, HEAD_DIM, HEAD_DIM), F32)], compiler_params=_params(2),
    )(q, k, v, bx, gx, z, gain, o, dmix, ss, tms)


_GELU_C = 0.7978845608028654
_GELU_A = 0.044715


def _gelu(x):
    t = jnp.tanh(_GELU_C * (x + _GELU_A * (x * x * x)))
    return 0.5 * x * (1.0 + t), t


def _ffn_mid_fwd(up, conv_w, conv_b, bt, bc, name):
    T, F = up.shape[0], up.shape[1] // 2
    nc = F // bc

    def body(g_ref, gp_ref, v_ref, vp_ref, wg_ref, wv_ref, bg_ref, bv_ref, o_ref, gext, vext):
        i = pl.program_id(0)
        for ext, cur, prev in ((gext, g_ref, gp_ref), (vext, v_ref, vp_ref)):
            ext[pl.ds(0, HALO), :] = jnp.where(i > 0, prev[...], 0.0)
            ext[pl.ds(HALO, bt), :] = cur[...]
        gate = _conv_taps(gext, wg_ref[...], bt, HALO) + bg_ref[...]
        val = _conv_taps(vext, wv_ref[...], bt, HALO) + bv_ref[...]
        o_ref[...] = (_gelu(gate)[0] * val).astype(BF16)

    K = conv_w.shape[0]
    prev = lambda i: jnp.maximum(i * (bt // HALO) - 1, 0)
    return pl.pallas_call(
        body, name=name, grid=(T // bt, nc),
        in_specs=[pl.BlockSpec((bt, bc), lambda i, j: (i, j)), pl.BlockSpec((HALO, bc), lambda i, j: (prev(i), j)),
                  pl.BlockSpec((bt, bc), lambda i, j: (i, nc + j)),
                  pl.BlockSpec((HALO, bc), lambda i, j: (prev(i), nc + j)),
                  pl.BlockSpec((K, bc), lambda i, j: (0, j)), pl.BlockSpec((K, bc), lambda i, j: (0, nc + j)),
                  pl.BlockSpec((1, bc), lambda i, j: (0, j)), pl.BlockSpec((1, bc), lambda i, j: (0, nc + j))],
        out_specs=pl.BlockSpec((bt, bc), lambda i, j: (i, j)),
        out_shape=jax.ShapeDtypeStruct((T, F), BF16),
        scratch_shapes=[pltpu.VMEM((bt + HALO, bc), F32)] * 2, compiler_params=_params(2),
    )(up, up, up, up, conv_w, conv_w, conv_b, conv_b)


def _ffn_mid_bwd(up, conv_w, conv_b, da, bt, bc, name):
    T, F = up.shape[0], up.shape[1] // 2
    nc = F // bc
    K = conv_w.shape[0]
    nb = T // bt
    n_ext = bt + HALO

    def body(g_ref, gp_ref, gn_ref, v_ref, vp_ref, vn_ref, da_ref, dan_ref, wg_ref, wv_ref, bg_ref, bv_ref,
             dg_ref, dv_ref, dwg_ref, dwv_ref, dbg_ref, dbv_ref, gext, vext, dgext, dvext):
        i = pl.program_id(1)
        last = i == nb - 1
        for ext, cur, prev, nxt in ((gext, g_ref, gp_ref, gn_ref), (vext, v_ref, vp_ref, vn_ref)):
            ext[pl.ds(0, HALO), :] = jnp.where(i > 0, prev[...], 0.0)
            ext[pl.ds(HALO, bt), :] = cur[...]
            ext[pl.ds(HALO + bt, HALO), :] = jnp.where(last, 0.0, nxt[...])
        wg, wv = wg_ref[...], wv_ref[...]
        gate = _conv_taps(gext, wg, n_ext, HALO) + bg_ref[...]
        val = _conv_taps(vext, wv, n_ext, HALO) + bv_ref[...]
        dact = jnp.concatenate([da_ref[...], jnp.where(last, 0.0, dan_ref[...])], axis=0)
        ge, t = _gelu(gate)
        dgelu = 0.5 * (1.0 + t) + 0.5 * gate * (1.0 - t * t) * (_GELU_C * (1.0 + 3.0 * _GELU_A * (gate * gate)))
        dgext[...] = dact * val * dgelu
        dvext[...] = dact * ge

        @pl.when(i == 0)
        def _():
            for ref in (dwg_ref, dwv_ref, dbg_ref, dbv_ref):
                ref[...] = jnp.zeros_like(ref)

        for dext, ext, w, dx_ref, dw_ref, db_ref in ((dgext, gext, wg, dg_ref, dwg_ref, dbg_ref),
                                                     (dvext, vext, wv, dv_ref, dwv_ref, dbv_ref)):
            dx = None
            for j in range(K):
                term = dext[pl.ds(K - 1 - j, bt), :] * w[j:j + 1, :]
                dx = term if dx is None else dx + term
            dx_ref[...] = dx.astype(BF16)
            dcur = dext[pl.ds(0, bt), :]
            dw_ref[...] += jnp.concatenate(
                [jnp.sum(dcur * ext[pl.ds(HALO - (K - 1) + j, bt), :], axis=0, keepdims=True) for j in range(K)], axis=0)
            db_ref[...] += jnp.sum(dcur, axis=0, keepdims=True)

    prev = lambda i: jnp.maximum(i * (bt // HALO) - 1, 0)
    nxt = lambda i: jnp.minimum((i + 1) * (bt // HALO), T // HALO - 1)
    cur_g = pl.BlockSpec((bt, bc), lambda j, i: (i, j))
    cur_v = pl.BlockSpec((bt, bc), lambda j, i: (i, nc + j))
    outs = pl.pallas_call(
        body, name=name, grid=(nc, nb),
        in_specs=[cur_g, pl.BlockSpec((HALO, bc), lambda j, i: (prev(i), j)),
                  pl.BlockSpec((HALO, bc), lambda j, i: (nxt(i), j)),
                  cur_v, pl.BlockSpec((HALO, bc), lambda j, i: (prev(i), nc + j)),
                  pl.BlockSpec((HALO, bc), lambda j, i: (nxt(i), nc + j)),
                  cur_g, pl.BlockSpec((HALO, bc), lambda j, i: (nxt(i), j)),
                  pl.BlockSpec((K, bc), lambda j, i: (0, j)), pl.BlockSpec((K, bc), lambda j, i: (0, nc + j)),
                  pl.BlockSpec((1, bc), lambda j, i: (0, j)), pl.BlockSpec((1, bc), lambda j, i: (0, nc + j))],
        out_specs=(cur_g, cur_g, pl.BlockSpec((K, bc), lambda j, i: (0, j)), pl.BlockSpec((K, bc), lambda j, i: (0, j)),
                   pl.BlockSpec((1, bc), lambda j, i: (0, j)), pl.BlockSpec((1, bc), lambda j, i: (0, j))),
        out_shape=(jax.ShapeDtypeStruct((T, F), BF16), jax.ShapeDtypeStruct((T, F), BF16),
                   jax.ShapeDtypeStruct((K, F), F32), jax.ShapeDtypeStruct((K, F), F32),
                   jax.ShapeDtypeStruct((1, F), F32), jax.ShapeDtypeStruct((1, F), F32)),
        scratch_shapes=[pltpu.VMEM((bt + 2 * HALO, bc), F32)] * 2 + [pltpu.VMEM((n_ext, bc), F32)] * 2,
        compiler_params=_params(2),
    )(up, up, up, up, up, up, da, da, conv_w, conv_w, conv_b, conv_b)
    return outs


def _adam_math(w, g, m, v):
    m2 = ADAM_B1 * m + (1.0 - ADAM_B1) * g
    v2 = ADAM_B2 * v + (1.0 - ADAM_B2) * (g * g)
    m_hat = m2 / (1.0 - ADAM_B1 ** ADAM_STEP)
    v_hat = v2 / (1.0 - ADAM_B2 ** ADAM_STEP)
    return -ADAM_LR * (m_hat / (jnp.sqrt(v_hat) + ADAM_EPS) + ADAM_WD * w), m2, v2


def _adamw_halves(w, mine, theirs, place, m, v, bt, name):
    R, C = w.shape
    h = R // 2
    bt = _fit_rows(h, bt)
    nh = h // bt

    def body(s_ref, w_ref, a_ref, b_ref, m_ref, v_ref, g_ref, d_ref, m2_ref, v2_ref):
        lower = pl.program_id(0) < nh
        gv = jnp.where(lower == (s_ref[0] == 0), a_ref[...], b_ref[...])
        g_ref[...] = gv
        d_ref[...], m2_ref[...], v2_ref[...] = _adam_math(w_ref[...], gv, m_ref[...], v_ref[...])

    full = pl.BlockSpec((bt, C), lambda i, s: (i, 0))
    half = pl.BlockSpec((bt, C), lambda i, s: (i % nh, 0))
    out = jax.ShapeDtypeStruct((R, C), F32)
    return pl.pallas_call(
        body, name=name,
        grid_spec=pltpu.PrefetchScalarGridSpec(num_scalar_prefetch=1, grid=(2 * nh,),
                                               in_specs=[full, half, half, full, full], out_specs=(full,) * 4),
        out_shape=(out,) * 4, compiler_params=_params(1),
    )(place, w, mine, theirs, m, v)


def _adamw(w, g, m, v, bt, name):
    R, C = w.shape
    bt = _fit_rows(R, bt)
    blk = pl.BlockSpec((bt, C), lambda i: (i, 0))

    def body(w_ref, g_ref, m_ref, v_ref, d_ref, m2_ref, v2_ref):
        gv = g_ref[...]
        m2 = ADAM_B1 * m_ref[...] + (1.0 - ADAM_B1) * gv
        v2 = ADAM_B2 * v_ref[...] + (1.0 - ADAM_B2) * (gv * gv)
        m_hat = m2 / (1.0 - ADAM_B1 ** ADAM_STEP)
        v_hat = v2 / (1.0 - ADAM_B2 ** ADAM_STEP)
        d_ref[...] = -ADAM_LR * (m_hat / (jnp.sqrt(v_hat) + ADAM_EPS) + ADAM_WD * w_ref[...])
        m2_ref[...] = m2
        v2_ref[...] = v2

    out = jax.ShapeDtypeStruct((R, C), F32)
    return pl.pallas_call(body, name=name, grid=(R // bt,), in_specs=[blk] * 4, out_specs=(blk,) * 3,
                          out_shape=(out,) * 3, compiler_params=_params(1))(w, g, m, v)


def _place():
    x, y, c = lax.axis_index("x"), lax.axis_index("y"), lax.axis_index("c")
    chips = [(1 - x, y), (x, 1 - y), (1 - x, 1 - y)]
    return x, y, c, chips


_HBM = pl.BlockSpec(memory_space=pltpu.HBM)


def _gather_xy(bufs, name):
    n = len(bufs)
    halves = [b.shape[0] // 2 for b in bufs]

    def body(*refs):
        ins, outs = refs[:n], refs[n:2 * n]
        send, recv, local = refs[2 * n:]
        x, y, c, chips = _place()
        me = 2 * x + y
        copies, forwards = [], []
        for b in range(n):
            h = halves[b]
            mine = pl.ds(c * h, h)
            own = pltpu.make_async_copy(ins[b], outs[b].at[me], local.at[b])
            own.start()
            copies.append(own)
            for j, (px, py) in enumerate(chips):
                cp = pltpu.make_async_remote_copy(
                    src_ref=ins[b].at[mine], dst_ref=outs[b].at[me, mine], send_sem=send.at[b, j],
                    recv_sem=recv.at[b, j], device_id=(px, py, c), device_id_type=MESH)
                cp.start()
                copies.append(cp)
        for b in range(n):
            h = halves[b]
            mine = pl.ds(c * h, h)
            for j, (px, py) in enumerate(chips):
                src = 2 * px + py
                landed = pltpu.make_async_remote_copy(
                    src_ref=ins[b].at[mine], dst_ref=outs[b].at[src, mine], send_sem=send.at[b, j],
                    recv_sem=recv.at[b, j], device_id=(px, py, c), device_id_type=MESH)
                landed.wait_recv()
                fw = pltpu.make_async_remote_copy(
                    src_ref=outs[b].at[src, mine], dst_ref=outs[b].at[src, mine], send_sem=send.at[b, 3 + j],
                    recv_sem=recv.at[b, 3 + j], device_id=(x, y, 1 - c), device_id_type=MESH)
                fw.start()
                forwards.append(fw)
        for b in range(n):
            h = halves[b]
            theirs = pl.ds((1 - c) * h, h)
            for j, (px, py) in enumerate(chips):
                src = 2 * px + py
                pltpu.make_async_remote_copy(
                    src_ref=outs[b].at[src, theirs], dst_ref=outs[b].at[src, theirs], send_sem=send.at[b, 3 + j],
                    recv_sem=recv.at[b, 3 + j], device_id=(x, y, 1 - c), device_id_type=MESH).wait_recv()
        for b in range(n):
            copies[b * 4].wait()
            for j in range(3):
                copies[b * 4 + 1 + j].wait_send()
        for fw in forwards:
            fw.wait_send()

    return pl.pallas_call(
        body, name=name, in_specs=[_HBM] * n, out_specs=[_HBM] * n,
        out_shape=[jax.ShapeDtypeStruct((4,) + b.shape, b.dtype) for b in bufs],
        scratch_shapes=[pltpu.SemaphoreType.DMA((n, 6)), pltpu.SemaphoreType.DMA((n, 6)), pltpu.SemaphoreType.DMA((n,))],
        compiler_params=pltpu.CompilerParams(has_side_effects=True),
    )(*bufs)


def _swap_half_c(buf, name):
    n, h = buf.shape[0], buf.shape[1] // 2

    def body(in_ref, out_ref, send, recv):
        x, y, c, _ = _place()
        cp = pltpu.make_async_remote_copy(
            src_ref=in_ref.at[:, pl.ds((1 - c) * h, h)], dst_ref=out_ref, send_sem=send, recv_sem=recv,
            device_id=(x, y, 1 - c), device_id_type=MESH)
        cp.start()
        cp.wait()

    return pl.pallas_call(
        body, name=name, in_specs=[_HBM], out_specs=_HBM,
        out_shape=jax.ShapeDtypeStruct((n, h, buf.shape[2]), buf.dtype),
        scratch_shapes=[pltpu.SemaphoreType.DMA, pltpu.SemaphoreType.DMA],
        compiler_params=pltpu.CompilerParams(has_side_effects=True),
    )(buf)


def _scatter_xy(buf, name):
    h = buf.shape[1]

    def body(in_ref, out_ref, send, recv):
        x, y, c, chips = _place()
        cps = []
        for j, (px, py) in enumerate(chips):
            cp = pltpu.make_async_remote_copy(
                src_ref=in_ref.at[2 * px + py], dst_ref=out_ref.at[j], send_sem=send.at[j], recv_sem=recv.at[j],
                device_id=(px, py, c), device_id_type=MESH)
            cp.start()
            cps.append(cp)
        for cp in cps:
            cp.wait()

    return pl.pallas_call(
        body, name=name, in_specs=[_HBM], out_specs=_HBM,
        out_shape=jax.ShapeDtypeStruct((3, h, buf.shape[2]), buf.dtype),
        scratch_shapes=[pltpu.SemaphoreType.DMA((3,)), pltpu.SemaphoreType.DMA((3,))],
        compiler_params=pltpu.CompilerParams(has_side_effects=True),
    )(buf)


def _join_c(half, name):
    h = half.shape[0]

    def body(in_ref, out_ref, send, recv, local):
        x, y, c, _ = _place()
        own = pltpu.make_async_copy(in_ref, out_ref.at[c], local)
        own.start()
        cp = pltpu.make_async_remote_copy(
            src_ref=in_ref, dst_ref=out_ref.at[c], send_sem=send, recv_sem=recv,
            device_id=(x, y, 1 - c), device_id_type=MESH)
        cp.start()
        cp.wait()
        own.wait()

    return pl.pallas_call(
        body, name=name, in_specs=[_HBM], out_specs=_HBM,
        out_shape=jax.ShapeDtypeStruct((2, h, half.shape[1]), half.dtype),
        scratch_shapes=[pltpu.SemaphoreType.DMA, pltpu.SemaphoreType.DMA, pltpu.SemaphoreType.DMA],
        compiler_params=pltpu.CompilerParams(has_side_effects=True),
    )(half)


def _add_half(buf, other, c, bt, name):
    n, _, h, lanes = buf.shape
    bt = _fit_rows(h, bt)

    def body(c_ref, a_ref, b_ref, o_ref):
        o_ref[...] = a_ref[0] + b_ref[...]

    return pl.pallas_call(
        body, name=name,
        grid_spec=pltpu.PrefetchScalarGridSpec(
            num_scalar_prefetch=1, grid=(n, h // bt),
            in_specs=[pl.BlockSpec((1, 1, bt, lanes), lambda k, i, c_ref: (k, c_ref[0], i, 0)),
                      pl.BlockSpec((1, bt, lanes), lambda k, i, c_ref: (k, i, 0))],
            out_specs=pl.BlockSpec((1, bt, lanes), lambda k, i, c_ref: (k, i, 0))),
        out_shape=jax.ShapeDtypeStruct((n, h, lanes), buf.dtype), compiler_params=_params(2),
    )(c, buf, other)


def _add_four(own, me, others, bt, name):
    _, h, lanes = own.shape
    bt = _fit_rows(h, bt)

    def body(me_ref, a_ref, b_ref, o_ref):
        o_ref[...] = ((a_ref[0] + b_ref[0]) + b_ref[1]) + b_ref[2]

    return pl.pallas_call(
        body, name=name,
        grid_spec=pltpu.PrefetchScalarGridSpec(
            num_scalar_prefetch=1, grid=(h // bt,),
            in_specs=[pl.BlockSpec((1, bt, lanes), lambda i, me_ref: (me_ref[0], i, 0)),
                      pl.BlockSpec((3, bt, lanes), lambda i, me_ref: (0, i, 0))],
            out_specs=pl.BlockSpec((bt, lanes), lambda i, me_ref: (i, 0))),
        out_shape=jax.ShapeDtypeStruct((h, lanes), own.dtype), compiler_params=_params(1),
    )(me, own, others)


def _gather_chips(bufs, split, name):
    n = len(bufs)

    def body(*refs):
        ins, outs = refs[:n], refs[n:2 * n]
        send, recv = refs[2 * n:]
        x, y, c, chips = _place()
        me = 2 * x + y

        def rows(b, core):
            h = bufs[b].shape[0] // 2
            return pl.ds(core * h, h) if split[b] else pl.ds(0, bufs[b].shape[0])

        def over_ici(b, j, block):
            px, py = chips[j]
            return pltpu.make_async_remote_copy(
                src_ref=ins[b].at[rows(b, c)], dst_ref=outs[b].at[block, rows(b, c)], send_sem=send.at[b, j],
                recv_sem=recv.at[b, j], device_id=(px, py, c), device_id_type=MESH)

        def over_d2d(b, j, block, core):
            return pltpu.make_async_remote_copy(
                src_ref=outs[b].at[block, rows(b, core)], dst_ref=outs[b].at[block, rows(b, core)],
                send_sem=send.at[b, 3 + j], recv_sem=recv.at[b, 3 + j], device_id=(x, y, 1 - c), device_id_type=MESH)

        started = []
        for b in range(n):
            for j in range(3):
                cp = over_ici(b, j, me)
                cp.start()
                started.append(cp.wait_send)
        for b in range(n):
            for j, (px, py) in enumerate(chips):
                over_ici(b, j, 2 * px + py).wait_recv()
                if split[b]:
                    fw = over_d2d(b, j, 2 * px + py, c)
                    fw.start()
                    started.append(fw.wait_send)
        for b in range(n):
            if split[b]:
                for j, (px, py) in enumerate(chips):
                    over_d2d(b, j, 2 * px + py, 1 - c).wait_recv()
        for wait in started:
            wait()

    return pl.pallas_call(
        body, name=name, in_specs=[_HBM] * n, out_specs=[_HBM] * n,
        out_shape=[jax.ShapeDtypeStruct((4,) + b.shape, b.dtype) for b in bufs],
        scratch_shapes=[pltpu.SemaphoreType.DMA((n, 6)), pltpu.SemaphoreType.DMA((n, 6))],
        compiler_params=pltpu.CompilerParams(has_side_effects=True),
    )(*bufs)


def _swap_halves(bufs, name):
    n = len(bufs)

    def body(*refs):
        ins, outs = refs[:n], refs[n:2 * n]
        send, recv = refs[2 * n:]
        x, y, c, _ = _place()
        cps = []
        for b in range(n):
            h = bufs[b].shape[1] // 2
            cp = pltpu.make_async_remote_copy(
                src_ref=ins[b].at[:, pl.ds((1 - c) * h, h)], dst_ref=outs[b], send_sem=send.at[b], recv_sem=recv.at[b],
                device_id=(x, y, 1 - c), device_id_type=MESH)
            cp.start()
            cps.append(cp)
        for cp in cps:
            cp.wait()

    return pl.pallas_call(
        body, name=name, in_specs=[_HBM] * n, out_specs=[_HBM] * n,
        out_shape=[jax.ShapeDtypeStruct((b.shape[0], b.shape[1] // 2, b.shape[2]), b.dtype) for b in bufs],
        scratch_shapes=[pltpu.SemaphoreType.DMA((n,)), pltpu.SemaphoreType.DMA((n,))],
        compiler_params=pltpu.CompilerParams(has_side_effects=True),
    )(*bufs)


def _scatter_chips(bufs, name):
    n = len(bufs)

    def body(*refs):
        ins, outs = refs[:n], refs[n:2 * n]
        send, recv = refs[2 * n:]
        x, y, c, chips = _place()
        cps = []
        for b in range(n):
            for j, (px, py) in enumerate(chips):
                cp = pltpu.make_async_remote_copy(
                    src_ref=ins[b].at[2 * px + py], dst_ref=outs[b].at[j], send_sem=send.at[b, j],
                    recv_sem=recv.at[b, j], device_id=(px, py, c), device_id_type=MESH)
                cp.start()
                cps.append(cp)
        for cp in cps:
            cp.wait()

    return pl.pallas_call(
        body, name=name, in_specs=[_HBM] * n, out_specs=[_HBM] * n,
        out_shape=[jax.ShapeDtypeStruct((3,) + b.shape[1:], b.dtype) for b in bufs],
        scratch_shapes=[pltpu.SemaphoreType.DMA((n, 3)), pltpu.SemaphoreType.DMA((n, 3))],
        compiler_params=pltpu.CompilerParams(has_side_effects=True),
    )(*bufs)


def _join_halves(halves, name):
    n = len(halves)

    def body(*refs):
        ins, outs = refs[:n], refs[n:2 * n]
        send, recv = refs[2 * n:]
        x, y, c, _ = _place()
        cps = []
        for b in range(n):
            cp = pltpu.make_async_remote_copy(
                src_ref=ins[b], dst_ref=outs[b], send_sem=send.at[b], recv_sem=recv.at[b],
                device_id=(x, y, 1 - c), device_id_type=MESH)
            cp.start()
            cps.append(cp)
        for cp in cps:
            cp.wait()

    return pl.pallas_call(
        body, name=name, in_specs=[_HBM] * n, out_specs=[_HBM] * n,
        out_shape=[jax.ShapeDtypeStruct(b.shape, b.dtype) for b in halves],
        scratch_shapes=[pltpu.SemaphoreType.DMA((n,)), pltpu.SemaphoreType.DMA((n,))],
        compiler_params=pltpu.CompilerParams(has_side_effects=True),
    )(*halves)


def _add_cores(buf, other, place, own_only, out_dtype, bt, name):
    n, _, h, cols = buf.shape
    bt = _fit_rows(h, bt)
    row = (lambda k, s: s[1]) if own_only else (lambda k, s: k)

    def body(s_ref, a_ref, b_ref, o_ref):
        o_ref[...] = (a_ref[...] + b_ref[...]).astype(out_dtype)

    return pl.pallas_call(
        body, name=name,
        grid_spec=pltpu.PrefetchScalarGridSpec(
            num_scalar_prefetch=1, grid=(1 if own_only else n, h // bt),
            in_specs=[pl.BlockSpec((None, None, bt, cols), lambda k, i, s: (row(k, s), s[0], i, 0)),
                      pl.BlockSpec((None, bt, cols), lambda k, i, s: (row(k, s), i, 0))],
            out_specs=(pl.BlockSpec((bt, cols), lambda k, i, s: (i, 0)) if own_only
                       else pl.BlockSpec((None, bt, cols), lambda k, i, s: (k, i, 0)))),
        out_shape=jax.ShapeDtypeStruct((h, cols) if own_only else (n, h, cols), out_dtype),
        compiler_params=_params(2),
    )(place, buf, other)


def _add_chips(own, others, bt, name):
    h, cols = own.shape
    bt = _fit_rows(h, bt)

    def body(a_ref, b_ref, o_ref):
        o_ref[...] = ((a_ref[...] + b_ref[0].astype(F32)) + b_ref[1].astype(F32)) + b_ref[2].astype(F32)

    return pl.pallas_call(
        body, name=name, grid=(h // bt,),
        in_specs=[pl.BlockSpec((bt, cols), lambda i: (i, 0)), pl.BlockSpec((3, bt, cols), lambda i: (0, i, 0))],
        out_specs=pl.BlockSpec((bt, cols), lambda i: (i, 0)),
        out_shape=jax.ShapeDtypeStruct((h, cols), F32), compiler_params=_params(1),
    )(own, others)


def _gather_plan(bufs, split):
    n = len(bufs)

    def phases(ins, outs, sems):
        send, recv = sems
        x, y, c, chips = _place()
        me = 2 * x + y

        def rows(b, core):
            h = bufs[b].shape[0] // 2
            return pl.ds(core * h, h) if split[b] else pl.ds(0, bufs[b].shape[0])

        def over_ici(b, j, block):
            px, py = chips[j]
            return pltpu.make_async_remote_copy(
                src_ref=ins[b].at[rows(b, c)], dst_ref=outs[b].at[block, rows(b, c)], send_sem=send.at[b, j],
                recv_sem=recv.at[b, j], device_id=(px, py, c), device_id_type=MESH)

        def over_d2d(b, j, block, core):
            return pltpu.make_async_remote_copy(
                src_ref=outs[b].at[block, rows(b, core)], dst_ref=outs[b].at[block, rows(b, core)],
                send_sem=send.at[b, 3 + j], recv_sem=recv.at[b, 3 + j], device_id=(x, y, 1 - c), device_id_type=MESH)

        pairs = [(b, j) for b in range(n) for j in range(3)]
        source = lambda j: 2 * chips[j][0] + chips[j][1]

        def first():
            for b, j in pairs:
                over_ici(b, j, me).start()

        def mid():
            for b, j in pairs:
                over_ici(b, j, source(j)).wait_recv()
                if split[b]:
                    over_d2d(b, j, source(j), c).start()

        def last():
            for b, j in pairs:
                if split[b]:
                    over_d2d(b, j, source(j), 1 - c).wait_recv()
            for b, j in pairs:
                over_ici(b, j, me).wait_send()
                if split[b]:
                    over_d2d(b, j, source(j), c).wait_send()

        return first, mid, last

    return ([jax.ShapeDtypeStruct((4,) + b.shape, b.dtype) for b in bufs],
            [pltpu.SemaphoreType.DMA((n, 6)), pltpu.SemaphoreType.DMA((n, 6))], phases)


def _exchange_plan(n, out_shapes, copy):
    def phases(ins, outs, sems):
        send, recv = sems
        place = _place()

        def first():
            for b in range(n):
                copy(b, ins, outs, send, recv, place).start()

        def last():
            for b in range(n):
                copy(b, ins, outs, send, recv, place).wait()

        return first, (lambda: None), last

    return out_shapes, [pltpu.SemaphoreType.DMA((n,)), pltpu.SemaphoreType.DMA((n,))], phases


def _swap_plan(bufs):
    def copy(b, ins, outs, send, recv, place):
        x, y, c, _ = place
        h = bufs[b].shape[1] // 2
        return pltpu.make_async_remote_copy(
            src_ref=ins[b].at[:, pl.ds((1 - c) * h, h)], dst_ref=outs[b], send_sem=send.at[b], recv_sem=recv.at[b],
            device_id=(x, y, 1 - c), device_id_type=MESH)

    shapes = [jax.ShapeDtypeStruct((b.shape[0], b.shape[1] // 2, b.shape[2]), b.dtype) for b in bufs]
    return _exchange_plan(len(bufs), shapes, copy)


def _scatter_plan(bufs):
    def copy(t, ins, outs, send, recv, place):
        x, y, c, chips = place
        b, j = divmod(t, 3)
        px, py = chips[j]
        return pltpu.make_async_remote_copy(
            src_ref=ins[b].at[2 * px + py], dst_ref=outs[b].at[j], send_sem=send.at[t], recv_sem=recv.at[t],
            device_id=(px, py, c), device_id_type=MESH)

    shapes = [jax.ShapeDtypeStruct((3,) + b.shape[1:], b.dtype) for b in bufs]
    return _exchange_plan(3 * len(bufs), shapes, copy)


def _join_plan(halves):
    def copy(b, ins, outs, send, recv, place):
        x, y, c, _ = place
        return pltpu.make_async_remote_copy(
            src_ref=ins[b], dst_ref=outs[b], send_sem=send.at[b], recv_sem=recv.at[b],
            device_id=(x, y, 1 - c), device_id_type=MESH)

    return _exchange_plan(len(halves), [jax.ShapeDtypeStruct(b.shape, b.dtype) for b in halves], copy)


def _run_plan(arrays, plan, name):
    out_shapes, sems, phases = plan
    n, m = len(arrays), len(out_shapes)

    def body(*refs):
        for phase in phases(refs[:n], refs[n:n + m], refs[n + m:]):
            phase()

    return pl.pallas_call(
        body, name=name, in_specs=[_HBM] * n, out_specs=[_HBM] * m, out_shape=out_shapes, scratch_shapes=sems,
        compiler_params=pltpu.CompilerParams(has_side_effects=True),
    )(*arrays)


def _fit_rows(n, target):
    for q in (2 * HALO, HALO):
        for t in range(min(n, target) // q * q, 0, -q):
            if n % t == 0:
                return t
    best = None
    for t in range(HALO, min(n, target) + 1, HALO):
        if n % t == 0:
            best = t
    assert best is not None, (n, target)
    return best


def _allreduce_small(buf, name):
    R, lanes = buf.shape

    def body(in_ref, out_ref, land, send, recv):
        x, y, c, _ = _place()
        me = 4 * x + 2 * y + c
        land[me] = in_ref[...]
        cps = []
        for r in range(1, 8):
            px, py, pc = x ^ (r >> 2), y ^ ((r >> 1) & 1), c ^ (r & 1)
            cp = pltpu.make_async_remote_copy(
                src_ref=in_ref, dst_ref=land.at[me], send_sem=send.at[r - 1], recv_sem=recv.at[me],
                device_id=(px, py, pc), device_id_type=MESH)
            cp.start()
            cps.append(cp)
        for r in range(1, 8):
            peer = 4 * (x ^ (r >> 2)) + 2 * (y ^ ((r >> 1) & 1)) + (c ^ (r & 1))
            pltpu.make_async_remote_copy(
                src_ref=in_ref, dst_ref=land.at[peer], send_sem=send.at[r - 1], recv_sem=recv.at[peer],
                device_id=(x, y, c), device_id_type=MESH).wait_recv()
        for cp in cps:
            cp.wait_send()
        acc = land[0]
        for d in range(1, 8):
            acc = acc + land[d]
        out_ref[...] = acc

    vm = pl.BlockSpec(memory_space=pltpu.VMEM)
    return pl.pallas_call(
        body, name=name, in_specs=[vm], out_specs=vm, out_shape=jax.ShapeDtypeStruct((R, lanes), buf.dtype),
        scratch_shapes=[pltpu.VMEM((8, R, lanes), buf.dtype), pltpu.SemaphoreType.DMA((7,)), pltpu.SemaphoreType.DMA((8,))],
        compiler_params=pltpu.CompilerParams(has_side_effects=True, vmem_limit_bytes=VMEM_LIMIT),
    )(buf)


ROW_BLOCK = 256
SB_BLOCK = 256
MM_TM, MM_TN, MM_TK = 1024, 512, 512
FFN_COLS = 512


def _lane_pad(vec, start):
    return jnp.pad(vec, ((0, 0), (start, LANES - start - vec.shape[1])))


def _local_step(x, target, wt):
    T, D = x.shape
    W = D // 2
    H = W // HEAD_DIM
    F = wt["w_down"].shape[0]
    bt = min(ROW_BLOCK, T)
    blk = min(SB_BLOCK, T)
    w_in = wt["w_in"]
    w_sb, w_dn = w_in[:, :3 * W], w_in[:, 3 * W:7 * W]
    w_ba = jnp.pad(w_in[:, 7 * W:], ((0, 0), (0, LANES - 2 * H)))
    w_out, w_up, w_down = wt["w_out"], wt["w_up"], wt["w_down"]
    a_log, dt_bias = _lane_pad(wt["dn_a_log"], H), _lane_pad(wt["dn_dt_bias"], H)
    mm = functools.partial(_mm, tm=MM_TM, tn=MM_TN)

    xn = _rms_fwd(x, wt["ln_mix_pre"], None, BF16, bt, "rms_mix_pre")
    psb = mm([(xn, w_sb, D)], "nn", BF16, name="proj_sb")
    pdn = mm([(xn, w_dn, D)], "nn", F32, name="proj_dn")
    pba = mm([(xn, w_ba, D)], "nn", F32, name="proj_ba")
    o_sb, mix_sb, lt = _sb_fwd(psb, wt["sb_out_gain"], blk, "sb_fwd")
    qn, kn, vv, bx, gx = _dn_pre_fwd(pdn, pba, wt["dn_conv_w"], a_log, dt_bias, bt, "dn_pre_fwd")
    o_dn, mix_dn, ss = _dn_core_fwd(qn, kn, vv, bx, gx, pdn, wt["dn_out_gain"], "dn_core_fwd")
    m = mm([(mix_sb, w_out[:W], MM_TK), (mix_dn, w_out[W:], MM_TK)], "nn", F32, name="out_proj")
    h = _rms_fwd(m, wt["ln_mix_post"], x, F32, bt, "rms_mix_post")
    hn = _rms_fwd(h, wt["ln_ffn_pre"], None, BF16, bt, "rms_ffn_pre")
    up = mm([(hn, w_up, D)], "nn", F32, name="ffn_up")
    bc = min(FFN_COLS, F)
    act = _ffn_mid_fwd(up, wt["ffn_conv_w"], wt["ffn_conv_b"], bt, bc, "ffn_mid_fwd")
    f = mm([(act, w_down, MM_TK)], "nn", F32, name="ffn_down")
    dy, df, g_ffn_post, sq = _loss_head(f, wt["ln_ffn_post"], h, target, bt, "loss_head")
    loss = 0.5 * jnp.sum(sq) / D

    da = mm([(df, w_down, D)], "nt", F32, name="d_act")
    g_w_down = mm([(act, df, MM_TK)], "tn", F32, name="g_w_down")
    dug, duv, dwg, dwv, dbg, dbv = _ffn_mid_bwd(up, wt["ffn_conv_w"], wt["ffn_conv_b"], da, bt, bc, "ffn_mid_bwd")
    dhn = mm([(dug, w_up[:, :F], MM_TK), (duv, w_up[:, F:], MM_TK)], "nt", F32, name="d_hn")
    g_w_up = jnp.concatenate([mm([(hn, dug, MM_TK)], "tn", F32, name="g_w_up_gate"),
                              mm([(hn, duv, MM_TK)], "tn", F32, name="g_w_up_val")], axis=1)
    dh, g_ffn_pre = _rms_bwd(h, wt["ln_ffn_pre"], dhn, dy, F32, bt, "rms_ffn_pre_bwd")
    dm, g_mix_post = _rms_bwd(m, wt["ln_mix_post"], dh, None, BF16, bt, "rms_mix_post_bwd")
    dmix = mm([(dm, w_out, D)], "nt", F32, name="d_mix")
    g_w_out = jnp.concatenate([mm([(mix_sb, dm, MM_TK)], "tn", F32, name="g_w_out_sb"),
                               mm([(mix_dn, dm, MM_TK)], "tn", F32, name="g_w_out_dn")], axis=0)
    do_sb, g_sb_gain = _headnorm_bwd(o_sb, wt["sb_out_gain"], dmix, bt, "sb_norm_bwd")
    dq, dk, dv = _sb_bwd(psb, do_sb, lt, blk, "sb_bwd")
    ddq, ddk, ddv, dbx, dgx, dz, g_dn_gain = _dn_core_bwd(qn, kn, vv, bx, gx, pdn, wt["dn_out_gain"], o_dn, dmix, ss,
                                                         W, "dn_core_bwd")
    dconv, dba, g_dn_conv, g_a_log, g_dt_bias = _dn_pre_bwd(pdn, pba, wt["dn_conv_w"], a_log, dt_bias,
                                                            ddq, ddk, ddv, dbx, dgx, bt, "dn_pre_bwd")
    pieces = [(dq, w_in[:, :W]), (dk, w_in[:, W:2 * W]), (dv, w_in[:, 2 * W:3 * W]), (dconv, w_in[:, 3 * W:6 * W]),
              (dz, w_in[:, 6 * W:7 * W]), (dba, w_ba)]
    dxn = mm([(d, wp, MM_TK) for d, wp in pieces], "nt", F32, name="d_xn")
    g_w_in = jnp.concatenate([mm([(xn, d, MM_TK)], "tn", F32, name=f"g_w_in_{i}") for i, (d, _) in enumerate(pieces)],
                             axis=1)[:, :7 * W + 2 * H]
    dx, g_mix_pre = _rms_bwd(x, wt["ln_mix_pre"], dxn, dh, F32, bt, "rms_mix_pre_bwd")

    grads = dict(
        w_in=g_w_in, sb_out_gain=g_sb_gain, dn_conv_w=g_dn_conv, dn_a_log=g_a_log[:, H:2 * H],
        dn_dt_bias=g_dt_bias[:, H:2 * H], dn_out_gain=jnp.sum(g_dn_gain, axis=0), w_out=g_w_out,
        ln_mix_pre=g_mix_pre, ln_mix_post=g_mix_post, w_up=g_w_up,
        ffn_conv_w=jnp.concatenate([dwg, dwv], axis=1), ffn_conv_b=jnp.concatenate([dbg, dbv], axis=1),
        w_down=g_w_down, ln_ffn_pre=g_ffn_pre, ln_ffn_post=g_ffn_post)
    return loss, dx, grads


WEIGHTS = ("w_in", "sb_out_gain", "dn_conv_w", "dn_a_log", "dn_dt_bias", "dn_out_gain", "w_out", "ln_mix_pre",
           "ln_mix_post", "w_up", "ffn_conv_w", "ffn_conv_b", "w_down", "ln_ffn_pre", "ln_ffn_post")
MATRICES = {"w_in": 1, "w_out": 0, "w_up": 1, "w_down": 0}
CONV_SHARDED = ("dn_conv_w", "ffn_conv_w")
SMALL = tuple(n for n in WEIGHTS if n not in MATRICES)
N_CHIPS = 4
ROW_QUANTUM = 32
ADD_ROWS = 2048
ADAM_ROWS = 128


def _pack(arrs, quantum):
    rows, layout, off = [], [], 0
    for a in arrs:
        n = int(np.prod(a.shape))
        r = -(-n // LANES)
        r = -(-r // HALO) * HALO
        rows.append(jnp.pad(a.reshape(-1), (0, r * LANES - n)).reshape(r, LANES))
        layout.append((off, r, n, a.shape))
        off += r
    total = -(-off // quantum) * quantum
    if total > off:
        rows.append(jnp.zeros((total - off, LANES), rows[0].dtype))
    return jnp.concatenate(rows, axis=0), layout


def _unpack(packed, layout):
    return [packed[off:off + r].reshape(-1)[:n].reshape(shape) for off, r, n, shape in layout]


def _kernel_packed(x, w_in, sb_out_gain, dn_conv_w, dn_a_log, dn_dt_bias, dn_out_gain, w_out, ln_mix_pre, ln_mix_post, w_up, ffn_conv_w, ffn_conv_b, w_down, ln_ffn_pre, ln_ffn_post, loss_target, m_w_in, m_sb_out_gain, m_dn_conv_w, m_dn_a_log, m_dn_dt_bias, m_dn_out_gain, m_w_out, m_ln_mix_pre, m_ln_mix_post, m_w_up, m_ffn_conv_w, m_ffn_conv_b, m_w_down, m_ln_ffn_pre, m_ln_ffn_post, v_w_in, v_sb_out_gain, v_dn_conv_w, v_dn_a_log, v_dn_dt_bias, v_dn_out_gain, v_w_out, v_ln_mix_pre, v_ln_mix_post, v_w_up, v_ffn_conv_w, v_ffn_conv_b, v_w_down, v_ln_ffn_pre, v_ln_ffn_post):
    given = dict(locals())
    wl = {n: given[n][0] for n in WEIGHTS}
    ml = {n: given["m_" + n][0] for n in WEIGHTS}
    vl = {n: given["v_" + n][0] for n in WEIGHTS}
    for d in (wl, ml, vl):
        for n in SMALL:
            if d[n].ndim == 1:
                d[n] = d[n][None]
    cx, cy, cc = lax.axis_index("x"), lax.axis_index("y"), lax.axis_index("c")
    chip = 2 * cx + cy

    mats, mat_layout = _pack([wl[n].astype(BF16) for n in MATRICES], ROW_QUANTUM)
    taps, tap_layout = _pack([wl[n] for n in CONV_SHARDED], ROW_QUANTUM)
    all_mats, all_taps = _gather_xy([mats, taps], "gather_weights")
    wt = {n: wl[n] for n in SMALL}
    for i, n in enumerate(MATRICES):
        wt[n] = jnp.concatenate([_unpack(all_mats[k], mat_layout)[i] for k in range(N_CHIPS)], axis=MATRICES[n])
    for i, n in enumerate(CONV_SHARDED):
        wt[n] = jnp.concatenate([_unpack(all_taps[k], tap_layout)[i] for k in range(N_CHIPS)], axis=1)

    loss, dx, grads = _local_step(x[0], loss_target[0], wt)
    loss = lax.psum(loss, ("x", "y", "c"))

    def shard_of(n, k):
        g, axis = grads[n], MATRICES[n]
        size = g.shape[axis] // N_CHIPS
        return lax.slice_in_dim(g, k * size, (k + 1) * size, axis=axis)

    packed = [_pack([shard_of(n, k) for n in MATRICES], ROW_QUANTUM) for k in range(N_CHIPS)]
    glayout = packed[0][1]
    gp = jnp.stack([p[0] for p in packed])
    half = gp.shape[1] // 2
    from_sibling = _swap_half_c(gp, "grad_swap_cores")
    chip_sum = _add_half(gp.reshape(N_CHIPS, 2, half, LANES), from_sibling, cc.reshape(1), ADD_ROWS, "grad_add_cores")
    from_chips = _scatter_xy(chip_sum, "grad_scatter_chips")
    reduced_half = _add_four(chip_sum, chip.reshape(1), from_chips, ADD_ROWS, "grad_add_chips")
    reduced = _join_c(reduced_half, "grad_join_cores").reshape(2 * half, LANES)
    gl = dict(zip(MATRICES, _unpack(reduced, glayout)))

    small, small_layout = _pack([grads[n] for n in SMALL], HALO)
    small = _allreduce_small(small, "grad_allreduce_small")
    for n, g in zip(SMALL, _unpack(small, small_layout)):
        if n in CONV_SHARDED:
            size = g.shape[1] // N_CHIPS
            g = lax.dynamic_slice_in_dim(g, chip * size, size, axis=1)
        gl[n] = g

    delta, new_m, new_v = {}, {}, {}
    for n, mine_half, sibling_half in zip(names, reduced, siblings):
        gl[n], delta[n], new_m[n], new_v[n] = _adamw_halves(wl[n], mine_half, sibling_half, place, ml[n], vl[n],
                                                            ADAM_ROWS, "adamw_" + n)
    packs = [_pack([d[n] for n in SMALL], HALO) for d in (wl, gl, ml, vl)]
    outs = _adamw(*[p[0] for p in packs], ADAM_ROWS, "adamw_small")
    for res, o in zip((delta, new_m, new_v), outs):
        res.update(zip(SMALL, _unpack(o, packs[0][1])))

    shaped = lambda d: [d[n].reshape(given[n].shape) for n in WEIGHTS]
    return (loss, dx[None], *shaped(gl), *shaped(delta), *shaped(new_m), *shaped(new_v))


UP_TILE = 1408
PAIR_ROWS = 256


def _reduce_to_chips(shares, place, names, swap_on, scatter_on):
    from_sibling = swap_on(shares, _swap_plan(shares))
    halves = [s.reshape(N_CHIPS, 2, s.shape[1] // 2, s.shape[2]) for s in shares]
    to_chips = [_add_cores(hv, fs, place, False, BF16, PAIR_ROWS, "grad_add_cores_" + n)
                for hv, fs, n in zip(halves, from_sibling, names)]
    own = [_add_cores(hv, fs, place, True, F32, PAIR_ROWS, "grad_add_cores_own_" + n)
           for hv, fs, n in zip(halves, from_sibling, names)]
    return own, scatter_on(to_chips, _scatter_plan(to_chips))


def _step(x, target, wt, late, chip, place):
    T, D = x.shape
    W = D // 2
    H = W // HEAD_DIM
    bt = min(ROW_BLOCK, T)
    blk = min(SB_BLOCK, T)
    w_in = wt["w_in"]
    a_log, dt_bias = _lane_pad(wt["dn_a_log"], H), _lane_pad(wt["dn_dt_bias"], H)
    mm = functools.partial(_mm, tm=MM_TM, tn=MM_TN)
    wide = functools.partial(_mm, tm=MM_TM, tn=2 * MM_TN)
    mm_up = functools.partial(_mm, tm=MM_TM, tn=UP_TILE)
    one = lambda a, b, tk=MM_TK: [(a, b, tk, 0, 0)]

    xn = _rms_fwd(x, wt["ln_mix_pre"], None, BF16, bt, "rms_mix_pre")
    psb = mm(one(xn, w_in, D), "nn", BF16, name="proj_sb", n_window=(0, 3 * W))
    pdn = mm(one(xn, w_in, D), "nn", F32, name="proj_dn", n_window=(3 * W, 4 * W))
    pba = mm(one(xn, w_in, D), "nn", F32, name="proj_ba", n_window=(7 * W, LANES))
    late_names = ("w_out", "w_up", "w_down")
    mine = [late[n] for n in late_names]
    (o_sb, mix_sb, lt), theirs = _sb_fwd(psb, wt["sb_out_gain"], blk, "sb_fwd",
                                         _Comm(mine, _gather_plan(mine, [True] * len(mine))))
    w_out, w_up, w_down = [lax.dynamic_update_index_in_dim(t, s, chip, 0) for t, s in zip(theirs, mine)]
    w_out, w_down = w_out.reshape(-1, D), w_down.reshape(-1, D)
    F = w_down.shape[0]
    qn, kn, vv, bx, gx = _dn_pre_fwd(pdn, pba, wt["dn_conv_w"], a_log, dt_bias, bt, "dn_pre_fwd")
    o_dn, mix_dn, ss, tms = _dn_core_fwd(qn, kn, vv, bx, gx, pdn, wt["dn_out_gain"], "dn_core_fwd")
    m = wide([(mix_sb, w_out, 2 * MM_TK, 0, 0), (mix_dn, w_out, 2 * MM_TK, 0, W)], "nn", F32, name="out_proj")
    h = _rms_fwd(m, wt["ln_mix_post"], x, F32, bt, "rms_mix_post")
    hn = _rms_fwd(h, wt["ln_ffn_pre"], None, BF16, bt, "rms_ffn_pre")
    up = mm_up(one(hn, w_up, D), "nn", F32, name="ffn_up")
    bc = min(FFN_COLS, F)
    act = _ffn_mid_fwd(up, wt["ffn_conv_w"], wt["ffn_conv_b"], bt, bc, "ffn_mid_fwd")
    f = mm(one(act, w_down, UP_TILE), "nn", F32, name="ffn_down")
    dy, df, g_ffn_post, sq = _loss_head(f, wt["ln_ffn_post"], h, target, bt, "loss_head")
    loss = 0.5 * jnp.sum(sq) / D

    da = mm(one(df, w_down, D), "nt", F32, name="d_act")
    g_w_down = _mm(one(act, df, 2 * MM_TK), "tn", F32, tm=UP_TILE, tn=2 * MM_TN, name="g_w_down")
    dug, duv, dwg, dwv, dbg, dbv = _ffn_mid_bwd(up, wt["ffn_conv_w"], wt["ffn_conv_b"], da, bt, bc, "ffn_mid_bwd")
    dhn = wide([(dug, w_up, UP_TILE, 0, 0), (duv, w_up, UP_TILE, 0, F)], "nt", F32, name="d_hn")
    shard = w_up.shape[2]
    g_w_up = mm_up(one(hn, dug, 2 * MM_TK), "tn", F32, name="g_w_up_gate", out_shard=shard,
                   into=(lax.empty(w_up.shape, F32), 0))
    g_w_up = mm_up(one(hn, duv, 2 * MM_TK), "tn", F32, name="g_w_up_val", out_shard=shard, into=(g_w_up, F))
    dh, g_ffn_pre = _rms_bwd(h, wt["ln_ffn_pre"], dhn, dy, F32, bt, "rms_ffn_pre_bwd")
    dm, g_mix_post = _rms_bwd(m, wt["ln_mix_post"], dh, None, BF16, bt, "rms_mix_post_bwd")
    dmix = mm(one(dm, w_out, D), "nt", F32, name="d_mix")
    g_w_out = jnp.concatenate([wide(one(mix_sb, dm, 2 * MM_TK), "tn", F32, name="g_w_out_sb"),
                               wide(one(mix_dn, dm, 2 * MM_TK), "tn", F32, name="g_w_out_dn")], axis=0)
    shares = [g_w_out.reshape(N_CHIPS, -1, D), g_w_up, g_w_down.reshape(N_CHIPS, -1, D)]
    carried = {}

    def swap_on(arrays, plan):
        (carried["do_sb"], carried["g_sb_gain"]), out = _headnorm_bwd(o_sb, wt["sb_out_gain"], dmix, bt, "sb_norm_bwd",
                                                                    _Comm(arrays, plan))
        return out

    def scatter_on(arrays, plan):
        carried["dqkv"], out = _sb_bwd(psb, carried["do_sb"], lt, blk, "sb_bwd", _Comm(arrays, plan))
        return out

    early = _reduce_to_chips(shares, place, late_names, swap_on, scatter_on)
    g_sb_gain = carried["g_sb_gain"]
    dq, dk, dv = carried["dqkv"]
    ddq, ddk, ddv, dbx, dgx, dz, g_dn_gain = _dn_core_bwd(qn, kn, vv, bx, gx, pdn, wt["dn_out_gain"], o_dn, dmix, ss,
                                                         tms, W, "dn_core_bwd")
    dconv, dba, g_dn_conv, g_a_log, g_dt_bias = _dn_pre_bwd(pdn, pba, wt["dn_conv_w"], a_log, dt_bias,
                                                            ddq, ddk, ddv, dbx, dgx, bt, "dn_pre_bwd")
    pieces = [(dq, 0), (dk, W), (dv, 2 * W), (dconv, 3 * W), (dz, 6 * W), (dba, 7 * W)]
    g_w_in = [wide(one(xn, d, 2 * MM_TK), "tn", F32, name=f"g_w_in_{i}") for i, (d, _) in enumerate(pieces)]
    g_w_in[-1] = g_w_in[-1][:, :2 * H]
    g_in = jnp.concatenate(g_w_in, axis=1)

    def with_d_xn(arrays, plan):
        carried["dxn"], out = mm([(d, w_in, MM_TK, 0, k0) for d, k0 in pieces], "nt", F32, name="d_xn",
                                 comm=_Comm(arrays, plan))
        return out

    last = _reduce_to_chips([g_in.reshape(D, N_CHIPS, -1).transpose(1, 0, 2)], place, ["w_in"],
                            lambda arrays, plan: _run_plan(arrays, plan, "grad_swap_cores"), with_d_xn)
    dx, g_mix_pre = _rms_bwd(x, wt["ln_mix_pre"], carried["dxn"], dh, F32, bt, "rms_mix_pre_bwd")
    exchanged = dict(zip(late_names, zip(*early)))
    exchanged["w_in"] = (last[0][0], last[1][0])

    grads = dict(
        sb_out_gain=g_sb_gain, dn_conv_w=g_dn_conv, dn_a_log=g_a_log[:, H:2 * H],
        dn_dt_bias=g_dt_bias[:, H:2 * H], dn_out_gain=jnp.sum(g_dn_gain, axis=0),
        ln_mix_pre=g_mix_pre, ln_mix_post=g_mix_post,
        ffn_conv_w=jnp.concatenate([dwg, dwv], axis=1), ffn_conv_b=jnp.concatenate([dbg, dbv], axis=1),
        ln_ffn_pre=g_ffn_pre, ln_ffn_post=g_ffn_post)
    return loss, dx, grads, exchanged


def kernel(x, w_in, sb_out_gain, dn_conv_w, dn_a_log, dn_dt_bias, dn_out_gain, w_out, ln_mix_pre, ln_mix_post, w_up, ffn_conv_w, ffn_conv_b, w_down, ln_ffn_pre, ln_ffn_post, loss_target, m_w_in, m_sb_out_gain, m_dn_conv_w, m_dn_a_log, m_dn_dt_bias, m_dn_out_gain, m_w_out, m_ln_mix_pre, m_ln_mix_post, m_w_up, m_ffn_conv_w, m_ffn_conv_b, m_w_down, m_ln_ffn_pre, m_ln_ffn_post, v_w_in, v_sb_out_gain, v_dn_conv_w, v_dn_a_log, v_dn_dt_bias, v_dn_out_gain, v_w_out, v_ln_mix_pre, v_ln_mix_post, v_w_up, v_ffn_conv_w, v_ffn_conv_b, v_w_down, v_ln_ffn_pre, v_ln_ffn_post):
    given = dict(locals())
    wl = {n: given[n][0] for n in WEIGHTS}
    ml = {n: given["m_" + n][0] for n in WEIGHTS}
    vl = {n: given["v_" + n][0] for n in WEIGHTS}
    for d in (wl, ml, vl):
        for n in SMALL:
            if d[n].ndim == 1:
                d[n] = d[n][None]
    cx, cy, cc = lax.axis_index("x"), lax.axis_index("y"), lax.axis_index("c")
    chip = 2 * cx + cy
    D = x.shape[2]
    W = D // 2

    first = ("w_in",) + CONV_SHARDED
    mine = [wl["w_in"].astype(BF16)] + [wl[n] for n in CONV_SHARDED]
    theirs = _run_plan(mine, _gather_plan(mine, [True, False, False]), "gather_w_in")
    got = {n: lax.dynamic_update_index_in_dim(t, s, chip, 0) for n, t, s in zip(first, theirs, mine)}
    columns = lambda g: g.transpose(1, 0, 2).reshape(g.shape[1], N_CHIPS * g.shape[2])
    wt = {n: wl[n] for n in SMALL}
    w_in_all = columns(got["w_in"])
    wt["w_in"] = jnp.pad(w_in_all, ((0, 0), (0, 7 * W + LANES - w_in_all.shape[1])))
    for n in CONV_SHARDED:
        wt[n] = columns(got[n])
    late = {n: wl[n].astype(BF16) for n in ("w_out", "w_up", "w_down")}

    place = jnp.stack([cc, chip]).astype(jnp.int32)
    loss, dx, grads, exchanged = _step(x[0], loss_target[0], wt, late, chip, place)
    loss = lax.psum(loss, ("x", "y", "c"))

    names = list(MATRICES)
    reduced = [_add_chips(*exchanged[n], PAIR_ROWS, "grad_add_chips_" + n) for n in names]
    siblings = _run_plan(reduced, _join_plan(reduced), "grad_join_cores")
    gl = {}

    small, small_layout = _pack([grads[n] for n in SMALL], HALO)
    small = _allreduce_small(small, "grad_allreduce_small")
    for n, g in zip(SMALL, _unpack(small, small_layout)):
        if n in CONV_SHARDED:
            size = g.shape[1] // N_CHIPS
            g = lax.dynamic_slice_in_dim(g, chip * size, size, axis=1)
        gl[n] = g

    delta, new_m, new_v = {}, {}, {}
    for n, mine_half, sibling_half in zip(names, reduced, siblings):
        gl[n], delta[n], new_m[n], new_v[n] = _adamw_halves(wl[n], mine_half, sibling_half, place, ml[n], vl[n],
                                                            ADAM_ROWS, "adamw_" + n)
    packs = [_pack([d[n] for n in SMALL], HALO) for d in (wl, gl, ml, vl)]
    outs = _adamw(*[p[0] for p in packs], ADAM_ROWS, "adamw_small")
    for res, o in zip((delta, new_m, new_v), outs):
        res.update(zip(SMALL, _unpack(o, packs[0][1])))

    shaped = lambda d: [d[n].reshape(given[n].shape) for n in WEIGHTS]
    return (loss, dx[None], *shaped(gl), *shaped(delta), *shaped(new_m), *shaped(new_v))
```

```python
import functools

import numpy as np
import jax
import jax.numpy as jnp
from jax import lax
from jax.experimental import pallas as pl
from jax.experimental.pallas import tpu as pltpu

F32 = jnp.float32
BF16 = jnp.bfloat16
HEAD_DIM = 128
CHUNK = 64
ROWS = 2 * CHUNK
EPS = 1e-6
EXP_UNDERFLOW = 110.0
LANES = 128
HALO = 8
VMEM_LIMIT = 48 * 1024 * 1024
ADAM_LR, ADAM_B1, ADAM_B2, ADAM_EPS, ADAM_WD, ADAM_STEP = 0.001, 0.9, 0.999, 1e-08, 0.01, 10
MESH = pl.DeviceIdType.MESH
HIGHEST = lax.Precision.HIGHEST

NN = (((1,), (0,)), ((), ()))
NT = (((1,), (1,)), ((), ()))
TN = (((0,), (0,)), ((), ()))


def _params(n_axes):
    return pltpu.CompilerParams(dimension_semantics=("arbitrary",) * n_axes, vmem_limit_bytes=VMEM_LIMIT)


def _bdot(a, b, dims=NN):
    return lax.dot_general(a.astype(BF16), b.astype(BF16), dims, preferred_element_type=F32)


def _split3(a):
    hi = a.astype(BF16)
    r1 = a - hi.astype(F32)
    mid = r1.astype(BF16)
    lo = (r1 - mid.astype(F32)).astype(BF16)
    return hi, mid, lo


def _dot3(a, sel, dims=NN):
    return sum(lax.dot_general(p, sel, dims, preferred_element_type=F32) for p in _split3(a))


def _dot3r(sel, a, dims=NN):
    return sum(lax.dot_general(sel, p, dims, preferred_element_type=F32) for p in _split3(a))


def _iota2(n, m):
    return lax.broadcasted_iota(jnp.int32, (n, m), 0), lax.broadcasted_iota(jnp.int32, (n, m), 1)


def _sigmoid(x):
    return 1.0 / (1.0 + jnp.exp(-x))


def _softplus(x):
    return jnp.maximum(x, 0.0) + jnp.log(1.0 + jnp.exp(-jnp.abs(x)))


def _fit(values, target):
    values = [v for v in (values if isinstance(values, (list, tuple)) else [values]) if v]
    best = None
    for t in range(LANES, min(min(values), target) + 1, LANES):
        if all(v % t == 0 for v in values):
            best = t
    assert best is not None, (values, target)
    return best


def _mm(parts, mode, out_dtype, tm, tn, name, n_window=None, out_shard=None, into=None, comm=None):
    dims = {"nn": NN, "nt": NT, "tn": TN}[mode]
    a0, b0 = parts[0][0], parts[0][1]
    b3 = b0.ndim == 3
    shard_c = b0.shape[2] if b3 else None
    M = a0.shape[1] if mode == "tn" else a0.shape[0]
    if mode == "nt":
        n_full = b0.shape[1] if b3 else b0.shape[0]
    else:
        n_full = b0.shape[0] * b0.shape[2] if b3 else b0.shape[1]
    n0, N = n_window if n_window is not None else (0, n_full)
    out_n0 = into[1] if into is not None else 0
    tm = _fit(M, tm)
    tn = _fit([N, n0, out_n0, out_shard, shard_c if mode != "nt" else None], tn)
    specs_a, specs_b, offs, nks = [], [], [], []
    off = 0
    for a, b, tk, a_k0, b_k0 in parts:
        K = a.shape[0] if mode == "tn" else a.shape[1]
        tk = _fit([K, a_k0, b_k0, shard_c if mode == "nt" else None], tk)
        nk = K // tk
        kk = lambda k, o=off, n=nk: jnp.clip(k - o, 0, n - 1)
        ao, bo, no = a_k0 // tk, b_k0 // tk, n0 // tn
        if mode == "tn":
            specs_a.append(pl.BlockSpec((tk, tm), lambda i, j, k, kk=kk, ao=ao: (kk(k) + ao, i)))
        else:
            specs_a.append(pl.BlockSpec((tm, tk), lambda i, j, k, kk=kk, ao=ao: (i, kk(k) + ao)))
        if mode == "nt":
            if b3:
                per = shard_c // tk
                specs_b.append(pl.BlockSpec((None, tn, tk), lambda i, j, k, kk=kk, bo=bo, per=per:
                                            ((kk(k) + bo) // per, j, (kk(k) + bo) % per)))
            else:
                specs_b.append(pl.BlockSpec((tn, tk), lambda i, j, k, kk=kk, bo=bo: (j, kk(k) + bo)))
        else:
            if b3:
                per = shard_c // tn
                specs_b.append(pl.BlockSpec((None, tk, tn), lambda i, j, k, kk=kk, bo=bo, no=no, per=per:
                                            ((j + no) // per, kk(k) + bo, (j + no) % per)))
            else:
                specs_b.append(pl.BlockSpec((tk, tn), lambda i, j, k, kk=kk, bo=bo, no=no: (kk(k) + bo, j + no)))
        offs.append(off)
        nks.append(nk)
        off += nk
    nk_total = off
    n_parts = len(parts)

    comm = comm if comm is not None else _Comm()
    grid = (M // tm, N // tn, nk_total)
    n_in = 2 * n_parts + (1 if into is not None else 0)

    def body(*refs):
        ins, (o_ref,), scratch, (first, mid, last) = comm.split(refs, n_in, 1, 0 if nk_total == 1 else 1)
        a_refs, b_refs = ins[:n_parts], ins[n_parts:2 * n_parts]
        at = lambda step: functools.reduce(lambda x, y: x & y, [pl.program_id(d) == step[d] for d in range(3)])
        pl.when(at((0, 0, 0)))(first)
        pl.when(at((grid[0] // 2, 0, 0)))(mid)
        if nk_total == 1:
            o_ref[...] = _bdot(a_refs[0][...], b_refs[0][...], dims).astype(out_dtype)
        else:
            acc = scratch[0]
            k = pl.program_id(2)

            @pl.when(k == 0)
            def _():
                acc[...] = jnp.zeros_like(acc)

            for p in range(n_parts):
                @pl.when((k >= offs[p]) & (k < offs[p] + nks[p]))
                def _(p=p):
                    acc[...] += _bdot(a_refs[p][...], b_refs[p][...], dims)

            @pl.when(k == nk_total - 1)
            def _():
                o_ref[...] = acc[...].astype(out_dtype)
        pl.when(at(tuple(g - 1 for g in grid)))(last)

    jo = out_n0 // tn
    if out_shard is not None:
        per_o = out_shard // tn
        out_spec = pl.BlockSpec((None, tm, tn), lambda i, j, k: ((j + jo) // per_o, i, (j + jo) % per_o))
        out_shape = jax.ShapeDtypeStruct((N // out_shard, M, out_shard), out_dtype)
    else:
        out_spec = pl.BlockSpec((tm, tn), lambda i, j, k: (i, j + jo))
        out_shape = jax.ShapeDtypeStruct((M, N), out_dtype)
    ins = [p[0] for p in parts] + [p[1] for p in parts]
    in_specs = specs_a + specs_b
    aliases = {}
    if into is not None:
        out_shape = jax.ShapeDtypeStruct(into[0].shape, into[0].dtype)
        aliases = {len(ins): 0}
        ins.append(into[0])
        in_specs.append(pl.BlockSpec(memory_space=pl.ANY))
    (out,), carried = comm.call(body, name, grid, in_specs, (out_spec,), (out_shape,),
                                [] if nk_total == 1 else [pltpu.VMEM((tm, tn), F32)], ins, aliases)
    return (out, carried) if comm.phases is not None else out


def _rms_fwd(x, gain, resid, out_dtype, bt, name):
    T, D = x.shape
    row = pl.BlockSpec((bt, D), lambda i: (i, 0))
    vec = pl.BlockSpec((1, D), lambda i: (0, 0))

    def body(*refs):
        x_ref, g_ref = refs[0], refs[1]
        o_ref = refs[-1]
        xv = x_ref[...]
        y = xv * lax.rsqrt(jnp.mean(xv * xv, axis=-1, keepdims=True) + EPS) * g_ref[...]
        if resid is not None:
            y = refs[2][...] + y
        o_ref[...] = y.astype(out_dtype)

    ins = [x, gain] + ([resid] if resid is not None else [])
    return pl.pallas_call(
        body, name=name, grid=(T // bt,),
        in_specs=[row, vec] + ([row] if resid is not None else []),
        out_specs=row, out_shape=jax.ShapeDtypeStruct((T, D), out_dtype), compiler_params=_params(1),
    )(*ins)


def _rms_bwd_math(xv, g, dy):
    r = lax.rsqrt(jnp.mean(xv * xv, axis=-1, keepdims=True) + EPS)
    n = xv * r
    gy = dy * g
    dx = r * (gy - n * jnp.mean(gy * n, axis=-1, keepdims=True))
    return dx, dy * n


def _rms_bwd(x, gain, dy, resid, out_dtype, bt, name):
    T, D = x.shape
    row = pl.BlockSpec((bt, D), lambda i: (i, 0))
    vec = pl.BlockSpec((1, D), lambda i: (0, 0))

    def body(*refs):
        x_ref, g_ref, dy_ref = refs[0], refs[1], refs[2]
        dx_ref, dg_ref = refs[-2], refs[-1]
        dx, dgp = _rms_bwd_math(x_ref[...], g_ref[...], dy_ref[...].astype(F32))
        if resid is not None:
            dx = refs[3][...] + dx
        dx_ref[...] = dx.astype(out_dtype)

        @pl.when(pl.program_id(0) == 0)
        def _():
            dg_ref[...] = jnp.zeros_like(dg_ref)

        dg_ref[...] += jnp.sum(dgp, axis=0, keepdims=True)

    ins = [x, gain, dy] + ([resid] if resid is not None else [])
    return pl.pallas_call(
        body, name=name, grid=(T // bt,),
        in_specs=[row, vec, row] + ([row] if resid is not None else []),
        out_specs=(row, vec),
        out_shape=(jax.ShapeDtypeStruct((T, D), out_dtype), jax.ShapeDtypeStruct((1, D), F32)),
        compiler_params=_params(1),
    )(*ins)


def _loss_head(f, gain, h, target, bt, name):
    T, D = f.shape
    row = pl.BlockSpec((bt, D), lambda i: (i, 0))
    vec = pl.BlockSpec((1, D), lambda i: (0, 0))

    def body(f_ref, g_ref, h_ref, t_ref, dy_ref, df_ref, dg_ref, sq_ref):
        fv, g = f_ref[...], g_ref[...]
        r = lax.rsqrt(jnp.mean(fv * fv, axis=-1, keepdims=True) + EPS)
        n = fv * r
        err = (h_ref[...] + n * g) - t_ref[...]
        dy = err * (1.0 / D)
        gy = dy * g
        df = r * (gy - n * jnp.mean(gy * n, axis=-1, keepdims=True))
        dy_ref[...] = dy
        df_ref[...] = df.astype(BF16)

        @pl.when(pl.program_id(0) == 0)
        def _():
            dg_ref[...] = jnp.zeros_like(dg_ref)
            sq_ref[...] = jnp.zeros_like(sq_ref)

        dg_ref[...] += jnp.sum(dy * n, axis=0, keepdims=True)
        sq_ref[...] += jnp.sum(err * err, axis=0, keepdims=True)

    return pl.pallas_call(
        body, name=name, grid=(T // bt,), in_specs=[row, vec, row, row], out_specs=(row, row, vec, vec),
        out_shape=(jax.ShapeDtypeStruct((T, D), F32), jax.ShapeDtypeStruct((T, D), BF16),
                   jax.ShapeDtypeStruct((1, D), F32), jax.ShapeDtypeStruct((1, D), F32)),
        compiler_params=_params(1),
    )(f, gain, h, target)


def _sb_logits(q, k, valid):
    z = lax.dot_general(q, k, NT, preferred_element_type=F32) * (HEAD_DIM ** -0.5)
    sp = jnp.log(1.0 + jnp.exp(-jnp.abs(z)))
    lb = jnp.minimum(z, 0.0) - sp
    l1 = -(jnp.maximum(z, 0.0) + sp)
    return lb, (l1 if valid is None else jnp.where(valid, l1, 0.0))


def _masked(valid, x):
    return x if valid is None else jnp.where(valid, x, 0.0)


def _heads_per_step(n_heads):
    return 2 if n_heads % 2 == 0 else 1


def _dot2(a, sel):
    hi = a.astype(BF16)
    lo = (a - hi.astype(F32)).astype(BF16)
    return jnp.dot(hi, sel, preferred_element_type=F32) + jnp.dot(lo, sel, preferred_element_type=F32)


class _Comm:
    def __init__(self, arrays=(), plan=((), (), None)):
        self.arrays = list(arrays)
        self.out_shapes, self.sems, self.phases = list(plan[0]), list(plan[1]), plan[2]

    def split(self, refs, n_in, n_out, n_scratch):
        a, o = len(self.arrays), len(self.out_shapes)
        cuts = np.cumsum([0, n_in, a, n_out, o, n_scratch])
        ins, cin, outs, cout, scratch = (refs[cuts[t]:cuts[t + 1]] for t in range(5))
        if self.phases is None:
            return ins, outs, scratch, (lambda: None,) * 3
        return ins, outs, scratch, self.phases(cin, cout, refs[cuts[5]:])

    def call(self, body, name, grid, in_specs, out_specs, out_shape, scratch_shapes, operands, aliases=None):
        outs = pl.pallas_call(
            body, name=name, grid=grid, in_specs=list(in_specs) + [_HBM] * len(self.arrays),
            out_specs=tuple(out_specs) + (_HBM,) * len(self.out_shapes),
            out_shape=tuple(out_shape) + tuple(self.out_shapes),
            scratch_shapes=list(scratch_shapes) + self.sems, input_output_aliases=aliases or {},
            compiler_params=pltpu.CompilerParams(dimension_semantics=("arbitrary",) * len(grid),
                                                 vmem_limit_bytes=VMEM_LIMIT, has_side_effects=self.phases is not None),
        )(*operands, *self.arrays)
        return outs[:len(out_shape)], outs[len(out_shape):]


def _sb_fwd(qkv, gain, blk, name, comm):
    T, W = qkv.shape[0], qkv.shape[1] // 3
    H = W // HEAD_DIM
    nq = T // blk
    hp = _heads_per_step(H)
    ng = H // hp
    lanes = [slice(t * HEAD_DIM, (t + 1) * HEAD_DIM) for t in range(hp)]

    def body(*refs):
        (q_ref, k_ref, v_ref, g_ref), (o_ref, mix_ref, lt_ref, swept_ref), _, (first, mid, last) = comm.split(refs, 4, 4, 0)
        h, i = pl.program_id(0), pl.program_id(1)
        pl.when((h == 0) & (i == 0))(first)
        pl.when((h == ng // 2) & (i == 0))(mid)
        q = [q_ref[:, ln] for ln in lanes]
        row, col = _iota2(blk, blk)
        after = (row > col).astype(BF16)

        def step(kb, carry, valid):
            ks = pl.ds(pl.multiple_of(kb * blk, blk), blk)
            out = []
            for t, (run, acc) in enumerate(carry):
                lb, l1 = _sb_logits(q[t], k_ref[ks, lanes[t]], valid)
                att = _masked(valid, jnp.exp(lb + _dot2(l1, after) + run))
                out.append((run + jnp.sum(l1, axis=1, keepdims=True), acc + _bdot(att, v_ref[ks, lanes[t]])))
            return tuple(out)

        zero = (jnp.zeros((blk, 1), F32), jnp.zeros((blk, HEAD_DIM), F32))
        def alive(state):
            jj, c = state
            return (jj < i) & (functools.reduce(jnp.maximum, [jnp.max(run) for run, _ in c]) > -EXP_UNDERFLOW)

        swept, carry = lax.while_loop(alive, lambda st: (st[0] + 1, step(i - 1 - st[0], st[1], None)),
                                      (jnp.int32(0), step(i, (zero,) * hp, col < row)))
        swept_ref[h, i] = swept
        for t, (run, o) in enumerate(carry):
            o_ref[:, lanes[t]] = o
            r = lax.rsqrt(jnp.mean(o * o, axis=-1, keepdims=True) + EPS)
            mix_ref[:, lanes[t]] = (o * r * g_ref[...]).astype(BF16)
            lt_ref[:, lanes[t]] = jnp.broadcast_to(run, (blk, HEAD_DIM))
        pl.when((h == ng - 1) & (i == nq - 1))(last)

    wide = hp * HEAD_DIM
    qb = pl.BlockSpec((blk, wide), lambda h, i: (i, h))
    return comm.call(
        body, name, (ng, nq),
        [qb, pl.BlockSpec((T, wide), lambda h, i: (0, ng + h)),
         pl.BlockSpec((T, wide), lambda h, i: (0, 2 * ng + h)), pl.BlockSpec((1, HEAD_DIM), lambda h, i: (0, 0))],
        (qb, qb, qb, pl.BlockSpec(memory_space=pltpu.SMEM)),
        (jax.ShapeDtypeStruct((T, W), F32), jax.ShapeDtypeStruct((T, W), BF16), jax.ShapeDtypeStruct((T, W), F32),
         jax.ShapeDtypeStruct((ng, nq), jnp.int32)),
        [], (qkv, qkv, qkv, gain))


def _headnorm_bwd(o, gain, dmix, bt, name, comm):
    T, W = o.shape
    H = W // HEAD_DIM
    nt = T // bt
    blk = pl.BlockSpec((bt, HEAD_DIM), lambda i, h: (i, h))
    vec = pl.BlockSpec((1, HEAD_DIM), lambda i, h: (0, 0))

    def body(*refs):
        (o_ref, g_ref, d_ref), (do_ref, dg_ref), _, (first, mid, last) = comm.split(refs, 3, 2, 0)
        i, h = pl.program_id(0), pl.program_id(1)
        pl.when((i == 0) & (h == 0))(first)
        pl.when((i == nt // 2) & (h == 0))(mid)
        do, dgp = _rms_bwd_math(o_ref[...], g_ref[...], d_ref[...])
        do_ref[...] = do

        @pl.when((i == 0) & (h == 0))
        def _():
            dg_ref[...] = jnp.zeros_like(dg_ref)

        dg_ref[...] += jnp.sum(dgp, axis=0, keepdims=True)
        pl.when((i == nt - 1) & (h == H - 1))(last)

    return comm.call(body, name, (nt, H), [blk, vec, blk], (blk, vec),
                     (jax.ShapeDtypeStruct((T, W), F32), jax.ShapeDtypeStruct((1, HEAD_DIM), F32)), [], (o, gain, dmix))


def _sb_bwd(qkv, do, lt, swept, blk, name, comm):
    T, W = qkv.shape[0], qkv.shape[1] // 3
    H = W // HEAD_DIM
    nq = T // blk
    scale = HEAD_DIM ** -0.5
    hp = _heads_per_step(H)
    ng = H // hp
    lanes = [slice(t * HEAD_DIM, (t + 1) * HEAD_DIM) for t in range(hp)]

    def body(*refs):
        (q_ref, k_ref, v_ref, do_ref, lt_ref, swept_ref), (dq_ref, dk_ref, dv_ref), _, (first, mid, last) = comm.split(
            refs, 6, 3, 0)
        h, i = pl.program_id(0), pl.program_id(1)
        pl.when((h == 0) & (i == 0))(first)
        pl.when((h == ng // 2) & (i == 0))(mid)

        @pl.when(i == 0)
        def _():
            dk_ref[...] = jnp.zeros_like(dk_ref)
            dv_ref[...] = jnp.zeros_like(dv_ref)

        q = [q_ref[:, ln] for ln in lanes]
        dob = [do_ref[:, ln].astype(BF16) for ln in lanes]
        total = [lt_ref[:, ln][:, :1] for ln in lanes]
        row, col = _iota2(blk, blk)
        upto = (row <= col).astype(BF16)
        before = (row < col).astype(BF16)

        def step(kb, carry, valid):
            ks = pl.ds(pl.multiple_of(kb * blk, blk), blk)
            out = []
            for t, (seen, psum, dq) in enumerate(carry):
                k, v = k_ref[ks, lanes[t]], v_ref[ks, lanes[t]]
                lb, l1 = _sb_logits(q[t], k, valid)
                later = total[t] - seen - _dot2(l1, upto)
                att = _masked(valid, jnp.exp(lb + later))
                p = att * lax.dot_general(dob[t], v, NT, preferred_element_type=F32)
                c = psum + _dot2(p, before)
                sig = jnp.exp(lb)
                dz = (_masked(valid, p * (1.0 - sig) - c * sig) * scale).astype(BF16)
                dq = dq + jnp.dot(dz, k, preferred_element_type=F32)
                dk_ref[ks, lanes[t]] += lax.dot_general(dz, q[t], TN, preferred_element_type=F32)
                dv_ref[ks, lanes[t]] += lax.dot_general(att.astype(BF16), dob[t], TN, preferred_element_type=F32)
                out.append((seen + jnp.sum(l1, axis=1, keepdims=True), psum + jnp.sum(p, axis=1, keepdims=True), dq))
            return tuple(out)

        zero = jnp.zeros((blk, 1), F32)
        start = ((zero, zero, jnp.zeros((blk, HEAD_DIM), F32)),) * hp
        carry = step(i, lax.fori_loop(i - swept_ref[h, i], i, lambda kb, c: step(kb, c, None), start), col < row)
        for t in range(hp):
            dq_ref[:, lanes[t]] = carry[t][2]
        pl.when((h == ng - 1) & (i == nq - 1))(last)

    wide = hp * HEAD_DIM
    qb = pl.BlockSpec((blk, wide), lambda h, i: (i, h))
    head = pl.BlockSpec((T, wide), lambda h, i: (0, h))
    out = jax.ShapeDtypeStruct((T, W), F32)
    return comm.call(
        body, name, (ng, nq),
        [qb, pl.BlockSpec((T, wide), lambda h, i: (0, ng + h)),
         pl.BlockSpec((T, wide), lambda h, i: (0, 2 * ng + h)), qb, qb, pl.BlockSpec(memory_space=pltpu.SMEM)],
        (qb, head, head), (out, out, out), [], (qkv, qkv, qkv, do, lt, swept))


def _expanders(H):
    lane = np.arange(H * HEAD_DIM) // HEAD_DIM
    eb = np.zeros((LANES, H * HEAD_DIM), np.float32)
    eg = np.zeros((LANES, H * HEAD_DIM), np.float32)
    eb[lane, np.arange(H * HEAD_DIM)] = 1.0
    eg[H + lane, np.arange(H * HEAD_DIM)] = 1.0
    sb = np.zeros((H * HEAD_DIM, LANES), np.float32)
    sg = np.zeros((H * HEAD_DIM, LANES), np.float32)
    sb[np.arange(H) * HEAD_DIM, np.arange(H)] = 1.0
    sg[np.arange(H) * HEAD_DIM, H + np.arange(H)] = 1.0
    return [jnp.asarray(m, BF16) for m in (eb, eg, sb, sg)]


def _conv_taps(ext_ref, w, n_out, lead):
    K = w.shape[0]
    out = None
    for j in range(K):
        term = ext_ref[pl.ds(lead - (K - 1) + j, n_out), :] * w[j:j + 1, :]
        out = term if out is None else out + term
    return out


def _l2_heads(s, H, fn):
    return jnp.concatenate([fn(s[:, h * HEAD_DIM:(h + 1) * HEAD_DIM]) for h in range(H)], axis=1)


def _dn_pre_fwd(pdn, pba, conv_w, a_log, dt_bias, bt, name):
    T, W = pdn.shape[0], pdn.shape[1] // 4
    H = W // HEAD_DIM
    eb, eg, _, _ = _expanders(H)
    nb = T // bt

    def body(x_ref, prev_ref, ba_ref, w_ref, al_ref, dt_ref, eb_ref, eg_ref,
             q_ref, k_ref, v_ref, bx_ref, gx_ref, ext):
        i = pl.program_id(0)
        ext[pl.ds(0, HALO), :] = jnp.where(i > 0, prev_ref[...], 0.0)
        ext[pl.ds(HALO, bt), :] = x_ref[...]
        c = _conv_taps(ext, w_ref[...], bt, HALO)
        s = c * _sigmoid(c)
        q_ref[...] = _l2_heads(s[:, :W], H, lambda t: t * lax.rsqrt(jnp.sum(t * t, axis=-1, keepdims=True) + EPS)
                               * (HEAD_DIM ** -0.5))
        k_ref[...] = _l2_heads(s[:, W:2 * W], H, lambda t: t * lax.rsqrt(jnp.sum(t * t, axis=-1, keepdims=True) + EPS))
        v_ref[...] = s[:, 2 * W:]
        ba = ba_ref[...]
        beta = _sigmoid(ba)
        graw = -jnp.exp(al_ref[...]) * _softplus(ba + dt_ref[...])
        row, col = _iota2(bt, bt)
        tri = ((row // CHUNK == col // CHUNK) & (row >= col)).astype(BF16)
        gcum = _dot3r(tri, graw)
        bx_ref[...] = _dot3(beta, eb_ref[...])
        gx_ref[...] = _dot3(gcum, eg_ref[...])

    C = 3 * W
    rowb = lambda w: pl.BlockSpec((bt, w), lambda i: (i, 0))
    full = lambda a: pl.BlockSpec(a.shape, lambda i: (0,) * a.ndim)
    out = jax.ShapeDtypeStruct((T, W), F32)
    return pl.pallas_call(
        body, name=name, grid=(nb,),
        in_specs=[rowb(C), pl.BlockSpec((HALO, C), lambda i: (jnp.maximum(i * (bt // HALO) - 1, 0), 0)),
                  rowb(LANES), full(conv_w), full(a_log), full(dt_bias), full(eb), full(eg)],
        out_specs=(rowb(W),) * 5, out_shape=(out,) * 5,
        scratch_shapes=[pltpu.VMEM((bt + HALO, C), F32)], compiler_params=_params(1),
    )(pdn, pdn, pba, conv_w, a_log, dt_bias, eb, eg)


def _dn_pre_bwd(pdn, pba, conv_w, a_log, dt_bias, dq, dk, dv, dbx, dgx, bt, name):
    T, W = pdn.shape[0], pdn.shape[1] // 4
    H = W // HEAD_DIM
    C = 3 * W
    K = conv_w.shape[0]
    _, _, sb, sg = _expanders(H)
    nb = T // bt
    n_ext = bt + HALO

    def body(x_ref, prev_ref, next_ref, ba_ref, w_ref, al_ref, dt_ref, sb_ref, sg_ref,
             dq_ref, dqn_ref, dk_ref, dkn_ref, dv_ref, dvn_ref, dbx_ref, dgx_ref,
             dx_ref, dba_ref, dw_ref, dal_ref, ddt_ref, ext, dext, dcext):
        i = pl.program_id(0)
        last = i == nb - 1
        ext[pl.ds(0, HALO), :] = jnp.where(i > 0, prev_ref[...], 0.0)
        ext[pl.ds(HALO, bt), :] = x_ref[...]
        ext[pl.ds(HALO + bt, HALO), :] = jnp.where(last, 0.0, next_ref[...])
        dext[pl.ds(0, bt), pl.ds(0, W)] = dq_ref[...]
        dext[pl.ds(0, bt), pl.ds(W, W)] = dk_ref[...]
        dext[pl.ds(0, bt), pl.ds(2 * W, W)] = dv_ref[...]
        dext[pl.ds(bt, HALO), pl.ds(0, W)] = jnp.where(last, 0.0, dqn_ref[...])
        dext[pl.ds(bt, HALO), pl.ds(W, W)] = jnp.where(last, 0.0, dkn_ref[...])
        dext[pl.ds(bt, HALO), pl.ds(2 * W, W)] = jnp.where(last, 0.0, dvn_ref[...])
        w = w_ref[...]
        c = _conv_taps(ext, w, n_ext, HALO)
        sg_c = _sigmoid(c)
        s = c * sg_c
        d = dext[...]

        def l2_bwd(scale):
            def fn(pair):
                t, dt = pair
                r = lax.rsqrt(jnp.sum(t * t, axis=-1, keepdims=True) + EPS)
                return scale * r * (dt - t * (r * r) * jnp.sum(t * dt, axis=-1, keepdims=True))
            return fn

        def heads(lo, fn):
            return jnp.concatenate(
                [fn((s[:, lo + h * HEAD_DIM:lo + (h + 1) * HEAD_DIM], d[:, lo + h * HEAD_DIM:lo + (h + 1) * HEAD_DIM]))
                 for h in range(H)], axis=1)

        ds = jnp.concatenate([heads(0, l2_bwd(HEAD_DIM ** -0.5)), heads(W, l2_bwd(1.0)), d[:, 2 * W:]], axis=1)
        dc = ds * (sg_c * (1.0 + c * (1.0 - sg_c)))
        dcext[...] = dc
        dx = None
        for j in range(K):
            term = dcext[pl.ds(K - 1 - j, bt), :] * w[j:j + 1, :]
            dx = term if dx is None else dx + term
        dx_ref[...] = dx.astype(BF16)

        @pl.when(i == 0)
        def _():
            dw_ref[...] = jnp.zeros_like(dw_ref)
            dal_ref[...] = jnp.zeros_like(dal_ref)
            ddt_ref[...] = jnp.zeros_like(ddt_ref)

        dcb = dc[:bt]
        dw_ref[...] += jnp.concatenate(
            [jnp.sum(dcb * ext[pl.ds(HALO - (K - 1) + j, bt), :], axis=0, keepdims=True) for j in range(K)], axis=0)

        ba = ba_ref[...]
        beta = _sigmoid(ba)
        al, dtb = al_ref[...], dt_ref[...]
        dbeta = _dot3(dbx_ref[...], sb_ref[...])
        dg = _dot3(dgx_ref[...], sg_ref[...])
        sp = _softplus(ba + dtb)
        da = dg * (-jnp.exp(al)) * _sigmoid(ba + dtb)
        dba_ref[...] = dbeta * beta * (1.0 - beta) + da
        dal_ref[...] += jnp.sum(dg * (-jnp.exp(al)) * sp, axis=0, keepdims=True)
        ddt_ref[...] += jnp.sum(da, axis=0, keepdims=True)

    rowb = lambda w: pl.BlockSpec((bt, w), lambda i: (i, 0))
    full = lambda a: pl.BlockSpec(a.shape, lambda i: (0,) * a.ndim)
    nxt = lambda w: pl.BlockSpec((HALO, w), lambda i: (jnp.minimum((i + 1) * (bt // HALO), T // HALO - 1), 0))
    vec = pl.BlockSpec((1, LANES), lambda i: (0, 0))
    return pl.pallas_call(
        body, name=name, grid=(nb,),
        in_specs=[rowb(C), pl.BlockSpec((HALO, C), lambda i: (jnp.maximum(i * (bt // HALO) - 1, 0), 0)), nxt(C),
                  rowb(LANES), full(conv_w), full(a_log), full(dt_bias), full(sb), full(sg),
                  rowb(W), nxt(W), rowb(W), nxt(W), rowb(W), nxt(W), rowb(W), rowb(W)],
        out_specs=(rowb(C), rowb(LANES), pl.BlockSpec((K, C), lambda i: (0, 0)), vec, vec),
        out_shape=(jax.ShapeDtypeStruct((T, C), BF16), jax.ShapeDtypeStruct((T, LANES), F32),
                   jax.ShapeDtypeStruct((K, C), F32), jax.ShapeDtypeStruct((1, LANES), F32),
                   jax.ShapeDtypeStruct((1, LANES), F32)),
        scratch_shapes=[pltpu.VMEM((bt + 2 * HALO, C), F32), pltpu.VMEM((n_ext, C), F32), pltpu.VMEM((n_ext, C), F32)],
        compiler_params=_params(1),
    )(pdn, pdn, pdn, pba, conv_w, a_log, dt_bias, sb, sg, dq, dq, dk, dk, dv, dv, dbx, dgx)


def _dot_split(a, b):
    a_hi, b_hi = a.astype(BF16), b.astype(BF16)
    a_lo, b_lo = (a - a_hi.astype(F32)).astype(BF16), (b - b_hi.astype(F32)).astype(BF16)
    dot = functools.partial(jnp.dot, preferred_element_type=F32)
    return dot(a_hi, b_hi) + (dot(a_hi, b_lo) + dot(a_lo, b_hi))


def _dn_local(q, k, v, beta, g, tm=None):
    row, col = _iota2(ROWS, ROWS)
    same = (row // CHUNK) == (col // CHUNK)
    causal = same & (row >= col)
    strict = same & (row > col)
    eye = (row == col).astype(F32)
    last_of = (col == (row // CHUNK) * CHUNK + (CHUNK - 1)).astype(BF16)
    eg = jnp.exp(g)
    decay = jnp.where(causal, jnp.exp(jnp.where(causal, g - g.T, 0.0)), 0.0)
    kb = k * beta
    vb = v * beta
    kk = _bdot(kb, k, NT)
    low = jnp.where(strict, kk * decay, 0.0)
    if tm is None:
        pw = -low
        tm = eye + pw
        for _ in range(5):
            pw = _dot_split(pw, pw)
            tm = tm + _dot_split(tm, pw)
    kbg = kb * eg
    u = _bdot(tm, vb)
    w = _bdot(tm, kbg)
    qk = _bdot(q, k, NT)
    qa = jnp.where(causal, qk * decay, 0.0)
    glast = _dot3r(last_of, g)
    e2 = jnp.exp(glast - g)
    return dict(row=row, col=col, same=same, causal=causal, strict=strict, eg=eg, decay=decay, kb=kb, vb=vb, kk=kk,
                tm=tm, kbg=kbg, u=u, w=w, qk=qk, qa=qa, glast=glast, e2=e2, kte=k * e2, qd=q * eg)


def _dn_core_fwd(q, k, v, bx, gx, z, gain, name, comm):
    T, W = q.shape
    H = W // HEAD_DIM
    nb = T // ROWS
    hp = _heads_per_step(H)
    ng = H // hp

    def body(*refs):
        ((q_ref, k_ref, v_ref, b_ref, g_ref, z_ref, gain_ref), (o_ref, mix_ref, ss_ref, tm_ref), (state,),
         (first, mid, last)) = comm.split(refs, 7, 4, 1)
        h, b = pl.program_id(0), pl.program_id(1)
        pl.when((h == 0) & (b == 0))(first)
        pl.when((h == ng // 2) & (b == 0))(mid)

        @pl.when(b == 0)
        def _():
            state[...] = jnp.zeros_like(state)

        for t in range(hp):
            ln = slice(t * HEAD_DIM, (t + 1) * HEAD_DIM)
            L = _dn_local(q_ref[:, ln], k_ref[:, ln], v_ref[:, ln], b_ref[:, ln], g_ref[:, ln])
            s = state[t]
            vns, qds = [], []
            for c in range(2):
                rows = slice(c * CHUNK, (c + 1) * CHUNK)
                ss_ref[t, c] = s
                vn = L["u"][rows] - _bdot(L["w"][rows], s)
                qds.append(_bdot(L["qd"][rows], s))
                vns.append(vn)
                s = s * jnp.exp(L["glast"][c * CHUNK:c * CHUNK + 1, :]) + _bdot(L["kte"][rows], vn, TN)
            state[t] = s
            tm_ref[:, ln] = L["tm"]
            o = jnp.concatenate(qds, axis=0) + _bdot(L["qa"], jnp.concatenate(vns, axis=0))
            o_ref[:, ln] = o
            zz = z_ref[:, ln]
            r = lax.rsqrt(jnp.mean(o * o, axis=-1, keepdims=True) + EPS)
            mix_ref[:, ln] = ((o * r * gain_ref[...]) * (zz * _sigmoid(zz))).astype(BF16)
        pl.when((h == ng - 1) & (b == nb - 1))(last)

    wide = hp * HEAD_DIM
    blk = pl.BlockSpec((ROWS, wide), lambda h, b: (b, h))
    zblk = pl.BlockSpec((ROWS, wide), lambda h, b: (b, 3 * ng + h))
    return comm.call(
        body, name, (ng, nb), [blk] * 5 + [zblk, pl.BlockSpec((1, HEAD_DIM), lambda h, b: (0, 0))],
        (blk, blk, pl.BlockSpec((hp, 2, HEAD_DIM, HEAD_DIM), lambda h, b: (h, b, 0, 0)), blk),
        (jax.ShapeDtypeStruct((T, W), F32), jax.ShapeDtypeStruct((T, W), BF16),
         jax.ShapeDtypeStruct((H, T // CHUNK, HEAD_DIM, HEAD_DIM), F32), jax.ShapeDtypeStruct((T, W), F32)),
        [pltpu.VMEM((hp, HEAD_DIM, HEAD_DIM), F32)], (q, k, v, bx, gx, z, gain))


def _dn_core_bwd(q, k, v, bx, gx, z, gain, o, dmix, ss, tms, dmix_col0, name, comm):
    T, W = q.shape
    H = W // HEAD_DIM
    nb = T // ROWS
    hp = _heads_per_step(H)
    ng = H // hp
    wide = hp * HEAD_DIM
    c0 = dmix_col0 // wide

    def body(*refs):
        ((q_ref, k_ref, v_ref, b_ref, g_ref, z_ref, gain_ref, o_ref, dm_ref, ss_ref, tm_ref),
         (dq_ref, dk_ref, dv_ref, dbx_ref, dgx_ref, dz_ref, dgain_ref), (dstate,),
         (first, mid, last)) = comm.split(refs, 11, 7, 1)
        pl.when((pl.program_id(0) == 0) & (pl.program_id(1) == 0))(first)
        pl.when((pl.program_id(0) == ng // 2) & (pl.program_id(1) == 0))(mid)

        @pl.when(pl.program_id(1) == 0)
        def _():
            dstate[...] = jnp.zeros_like(dstate)
            dgain_ref[...] = jnp.zeros_like(dgain_ref)

        refs = (q_ref, k_ref, v_ref, b_ref, g_ref, z_ref, gain_ref, o_ref, dm_ref, ss_ref, tm_ref,
                dq_ref, dk_ref, dv_ref, dbx_ref, dgx_ref, dz_ref, dgain_ref, dstate)
        for t in range(hp):
            one_head(t, *refs)
        pl.when((pl.program_id(0) == ng - 1) & (pl.program_id(1) == nb - 1))(last)

    def one_head(t, q_ref, k_ref, v_ref, b_ref, g_ref, z_ref, gain_ref, o_ref, dm_ref, ss_ref, tm_ref,
                 dq_ref, dk_ref, dv_ref, dbx_ref, dgx_ref, dz_ref, dgain_ref, dstate):
        ln = slice(t * HEAD_DIM, (t + 1) * HEAD_DIM)
        qv, kv, vv, beta, g = q_ref[:, ln], k_ref[:, ln], v_ref[:, ln], b_ref[:, ln], g_ref[:, ln]
        gain_v = gain_ref[...]
        ov, zz, dm = o_ref[:, ln], z_ref[:, ln], dm_ref[:, ln]
        r = lax.rsqrt(jnp.mean(ov * ov, axis=-1, keepdims=True) + EPS)
        n = ov * r
        sgz = _sigmoid(zz)
        d_on = dm * (zz * sgz)
        dz_ref[:, ln] = dm * (n * gain_v) * (sgz * (1.0 + zz * (1.0 - sgz)))
        dgain_ref[t] += jnp.sum(d_on * n, axis=0, keepdims=True)
        gy = d_on * gain_v
        do = r * (gy - n * jnp.mean(gy * n, axis=-1, keepdims=True))

        L = _dn_local(qv, kv, vv, beta, g, tm_ref[:, ln])
        row, causal, strict = L["row"], L["causal"], L["strict"]
        u, w, qa, qd, kte, tm = L["u"], L["w"], L["qa"], L["qd"], L["kte"], L["tm"]
        s_in = [ss_ref[t, 0], ss_ref[t, 1]]
        vn = [u[c * CHUNK:(c + 1) * CHUNK] - _bdot(w[c * CHUNK:(c + 1) * CHUNK], s_in[c]) for c in range(2)]
        vn_all = jnp.concatenate(vn, axis=0)
        qat_do = _bdot(qa, do, TN)
        d_qa = jnp.where(causal, _bdot(do, vn_all, NT), 0.0)
        ds = dstate[t]
        d_vn, d_kte, d_qd, d_w, d_gl = [None] * 2, [None] * 2, [None] * 2, [None] * 2, [None] * 2
        for c in (1, 0):
            rows = slice(c * CHUNK, (c + 1) * CHUNK)
            egl = jnp.exp(L["glast"][c * CHUNK:c * CHUNK + 1, :])
            d_vn[c] = qat_do[rows] + _bdot(kte[rows], ds)
            d_kte[c] = _bdot(vn[c], ds, NT)
            d_gl[c] = jnp.sum(jnp.sum(ds * s_in[c], axis=1, keepdims=True), axis=0, keepdims=True) * egl
            d_qd[c] = _bdot(do[rows], s_in[c], NT)
            d_w[c] = -_bdot(d_vn[c], s_in[c], NT)
            ds = ds * egl + _bdot(qd[rows], do[rows], TN) - _bdot(w[rows], d_vn[c], TN)
        dstate[t] = ds
        d_u = jnp.concatenate(d_vn, axis=0)
        d_w = jnp.concatenate(d_w, axis=0)
        d_qd = jnp.concatenate(d_qd, axis=0)
        d_kte = jnp.concatenate(d_kte, axis=0)

        d_tm = _bdot(d_u, L["vb"], NT) + _bdot(d_w, L["kbg"], NT)
        d_vb = _bdot(tm, d_u, TN)
        d_kbg = _bdot(tm, d_w, TN)
        d_low = jnp.where(strict, -_bdot(_bdot(tm, d_tm, TN), tm, NT), 0.0)
        decay = L["decay"]
        d_kk = d_low * decay
        d_qk = d_qa * decay
        d_decay = d_low * L["kk"] + d_qa * L["qk"]
        eg, e2 = L["eg"], L["e2"]
        d_kb = _bdot(d_kk, kv) + d_kbg * eg
        dk_ref[:, ln] = _bdot(d_kk, L["kb"], TN) + _bdot(d_qk, qv, TN) + d_kb * beta + d_kte * e2
        dq_ref[:, ln] = _bdot(d_qk, kv) + d_qd * eg
        dv_ref[:, ln] = d_vb * beta
        rsum = lambda a: jnp.sum(a, axis=-1, keepdims=True)
        dbx_ref[:, ln] = jnp.broadcast_to(rsum(d_kb * kv) + rsum(d_vb * vv), (ROWS, HEAD_DIM))
        d_eg = rsum(d_kbg * L["kb"]) + rsum(d_qd * qv)
        t2 = rsum(d_kte * kv) * e2
        ed = d_decay * decay
        d_g = d_eg * eg - t2 + rsum(ed) - rsum(ed.T)
        col = L["col"]
        chunk_sum = L["same"].astype(BF16)
        is_last = (row % CHUNK) == (CHUNK - 1)
        d_glast = _dot3r(chunk_sum, t2) + jnp.where(row < CHUNK, d_gl[0], d_gl[1])
        d_g = d_g + jnp.where(is_last, d_glast, 0.0)
        suffix = (L["same"] & (col >= row)).astype(BF16)
        dgx_ref[:, ln] = _dot3r(suffix, d_g)

    rev = lambda b: nb - 1 - b
    blk = pl.BlockSpec((ROWS, wide), lambda h, b: (rev(b), h))
    zblk = pl.BlockSpec((ROWS, wide), lambda h, b: (rev(b), 3 * ng + h))
    dmblk = pl.BlockSpec((ROWS, wide), lambda h, b: (rev(b), c0 + h))
    out = jax.ShapeDtypeStruct((T, W), F32)
    return comm.call(
        body, name, (ng, nb),
        [blk] * 5 + [zblk, pl.BlockSpec((1, HEAD_DIM), lambda h, b: (0, 0)), blk, dmblk,
                     pl.BlockSpec((hp, 2, HEAD_DIM, HEAD_DIM), lambda h, b: (h, rev(b), 0, 0)), blk],
        (blk,) * 6 + (pl.BlockSpec((hp, 1, HEAD_DIM), lambda h, b: (h, 0, 0)),),
        (out,) * 6 + (jax.ShapeDtypeStruct((H, 1, HEAD_DIM), F32),),
        [pltpu.VMEM((hp, HEAD_DIM, HEAD_DIM), F32)], (q, k, v, bx, gx, z, gain, o, dmix, ss, tms))


_GELU_C = 0.7978845608028654
_GELU_A = 0.044715


def _gelu(x):
    t = jnp.tanh(_GELU_C * (x + _GELU_A * (x * x * x)))
    return 0.5 * x * (1.0 + t), t


def _ffn_mid_fwd(up, conv_w, conv_b, bt, bc, name):
    T, F = up.shape[0], up.shape[1] // 2
    nc = F // bc

    def body(g_ref, gp_ref, v_ref, vp_ref, wg_ref, wv_ref, bg_ref, bv_ref, o_ref, gext, vext):
        i = pl.program_id(0)
        for ext, cur, prev in ((gext, g_ref, gp_ref), (vext, v_ref, vp_ref)):
            ext[pl.ds(0, HALO), :] = jnp.where(i > 0, prev[...], 0.0)
            ext[pl.ds(HALO, bt), :] = cur[...]
        gate = _conv_taps(gext, wg_ref[...], bt, HALO) + bg_ref[...]
        val = _conv_taps(vext, wv_ref[...], bt, HALO) + bv_ref[...]
        o_ref[...] = (_gelu(gate)[0] * val).astype(BF16)

    K = conv_w.shape[0]
    prev = lambda i: jnp.maximum(i * (bt // HALO) - 1, 0)
    return pl.pallas_call(
        body, name=name, grid=(T // bt, nc),
        in_specs=[pl.BlockSpec((bt, bc), lambda i, j: (i, j)), pl.BlockSpec((HALO, bc), lambda i, j: (prev(i), j)),
                  pl.BlockSpec((bt, bc), lambda i, j: (i, nc + j)),
                  pl.BlockSpec((HALO, bc), lambda i, j: (prev(i), nc + j)),
                  pl.BlockSpec((K, bc), lambda i, j: (0, j)), pl.BlockSpec((K, bc), lambda i, j: (0, nc + j)),
                  pl.BlockSpec((1, bc), lambda i, j: (0, j)), pl.BlockSpec((1, bc), lambda i, j: (0, nc + j))],
        out_specs=pl.BlockSpec((bt, bc), lambda i, j: (i, j)),
        out_shape=jax.ShapeDtypeStruct((T, F), BF16),
        scratch_shapes=[pltpu.VMEM((bt + HALO, bc), F32)] * 2, compiler_params=_params(2),
    )(up, up, up, up, conv_w, conv_w, conv_b, conv_b)


def _ffn_mid_bwd(up, conv_w, conv_b, da, bt, bc, name):
    T, F = up.shape[0], up.shape[1] // 2
    nc = F // bc
    K = conv_w.shape[0]
    nb = T // bt
    n_ext = bt + HALO

    def body(g_ref, gp_ref, gn_ref, v_ref, vp_ref, vn_ref, da_ref, dan_ref, wg_ref, wv_ref, bg_ref, bv_ref,
             dg_ref, dv_ref, dwg_ref, dwv_ref, dbg_ref, dbv_ref, gext, vext, dgext, dvext):
        i = pl.program_id(1)
        last = i == nb - 1
        for ext, cur, prev, nxt in ((gext, g_ref, gp_ref, gn_ref), (vext, v_ref, vp_ref, vn_ref)):
            ext[pl.ds(0, HALO), :] = jnp.where(i > 0, prev[...], 0.0)
            ext[pl.ds(HALO, bt), :] = cur[...]
            ext[pl.ds(HALO + bt, HALO), :] = jnp.where(last, 0.0, nxt[...])
        wg, wv = wg_ref[...], wv_ref[...]
        gate = _conv_taps(gext, wg, n_ext, HALO) + bg_ref[...]
        val = _conv_taps(vext, wv, n_ext, HALO) + bv_ref[...]
        dact = jnp.concatenate([da_ref[...], jnp.where(last, 0.0, dan_ref[...])], axis=0)
        ge, t = _gelu(gate)
        dgelu = 0.5 * (1.0 + t) + 0.5 * gate * (1.0 - t * t) * (_GELU_C * (1.0 + 3.0 * _GELU_A * (gate * gate)))
        dgext[...] = dact * val * dgelu
        dvext[...] = dact * ge

        @pl.when(i == 0)
        def _():
            for ref in (dwg_ref, dwv_ref, dbg_ref, dbv_ref):
                ref[...] = jnp.zeros_like(ref)

        for dext, ext, w, dx_ref, dw_ref, db_ref in ((dgext, gext, wg, dg_ref, dwg_ref, dbg_ref),
                                                     (dvext, vext, wv, dv_ref, dwv_ref, dbv_ref)):
            dx = None
            for j in range(K):
                term = dext[pl.ds(K - 1 - j, bt), :] * w[j:j + 1, :]
                dx = term if dx is None else dx + term
            dx_ref[...] = dx.astype(BF16)
            dcur = dext[pl.ds(0, bt), :]
            dw_ref[...] += jnp.concatenate(
                [jnp.sum(dcur * ext[pl.ds(HALO - (K - 1) + j, bt), :], axis=0, keepdims=True) for j in range(K)], axis=0)
            db_ref[...] += jnp.sum(dcur, axis=0, keepdims=True)

    prev = lambda i: jnp.maximum(i * (bt // HALO) - 1, 0)
    nxt = lambda i: jnp.minimum((i + 1) * (bt // HALO), T // HALO - 1)
    cur_g = pl.BlockSpec((bt, bc), lambda j, i: (i, j))
    cur_v = pl.BlockSpec((bt, bc), lambda j, i: (i, nc + j))
    outs = pl.pallas_call(
        body, name=name, grid=(nc, nb),
        in_specs=[cur_g, pl.BlockSpec((HALO, bc), lambda j, i: (prev(i), j)),
                  pl.BlockSpec((HALO, bc), lambda j, i: (nxt(i), j)),
                  cur_v, pl.BlockSpec((HALO, bc), lambda j, i: (prev(i), nc + j)),
                  pl.BlockSpec((HALO, bc), lambda j, i: (nxt(i), nc + j)),
                  cur_g, pl.BlockSpec((HALO, bc), lambda j, i: (nxt(i), j)),
                  pl.BlockSpec((K, bc), lambda j, i: (0, j)), pl.BlockSpec((K, bc), lambda j, i: (0, nc + j)),
                  pl.BlockSpec((1, bc), lambda j, i: (0, j)), pl.BlockSpec((1, bc), lambda j, i: (0, nc + j))],
        out_specs=(cur_g, cur_g, pl.BlockSpec((K, bc), lambda j, i: (0, j)), pl.BlockSpec((K, bc), lambda j, i: (0, j)),
                   pl.BlockSpec((1, bc), lambda j, i: (0, j)), pl.BlockSpec((1, bc), lambda j, i: (0, j))),
        out_shape=(jax.ShapeDtypeStruct((T, F), BF16), jax.ShapeDtypeStruct((T, F), BF16),
                   jax.ShapeDtypeStruct((K, F), F32), jax.ShapeDtypeStruct((K, F), F32),
                   jax.ShapeDtypeStruct((1, F), F32), jax.ShapeDtypeStruct((1, F), F32)),
        scratch_shapes=[pltpu.VMEM((bt + 2 * HALO, bc), F32)] * 2 + [pltpu.VMEM((n_ext, bc), F32)] * 2,
        compiler_params=_params(2),
    )(up, up, up, up, up, up, da, da, conv_w, conv_w, conv_b, conv_b)
    return outs


def _adam_math(w, g, m, v):
    m2 = ADAM_B1 * m + (1.0 - ADAM_B1) * g
    v2 = ADAM_B2 * v + (1.0 - ADAM_B2) * (g * g)
    m_hat = m2 / (1.0 - ADAM_B1 ** ADAM_STEP)
    v_hat = v2 / (1.0 - ADAM_B2 ** ADAM_STEP)
    return -ADAM_LR * (m_hat / (jnp.sqrt(v_hat) + ADAM_EPS) + ADAM_WD * w), m2, v2


def _adamw_halves(w, mine, theirs, place, m, v, bt, name):
    R, C = w.shape
    h = R // 2
    bt = _fit_rows(h, bt)
    nh = h // bt

    def body(s_ref, w_ref, a_ref, b_ref, m_ref, v_ref, g_ref, d_ref, m2_ref, v2_ref):
        lower = pl.program_id(0) < nh
        gv = jnp.where(lower == (s_ref[0] == 0), a_ref[...], b_ref[...])
        g_ref[...] = gv
        d_ref[...], m2_ref[...], v2_ref[...] = _adam_math(w_ref[...], gv, m_ref[...], v_ref[...])

    full = pl.BlockSpec((bt, C), lambda i, s: (i, 0))
    half = pl.BlockSpec((bt, C), lambda i, s: (i % nh, 0))
    out = jax.ShapeDtypeStruct((R, C), F32)
    return pl.pallas_call(
        body, name=name,
        grid_spec=pltpu.PrefetchScalarGridSpec(num_scalar_prefetch=1, grid=(2 * nh,),
                                               in_specs=[full, half, half, full, full], out_specs=(full,) * 4),
        out_shape=(out,) * 4, compiler_params=_params(1),
    )(place, w, mine, theirs, m, v)


def _adamw(w, g, m, v, bt, name):
    R, C = w.shape
    bt = _fit_rows(R, bt)
    blk = pl.BlockSpec((bt, C), lambda i: (i, 0))

    def body(w_ref, g_ref, m_ref, v_ref, d_ref, m2_ref, v2_ref):
        gv = g_ref[...]
        m2 = ADAM_B1 * m_ref[...] + (1.0 - ADAM_B1) * gv
        v2 = ADAM_B2 * v_ref[...] + (1.0 - ADAM_B2) * (gv * gv)
        m_hat = m2 / (1.0 - ADAM_B1 ** ADAM_STEP)
        v_hat = v2 / (1.0 - ADAM_B2 ** ADAM_STEP)
        d_ref[...] = -ADAM_LR * (m_hat / (jnp.sqrt(v_hat) + ADAM_EPS) + ADAM_WD * w_ref[...])
        m2_ref[...] = m2
        v2_ref[...] = v2

    out = jax.ShapeDtypeStruct((R, C), F32)
    return pl.pallas_call(body, name=name, grid=(R // bt,), in_specs=[blk] * 4, out_specs=(blk,) * 3,
                          out_shape=(out,) * 3, compiler_params=_params(1))(w, g, m, v)


def _place():
    x, y, c = lax.axis_index("x"), lax.axis_index("y"), lax.axis_index("c")
    chips = [(1 - x, y), (x, 1 - y), (1 - x, 1 - y)]
    return x, y, c, chips


_HBM = pl.BlockSpec(memory_space=pltpu.HBM)


def _gather_xy(bufs, name):
    n = len(bufs)
    halves = [b.shape[0] // 2 for b in bufs]

    def body(*refs):
        ins, outs = refs[:n], refs[n:2 * n]
        send, recv, local = refs[2 * n:]
        x, y, c, chips = _place()
        me = 2 * x + y
        copies, forwards = [], []
        for b in range(n):
            h = halves[b]
            mine = pl.ds(c * h, h)
            own = pltpu.make_async_copy(ins[b], outs[b].at[me], local.at[b])
            own.start()
            copies.append(own)
            for j, (px, py) in enumerate(chips):
                cp = pltpu.make_async_remote_copy(
                    src_ref=ins[b].at[mine], dst_ref=outs[b].at[me, mine], send_sem=send.at[b, j],
                    recv_sem=recv.at[b, j], device_id=(px, py, c), device_id_type=MESH)
                cp.start()
                copies.append(cp)
        for b in range(n):
            h = halves[b]
            mine = pl.ds(c * h, h)
            for j, (px, py) in enumerate(chips):
                src = 2 * px + py
                landed = pltpu.make_async_remote_copy(
                    src_ref=ins[b].at[mine], dst_ref=outs[b].at[src, mine], send_sem=send.at[b, j],
                    recv_sem=recv.at[b, j], device_id=(px, py, c), device_id_type=MESH)
                landed.wait_recv()
                fw = pltpu.make_async_remote_copy(
                    src_ref=outs[b].at[src, mine], dst_ref=outs[b].at[src, mine], send_sem=send.at[b, 3 + j],
                    recv_sem=recv.at[b, 3 + j], device_id=(x, y, 1 - c), device_id_type=MESH)
                fw.start()
                forwards.append(fw)
        for b in range(n):
            h = halves[b]
            theirs = pl.ds((1 - c) * h, h)
            for j, (px, py) in enumerate(chips):
                src = 2 * px + py
                pltpu.make_async_remote_copy(
                    src_ref=outs[b].at[src, theirs], dst_ref=outs[b].at[src, theirs], send_sem=send.at[b, 3 + j],
                    recv_sem=recv.at[b, 3 + j], device_id=(x, y, 1 - c), device_id_type=MESH).wait_recv()
        for b in range(n):
            copies[b * 4].wait()
            for j in range(3):
                copies[b * 4 + 1 + j].wait_send()
        for fw in forwards:
            fw.wait_send()

    return pl.pallas_call(
        body, name=name, in_specs=[_HBM] * n, out_specs=[_HBM] * n,
        out_shape=[jax.ShapeDtypeStruct((4,) + b.shape, b.dtype) for b in bufs],
        scratch_shapes=[pltpu.SemaphoreType.DMA((n, 6)), pltpu.SemaphoreType.DMA((n, 6)), pltpu.SemaphoreType.DMA((n,))],
        compiler_params=pltpu.CompilerParams(has_side_effects=True),
    )(*bufs)


def _swap_half_c(buf, name):
    n, h = buf.shape[0], buf.shape[1] // 2

    def body(in_ref, out_ref, send, recv):
        x, y, c, _ = _place()
        cp = pltpu.make_async_remote_copy(
            src_ref=in_ref.at[:, pl.ds((1 - c) * h, h)], dst_ref=out_ref, send_sem=send, recv_sem=recv,
            device_id=(x, y, 1 - c), device_id_type=MESH)
        cp.start()
        cp.wait()

    return pl.pallas_call(
        body, name=name, in_specs=[_HBM], out_specs=_HBM,
        out_shape=jax.ShapeDtypeStruct((n, h, buf.shape[2]), buf.dtype),
        scratch_shapes=[pltpu.SemaphoreType.DMA, pltpu.SemaphoreType.DMA],
        compiler_params=pltpu.CompilerParams(has_side_effects=True),
    )(buf)


def _scatter_xy(buf, name):
    h = buf.shape[1]

    def body(in_ref, out_ref, send, recv):
        x, y, c, chips = _place()
        cps = []
        for j, (px, py) in enumerate(chips):
            cp = pltpu.make_async_remote_copy(
                src_ref=in_ref.at[2 * px + py], dst_ref=out_ref.at[j], send_sem=send.at[j], recv_sem=recv.at[j],
                device_id=(px, py, c), device_id_type=MESH)
            cp.start()
            cps.append(cp)
        for cp in cps:
            cp.wait()

    return pl.pallas_call(
        body, name=name, in_specs=[_HBM], out_specs=_HBM,
        out_shape=jax.ShapeDtypeStruct((3, h, buf.shape[2]), buf.dtype),
        scratch_shapes=[pltpu.SemaphoreType.DMA((3,)), pltpu.SemaphoreType.DMA((3,))],
        compiler_params=pltpu.CompilerParams(has_side_effects=True),
    )(buf)


def _join_c(half, name):
    h = half.shape[0]

    def body(in_ref, out_ref, send, recv, local):
        x, y, c, _ = _place()
        own = pltpu.make_async_copy(in_ref, out_ref.at[c], local)
        own.start()
        cp = pltpu.make_async_remote_copy(
            src_ref=in_ref, dst_ref=out_ref.at[c], send_sem=send, recv_sem=recv,
            device_id=(x, y, 1 - c), device_id_type=MESH)
        cp.start()
        cp.wait()
        own.wait()

    return pl.pallas_call(
        body, name=name, in_specs=[_HBM], out_specs=_HBM,
        out_shape=jax.ShapeDtypeStruct((2, h, half.shape[1]), half.dtype),
        scratch_shapes=[pltpu.SemaphoreType.DMA, pltpu.SemaphoreType.DMA, pltpu.SemaphoreType.DMA],
        compiler_params=pltpu.CompilerParams(has_side_effects=True),
    )(half)


def _add_half(buf, other, c, bt, name):
    n, _, h, lanes = buf.shape
    bt = _fit_rows(h, bt)

    def body(c_ref, a_ref, b_ref, o_ref):
        o_ref[...] = a_ref[0] + b_ref[...]

    return pl.pallas_call(
        body, name=name,
        grid_spec=pltpu.PrefetchScalarGridSpec(
            num_scalar_prefetch=1, grid=(n, h // bt),
            in_specs=[pl.BlockSpec((1, 1, bt, lanes), lambda k, i, c_ref: (k, c_ref[0], i, 0)),
                      pl.BlockSpec((1, bt, lanes), lambda k, i, c_ref: (k, i, 0))],
            out_specs=pl.BlockSpec((1, bt, lanes), lambda k, i, c_ref: (k, i, 0))),
        out_shape=jax.ShapeDtypeStruct((n, h, lanes), buf.dtype), compiler_params=_params(2),
    )(c, buf, other)


def _add_four(own, me, others, bt, name):
    _, h, lanes = own.shape
    bt = _fit_rows(h, bt)

    def body(me_ref, a_ref, b_ref, o_ref):
        o_ref[...] = ((a_ref[0] + b_ref[0]) + b_ref[1]) + b_ref[2]

    return pl.pallas_call(
        body, name=name,
        grid_spec=pltpu.PrefetchScalarGridSpec(
            num_scalar_prefetch=1, grid=(h // bt,),
            in_specs=[pl.BlockSpec((1, bt, lanes), lambda i, me_ref: (me_ref[0], i, 0)),
                      pl.BlockSpec((3, bt, lanes), lambda i, me_ref: (0, i, 0))],
            out_specs=pl.BlockSpec((bt, lanes), lambda i, me_ref: (i, 0))),
        out_shape=jax.ShapeDtypeStruct((h, lanes), own.dtype), compiler_params=_params(1),
    )(me, own, others)


def _gather_chips(bufs, split, name):
    n = len(bufs)

    def body(*refs):
        ins, outs = refs[:n], refs[n:2 * n]
        send, recv = refs[2 * n:]
        x, y, c, chips = _place()
        me = 2 * x + y

        def rows(b, core):
            h = bufs[b].shape[0] // 2
            return pl.ds(core * h, h) if split[b] else pl.ds(0, bufs[b].shape[0])

        def over_ici(b, j, block):
            px, py = chips[j]
            return pltpu.make_async_remote_copy(
                src_ref=ins[b].at[rows(b, c)], dst_ref=outs[b].at[block, rows(b, c)], send_sem=send.at[b, j],
                recv_sem=recv.at[b, j], device_id=(px, py, c), device_id_type=MESH)

        def over_d2d(b, j, block, core):
            return pltpu.make_async_remote_copy(
                src_ref=outs[b].at[block, rows(b, core)], dst_ref=outs[b].at[block, rows(b, core)],
                send_sem=send.at[b, 3 + j], recv_sem=recv.at[b, 3 + j], device_id=(x, y, 1 - c), device_id_type=MESH)

        started = []
        for b in range(n):
            for j in range(3):
                cp = over_ici(b, j, me)
                cp.start()
                started.append(cp.wait_send)
        for b in range(n):
            for j, (px, py) in enumerate(chips):
                over_ici(b, j, 2 * px + py).wait_recv()
                if split[b]:
                    fw = over_d2d(b, j, 2 * px + py, c)
                    fw.start()
                    started.append(fw.wait_send)
        for b in range(n):
            if split[b]:
                for j, (px, py) in enumerate(chips):
                    over_d2d(b, j, 2 * px + py, 1 - c).wait_recv()
        for wait in started:
            wait()

    return pl.pallas_call(
        body, name=name, in_specs=[_HBM] * n, out_specs=[_HBM] * n,
        out_shape=[jax.ShapeDtypeStruct((4,) + b.shape, b.dtype) for b in bufs],
        scratch_shapes=[pltpu.SemaphoreType.DMA((n, 6)), pltpu.SemaphoreType.DMA((n, 6))],
        compiler_params=pltpu.CompilerParams(has_side_effects=True),
    )(*bufs)


def _swap_halves(bufs, name):
    n = len(bufs)

    def body(*refs):
        ins, outs = refs[:n], refs[n:2 * n]
        send, recv = refs[2 * n:]
        x, y, c, _ = _place()
        cps = []
        for b in range(n):
            h = bufs[b].shape[1] // 2
            cp = pltpu.make_async_remote_copy(
                src_ref=ins[b].at[:, pl.ds((1 - c) * h, h)], dst_ref=outs[b], send_sem=send.at[b], recv_sem=recv.at[b],
                device_id=(x, y, 1 - c), device_id_type=MESH)
            cp.start()
            cps.append(cp)
        for cp in cps:
            cp.wait()

    return pl.pallas_call(
        body, name=name, in_specs=[_HBM] * n, out_specs=[_HBM] * n,
        out_shape=[jax.ShapeDtypeStruct((b.shape[0], b.shape[1] // 2, b.shape[2]), b.dtype) for b in bufs],
        scratch_shapes=[pltpu.SemaphoreType.DMA((n,)), pltpu.SemaphoreType.DMA((n,))],
        compiler_params=pltpu.CompilerParams(has_side_effects=True),
    )(*bufs)


def _scatter_chips(bufs, name):
    n = len(bufs)

    def body(*refs):
        ins, outs = refs[:n], refs[n:2 * n]
        send, recv = refs[2 * n:]
        x, y, c, chips = _place()
        cps = []
        for b in range(n):
            for j, (px, py) in enumerate(chips):
                cp = pltpu.make_async_remote_copy(
                    src_ref=ins[b].at[2 * px + py], dst_ref=outs[b].at[j], send_sem=send.at[b, j],
                    recv_sem=recv.at[b, j], device_id=(px, py, c), device_id_type=MESH)
                cp.start()
                cps.append(cp)
        for cp in cps:
            cp.wait()

    return pl.pallas_call(
        body, name=name, in_specs=[_HBM] * n, out_specs=[_HBM] * n,
        out_shape=[jax.ShapeDtypeStruct((3,) + b.shape[1:], b.dtype) for b in bufs],
        scratch_shapes=[pltpu.SemaphoreType.DMA((n, 3)), pltpu.SemaphoreType.DMA((n, 3))],
        compiler_params=pltpu.CompilerParams(has_side_effects=True),
    )(*bufs)


def _join_halves(halves, name):
    n = len(halves)

    def body(*refs):
        ins, outs = refs[:n], refs[n:2 * n]
        send, recv = refs[2 * n:]
        x, y, c, _ = _place()
        cps = []
        for b in range(n):
            cp = pltpu.make_async_remote_copy(
                src_ref=ins[b], dst_ref=outs[b], send_sem=send.at[b], recv_sem=recv.at[b],
                device_id=(x, y, 1 - c), device_id_type=MESH)
            cp.start()
            cps.append(cp)
        for cp in cps:
            cp.wait()

    return pl.pallas_call(
        body, name=name, in_specs=[_HBM] * n, out_specs=[_HBM] * n,
        out_shape=[jax.ShapeDtypeStruct(b.shape, b.dtype) for b in halves],
        scratch_shapes=[pltpu.SemaphoreType.DMA((n,)), pltpu.SemaphoreType.DMA((n,))],
        compiler_params=pltpu.CompilerParams(has_side_effects=True),
    )(*halves)


def _add_cores(buf, other, place, own_only, out_dtype, bt, name):
    n, _, h, cols = buf.shape
    bt = _fit_rows(h, bt)
    row = (lambda k, s: s[1]) if own_only else (lambda k, s: k)

    def body(s_ref, a_ref, b_ref, o_ref):
        o_ref[...] = (a_ref[...] + b_ref[...]).astype(out_dtype)

    return pl.pallas_call(
        body, name=name,
        grid_spec=pltpu.PrefetchScalarGridSpec(
            num_scalar_prefetch=1, grid=(1 if own_only else n, h // bt),
            in_specs=[pl.BlockSpec((None, None, bt, cols), lambda k, i, s: (row(k, s), s[0], i, 0)),
                      pl.BlockSpec((None, bt, cols), lambda k, i, s: (row(k, s), i, 0))],
            out_specs=(pl.BlockSpec((bt, cols), lambda k, i, s: (i, 0)) if own_only
                       else pl.BlockSpec((None, bt, cols), lambda k, i, s: (k, i, 0)))),
        out_shape=jax.ShapeDtypeStruct((h, cols) if own_only else (n, h, cols), out_dtype),
        compiler_params=_params(2),
    )(place, buf, other)


def _add_chips(own, others, bt, name):
    h, cols = own.shape
    bt = _fit_rows(h, bt)

    def body(a_ref, b_ref, o_ref):
        o_ref[...] = ((a_ref[...] + b_ref[0].astype(F32)) + b_ref[1].astype(F32)) + b_ref[2].astype(F32)

    return pl.pallas_call(
        body, name=name, grid=(h // bt,),
        in_specs=[pl.BlockSpec((bt, cols), lambda i: (i, 0)), pl.BlockSpec((3, bt, cols), lambda i: (0, i, 0))],
        out_specs=pl.BlockSpec((bt, cols), lambda i: (i, 0)),
        out_shape=jax.ShapeDtypeStruct((h, cols), F32), compiler_params=_params(1),
    )(own, others)


def _gather_plan(bufs, split):
    n = len(bufs)

    def phases(ins, outs, sems):
        send, recv = sems
        x, y, c, chips = _place()
        me = 2 * x + y

        def rows(b, core):
            h = bufs[b].shape[0] // 2
            return pl.ds(core * h, h) if split[b] else pl.ds(0, bufs[b].shape[0])

        def over_ici(b, j, block):
            px, py = chips[j]
            return pltpu.make_async_remote_copy(
                src_ref=ins[b].at[rows(b, c)], dst_ref=outs[b].at[block, rows(b, c)], send_sem=send.at[b, j],
                recv_sem=recv.at[b, j], device_id=(px, py, c), device_id_type=MESH)

        def over_d2d(b, j, block, core):
            return pltpu.make_async_remote_copy(
                src_ref=outs[b].at[block, rows(b, core)], dst_ref=outs[b].at[block, rows(b, core)],
                send_sem=send.at[b, 3 + j], recv_sem=recv.at[b, 3 + j], device_id=(x, y, 1 - c), device_id_type=MESH)

        pairs = [(b, j) for b in range(n) for j in range(3)]
        source = lambda j: 2 * chips[j][0] + chips[j][1]

        def first():
            for b, j in pairs:
                over_ici(b, j, me).start()

        def mid():
            for b, j in pairs:
                over_ici(b, j, source(j)).wait_recv()
                if split[b]:
                    over_d2d(b, j, source(j), c).start()

        def last():
            for b, j in pairs:
                if split[b]:
                    over_d2d(b, j, source(j), 1 - c).wait_recv()
            for b, j in pairs:
                over_ici(b, j, me).wait_send()
                if split[b]:
                    over_d2d(b, j, source(j), c).wait_send()

        return first, mid, last

    return ([jax.ShapeDtypeStruct((4,) + b.shape, b.dtype) for b in bufs],
            [pltpu.SemaphoreType.DMA((n, 6)), pltpu.SemaphoreType.DMA((n, 6))], phases)


def _exchange_plan(n, out_shapes, copy):
    def phases(ins, outs, sems):
        send, recv = sems
        place = _place()

        def first():
            for b in range(n):
                copy(b, ins, outs, send, recv, place).start()

        def last():
            for b in range(n):
                copy(b, ins, outs, send, recv, place).wait()

        return first, (lambda: None), last

    return out_shapes, [pltpu.SemaphoreType.DMA((n,)), pltpu.SemaphoreType.DMA((n,))], phases


def _swap_plan(bufs):
    def copy(b, ins, outs, send, recv, place):
        x, y, c, _ = place
        h = bufs[b].shape[1] // 2
        return pltpu.make_async_remote_copy(
            src_ref=ins[b].at[:, pl.ds((1 - c) * h, h)], dst_ref=outs[b], send_sem=send.at[b], recv_sem=recv.at[b],
            device_id=(x, y, 1 - c), device_id_type=MESH)

    shapes = [jax.ShapeDtypeStruct((b.shape[0], b.shape[1] // 2, b.shape[2]), b.dtype) for b in bufs]
    return _exchange_plan(len(bufs), shapes, copy)


def _scatter_plan(bufs):
    def copy(t, ins, outs, send, recv, place):
        x, y, c, chips = place
        b, j = divmod(t, 3)
        px, py = chips[j]
        return pltpu.make_async_remote_copy(
            src_ref=ins[b].at[2 * px + py], dst_ref=outs[b].at[j], send_sem=send.at[t], recv_sem=recv.at[t],
            device_id=(px, py, c), device_id_type=MESH)

    shapes = [jax.ShapeDtypeStruct((3,) + b.shape[1:], b.dtype) for b in bufs]
    return _exchange_plan(3 * len(bufs), shapes, copy)


def _join_plan(halves):
    def copy(b, ins, outs, send, recv, place):
        x, y, c, _ = place
        return pltpu.make_async_remote_copy(
            src_ref=ins[b], dst_ref=outs[b], send_sem=send.at[b], recv_sem=recv.at[b],
            device_id=(x, y, 1 - c), device_id_type=MESH)

    return _exchange_plan(len(halves), [jax.ShapeDtypeStruct(b.shape, b.dtype) for b in halves], copy)


def _run_plan(arrays, plan, name):
    out_shapes, sems, phases = plan
    n, m = len(arrays), len(out_shapes)

    def body(*refs):
        for phase in phases(refs[:n], refs[n:n + m], refs[n + m:]):
            phase()

    return pl.pallas_call(
        body, name=name, in_specs=[_HBM] * n, out_specs=[_HBM] * m, out_shape=out_shapes, scratch_shapes=sems,
        compiler_params=pltpu.CompilerParams(has_side_effects=True),
    )(*arrays)


def _fit_rows(n, target):
    for q in (2 * HALO, HALO):
        for t in range(min(n, target) // q * q, 0, -q):
            if n % t == 0:
                return t
    best = None
    for t in range(HALO, min(n, target) + 1, HALO):
        if n % t == 0:
            best = t
    assert best is not None, (n, target)
    return best


def _allreduce_small(buf, name):
    R, lanes = buf.shape

    def body(in_ref, out_ref, land, send, recv):
        x, y, c, _ = _place()
        me = 4 * x + 2 * y + c
        land[me] = in_ref[...]
        cps = []
        for r in range(1, 8):
            px, py, pc = x ^ (r >> 2), y ^ ((r >> 1) & 1), c ^ (r & 1)
            cp = pltpu.make_async_remote_copy(
                src_ref=in_ref, dst_ref=land.at[me], send_sem=send.at[r - 1], recv_sem=recv.at[me],
                device_id=(px, py, pc), device_id_type=MESH)
            cp.start()
            cps.append(cp)
        for r in range(1, 8):
            peer = 4 * (x ^ (r >> 2)) + 2 * (y ^ ((r >> 1) & 1)) + (c ^ (r & 1))
            pltpu.make_async_remote_copy(
                src_ref=in_ref, dst_ref=land.at[peer], send_sem=send.at[r - 1], recv_sem=recv.at[peer],
                device_id=(x, y, c), device_id_type=MESH).wait_recv()
        for cp in cps:
            cp.wait_send()
        acc = land[0]
        for d in range(1, 8):
            acc = acc + land[d]
        out_ref[...] = acc

    vm = pl.BlockSpec(memory_space=pltpu.VMEM)
    return pl.pallas_call(
        body, name=name, in_specs=[vm], out_specs=vm, out_shape=jax.ShapeDtypeStruct((R, lanes), buf.dtype),
        scratch_shapes=[pltpu.VMEM((8, R, lanes), buf.dtype), pltpu.SemaphoreType.DMA((7,)), pltpu.SemaphoreType.DMA((8,))],
        compiler_params=pltpu.CompilerParams(has_side_effects=True, vmem_limit_bytes=VMEM_LIMIT),
    )(buf)


ROW_BLOCK = 256
SB_BLOCK = 256
MM_TM, MM_TN, MM_TK = 1024, 512, 512
FFN_COLS = 512


def _lane_pad(vec, start):
    return jnp.pad(vec, ((0, 0), (start, LANES - start - vec.shape[1])))


def _local_step(x, target, wt):
    T, D = x.shape
    W = D // 2
    H = W // HEAD_DIM
    F = wt["w_down"].shape[0]
    bt = min(ROW_BLOCK, T)
    blk = min(SB_BLOCK, T)
    w_in = wt["w_in"]
    w_sb, w_dn = w_in[:, :3 * W], w_in[:, 3 * W:7 * W]
    w_ba = jnp.pad(w_in[:, 7 * W:], ((0, 0), (0, LANES - 2 * H)))
    w_out, w_up, w_down = wt["w_out"], wt["w_up"], wt["w_down"]
    a_log, dt_bias = _lane_pad(wt["dn_a_log"], H), _lane_pad(wt["dn_dt_bias"], H)
    mm = functools.partial(_mm, tm=MM_TM, tn=MM_TN)

    xn = _rms_fwd(x, wt["ln_mix_pre"], None, BF16, bt, "rms_mix_pre")
    psb = mm([(xn, w_sb, D)], "nn", BF16, name="proj_sb")
    pdn = mm([(xn, w_dn, D)], "nn", F32, name="proj_dn")
    pba = mm([(xn, w_ba, D)], "nn", F32, name="proj_ba")
    o_sb, mix_sb, lt = _sb_fwd(psb, wt["sb_out_gain"], blk, "sb_fwd")
    qn, kn, vv, bx, gx = _dn_pre_fwd(pdn, pba, wt["dn_conv_w"], a_log, dt_bias, bt, "dn_pre_fwd")
    o_dn, mix_dn, ss = _dn_core_fwd(qn, kn, vv, bx, gx, pdn, wt["dn_out_gain"], "dn_core_fwd")
    m = mm([(mix_sb, w_out[:W], MM_TK), (mix_dn, w_out[W:], MM_TK)], "nn", F32, name="out_proj")
    h = _rms_fwd(m, wt["ln_mix_post"], x, F32, bt, "rms_mix_post")
    hn = _rms_fwd(h, wt["ln_ffn_pre"], None, BF16, bt, "rms_ffn_pre")
    up = mm([(hn, w_up, D)], "nn", F32, name="ffn_up")
    bc = min(FFN_COLS, F)
    act = _ffn_mid_fwd(up, wt["ffn_conv_w"], wt["ffn_conv_b"], bt, bc, "ffn_mid_fwd")
    f = mm([(act, w_down, MM_TK)], "nn", F32, name="ffn_down")
    dy, df, g_ffn_post, sq = _loss_head(f, wt["ln_ffn_post"], h, target, bt, "loss_head")
    loss = 0.5 * jnp.sum(sq) / D

    da = mm([(df, w_down, D)], "nt", F32, name="d_act")
    g_w_down = mm([(act, df, MM_TK)], "tn", F32, name="g_w_down")
    dug, duv, dwg, dwv, dbg, dbv = _ffn_mid_bwd(up, wt["ffn_conv_w"], wt["ffn_conv_b"], da, bt, bc, "ffn_mid_bwd")
    dhn = mm([(dug, w_up[:, :F], MM_TK), (duv, w_up[:, F:], MM_TK)], "nt", F32, name="d_hn")
    g_w_up = jnp.concatenate([mm([(hn, dug, MM_TK)], "tn", F32, name="g_w_up_gate"),
                              mm([(hn, duv, MM_TK)], "tn", F32, name="g_w_up_val")], axis=1)
    dh, g_ffn_pre = _rms_bwd(h, wt["ln_ffn_pre"], dhn, dy, F32, bt, "rms_ffn_pre_bwd")
    dm, g_mix_post = _rms_bwd(m, wt["ln_mix_post"], dh, None, BF16, bt, "rms_mix_post_bwd")
    dmix = mm([(dm, w_out, D)], "nt", F32, name="d_mix")
    g_w_out = jnp.concatenate([mm([(mix_sb, dm, MM_TK)], "tn", F32, name="g_w_out_sb"),
                               mm([(mix_dn, dm, MM_TK)], "tn", F32, name="g_w_out_dn")], axis=0)
    do_sb, g_sb_gain = _headnorm_bwd(o_sb, wt["sb_out_gain"], dmix, bt, "sb_norm_bwd")
    dq, dk, dv = _sb_bwd(psb, do_sb, lt, blk, "sb_bwd")
    ddq, ddk, ddv, dbx, dgx, dz, g_dn_gain = _dn_core_bwd(qn, kn, vv, bx, gx, pdn, wt["dn_out_gain"], o_dn, dmix, ss,
                                                         W, "dn_core_bwd")
    dconv, dba, g_dn_conv, g_a_log, g_dt_bias = _dn_pre_bwd(pdn, pba, wt["dn_conv_w"], a_log, dt_bias,
                                                            ddq, ddk, ddv, dbx, dgx, bt, "dn_pre_bwd")
    pieces = [(dq, w_in[:, :W]), (dk, w_in[:, W:2 * W]), (dv, w_in[:, 2 * W:3 * W]), (dconv, w_in[:, 3 * W:6 * W]),
              (dz, w_in[:, 6 * W:7 * W]), (dba, w_ba)]
    dxn = mm([(d, wp, MM_TK) for d, wp in pieces], "nt", F32, name="d_xn")
    g_w_in = jnp.concatenate([mm([(xn, d, MM_TK)], "tn", F32, name=f"g_w_in_{i}") for i, (d, _) in enumerate(pieces)],
                             axis=1)[:, :7 * W + 2 * H]
    dx, g_mix_pre = _rms_bwd(x, wt["ln_mix_pre"], dxn, dh, F32, bt, "rms_mix_pre_bwd")

    grads = dict(
        w_in=g_w_in, sb_out_gain=g_sb_gain, dn_conv_w=g_dn_conv, dn_a_log=g_a_log[:, H:2 * H],
        dn_dt_bias=g_dt_bias[:, H:2 * H], dn_out_gain=jnp.sum(g_dn_gain, axis=0), w_out=g_w_out,
        ln_mix_pre=g_mix_pre, ln_mix_post=g_mix_post, w_up=g_w_up,
        ffn_conv_w=jnp.concatenate([dwg, dwv], axis=1), ffn_conv_b=jnp.concatenate([dbg, dbv], axis=1),
        w_down=g_w_down, ln_ffn_pre=g_ffn_pre, ln_ffn_post=g_ffn_post)
    return loss, dx, grads


WEIGHTS = ("w_in", "sb_out_gain", "dn_conv_w", "dn_a_log", "dn_dt_bias", "dn_out_gain", "w_out", "ln_mix_pre",
           "ln_mix_post", "w_up", "ffn_conv_w", "ffn_conv_b", "w_down", "ln_ffn_pre", "ln_ffn_post")
MATRICES = {"w_in": 1, "w_out": 0, "w_up": 1, "w_down": 0}
CONV_SHARDED = ("dn_conv_w", "ffn_conv_w")
SMALL = tuple(n for n in WEIGHTS if n not in MATRICES)
N_CHIPS = 4
ROW_QUANTUM = 32
ADD_ROWS = 2048
ADAM_ROWS = 128


def _pack(arrs, quantum):
    rows, layout, off = [], [], 0
    for a in arrs:
        n = int(np.prod(a.shape))
        r = -(-n // LANES)
        r = -(-r // HALO) * HALO
        rows.append(jnp.pad(a.reshape(-1), (0, r * LANES - n)).reshape(r, LANES))
        layout.append((off, r, n, a.shape))
        off += r
    total = -(-off // quantum) * quantum
    if total > off:
        rows.append(jnp.zeros((total - off, LANES), rows[0].dtype))
    return jnp.concatenate(rows, axis=0), layout


def _unpack(packed, layout):
    return [packed[off:off + r].reshape(-1)[:n].reshape(shape) for off, r, n, shape in layout]


def _kernel_packed(x, w_in, sb_out_gain, dn_conv_w, dn_a_log, dn_dt_bias, dn_out_gain, w_out, ln_mix_pre, ln_mix_post, w_up, ffn_conv_w, ffn_conv_b, w_down, ln_ffn_pre, ln_ffn_post, loss_target, m_w_in, m_sb_out_gain, m_dn_conv_w, m_dn_a_log, m_dn_dt_bias, m_dn_out_gain, m_w_out, m_ln_mix_pre, m_ln_mix_post, m_w_up, m_ffn_conv_w, m_ffn_conv_b, m_w_down, m_ln_ffn_pre, m_ln_ffn_post, v_w_in, v_sb_out_gain, v_dn_conv_w, v_dn_a_log, v_dn_dt_bias, v_dn_out_gain, v_w_out, v_ln_mix_pre, v_ln_mix_post, v_w_up, v_ffn_conv_w, v_ffn_conv_b, v_w_down, v_ln_ffn_pre, v_ln_ffn_post):
    given = dict(locals())
    wl = {n: given[n][0] for n in WEIGHTS}
    ml = {n: given["m_" + n][0] for n in WEIGHTS}
    vl = {n: given["v_" + n][0] for n in WEIGHTS}
    for d in (wl, ml, vl):
        for n in SMALL:
            if d[n].ndim == 1:
                d[n] = d[n][None]
    cx, cy, cc = lax.axis_index("x"), lax.axis_index("y"), lax.axis_index("c")
    chip = 2 * cx + cy

    mats, mat_layout = _pack([wl[n].astype(BF16) for n in MATRICES], ROW_QUANTUM)
    taps, tap_layout = _pack([wl[n] for n in CONV_SHARDED], ROW_QUANTUM)
    all_mats, all_taps = _gather_xy([mats, taps], "gather_weights")
    wt = {n: wl[n] for n in SMALL}
    for i, n in enumerate(MATRICES):
        wt[n] = jnp.concatenate([_unpack(all_mats[k], mat_layout)[i] for k in range(N_CHIPS)], axis=MATRICES[n])
    for i, n in enumerate(CONV_SHARDED):
        wt[n] = jnp.concatenate([_unpack(all_taps[k], tap_layout)[i] for k in range(N_CHIPS)], axis=1)

    loss, dx, grads = _local_step(x[0], loss_target[0], wt)
    loss = lax.psum(loss, ("x", "y", "c"))

    def shard_of(n, k):
        g, axis = grads[n], MATRICES[n]
        size = g.shape[axis] // N_CHIPS
        return lax.slice_in_dim(g, k * size, (k + 1) * size, axis=axis)

    packed = [_pack([shard_of(n, k) for n in MATRICES], ROW_QUANTUM) for k in range(N_CHIPS)]
    glayout = packed[0][1]
    gp = jnp.stack([p[0] for p in packed])
    half = gp.shape[1] // 2
    from_sibling = _swap_half_c(gp, "grad_swap_cores")
    chip_sum = _add_half(gp.reshape(N_CHIPS, 2, half, LANES), from_sibling, cc.reshape(1), ADD_ROWS, "grad_add_cores")
    from_chips = _scatter_xy(chip_sum, "grad_scatter_chips")
    reduced_half = _add_four(chip_sum, chip.reshape(1), from_chips, ADD_ROWS, "grad_add_chips")
    reduced = _join_c(reduced_half, "grad_join_cores").reshape(2 * half, LANES)
    gl = dict(zip(MATRICES, _unpack(reduced, glayout)))

    small, small_layout = _pack([grads[n] for n in SMALL], HALO)
    small = _allreduce_small(small, "grad_allreduce_small")
    for n, g in zip(SMALL, _unpack(small, small_layout)):
        if n in CONV_SHARDED:
            size = g.shape[1] // N_CHIPS
            g = lax.dynamic_slice_in_dim(g, chip * size, size, axis=1)
        gl[n] = g

    delta, new_m, new_v = {}, {}, {}
    for n, mine_half, sibling_half in zip(names, reduced, siblings):
        gl[n], delta[n], new_m[n], new_v[n] = _adamw_halves(wl[n], mine_half, sibling_half, place, ml[n], vl[n],
                                                            ADAM_ROWS, "adamw_" + n)
    packs = [_pack([d[n] for n in SMALL], HALO) for d in (wl, gl, ml, vl)]
    outs = _adamw(*[p[0] for p in packs], ADAM_ROWS, "adamw_small")
    for res, o in zip((delta, new_m, new_v), outs):
        res.update(zip(SMALL, _unpack(o, packs[0][1])))

    shaped = lambda d: [d[n].reshape(given[n].shape) for n in WEIGHTS]
    return (loss, dx[None], *shaped(gl), *shaped(delta), *shaped(new_m), *shaped(new_v))


UP_TILE = 1408
PAIR_ROWS = 256


def _reduce_to_chips(shares, place, names, swap_on, scatter_on):
    from_sibling = swap_on(shares, _swap_plan(shares))
    halves = [s.reshape(N_CHIPS, 2, s.shape[1] // 2, s.shape[2]) for s in shares]
    to_chips = [_add_cores(hv, fs, place, False, BF16, PAIR_ROWS, "grad_add_cores_" + n)
                for hv, fs, n in zip(halves, from_sibling, names)]
    own = [_add_cores(hv, fs, place, True, F32, PAIR_ROWS, "grad_add_cores_own_" + n)
           for hv, fs, n in zip(halves, from_sibling, names)]
    return own, scatter_on(to_chips, _scatter_plan(to_chips))


def _step(x, target, wt, late, chip, place):
    T, D = x.shape
    W = D // 2
    H = W // HEAD_DIM
    bt = min(ROW_BLOCK, T)
    blk = min(SB_BLOCK, T)
    w_in = wt["w_in"]
    a_log, dt_bias = _lane_pad(wt["dn_a_log"], H), _lane_pad(wt["dn_dt_bias"], H)
    mm = functools.partial(_mm, tm=MM_TM, tn=MM_TN)
    wide = functools.partial(_mm, tm=MM_TM, tn=2 * MM_TN)
    mm_up = functools.partial(_mm, tm=MM_TM, tn=UP_TILE)
    one = lambda a, b, tk=MM_TK: [(a, b, tk, 0, 0)]

    xn = _rms_fwd(x, wt["ln_mix_pre"], None, BF16, bt, "rms_mix_pre")
    psb = mm(one(xn, w_in, D), "nn", BF16, name="proj_sb", n_window=(0, 3 * W))
    pdn = mm(one(xn, w_in, D), "nn", F32, name="proj_dn", n_window=(3 * W, 4 * W))
    pba = mm(one(xn, w_in, D), "nn", F32, name="proj_ba", n_window=(7 * W, LANES))
    late_names = ("w_out", "w_up", "w_down")
    gathered_with = lambda names: _Comm([late[n] for n in names], _gather_plan([late[n] for n in names], [True] * len(names)))
    own_block_in = lambda theirs, names: [lax.dynamic_update_index_in_dim(t, late[n], chip, 0) for t, n in zip(theirs, names)]
    (o_sb, mix_sb, lt, swept), theirs = _sb_fwd(psb, wt["sb_out_gain"], blk, "sb_fwd", gathered_with(("w_out", "w_down")))
    w_out, w_down = [w.reshape(-1, D) for w in own_block_in(theirs, ("w_out", "w_down"))]
    qn, kn, vv, bx, gx = _dn_pre_fwd(pdn, pba, wt["dn_conv_w"], a_log, dt_bias, bt, "dn_pre_fwd")
    (o_dn, mix_dn, ss, tms), theirs = _dn_core_fwd(qn, kn, vv, bx, gx, pdn, wt["dn_out_gain"], "dn_core_fwd",
                                                  gathered_with(("w_up",)))
    w_up, = own_block_in(theirs, ("w_up",))
    F = w_down.shape[0]
    m = wide([(mix_sb, w_out, 2 * MM_TK, 0, 0), (mix_dn, w_out, 2 * MM_TK, 0, W)], "nn", F32, name="out_proj")
    h = _rms_fwd(m, wt["ln_mix_post"], x, F32, bt, "rms_mix_post")
    hn = _rms_fwd(h, wt["ln_ffn_pre"], None, BF16, bt, "rms_ffn_pre")
    up = mm_up(one(hn, w_up, D), "nn", F32, name="ffn_up")
    bc = min(FFN_COLS, F)
    act = _ffn_mid_fwd(up, wt["ffn_conv_w"], wt["ffn_conv_b"], bt, bc, "ffn_mid_fwd")
    f = mm(one(act, w_down, UP_TILE), "nn", F32, name="ffn_down")
    dy, df, g_ffn_post, sq = _loss_head(f, wt["ln_ffn_post"], h, target, bt, "loss_head")
    loss = 0.5 * jnp.sum(sq) / D

    da = mm(one(df, w_down, D), "nt", F32, name="d_act")
    g_w_down = _mm(one(act, df, 2 * MM_TK), "tn", F32, tm=UP_TILE, tn=2 * MM_TN, name="g_w_down")
    dug, duv, dwg, dwv, dbg, dbv = _ffn_mid_bwd(up, wt["ffn_conv_w"], wt["ffn_conv_b"], da, bt, bc, "ffn_mid_bwd")
    dhn = wide([(dug, w_up, UP_TILE, 0, 0), (duv, w_up, UP_TILE, 0, F)], "nt", F32, name="d_hn")
    shard = w_up.shape[2]
    g_w_up = mm_up(one(hn, dug, 2 * MM_TK), "tn", F32, name="g_w_up_gate", out_shard=shard,
                   into=(lax.empty(w_up.shape, F32), 0))
    g_w_up = mm_up(one(hn, duv, 2 * MM_TK), "tn", F32, name="g_w_up_val", out_shard=shard, into=(g_w_up, F))
    dh, g_ffn_pre = _rms_bwd(h, wt["ln_ffn_pre"], dhn, dy, F32, bt, "rms_ffn_pre_bwd")
    dm, g_mix_post = _rms_bwd(m, wt["ln_mix_post"], dh, None, BF16, bt, "rms_mix_post_bwd")
    dmix = mm(one(dm, w_out, D), "nt", F32, name="d_mix")
    g_w_out = jnp.concatenate([wide(one(mix_sb, dm, 2 * MM_TK), "tn", F32, name="g_w_out_sb"),
                               wide(one(mix_dn, dm, 2 * MM_TK), "tn", F32, name="g_w_out_dn")], axis=0)
    shares = [g_w_out.reshape(N_CHIPS, -1, D), g_w_up, g_w_down.reshape(N_CHIPS, -1, D)]
    carried = {}

    def swap_on(arrays, plan):
        (carried["do_sb"], carried["g_sb_gain"]), out = _headnorm_bwd(o_sb, wt["sb_out_gain"], dmix, bt, "sb_norm_bwd",
                                                                    _Comm(arrays, plan))
        return out

    def scatter_on(arrays, plan):
        carried["dn"], out = _dn_core_bwd(qn, kn, vv, bx, gx, pdn, wt["dn_out_gain"], o_dn, dmix, ss, tms, W,
                                          "dn_core_bwd", _Comm(arrays, plan))
        return out

    early = _reduce_to_chips(shares, place, late_names, swap_on, scatter_on)
    g_sb_gain = carried["g_sb_gain"]
    (dq, dk, dv), _ = _sb_bwd(psb, carried["do_sb"], lt, swept, blk, "sb_bwd", _Comm())
    ddq, ddk, ddv, dbx, dgx, dz, g_dn_gain = carried["dn"]
    dconv, dba, g_dn_conv, g_a_log, g_dt_bias = _dn_pre_bwd(pdn, pba, wt["dn_conv_w"], a_log, dt_bias,
                                                            ddq, ddk, ddv, dbx, dgx, bt, "dn_pre_bwd")
    pieces = [(dq, 0), (dk, W), (dv, 2 * W), (dconv, 3 * W), (dz, 6 * W), (dba, 7 * W)]
    g_w_in = [wide(one(xn, d, 2 * MM_TK), "tn", F32, name=f"g_w_in_{i}") for i, (d, _) in enumerate(pieces)]
    g_w_in[-1] = g_w_in[-1][:, :2 * H]
    g_in = jnp.concatenate(g_w_in, axis=1)

    def with_d_xn(arrays, plan):
        carried["dxn"], out = mm([(d, w_in, MM_TK, 0, k0) for d, k0 in pieces], "nt", F32, name="d_xn",
                                 comm=_Comm(arrays, plan))
        return out

    last = _reduce_to_chips([g_in.reshape(D, N_CHIPS, -1).transpose(1, 0, 2)], place, ["w_in"],
                            lambda arrays, plan: _run_plan(arrays, plan, "grad_swap_cores"), with_d_xn)
    dx, g_mix_pre = _rms_bwd(x, wt["ln_mix_pre"], carried["dxn"], dh, F32, bt, "rms_mix_pre_bwd")
    exchanged = dict(zip(late_names, zip(*early)))
    exchanged["w_in"] = (last[0][0], last[1][0])

    grads = dict(
        sb_out_gain=g_sb_gain, dn_conv_w=g_dn_conv, dn_a_log=g_a_log[:, H:2 * H],
        dn_dt_bias=g_dt_bias[:, H:2 * H], dn_out_gain=jnp.sum(g_dn_gain, axis=0),
        ln_mix_pre=g_mix_pre, ln_mix_post=g_mix_post,
        ffn_conv_w=jnp.concatenate([dwg, dwv], axis=1), ffn_conv_b=jnp.concatenate([dbg, dbv], axis=1),
        ln_ffn_pre=g_ffn_pre, ln_ffn_post=g_ffn_post)
    return loss, dx, grads, exchanged


def kernel(x, w_in, sb_out_gain, dn_conv_w, dn_a_log, dn_dt_bias, dn_out_gain, w_out, ln_mix_pre, ln_mix_post, w_up, ffn_conv_w, ffn_conv_b, w_down, ln_ffn_pre, ln_ffn_post, loss_target, m_w_in, m_sb_out_gain, m_dn_conv_w, m_dn_a_log, m_dn_dt_bias, m_dn_out_gain, m_w_out, m_ln_mix_pre, m_ln_mix_post, m_w_up, m_ffn_conv_w, m_ffn_conv_b, m_w_down, m_ln_ffn_pre, m_ln_ffn_post, v_w_in, v_sb_out_gain, v_dn_conv_w, v_dn_a_log, v_dn_dt_bias, v_dn_out_gain, v_w_out, v_ln_mix_pre, v_ln_mix_post, v_w_up, v_ffn_conv_w, v_ffn_conv_b, v_w_down, v_ln_ffn_pre, v_ln_ffn_post):
    given = dict(locals())
    wl = {n: given[n][0] for n in WEIGHTS}
    ml = {n: given["m_" + n][0] for n in WEIGHTS}
    vl = {n: given["v_" + n][0] for n in WEIGHTS}
    for d in (wl, ml, vl):
        for n in SMALL:
            if d[n].ndim == 1:
                d[n] = d[n][None]
    cx, cy, cc = lax.axis_index("x"), lax.axis_index("y"), lax.axis_index("c")
    chip = 2 * cx + cy
    D = x.shape[2]
    W = D // 2

    first = ("w_in",) + CONV_SHARDED
    mine = [wl["w_in"].astype(BF16)] + [wl[n] for n in CONV_SHARDED]
    theirs = _run_plan(mine, _gather_plan(mine, [True, False, False]), "gather_w_in")
    got = {n: lax.dynamic_update_index_in_dim(t, s, chip, 0) for n, t, s in zip(first, theirs, mine)}
    columns = lambda g: g.transpose(1, 0, 2).reshape(g.shape[1], N_CHIPS * g.shape[2])
    wt = {n: wl[n] for n in SMALL}
    w_in_all = columns(got["w_in"])
    wt["w_in"] = jnp.pad(w_in_all, ((0, 0), (0, 7 * W + LANES - w_in_all.shape[1])))
    for n in CONV_SHARDED:
        wt[n] = columns(got[n])
    late = {n: wl[n].astype(BF16) for n in ("w_out", "w_up", "w_down")}

    place = jnp.stack([cc, chip]).astype(jnp.int32)
    loss, dx, grads, exchanged = _step(x[0], loss_target[0], wt, late, chip, place)
    loss = lax.psum(loss, ("x", "y", "c"))

    names = list(MATRICES)
    reduced = [_add_chips(*exchanged[n], PAIR_ROWS, "grad_add_chips_" + n) for n in names]
    siblings = _run_plan(reduced, _join_plan(reduced), "grad_join_cores")
    gl = {}

    small, small_layout = _pack([grads[n] for n in SMALL], HALO)
    small = _allreduce_small(small, "grad_allreduce_small")
    for n, g in zip(SMALL, _unpack(small, small_layout)):
        if n in CONV_SHARDED:
            size = g.shape[1] // N_CHIPS
            g = lax.dynamic_slice_in_dim(g, chip * size, size, axis=1)
        gl[n] = g

    delta, new_m, new_v = {}, {}, {}
    for n, mine_half, sibling_half in zip(names, reduced, siblings):
        gl[n], delta[n], new_m[n], new_v[n] = _adamw_halves(wl[n], mine_half, sibling_half, place, ml[n], vl[n],
                                                            ADAM_ROWS, "adamw_" + n)
    packs = [_pack([d[n] for n in SMALL], HALO) for d in (wl, gl, ml, vl)]
    outs = _adamw(*[p[0] for p in packs], ADAM_ROWS, "adamw_small")
    for res, o in zip((delta, new_m, new_v), outs):
        res.update(zip(SMALL, _unpack(o, packs[0][1])))

    shaped = lambda d: [d[n].reshape(given[n].shape) for n in WEIGHTS]
    return (loss, dx[None], *shaped(gl), *shaped(delta), *shaped(new_m), *shaped(new_v))
```

```python
import functools

import numpy as np
import jax
import jax.numpy as jnp
from jax import lax
from jax.experimental import pallas as pl
from jax.experimental.pallas import tpu as pltpu

F32 = jnp.float32
BF16 = jnp.bfloat16
HEAD_DIM = 128
CHUNK = 64
ROWS = 2 * CHUNK
EPS = 1e-6
EXP_UNDERFLOW = 110.0
LANES = 128
HALO = 8
VMEM_LIMIT = 48 * 1024 * 1024
ADAM_LR, ADAM_B1, ADAM_B2, ADAM_EPS, ADAM_WD, ADAM_STEP = 0.001, 0.9, 0.999, 1e-08, 0.01, 10
MESH = pl.DeviceIdType.MESH

NN = (((1,), (0,)), ((), ()))
NT = (((1,), (1,)), ((), ()))
TN = (((0,), (0,)), ((), ()))


def _params(n_axes):
    return pltpu.CompilerParams(dimension_semantics=("arbitrary",) * n_axes, vmem_limit_bytes=VMEM_LIMIT)


def _bdot(a, b, dims=NN):
    return lax.dot_general(a.astype(BF16), b.astype(BF16), dims, preferred_element_type=F32)


def _split3(a):
    hi = a.astype(BF16)
    r1 = a - hi.astype(F32)
    mid = r1.astype(BF16)
    lo = (r1 - mid.astype(F32)).astype(BF16)
    return hi, mid, lo


def _dot3(a, sel, dims=NN):
    return sum(lax.dot_general(p, sel, dims, preferred_element_type=F32) for p in _split3(a))


def _dot3r(sel, a, dims=NN):
    return sum(lax.dot_general(sel, p, dims, preferred_element_type=F32) for p in _split3(a))


def _iota2(n, m):
    return lax.broadcasted_iota(jnp.int32, (n, m), 0), lax.broadcasted_iota(jnp.int32, (n, m), 1)


def _sigmoid(x):
    return 1.0 / (1.0 + jnp.exp(-x))


def _softplus(x):
    return jnp.maximum(x, 0.0) + jnp.log(1.0 + jnp.exp(-jnp.abs(x)))


def _fit(values, target):
    values = [v for v in (values if isinstance(values, (list, tuple)) else [values]) if v]
    best = None
    for t in range(LANES, min(min(values), target) + 1, LANES):
        if all(v % t == 0 for v in values):
            best = t
    assert best is not None, (values, target)
    return best


def _mm(parts, mode, out_dtype, tm, tn, name, n_window=None, out_shard=None, into=None, comm=None):
    dims = {"nn": NN, "nt": NT, "tn": TN}[mode]
    a0, b0 = parts[0][0], parts[0][1]
    b3 = b0.ndim == 3
    shard_c = b0.shape[2] if b3 else None
    M = a0.shape[1] if mode == "tn" else a0.shape[0]
    if mode == "nt":
        n_full = b0.shape[1] if b3 else b0.shape[0]
    else:
        n_full = b0.shape[0] * b0.shape[2] if b3 else b0.shape[1]
    n0, N = n_window if n_window is not None else (0, n_full)
    out_n0 = into[1] if into is not None else 0
    tm = _fit(M, tm)
    tn = _fit([N, n0, out_n0, out_shard, shard_c if mode != "nt" else None], tn)
    specs_a, specs_b, offs, nks = [], [], [], []
    off = 0
    for a, b, tk, a_k0, b_k0 in parts:
        K = a.shape[0] if mode == "tn" else a.shape[1]
        tk = _fit([K, a_k0, b_k0, shard_c if mode == "nt" else None], tk)
        nk = K // tk
        kk = lambda k, o=off, n=nk: jnp.clip(k - o, 0, n - 1)
        ao, bo, no = a_k0 // tk, b_k0 // tk, n0 // tn
        if mode == "tn":
            specs_a.append(pl.BlockSpec((tk, tm), lambda i, j, k, kk=kk, ao=ao: (kk(k) + ao, i)))
        else:
            specs_a.append(pl.BlockSpec((tm, tk), lambda i, j, k, kk=kk, ao=ao: (i, kk(k) + ao)))
        if mode == "nt":
            if b3:
                per = shard_c // tk
                specs_b.append(pl.BlockSpec((None, tn, tk), lambda i, j, k, kk=kk, bo=bo, per=per:
                                            ((kk(k) + bo) // per, j, (kk(k) + bo) % per)))
            else:
                specs_b.append(pl.BlockSpec((tn, tk), lambda i, j, k, kk=kk, bo=bo: (j, kk(k) + bo)))
        else:
            if b3:
                per = shard_c // tn
                specs_b.append(pl.BlockSpec((None, tk, tn), lambda i, j, k, kk=kk, bo=bo, no=no, per=per:
                                            ((j + no) // per, kk(k) + bo, (j + no) % per)))
            else:
                specs_b.append(pl.BlockSpec((tk, tn), lambda i, j, k, kk=kk, bo=bo, no=no: (kk(k) + bo, j + no)))
        offs.append(off)
        nks.append(nk)
        off += nk
    nk_total = off
    n_parts = len(parts)

    comm = comm if comm is not None else _Comm()
    grid = (M // tm, N // tn, nk_total)
    n_in = 2 * n_parts + (1 if into is not None else 0)

    def body(*refs):
        ins, (o_ref,), scratch, (first, mid, last) = comm.split(refs, n_in, 1, 0 if nk_total == 1 else 1)
        a_refs, b_refs = ins[:n_parts], ins[n_parts:2 * n_parts]
        at = lambda step: functools.reduce(lambda x, y: x & y, [pl.program_id(d) == step[d] for d in range(3)])
        pl.when(at((0, 0, 0)))(first)
        pl.when(at((grid[0] // 2, 0, 0)))(mid)
        if nk_total == 1:
            o_ref[...] = _bdot(a_refs[0][...], b_refs[0][...], dims).astype(out_dtype)
        else:
            acc = scratch[0]
            k = pl.program_id(2)

            @pl.when(k == 0)
            def _():
                acc[...] = jnp.zeros_like(acc)

            for p in range(n_parts):
                @pl.when((k >= offs[p]) & (k < offs[p] + nks[p]))
                def _(p=p):
                    acc[...] += _bdot(a_refs[p][...], b_refs[p][...], dims)

            @pl.when(k == nk_total - 1)
            def _():
                o_ref[...] = acc[...].astype(out_dtype)
        pl.when(at(tuple(g - 1 for g in grid)))(last)

    jo = out_n0 // tn
    if out_shard is not None:
        per_o = out_shard // tn
        out_spec = pl.BlockSpec((None, tm, tn), lambda i, j, k: ((j + jo) // per_o, i, (j + jo) % per_o))
        out_shape = jax.ShapeDtypeStruct((N // out_shard, M, out_shard), out_dtype)
    else:
        out_spec = pl.BlockSpec((tm, tn), lambda i, j, k: (i, j + jo))
        out_shape = jax.ShapeDtypeStruct((M, N), out_dtype)
    ins = [p[0] for p in parts] + [p[1] for p in parts]
    in_specs = specs_a + specs_b
    aliases = {}
    if into is not None:
        out_shape = jax.ShapeDtypeStruct(into[0].shape, into[0].dtype)
        aliases = {len(ins): 0}
        ins.append(into[0])
        in_specs.append(pl.BlockSpec(memory_space=pl.ANY))
    (out,), carried = comm.call(body, name, grid, in_specs, (out_spec,), (out_shape,),
                                [] if nk_total == 1 else [pltpu.VMEM((tm, tn), F32)], ins, aliases)
    return (out, carried) if comm.phases is not None else out


def _rms_fwd(x, gain, resid, out_dtype, bt, name):
    T, D = x.shape
    row = pl.BlockSpec((bt, D), lambda i: (i, 0))
    vec = pl.BlockSpec((1, D), lambda i: (0, 0))

    def body(*refs):
        x_ref, g_ref = refs[0], refs[1]
        o_ref = refs[-1]
        xv = x_ref[...]
        y = xv * lax.rsqrt(jnp.mean(xv * xv, axis=-1, keepdims=True) + EPS) * g_ref[...]
        if resid is not None:
            y = refs[2][...] + y
        o_ref[...] = y.astype(out_dtype)

    ins = [x, gain] + ([resid] if resid is not None else [])
    return pl.pallas_call(
        body, name=name, grid=(T // bt,),
        in_specs=[row, vec] + ([row] if resid is not None else []),
        out_specs=row, out_shape=jax.ShapeDtypeStruct((T, D), out_dtype), compiler_params=_params(1),
    )(*ins)


def _rms_bwd_math(xv, g, dy):
    r = lax.rsqrt(jnp.mean(xv * xv, axis=-1, keepdims=True) + EPS)
    n = xv * r
    gy = dy * g
    dx = r * (gy - n * jnp.mean(gy * n, axis=-1, keepdims=True))
    return dx, dy * n


def _rms_bwd(x, gain, dy, resid, out_dtype, bt, name):
    T, D = x.shape
    row = pl.BlockSpec((bt, D), lambda i: (i, 0))
    vec = pl.BlockSpec((1, D), lambda i: (0, 0))

    def body(*refs):
        x_ref, g_ref, dy_ref = refs[0], refs[1], refs[2]
        dx_ref, dg_ref = refs[-2], refs[-1]
        dx, dgp = _rms_bwd_math(x_ref[...], g_ref[...], dy_ref[...].astype(F32))
        if resid is not None:
            dx = refs[3][...] + dx
        dx_ref[...] = dx.astype(out_dtype)

        @pl.when(pl.program_id(0) == 0)
        def _():
            dg_ref[...] = jnp.zeros_like(dg_ref)

        dg_ref[...] += jnp.sum(dgp, axis=0, keepdims=True)

    ins = [x, gain, dy] + ([resid] if resid is not None else [])
    return pl.pallas_call(
        body, name=name, grid=(T // bt,),
        in_specs=[row, vec, row] + ([row] if resid is not None else []),
        out_specs=(row, vec),
        out_shape=(jax.ShapeDtypeStruct((T, D), out_dtype), jax.ShapeDtypeStruct((1, D), F32)),
        compiler_params=_params(1),
    )(*ins)


def _loss_head(f, gain, h, target, bt, name):
    T, D = f.shape
    row = pl.BlockSpec((bt, D), lambda i: (i, 0))
    vec = pl.BlockSpec((1, D), lambda i: (0, 0))

    def body(f_ref, g_ref, h_ref, t_ref, dy_ref, df_ref, dg_ref, sq_ref):
        fv, g = f_ref[...], g_ref[...]
        r = lax.rsqrt(jnp.mean(fv * fv, axis=-1, keepdims=True) + EPS)
        n = fv * r
        err = (h_ref[...] + n * g) - t_ref[...]
        dy = err * (1.0 / D)
        gy = dy * g
        df = r * (gy - n * jnp.mean(gy * n, axis=-1, keepdims=True))
        dy_ref[...] = dy
        df_ref[...] = df.astype(BF16)

        @pl.when(pl.program_id(0) == 0)
        def _():
            dg_ref[...] = jnp.zeros_like(dg_ref)
            sq_ref[...] = jnp.zeros_like(sq_ref)

        dg_ref[...] += jnp.sum(dy * n, axis=0, keepdims=True)
        sq_ref[...] += jnp.sum(err * err, axis=0, keepdims=True)

    return pl.pallas_call(
        body, name=name, grid=(T // bt,), in_specs=[row, vec, row, row], out_specs=(row, row, vec, vec),
        out_shape=(jax.ShapeDtypeStruct((T, D), F32), jax.ShapeDtypeStruct((T, D), BF16),
                   jax.ShapeDtypeStruct((1, D), F32), jax.ShapeDtypeStruct((1, D), F32)),
        compiler_params=_params(1),
    )(f, gain, h, target)


def _sb_logits(q, k, valid):
    z = lax.dot_general(q, k, NT, preferred_element_type=F32) * (HEAD_DIM ** -0.5)
    sp = jnp.log(1.0 + jnp.exp(-jnp.abs(z)))
    lb = jnp.minimum(z, 0.0) - sp
    l1 = -(jnp.maximum(z, 0.0) + sp)
    return lb, (l1 if valid is None else jnp.where(valid, l1, 0.0))


def _masked(valid, x):
    return x if valid is None else jnp.where(valid, x, 0.0)


def _heads_per_step(n_heads):
    return 2 if n_heads % 2 == 0 else 1


def _dot2(a, sel):
    hi = a.astype(BF16)
    lo = (a - hi.astype(F32)).astype(BF16)
    return jnp.dot(hi, sel, preferred_element_type=F32) + jnp.dot(lo, sel, preferred_element_type=F32)


class _Comm:
    def __init__(self, arrays=(), plan=((), (), None)):
        self.arrays = list(arrays)
        self.out_shapes, self.sems, self.phases = list(plan[0]), list(plan[1]), plan[2]

    def split(self, refs, n_in, n_out, n_scratch):
        a, o = len(self.arrays), len(self.out_shapes)
        cuts = np.cumsum([0, n_in, a, n_out, o, n_scratch])
        ins, cin, outs, cout, scratch = (refs[cuts[t]:cuts[t + 1]] for t in range(5))
        if self.phases is None:
            return ins, outs, scratch, (lambda: None,) * 3
        return ins, outs, scratch, self.phases(cin, cout, refs[cuts[5]:])

    def call(self, body, name, grid, in_specs, out_specs, out_shape, scratch_shapes, operands, aliases=None):
        outs = pl.pallas_call(
            body, name=name, grid=grid, in_specs=list(in_specs) + [_HBM] * len(self.arrays),
            out_specs=tuple(out_specs) + (_HBM,) * len(self.out_shapes),
            out_shape=tuple(out_shape) + tuple(self.out_shapes),
            scratch_shapes=list(scratch_shapes) + self.sems, input_output_aliases=aliases or {},
            compiler_params=pltpu.CompilerParams(dimension_semantics=("arbitrary",) * len(grid),
                                                 vmem_limit_bytes=VMEM_LIMIT, has_side_effects=self.phases is not None),
        )(*operands, *self.arrays)
        return outs[:len(out_shape)], outs[len(out_shape):]


def _sb_fwd(qkv, gain, blk, name, comm):
    T, W = qkv.shape[0], qkv.shape[1] // 3
    H = W // HEAD_DIM
    nq = T // blk
    hp = _heads_per_step(H)
    ng = H // hp
    lanes = [slice(t * HEAD_DIM, (t + 1) * HEAD_DIM) for t in range(hp)]

    def body(*refs):
        (q_ref, k_ref, v_ref, g_ref), (o_ref, mix_ref, lt_ref, swept_ref), _, (first, mid, last) = comm.split(refs, 4, 4, 0)
        h, i = pl.program_id(0), pl.program_id(1)
        pl.when((h == 0) & (i == 0))(first)
        pl.when((h == ng // 2) & (i == 0))(mid)
        q = [q_ref[:, ln] for ln in lanes]
        row, col = _iota2(blk, blk)
        after = (row > col).astype(BF16)

        def step(kb, carry, valid):
            ks = pl.ds(pl.multiple_of(kb * blk, blk), blk)
            out = []
            for t, (run, acc) in enumerate(carry):
                lb, l1 = _sb_logits(q[t], k_ref[ks, lanes[t]], valid)
                att = _masked(valid, jnp.exp(lb + _dot2(l1, after) + run))
                out.append((run + jnp.sum(l1, axis=1, keepdims=True), acc + _bdot(att, v_ref[ks, lanes[t]])))
            return tuple(out)

        zero = (jnp.zeros((blk, 1), F32), jnp.zeros((blk, HEAD_DIM), F32))
        def alive(state):
            jj, c = state
            return (jj < i) & (functools.reduce(jnp.maximum, [jnp.max(run) for run, _ in c]) > -EXP_UNDERFLOW)

        swept, carry = lax.while_loop(alive, lambda st: (st[0] + 1, step(i - 1 - st[0], st[1], None)),
                                      (jnp.int32(0), step(i, (zero,) * hp, col < row)))
        swept_ref[h, i] = swept
        for t, (run, o) in enumerate(carry):
            o_ref[:, lanes[t]] = o
            r = lax.rsqrt(jnp.mean(o * o, axis=-1, keepdims=True) + EPS)
            mix_ref[:, lanes[t]] = (o * r * g_ref[...]).astype(BF16)
            lt_ref[:, lanes[t]] = jnp.broadcast_to(run, (blk, HEAD_DIM))
        pl.when((h == ng - 1) & (i == nq - 1))(last)

    wide = hp * HEAD_DIM
    qb = pl.BlockSpec((blk, wide), lambda h, i: (i, h))
    return comm.call(
        body, name, (ng, nq),
        [qb, pl.BlockSpec((T, wide), lambda h, i: (0, ng + h)),
         pl.BlockSpec((T, wide), lambda h, i: (0, 2 * ng + h)), pl.BlockSpec((1, HEAD_DIM), lambda h, i: (0, 0))],
        (qb, qb, qb, pl.BlockSpec(memory_space=pltpu.SMEM)),
        (jax.ShapeDtypeStruct((T, W), F32), jax.ShapeDtypeStruct((T, W), BF16), jax.ShapeDtypeStruct((T, W), F32),
         jax.ShapeDtypeStruct((ng, nq), jnp.int32)),
        [], (qkv, qkv, qkv, gain))


def _headnorm_bwd(o, gain, dmix, bt, name, comm):
    T, W = o.shape
    H = W // HEAD_DIM
    nt = T // bt
    blk = pl.BlockSpec((bt, HEAD_DIM), lambda i, h: (i, h))
    vec = pl.BlockSpec((1, HEAD_DIM), lambda i, h: (0, 0))

    def body(*refs):
        (o_ref, g_ref, d_ref), (do_ref, dg_ref), _, (first, mid, last) = comm.split(refs, 3, 2, 0)
        i, h = pl.program_id(0), pl.program_id(1)
        pl.when((i == 0) & (h == 0))(first)
        pl.when((i == nt // 2) & (h == 0))(mid)
        do, dgp = _rms_bwd_math(o_ref[...], g_ref[...], d_ref[...])
        do_ref[...] = do

        @pl.when((i == 0) & (h == 0))
        def _():
            dg_ref[...] = jnp.zeros_like(dg_ref)

        dg_ref[...] += jnp.sum(dgp, axis=0, keepdims=True)
        pl.when((i == nt - 1) & (h == H - 1))(last)

    return comm.call(body, name, (nt, H), [blk, vec, blk], (blk, vec),
                     (jax.ShapeDtypeStruct((T, W), F32), jax.ShapeDtypeStruct((1, HEAD_DIM), F32)), [], (o, gain, dmix))


def _sb_bwd(qkv, do, lt, swept, blk, name, comm):
    T, W = qkv.shape[0], qkv.shape[1] // 3
    H = W // HEAD_DIM
    nq = T // blk
    scale = HEAD_DIM ** -0.5
    hp = _heads_per_step(H)
    ng = H // hp
    lanes = [slice(t * HEAD_DIM, (t + 1) * HEAD_DIM) for t in range(hp)]

    def body(*refs):
        ((q_ref, k_ref, v_ref, do_ref, lt_ref, swept_ref), (dq_ref, dk_out, dv_out), (dk_ref, dv_ref),
         (first, mid, last)) = comm.split(refs, 6, 3, 2)
        h, i = pl.program_id(0), pl.program_id(1)
        pl.when((h == 0) & (i == 0))(first)
        pl.when((h == ng // 2) & (i == 0))(mid)

        @pl.when(i == 0)
        def _():
            dk_ref[...] = jnp.zeros_like(dk_ref)
            dv_ref[...] = jnp.zeros_like(dv_ref)

        q = [q_ref[:, ln] for ln in lanes]
        dob = [do_ref[:, ln].astype(BF16) for ln in lanes]
        total = [lt_ref[:, ln][:, :1] for ln in lanes]
        row, col = _iota2(blk, blk)
        upto = (row <= col).astype(BF16)
        before = (row < col).astype(BF16)

        def step(kb, carry, valid):
            ks = pl.ds(pl.multiple_of(kb * blk, blk), blk)
            out = []
            for t, (seen, psum, dq) in enumerate(carry):
                k, v = k_ref[ks, lanes[t]], v_ref[ks, lanes[t]]
                lb, l1 = _sb_logits(q[t], k, valid)
                later = total[t] - seen - _dot2(l1, upto)
                att = _masked(valid, jnp.exp(lb + later))
                p = att * lax.dot_general(dob[t], v, NT, preferred_element_type=F32)
                c = psum + _dot2(p, before)
                sig = jnp.exp(lb)
                dz = (_masked(valid, p * (1.0 - sig) - c * sig) * scale).astype(BF16)
                dq = dq + jnp.dot(dz, k, preferred_element_type=F32)
                dk_ref[ks, lanes[t]] += lax.dot_general(dz, q[t], TN, preferred_element_type=F32)
                dv_ref[ks, lanes[t]] += lax.dot_general(att.astype(BF16), dob[t], TN, preferred_element_type=F32)
                out.append((seen + jnp.sum(l1, axis=1, keepdims=True), psum + jnp.sum(p, axis=1, keepdims=True), dq))
            return tuple(out)

        zero = jnp.zeros((blk, 1), F32)
        start = ((zero, zero, jnp.zeros((blk, HEAD_DIM), F32)),) * hp
        carry = step(i, lax.fori_loop(i - swept_ref[h, i], i, lambda kb, c: step(kb, c, None), start), col < row)
        for t in range(hp):
            dq_ref[:, lanes[t]] = carry[t][2].astype(BF16)

        @pl.when(i == nq - 1)
        def _():
            dk_out[...] = dk_ref[...].astype(BF16)
            dv_out[...] = dv_ref[...].astype(BF16)

        pl.when((h == ng - 1) & (i == nq - 1))(last)

    wide = hp * HEAD_DIM
    qb = pl.BlockSpec((blk, wide), lambda h, i: (i, h))
    head = pl.BlockSpec((T, wide), lambda h, i: (0, h))
    out = jax.ShapeDtypeStruct((T, W), BF16)
    return comm.call(
        body, name, (ng, nq),
        [qb, pl.BlockSpec((T, wide), lambda h, i: (0, ng + h)),
         pl.BlockSpec((T, wide), lambda h, i: (0, 2 * ng + h)), qb, qb, pl.BlockSpec(memory_space=pltpu.SMEM)],
        (qb, head, head), (out, out, out), [pltpu.VMEM((T, wide), F32)] * 2, (qkv, qkv, qkv, do, lt, swept))


def _expanders(H):
    lane = np.arange(H * HEAD_DIM) // HEAD_DIM
    eb = np.zeros((LANES, H * HEAD_DIM), np.float32)
    eg = np.zeros((LANES, H * HEAD_DIM), np.float32)
    eb[lane, np.arange(H * HEAD_DIM)] = 1.0
    eg[H + lane, np.arange(H * HEAD_DIM)] = 1.0
    sb = np.zeros((H * HEAD_DIM, LANES), np.float32)
    sg = np.zeros((H * HEAD_DIM, LANES), np.float32)
    sb[np.arange(H) * HEAD_DIM, np.arange(H)] = 1.0
    sg[np.arange(H) * HEAD_DIM, H + np.arange(H)] = 1.0
    return [jnp.asarray(m, BF16) for m in (eb, eg, sb, sg)]


def _conv_taps(ext_ref, w, n_out, lead):
    K = w.shape[0]
    out = None
    for j in range(K):
        term = ext_ref[pl.ds(lead - (K - 1) + j, n_out), :] * w[j:j + 1, :]
        out = term if out is None else out + term
    return out


def _l2_heads(s, H, fn):
    return jnp.concatenate([fn(s[:, h * HEAD_DIM:(h + 1) * HEAD_DIM]) for h in range(H)], axis=1)


def _dn_pre_fwd(pdn, pba, conv_w, a_log, dt_bias, bt, name):
    T, W = pdn.shape[0], pdn.shape[1] // 4
    H = W // HEAD_DIM
    eb, eg, _, _ = _expanders(H)
    nb = T // bt

    def body(x_ref, prev_ref, ba_ref, w_ref, al_ref, dt_ref, eb_ref, eg_ref,
             q_ref, k_ref, v_ref, bx_ref, gx_ref, ext):
        i = pl.program_id(0)
        ext[pl.ds(0, HALO), :] = jnp.where(i > 0, prev_ref[...], 0.0)
        ext[pl.ds(HALO, bt), :] = x_ref[...]
        c = _conv_taps(ext, w_ref[...], bt, HALO)
        s = c * _sigmoid(c)
        q_ref[...] = _l2_heads(s[:, :W], H, lambda t: t * lax.rsqrt(jnp.sum(t * t, axis=-1, keepdims=True) + EPS)
                               * (HEAD_DIM ** -0.5))
        k_ref[...] = _l2_heads(s[:, W:2 * W], H, lambda t: t * lax.rsqrt(jnp.sum(t * t, axis=-1, keepdims=True) + EPS))
        v_ref[...] = s[:, 2 * W:]
        ba = ba_ref[...]
        beta = _sigmoid(ba)
        graw = -jnp.exp(al_ref[...]) * _softplus(ba + dt_ref[...])
        row, col = _iota2(bt, bt)
        tri = ((row // CHUNK == col // CHUNK) & (row >= col)).astype(BF16)
        gcum = _dot3r(tri, graw)
        bx_ref[...] = _dot3(beta, eb_ref[...])
        gx_ref[...] = _dot3(gcum, eg_ref[...])

    C = 3 * W
    rowb = lambda w: pl.BlockSpec((bt, w), lambda i: (i, 0))
    full = lambda a: pl.BlockSpec(a.shape, lambda i: (0,) * a.ndim)
    out = jax.ShapeDtypeStruct((T, W), F32)
    return pl.pallas_call(
        body, name=name, grid=(nb,),
        in_specs=[rowb(C), pl.BlockSpec((HALO, C), lambda i: (jnp.maximum(i * (bt // HALO) - 1, 0), 0)),
                  rowb(LANES), full(conv_w), full(a_log), full(dt_bias), full(eb), full(eg)],
        out_specs=(rowb(W),) * 5, out_shape=(out,) * 5,
        scratch_shapes=[pltpu.VMEM((bt + HALO, C), F32)], compiler_params=_params(1),
    )(pdn, pdn, pba, conv_w, a_log, dt_bias, eb, eg)


def _dn_pre_bwd(pdn, pba, conv_w, a_log, dt_bias, dq, dk, dv, dbx, dgx, bt, name):
    T, W = pdn.shape[0], pdn.shape[1] // 4
    H = W // HEAD_DIM
    C = 3 * W
    K = conv_w.shape[0]
    _, _, sb, sg = _expanders(H)
    nb = T // bt
    n_ext = bt + HALO

    def body(x_ref, prev_ref, next_ref, ba_ref, w_ref, al_ref, dt_ref, sb_ref, sg_ref,
             dq_ref, dqn_ref, dk_ref, dkn_ref, dv_ref, dvn_ref, dbx_ref, dgx_ref,
             dx_ref, dba_ref, dw_ref, dal_ref, ddt_ref, ext, dext, dcext):
        i = pl.program_id(0)
        last = i == nb - 1
        ext[pl.ds(0, HALO), :] = jnp.where(i > 0, prev_ref[...], 0.0)
        ext[pl.ds(HALO, bt), :] = x_ref[...]
        ext[pl.ds(HALO + bt, HALO), :] = jnp.where(last, 0.0, next_ref[...])
        dext[pl.ds(0, bt), pl.ds(0, W)] = dq_ref[...]
        dext[pl.ds(0, bt), pl.ds(W, W)] = dk_ref[...]
        dext[pl.ds(0, bt), pl.ds(2 * W, W)] = dv_ref[...]
        dext[pl.ds(bt, HALO), pl.ds(0, W)] = jnp.where(last, 0.0, dqn_ref[...])
        dext[pl.ds(bt, HALO), pl.ds(W, W)] = jnp.where(last, 0.0, dkn_ref[...])
        dext[pl.ds(bt, HALO), pl.ds(2 * W, W)] = jnp.where(last, 0.0, dvn_ref[...])
        w = w_ref[...]
        c = _conv_taps(ext, w, n_ext, HALO)
        sg_c = _sigmoid(c)
        s = c * sg_c
        d = dext[...]

        def l2_bwd(scale):
            def fn(pair):
                t, dt = pair
                r = lax.rsqrt(jnp.sum(t * t, axis=-1, keepdims=True) + EPS)
                return scale * r * (dt - t * (r * r) * jnp.sum(t * dt, axis=-1, keepdims=True))
            return fn

        def heads(lo, fn):
            return jnp.concatenate(
                [fn((s[:, lo + h * HEAD_DIM:lo + (h + 1) * HEAD_DIM], d[:, lo + h * HEAD_DIM:lo + (h + 1) * HEAD_DIM]))
                 for h in range(H)], axis=1)

        ds = jnp.concatenate([heads(0, l2_bwd(HEAD_DIM ** -0.5)), heads(W, l2_bwd(1.0)), d[:, 2 * W:]], axis=1)
        dc = ds * (sg_c * (1.0 + c * (1.0 - sg_c)))
        dcext[...] = dc
        dx = None
        for j in range(K):
            term = dcext[pl.ds(K - 1 - j, bt), :] * w[j:j + 1, :]
            dx = term if dx is None else dx + term
        dx_ref[...] = dx.astype(BF16)

        @pl.when(i == 0)
        def _():
            dw_ref[...] = jnp.zeros_like(dw_ref)
            dal_ref[...] = jnp.zeros_like(dal_ref)
            ddt_ref[...] = jnp.zeros_like(ddt_ref)

        dcb = dc[:bt]
        dw_ref[...] += jnp.concatenate(
            [jnp.sum(dcb * ext[pl.ds(HALO - (K - 1) + j, bt), :], axis=0, keepdims=True) for j in range(K)], axis=0)

        ba = ba_ref[...]
        beta = _sigmoid(ba)
        al, dtb = al_ref[...], dt_ref[...]
        dbeta = _dot3(dbx_ref[...], sb_ref[...])
        dg = _dot3(dgx_ref[...], sg_ref[...])
        sp = _softplus(ba + dtb)
        da = dg * (-jnp.exp(al)) * _sigmoid(ba + dtb)
        dba_ref[...] = dbeta * beta * (1.0 - beta) + da
        dal_ref[...] += jnp.sum(dg * (-jnp.exp(al)) * sp, axis=0, keepdims=True)
        ddt_ref[...] += jnp.sum(da, axis=0, keepdims=True)

    rowb = lambda w: pl.BlockSpec((bt, w), lambda i: (i, 0))
    full = lambda a: pl.BlockSpec(a.shape, lambda i: (0,) * a.ndim)
    nxt = lambda w: pl.BlockSpec((HALO, w), lambda i: (jnp.minimum((i + 1) * (bt // HALO), T // HALO - 1), 0))
    vec = pl.BlockSpec((1, LANES), lambda i: (0, 0))
    return pl.pallas_call(
        body, name=name, grid=(nb,),
        in_specs=[rowb(C), pl.BlockSpec((HALO, C), lambda i: (jnp.maximum(i * (bt // HALO) - 1, 0), 0)), nxt(C),
                  rowb(LANES), full(conv_w), full(a_log), full(dt_bias), full(sb), full(sg),
                  rowb(W), nxt(W), rowb(W), nxt(W), rowb(W), nxt(W), rowb(W), rowb(W)],
        out_specs=(rowb(C), rowb(LANES), pl.BlockSpec((K, C), lambda i: (0, 0)), vec, vec),
        out_shape=(jax.ShapeDtypeStruct((T, C), BF16), jax.ShapeDtypeStruct((T, LANES), F32),
                   jax.ShapeDtypeStruct((K, C), F32), jax.ShapeDtypeStruct((1, LANES), F32),
                   jax.ShapeDtypeStruct((1, LANES), F32)),
        scratch_shapes=[pltpu.VMEM((bt + 2 * HALO, C), F32), pltpu.VMEM((n_ext, C), F32), pltpu.VMEM((n_ext, C), F32)],
        compiler_params=_params(1),
    )(pdn, pdn, pdn, pba, conv_w, a_log, dt_bias, sb, sg, dq, dq, dk, dk, dv, dv, dbx, dgx)


def _dot_split(a, b):
    a_hi, b_hi = a.astype(BF16), b.astype(BF16)
    a_lo, b_lo = (a - a_hi.astype(F32)).astype(BF16), (b - b_hi.astype(F32)).astype(BF16)
    dot = functools.partial(jnp.dot, preferred_element_type=F32)
    return dot(a_hi, b_hi) + (dot(a_hi, b_lo) + dot(a_lo, b_hi))


def _dn_local(q, k, v, beta, g, tm=None):
    row, col = _iota2(ROWS, ROWS)
    same = (row // CHUNK) == (col // CHUNK)
    causal = same & (row >= col)
    strict = same & (row > col)
    eye = (row == col).astype(F32)
    last_of = (col == (row // CHUNK) * CHUNK + (CHUNK - 1)).astype(BF16)
    eg = jnp.exp(g)
    decay = jnp.where(causal, jnp.exp(jnp.where(causal, g - g.T, 0.0)), 0.0)
    kb = k * beta
    vb = v * beta
    kk = _bdot(kb, k, NT)
    low = jnp.where(strict, kk * decay, 0.0)
    if tm is None:
        pw = -low
        tm = eye + pw
        for _ in range(5):
            pw = _dot_split(pw, pw)
            tm = tm + _dot_split(tm, pw)
    kbg = kb * eg
    u = _bdot(tm, vb)
    w = _bdot(tm, kbg)
    qk = _bdot(q, k, NT)
    qa = jnp.where(causal, qk * decay, 0.0)
    glast = _dot3r(last_of, g)
    e2 = jnp.exp(glast - g)
    return dict(row=row, col=col, same=same, causal=causal, strict=strict, eg=eg, decay=decay, kb=kb, vb=vb, kk=kk,
                tm=tm, kbg=kbg, u=u, w=w, qk=qk, qa=qa, glast=glast, e2=e2, kte=k * e2, qd=q * eg)


def _dn_core_fwd(q, k, v, bx, gx, z, gain, name, comm):
    T, W = q.shape
    H = W // HEAD_DIM
    nb = T // ROWS
    hp = _heads_per_step(H)
    ng = H // hp

    def body(*refs):
        ((q_ref, k_ref, v_ref, b_ref, g_ref, z_ref, gain_ref), (o_ref, mix_ref, ss_ref, tm_ref), (state,),
         (first, mid, last)) = comm.split(refs, 7, 4, 1)
        h, b = pl.program_id(0), pl.program_id(1)
        pl.when((h == 0) & (b == 0))(first)
        pl.when((h == ng // 2) & (b == 0))(mid)

        @pl.when(b == 0)
        def _():
            state[...] = jnp.zeros_like(state)

        for t in range(hp):
            ln = slice(t * HEAD_DIM, (t + 1) * HEAD_DIM)
            L = _dn_local(q_ref[:, ln], k_ref[:, ln], v_ref[:, ln], b_ref[:, ln], g_ref[:, ln])
            s = state[t]
            vns, qds = [], []
            for c in range(2):
                rows = slice(c * CHUNK, (c + 1) * CHUNK)
                ss_ref[t, c] = s
                vn = L["u"][rows] - _bdot(L["w"][rows], s)
                qds.append(_bdot(L["qd"][rows], s))
                vns.append(vn)
                s = s * jnp.exp(L["glast"][c * CHUNK:c * CHUNK + 1, :]) + _bdot(L["kte"][rows], vn, TN)
            state[t] = s
            tm_ref[:, ln] = L["tm"]
            o = jnp.concatenate(qds, axis=0) + _bdot(L["qa"], jnp.concatenate(vns, axis=0))
            o_ref[:, ln] = o
            zz = z_ref[:, ln]
            r = lax.rsqrt(jnp.mean(o * o, axis=-1, keepdims=True) + EPS)
            mix_ref[:, ln] = ((o * r * gain_ref[...]) * (zz * _sigmoid(zz))).astype(BF16)
        pl.when((h == ng - 1) & (b == nb - 1))(last)

    wide = hp * HEAD_DIM
    blk = pl.BlockSpec((ROWS, wide), lambda h, b: (b, h))
    zblk = pl.BlockSpec((ROWS, wide), lambda h, b: (b, 3 * ng + h))
    return comm.call(
        body, name, (ng, nb), [blk] * 5 + [zblk, pl.BlockSpec((1, HEAD_DIM), lambda h, b: (0, 0))],
        (blk, blk, pl.BlockSpec((hp, 2, HEAD_DIM, HEAD_DIM), lambda h, b: (h, b, 0, 0)), blk),
        (jax.ShapeDtypeStruct((T, W), F32), jax.ShapeDtypeStruct((T, W), BF16),
         jax.ShapeDtypeStruct((H, T // CHUNK, HEAD_DIM, HEAD_DIM), F32), jax.ShapeDtypeStruct((T, W), F32)),
        [pltpu.VMEM((hp, HEAD_DIM, HEAD_DIM), F32)], (q, k, v, bx, gx, z, gain))


def _dn_core_bwd(q, k, v, bx, gx, z, gain, o, dmix, ss, tms, dmix_col0, name, comm):
    T, W = q.shape
    H = W // HEAD_DIM
    nb = T // ROWS
    hp = _heads_per_step(H)
    ng = H // hp
    wide = hp * HEAD_DIM
    c0 = dmix_col0 // wide

    def body(*refs):
        ((q_ref, k_ref, v_ref, b_ref, g_ref, z_ref, gain_ref, o_ref, dm_ref, ss_ref, tm_ref),
         (dq_ref, dk_ref, dv_ref, dbx_ref, dgx_ref, dz_ref, dgain_ref), (dstate,),
         (first, mid, last)) = comm.split(refs, 11, 7, 1)
        pl.when((pl.program_id(0) == 0) & (pl.program_id(1) == 0))(first)
        pl.when((pl.program_id(0) == ng // 2) & (pl.program_id(1) == 0))(mid)

        @pl.when(pl.program_id(1) == 0)
        def _():
            dstate[...] = jnp.zeros_like(dstate)
            dgain_ref[...] = jnp.zeros_like(dgain_ref)

        refs = (q_ref, k_ref, v_ref, b_ref, g_ref, z_ref, gain_ref, o_ref, dm_ref, ss_ref, tm_ref,
                dq_ref, dk_ref, dv_ref, dbx_ref, dgx_ref, dz_ref, dgain_ref, dstate)
        for t in range(hp):
            one_head(t, *refs)
        pl.when((pl.program_id(0) == ng - 1) & (pl.program_id(1) == nb - 1))(last)

    def one_head(t, q_ref, k_ref, v_ref, b_ref, g_ref, z_ref, gain_ref, o_ref, dm_ref, ss_ref, tm_ref,
                 dq_ref, dk_ref, dv_ref, dbx_ref, dgx_ref, dz_ref, dgain_ref, dstate):
        ln = slice(t * HEAD_DIM, (t + 1) * HEAD_DIM)
        qv, kv, vv, beta, g = q_ref[:, ln], k_ref[:, ln], v_ref[:, ln], b_ref[:, ln], g_ref[:, ln]
        gain_v = gain_ref[...]
        ov, zz, dm = o_ref[:, ln], z_ref[:, ln], dm_ref[:, ln]
        r = lax.rsqrt(jnp.mean(ov * ov, axis=-1, keepdims=True) + EPS)
        n = ov * r
        sgz = _sigmoid(zz)
        d_on = dm * (zz * sgz)
        dz_ref[:, ln] = (dm * (n * gain_v) * (sgz * (1.0 + zz * (1.0 - sgz)))).astype(BF16)
        dgain_ref[t] += jnp.sum(d_on * n, axis=0, keepdims=True)
        gy = d_on * gain_v
        do = r * (gy - n * jnp.mean(gy * n, axis=-1, keepdims=True))

        L = _dn_local(qv, kv, vv, beta, g, tm_ref[:, ln])
        row, causal, strict = L["row"], L["causal"], L["strict"]
        u, w, qa, qd, kte, tm = L["u"], L["w"], L["qa"], L["qd"], L["kte"], L["tm"]
        s_in = [ss_ref[t, 0], ss_ref[t, 1]]
        vn = [u[c * CHUNK:(c + 1) * CHUNK] - _bdot(w[c * CHUNK:(c + 1) * CHUNK], s_in[c]) for c in range(2)]
        vn_all = jnp.concatenate(vn, axis=0)
        qat_do = _bdot(qa, do, TN)
        d_qa = jnp.where(causal, _bdot(do, vn_all, NT), 0.0)
        ds = dstate[t]
        d_vn, d_kte, d_qd, d_w, d_gl = [None] * 2, [None] * 2, [None] * 2, [None] * 2, [None] * 2
        for c in (1, 0):
            rows = slice(c * CHUNK, (c + 1) * CHUNK)
            egl = jnp.exp(L["glast"][c * CHUNK:c * CHUNK + 1, :])
            d_vn[c] = qat_do[rows] + _bdot(kte[rows], ds)
            d_kte[c] = _bdot(vn[c], ds, NT)
            d_gl[c] = jnp.sum(jnp.sum(ds * s_in[c], axis=1, keepdims=True), axis=0, keepdims=True) * egl
            d_qd[c] = _bdot(do[rows], s_in[c], NT)
            d_w[c] = -_bdot(d_vn[c], s_in[c], NT)
            ds = ds * egl + _bdot(qd[rows], do[rows], TN) - _bdot(w[rows], d_vn[c], TN)
        dstate[t] = ds
        d_u = jnp.concatenate(d_vn, axis=0)
        d_w = jnp.concatenate(d_w, axis=0)
        d_qd = jnp.concatenate(d_qd, axis=0)
        d_kte = jnp.concatenate(d_kte, axis=0)

        d_tm = _bdot(d_u, L["vb"], NT) + _bdot(d_w, L["kbg"], NT)
        d_vb = _bdot(tm, d_u, TN)
        d_kbg = _bdot(tm, d_w, TN)
        d_low = jnp.where(strict, -_bdot(_bdot(tm, d_tm, TN), tm, NT), 0.0)
        decay = L["decay"]
        d_kk = d_low * decay
        d_qk = d_qa * decay
        d_decay = d_low * L["kk"] + d_qa * L["qk"]
        eg, e2 = L["eg"], L["e2"]
        d_kb = _bdot(d_kk, kv) + d_kbg * eg
        dk_ref[:, ln] = _bdot(d_kk, L["kb"], TN) + _bdot(d_qk, qv, TN) + d_kb * beta + d_kte * e2
        dq_ref[:, ln] = _bdot(d_qk, kv) + d_qd * eg
        dv_ref[:, ln] = d_vb * beta
        rsum = lambda a: jnp.sum(a, axis=-1, keepdims=True)
        dbx_ref[:, ln] = jnp.broadcast_to(rsum(d_kb * kv) + rsum(d_vb * vv), (ROWS, HEAD_DIM))
        d_eg = rsum(d_kbg * L["kb"]) + rsum(d_qd * qv)
        t2 = rsum(d_kte * kv) * e2
        ed = d_decay * decay
        d_g = d_eg * eg - t2 + rsum(ed) - rsum(ed.T)
        col = L["col"]
        chunk_sum = L["same"].astype(BF16)
        is_last = (row % CHUNK) == (CHUNK - 1)
        d_glast = _dot3r(chunk_sum, t2) + jnp.where(row < CHUNK, d_gl[0], d_gl[1])
        d_g = d_g + jnp.where(is_last, d_glast, 0.0)
        suffix = (L["same"] & (col >= row)).astype(BF16)
        dgx_ref[:, ln] = _dot3r(suffix, d_g)

    rev = lambda b: nb - 1 - b
    blk = pl.BlockSpec((ROWS, wide), lambda h, b: (rev(b), h))
    zblk = pl.BlockSpec((ROWS, wide), lambda h, b: (rev(b), 3 * ng + h))
    dmblk = pl.BlockSpec((ROWS, wide), lambda h, b: (rev(b), c0 + h))
    out = jax.ShapeDtypeStruct((T, W), F32)
    return comm.call(
        body, name, (ng, nb),
        [blk] * 5 + [zblk, pl.BlockSpec((1, HEAD_DIM), lambda h, b: (0, 0)), blk, dmblk,
                     pl.BlockSpec((hp, 2, HEAD_DIM, HEAD_DIM), lambda h, b: (h, rev(b), 0, 0)), blk],
        (blk,) * 6 + (pl.BlockSpec((hp, 1, HEAD_DIM), lambda h, b: (h, 0, 0)),),
        (out,) * 5 + (jax.ShapeDtypeStruct((T, W), BF16), jax.ShapeDtypeStruct((H, 1, HEAD_DIM), F32)),
        [pltpu.VMEM((hp, HEAD_DIM, HEAD_DIM), F32)], (q, k, v, bx, gx, z, gain, o, dmix, ss, tms))


_GELU_C = 0.7978845608028654
_GELU_A = 0.044715


def _gelu(x):
    t = jnp.tanh(_GELU_C * (x + _GELU_A * (x * x * x)))
    return 0.5 * x * (1.0 + t), t


def _ffn_mid_fwd(up, conv_w, conv_b, bt, bc, name):
    T, F = up.shape[0], up.shape[1] // 2
    nc = F // bc

    def body(g_ref, gp_ref, v_ref, vp_ref, wg_ref, wv_ref, bg_ref, bv_ref, o_ref, gext, vext):
        i = pl.program_id(0)
        for ext, cur, prev in ((gext, g_ref, gp_ref), (vext, v_ref, vp_ref)):
            ext[pl.ds(0, HALO), :] = jnp.where(i > 0, prev[...], 0.0)
            ext[pl.ds(HALO, bt), :] = cur[...]
        gate = _conv_taps(gext, wg_ref[...], bt, HALO) + bg_ref[...]
        val = _conv_taps(vext, wv_ref[...], bt, HALO) + bv_ref[...]
        o_ref[...] = (_gelu(gate)[0] * val).astype(BF16)

    K = conv_w.shape[0]
    prev = lambda i: jnp.maximum(i * (bt // HALO) - 1, 0)
    return pl.pallas_call(
        body, name=name, grid=(T // bt, nc),
        in_specs=[pl.BlockSpec((bt, bc), lambda i, j: (i, j)), pl.BlockSpec((HALO, bc), lambda i, j: (prev(i), j)),
                  pl.BlockSpec((bt, bc), lambda i, j: (i, nc + j)),
                  pl.BlockSpec((HALO, bc), lambda i, j: (prev(i), nc + j)),
                  pl.BlockSpec((K, bc), lambda i, j: (0, j)), pl.BlockSpec((K, bc), lambda i, j: (0, nc + j)),
                  pl.BlockSpec((1, bc), lambda i, j: (0, j)), pl.BlockSpec((1, bc), lambda i, j: (0, nc + j))],
        out_specs=pl.BlockSpec((bt, bc), lambda i, j: (i, j)),
        out_shape=jax.ShapeDtypeStruct((T, F), BF16),
        scratch_shapes=[pltpu.VMEM((bt + HALO, bc), F32)] * 2, compiler_params=_params(2),
    )(up, up, up, up, conv_w, conv_w, conv_b, conv_b)


def _ffn_mid_bwd(up, conv_w, conv_b, da, bt, bc, name):
    T, F = up.shape[0], up.shape[1] // 2
    nc = F // bc
    K = conv_w.shape[0]
    nb = T // bt
    n_ext = bt + HALO

    def body(g_ref, gp_ref, gn_ref, v_ref, vp_ref, vn_ref, da_ref, dan_ref, wg_ref, wv_ref, bg_ref, bv_ref,
             dg_ref, dv_ref, dwg_ref, dwv_ref, dbg_ref, dbv_ref, gext, vext, dgext, dvext):
        i = pl.program_id(1)
        last = i == nb - 1
        for ext, cur, prev, nxt in ((gext, g_ref, gp_ref, gn_ref), (vext, v_ref, vp_ref, vn_ref)):
            ext[pl.ds(0, HALO), :] = jnp.where(i > 0, prev[...], 0.0)
            ext[pl.ds(HALO, bt), :] = cur[...]
            ext[pl.ds(HALO + bt, HALO), :] = jnp.where(last, 0.0, nxt[...])
        wg, wv = wg_ref[...], wv_ref[...]
        gate = _conv_taps(gext, wg, n_ext, HALO) + bg_ref[...]
        val = _conv_taps(vext, wv, n_ext, HALO) + bv_ref[...]
        dact = jnp.concatenate([da_ref[...], jnp.where(last, 0.0, dan_ref[...])], axis=0)
        ge, t = _gelu(gate)
        dgelu = 0.5 * (1.0 + t) + 0.5 * gate * (1.0 - t * t) * (_GELU_C * (1.0 + 3.0 * _GELU_A * (gate * gate)))
        dgext[...] = dact * val * dgelu
        dvext[...] = dact * ge

        @pl.when(i == 0)
        def _():
            for ref in (dwg_ref, dwv_ref, dbg_ref, dbv_ref):
                ref[...] = jnp.zeros_like(ref)

        for dext, ext, w, dx_ref, dw_ref, db_ref in ((dgext, gext, wg, dg_ref, dwg_ref, dbg_ref),
                                                     (dvext, vext, wv, dv_ref, dwv_ref, dbv_ref)):
            dx = None
            for j in range(K):
                term = dext[pl.ds(K - 1 - j, bt), :] * w[j:j + 1, :]
                dx = term if dx is None else dx + term
            dx_ref[...] = dx.astype(BF16)
            dcur = dext[pl.ds(0, bt), :]
            dw_ref[...] += jnp.concatenate(
                [jnp.sum(dcur * ext[pl.ds(HALO - (K - 1) + j, bt), :], axis=0, keepdims=True) for j in range(K)], axis=0)
            db_ref[...] += jnp.sum(dcur, axis=0, keepdims=True)

    prev = lambda i: jnp.maximum(i * (bt // HALO) - 1, 0)
    nxt = lambda i: jnp.minimum((i + 1) * (bt // HALO), T // HALO - 1)
    cur_g = pl.BlockSpec((bt, bc), lambda j, i: (i, j))
    cur_v = pl.BlockSpec((bt, bc), lambda j, i: (i, nc + j))
    outs = pl.pallas_call(
        body, name=name, grid=(nc, nb),
        in_specs=[cur_g, pl.BlockSpec((HALO, bc), lambda j, i: (prev(i), j)),
                  pl.BlockSpec((HALO, bc), lambda j, i: (nxt(i), j)),
                  cur_v, pl.BlockSpec((HALO, bc), lambda j, i: (prev(i), nc + j)),
                  pl.BlockSpec((HALO, bc), lambda j, i: (nxt(i), nc + j)),
                  cur_g, pl.BlockSpec((HALO, bc), lambda j, i: (nxt(i), j)),
                  pl.BlockSpec((K, bc), lambda j, i: (0, j)), pl.BlockSpec((K, bc), lambda j, i: (0, nc + j)),
                  pl.BlockSpec((1, bc), lambda j, i: (0, j)), pl.BlockSpec((1, bc), lambda j, i: (0, nc + j))],
        out_specs=(cur_g, cur_g, pl.BlockSpec((K, bc), lambda j, i: (0, j)), pl.BlockSpec((K, bc), lambda j, i: (0, j)),
                   pl.BlockSpec((1, bc), lambda j, i: (0, j)), pl.BlockSpec((1, bc), lambda j, i: (0, j))),
        out_shape=(jax.ShapeDtypeStruct((T, F), BF16), jax.ShapeDtypeStruct((T, F), BF16),
                   jax.ShapeDtypeStruct((K, F), F32), jax.ShapeDtypeStruct((K, F), F32),
                   jax.ShapeDtypeStruct((1, F), F32), jax.ShapeDtypeStruct((1, F), F32)),
        scratch_shapes=[pltpu.VMEM((bt + 2 * HALO, bc), F32)] * 2 + [pltpu.VMEM((n_ext, bc), F32)] * 2,
        compiler_params=_params(2),
    )(up, up, up, up, up, up, da, da, conv_w, conv_w, conv_b, conv_b)
    return outs


def _adam_math(w, g, m, v):
    m2 = ADAM_B1 * m + (1.0 - ADAM_B1) * g
    v2 = ADAM_B2 * v + (1.0 - ADAM_B2) * (g * g)
    m_hat = m2 / (1.0 - ADAM_B1 ** ADAM_STEP)
    v_hat = v2 / (1.0 - ADAM_B2 ** ADAM_STEP)
    return -ADAM_LR * (m_hat / (jnp.sqrt(v_hat) + ADAM_EPS) + ADAM_WD * w), m2, v2


def _adamw_halves(w, mine, theirs, place, m, v, bt, name):
    R, C = w.shape
    h = R // 2
    bt = _fit_rows(h, bt)
    nh = h // bt

    def body(s_ref, w_ref, a_ref, b_ref, m_ref, v_ref, g_ref, d_ref, m2_ref, v2_ref):
        lower = pl.program_id(0) < nh
        gv = jnp.where(lower == (s_ref[0] == 0), a_ref[...], b_ref[...])
        g_ref[...] = gv
        d_ref[...], m2_ref[...], v2_ref[...] = _adam_math(w_ref[...], gv, m_ref[...], v_ref[...])

    full = pl.BlockSpec((bt, C), lambda i, s: (i, 0))
    half = pl.BlockSpec((bt, C), lambda i, s: (i % nh, 0))
    out = jax.ShapeDtypeStruct((R, C), F32)
    return pl.pallas_call(
        body, name=name,
        grid_spec=pltpu.PrefetchScalarGridSpec(num_scalar_prefetch=1, grid=(2 * nh,),
                                               in_specs=[full, half, half, full, full], out_specs=(full,) * 4),
        out_shape=(out,) * 4, compiler_params=_params(1),
    )(place, w, mine, theirs, m, v)


def _adamw(w, g, m, v, bt, name):
    R, C = w.shape
    bt = _fit_rows(R, bt)
    blk = pl.BlockSpec((bt, C), lambda i: (i, 0))

    def body(w_ref, g_ref, m_ref, v_ref, d_ref, m2_ref, v2_ref):
        gv = g_ref[...]
        m2 = ADAM_B1 * m_ref[...] + (1.0 - ADAM_B1) * gv
        v2 = ADAM_B2 * v_ref[...] + (1.0 - ADAM_B2) * (gv * gv)
        m_hat = m2 / (1.0 - ADAM_B1 ** ADAM_STEP)
        v_hat = v2 / (1.0 - ADAM_B2 ** ADAM_STEP)
        d_ref[...] = -ADAM_LR * (m_hat / (jnp.sqrt(v_hat) + ADAM_EPS) + ADAM_WD * w_ref[...])
        m2_ref[...] = m2
        v2_ref[...] = v2

    out = jax.ShapeDtypeStruct((R, C), F32)
    return pl.pallas_call(body, name=name, grid=(R // bt,), in_specs=[blk] * 4, out_specs=(blk,) * 3,
                          out_shape=(out,) * 3, compiler_params=_params(1))(w, g, m, v)


def _place():
    x, y, c = lax.axis_index("x"), lax.axis_index("y"), lax.axis_index("c")
    chips = [(1 - x, y), (x, 1 - y), (1 - x, 1 - y)]
    return x, y, c, chips


_HBM = pl.BlockSpec(memory_space=pltpu.HBM)


def _add_cores(buf, other, place, own_only, out_dtype, bt, name):
    n, _, h, cols = buf.shape
    bt = _fit_rows(h, bt)
    row = (lambda k, s: s[1]) if own_only else (lambda k, s: k)

    def body(s_ref, a_ref, b_ref, o_ref):
        o_ref[...] = (a_ref[...] + b_ref[...]).astype(out_dtype)

    return pl.pallas_call(
        body, name=name,
        grid_spec=pltpu.PrefetchScalarGridSpec(
            num_scalar_prefetch=1, grid=(1 if own_only else n, h // bt),
            in_specs=[pl.BlockSpec((None, None, bt, cols), lambda k, i, s: (row(k, s), s[0], i, 0)),
                      pl.BlockSpec((None, bt, cols), lambda k, i, s: (row(k, s), i, 0))],
            out_specs=(pl.BlockSpec((bt, cols), lambda k, i, s: (i, 0)) if own_only
                       else pl.BlockSpec((None, bt, cols), lambda k, i, s: (k, i, 0)))),
        out_shape=jax.ShapeDtypeStruct((h, cols) if own_only else (n, h, cols), out_dtype),
        compiler_params=_params(2),
    )(place, buf, other)


def _add_chips(own, others, bt, name):
    h, cols = own.shape
    bt = _fit_rows(h, bt)

    def body(a_ref, b_ref, o_ref):
        o_ref[...] = ((a_ref[...] + b_ref[0].astype(F32)) + b_ref[1].astype(F32)) + b_ref[2].astype(F32)

    return pl.pallas_call(
        body, name=name, grid=(h // bt,),
        in_specs=[pl.BlockSpec((bt, cols), lambda i: (i, 0)), pl.BlockSpec((3, bt, cols), lambda i: (0, i, 0))],
        out_specs=pl.BlockSpec((bt, cols), lambda i: (i, 0)),
        out_shape=jax.ShapeDtypeStruct((h, cols), F32), compiler_params=_params(1),
    )(own, others)


def _gather_plan(bufs, split):
    n = len(bufs)

    def phases(ins, outs, sems):
        send, recv = sems
        x, y, c, chips = _place()
        me = 2 * x + y

        def rows(b, core):
            h = bufs[b].shape[0] // 2
            return pl.ds(core * h, h) if split[b] else pl.ds(0, bufs[b].shape[0])

        def over_ici(b, j, block):
            px, py = chips[j]
            return pltpu.make_async_remote_copy(
                src_ref=ins[b].at[rows(b, c)], dst_ref=outs[b].at[block, rows(b, c)], send_sem=send.at[b, j],
                recv_sem=recv.at[b, j], device_id=(px, py, c), device_id_type=MESH)

        def over_d2d(b, j, block, core):
            return pltpu.make_async_remote_copy(
                src_ref=outs[b].at[block, rows(b, core)], dst_ref=outs[b].at[block, rows(b, core)],
                send_sem=send.at[b, 3 + j], recv_sem=recv.at[b, 3 + j], device_id=(x, y, 1 - c), device_id_type=MESH)

        pairs = [(b, j) for b in range(n) for j in range(3)]
        source = lambda j: 2 * chips[j][0] + chips[j][1]

        def first():
            for b, j in pairs:
                over_ici(b, j, me).start()

        def mid():
            for b, j in pairs:
                over_ici(b, j, source(j)).wait_recv()
                if split[b]:
                    over_d2d(b, j, source(j), c).start()

        def last():
            for b, j in pairs:
                if split[b]:
                    over_d2d(b, j, source(j), 1 - c).wait_recv()
            for b, j in pairs:
                over_ici(b, j, me).wait_send()
                if split[b]:
                    over_d2d(b, j, source(j), c).wait_send()

        return first, mid, last

    return ([jax.ShapeDtypeStruct((4,) + b.shape, b.dtype) for b in bufs],
            [pltpu.SemaphoreType.DMA((n, 6)), pltpu.SemaphoreType.DMA((n, 6))], phases)


def _exchange_plan(n, out_shapes, copy):
    def phases(ins, outs, sems):
        send, recv = sems
        place = _place()

        def first():
            for b in range(n):
                copy(b, ins, outs, send, recv, place).start()

        def last():
            for b in range(n):
                copy(b, ins, outs, send, recv, place).wait()

        return first, (lambda: None), last

    return out_shapes, [pltpu.SemaphoreType.DMA((n,)), pltpu.SemaphoreType.DMA((n,))], phases


def _swap_plan(bufs):
    def copy(b, ins, outs, send, recv, place):
        x, y, c, _ = place
        h = bufs[b].shape[1] // 2
        return pltpu.make_async_remote_copy(
            src_ref=ins[b].at[:, pl.ds((1 - c) * h, h)], dst_ref=outs[b], send_sem=send.at[b], recv_sem=recv.at[b],
            device_id=(x, y, 1 - c), device_id_type=MESH)

    shapes = [jax.ShapeDtypeStruct((b.shape[0], b.shape[1] // 2, b.shape[2]), b.dtype) for b in bufs]
    return _exchange_plan(len(bufs), shapes, copy)


def _scatter_plan(bufs):
    def copy(t, ins, outs, send, recv, place):
        x, y, c, chips = place
        b, j = divmod(t, 3)
        px, py = chips[j]
        return pltpu.make_async_remote_copy(
            src_ref=ins[b].at[2 * px + py], dst_ref=outs[b].at[j], send_sem=send.at[t], recv_sem=recv.at[t],
            device_id=(px, py, c), device_id_type=MESH)

    shapes = [jax.ShapeDtypeStruct((3,) + b.shape[1:], b.dtype) for b in bufs]
    return _exchange_plan(3 * len(bufs), shapes, copy)


def _join_plan(halves):
    def copy(b, ins, outs, send, recv, place):
        x, y, c, _ = place
        return pltpu.make_async_remote_copy(
            src_ref=ins[b], dst_ref=outs[b], send_sem=send.at[b], recv_sem=recv.at[b],
            device_id=(x, y, 1 - c), device_id_type=MESH)

    return _exchange_plan(len(halves), [jax.ShapeDtypeStruct(b.shape, b.dtype) for b in halves], copy)


def _run_plan(arrays, plan, name):
    out_shapes, sems, phases = plan
    n, m = len(arrays), len(out_shapes)

    def body(*refs):
        for phase in phases(refs[:n], refs[n:n + m], refs[n + m:]):
            phase()

    return pl.pallas_call(
        body, name=name, in_specs=[_HBM] * n, out_specs=[_HBM] * m, out_shape=out_shapes, scratch_shapes=sems,
        compiler_params=pltpu.CompilerParams(has_side_effects=True),
    )(*arrays)


def _fit_rows(n, target):
    for q in (2 * HALO, HALO):
        for t in range(min(n, target) // q * q, 0, -q):
            if n % t == 0:
                return t
    raise ValueError((n, target))


def _allreduce_small(buf, name):
    R, lanes = buf.shape

    def body(in_ref, out_ref, land, send, recv):
        x, y, c, _ = _place()
        me = 4 * x + 2 * y + c
        land[me] = in_ref[...]
        cps = []
        for r in range(1, 8):
            px, py, pc = x ^ (r >> 2), y ^ ((r >> 1) & 1), c ^ (r & 1)
            cp = pltpu.make_async_remote_copy(
                src_ref=in_ref, dst_ref=land.at[me], send_sem=send.at[r - 1], recv_sem=recv.at[me],
                device_id=(px, py, pc), device_id_type=MESH)
            cp.start()
            cps.append(cp)
        for r in range(1, 8):
            peer = 4 * (x ^ (r >> 2)) + 2 * (y ^ ((r >> 1) & 1)) + (c ^ (r & 1))
            pltpu.make_async_remote_copy(
                src_ref=in_ref, dst_ref=land.at[peer], send_sem=send.at[r - 1], recv_sem=recv.at[peer],
                device_id=(x, y, c), device_id_type=MESH).wait_recv()
        for cp in cps:
            cp.wait_send()
        acc = land[0]
        for d in range(1, 8):
            acc = acc + land[d]
        out_ref[...] = acc

    vm = pl.BlockSpec(memory_space=pltpu.VMEM)
    return pl.pallas_call(
        body, name=name, in_specs=[vm], out_specs=vm, out_shape=jax.ShapeDtypeStruct((R, lanes), buf.dtype),
        scratch_shapes=[pltpu.VMEM((8, R, lanes), buf.dtype), pltpu.SemaphoreType.DMA((7,)), pltpu.SemaphoreType.DMA((8,))],
        compiler_params=pltpu.CompilerParams(has_side_effects=True, vmem_limit_bytes=VMEM_LIMIT),
    )(buf)


ROW_BLOCK = 256
SB_BLOCK = 256
MM_TM, MM_TN, MM_TK = 1024, 512, 512
FFN_COLS = 512


def _lane_pad(vec, start):
    return jnp.pad(vec, ((0, 0), (start, LANES - start - vec.shape[1])))


WEIGHTS = ("w_in", "sb_out_gain", "dn_conv_w", "dn_a_log", "dn_dt_bias", "dn_out_gain", "w_out", "ln_mix_pre",
           "ln_mix_post", "w_up", "ffn_conv_w", "ffn_conv_b", "w_down", "ln_ffn_pre", "ln_ffn_post")
MATRICES = {"w_in": 1, "w_out": 0, "w_up": 1, "w_down": 0}
CONV_SHARDED = ("dn_conv_w", "ffn_conv_w")
SMALL = tuple(n for n in WEIGHTS if n not in MATRICES)
N_CHIPS = 4
ADAM_ROWS = 128


def _pack(arrs, quantum):
    rows, layout, off = [], [], 0
    for a in arrs:
        n = int(np.prod(a.shape))
        r = -(-n // LANES)
        r = -(-r // HALO) * HALO
        rows.append(jnp.pad(a.reshape(-1), (0, r * LANES - n)).reshape(r, LANES))
        layout.append((off, r, n, a.shape))
        off += r
    total = -(-off // quantum) * quantum
    if total > off:
        rows.append(jnp.zeros((total - off, LANES), rows[0].dtype))
    return jnp.concatenate(rows, axis=0), layout


def _unpack(packed, layout):
    return [packed[off:off + r].reshape(-1)[:n].reshape(shape) for off, r, n, shape in layout]


UP_TILE = 1408
PAIR_ROWS = 256


def _reduce_to_chips(shares, place, names, swap_on, scatter_on):
    from_sibling = swap_on(shares, _swap_plan(shares))
    halves = [s.reshape(N_CHIPS, 2, s.shape[1] // 2, s.shape[2]) for s in shares]
    to_chips = [_add_cores(hv, fs, place, False, BF16, PAIR_ROWS, "grad_add_cores_" + n)
                for hv, fs, n in zip(halves, from_sibling, names)]
    own = [_add_cores(hv, fs, place, True, F32, PAIR_ROWS, "grad_add_cores_own_" + n)
           for hv, fs, n in zip(halves, from_sibling, names)]
    return own, scatter_on(to_chips, _scatter_plan(to_chips))


def _step(x, target, wt, late, chip, place):
    T, D = x.shape
    W = D // 2
    H = W // HEAD_DIM
    bt = min(ROW_BLOCK, T)
    blk = min(SB_BLOCK, T)
    w_in = wt["w_in"]
    a_log, dt_bias = _lane_pad(wt["dn_a_log"], H), _lane_pad(wt["dn_dt_bias"], H)
    mm = functools.partial(_mm, tm=MM_TM, tn=MM_TN)
    wide = functools.partial(_mm, tm=MM_TM, tn=2 * MM_TN)
    mm_up = functools.partial(_mm, tm=MM_TM, tn=UP_TILE)
    one = lambda a, b, tk=MM_TK: [(a, b, tk, 0, 0)]

    xn = _rms_fwd(x, wt["ln_mix_pre"], None, BF16, bt, "rms_mix_pre")
    psb = mm(one(xn, w_in, D), "nn", BF16, name="proj_sb", n_window=(0, 3 * W))
    pdn = mm(one(xn, w_in, D), "nn", F32, name="proj_dn", n_window=(3 * W, 4 * W))
    pba = mm(one(xn, w_in, D), "nn", F32, name="proj_ba", n_window=(7 * W, LANES))
    late_names = ("w_out", "w_up", "w_down")
    gathered_with = lambda names: _Comm([late[n] for n in names], _gather_plan([late[n] for n in names], [True] * len(names)))
    own_block_in = lambda theirs, names: [lax.dynamic_update_index_in_dim(t, late[n], chip, 0) for t, n in zip(theirs, names)]
    (o_sb, mix_sb, lt, swept), theirs = _sb_fwd(psb, wt["sb_out_gain"], blk, "sb_fwd", gathered_with(("w_out", "w_down")))
    w_out, w_down = [w.reshape(-1, D) for w in own_block_in(theirs, ("w_out", "w_down"))]
    qn, kn, vv, bx, gx = _dn_pre_fwd(pdn, pba, wt["dn_conv_w"], a_log, dt_bias, bt, "dn_pre_fwd")
    (o_dn, mix_dn, ss, tms), theirs = _dn_core_fwd(qn, kn, vv, bx, gx, pdn, wt["dn_out_gain"], "dn_core_fwd",
                                                  gathered_with(("w_up",)))
    w_up, = own_block_in(theirs, ("w_up",))
    F = w_down.shape[0]
    m = wide([(mix_sb, w_out, 2 * MM_TK, 0, 0), (mix_dn, w_out, 2 * MM_TK, 0, W)], "nn", F32, name="out_proj")
    h = _rms_fwd(m, wt["ln_mix_post"], x, F32, bt, "rms_mix_post")
    hn = _rms_fwd(h, wt["ln_ffn_pre"], None, BF16, bt, "rms_ffn_pre")
    up = mm_up(one(hn, w_up, D), "nn", F32, name="ffn_up")
    bc = min(FFN_COLS, F)
    act = _ffn_mid_fwd(up, wt["ffn_conv_w"], wt["ffn_conv_b"], bt, bc, "ffn_mid_fwd")
    f = mm(one(act, w_down, UP_TILE), "nn", F32, name="ffn_down")
    dy, df, g_ffn_post, sq = _loss_head(f, wt["ln_ffn_post"], h, target, bt, "loss_head")
    loss = 0.5 * jnp.sum(sq) / D

    da = mm(one(df, w_down, D), "nt", F32, name="d_act")
    g_w_down = _mm(one(act, df, 2 * MM_TK), "tn", F32, tm=UP_TILE, tn=2 * MM_TN, name="g_w_down")
    dug, duv, dwg, dwv, dbg, dbv = _ffn_mid_bwd(up, wt["ffn_conv_w"], wt["ffn_conv_b"], da, bt, bc, "ffn_mid_bwd")
    dhn = wide([(dug, w_up, UP_TILE, 0, 0), (duv, w_up, UP_TILE, 0, F)], "nt", F32, name="d_hn")
    shard = w_up.shape[2]
    g_w_up = mm_up(one(hn, dug, 2 * MM_TK), "tn", F32, name="g_w_up_gate", out_shard=shard,
                   into=(lax.empty(w_up.shape, F32), 0))
    g_w_up = mm_up(one(hn, duv, 2 * MM_TK), "tn", F32, name="g_w_up_val", out_shard=shard, into=(g_w_up, F))
    dh, g_ffn_pre = _rms_bwd(h, wt["ln_ffn_pre"], dhn, dy, F32, bt, "rms_ffn_pre_bwd")
    dm, g_mix_post = _rms_bwd(m, wt["ln_mix_post"], dh, None, BF16, bt, "rms_mix_post_bwd")
    dmix = mm(one(dm, w_out, D), "nt", F32, name="d_mix")
    g_w_out = jnp.concatenate([wide(one(mix_sb, dm, 2 * MM_TK), "tn", F32, name="g_w_out_sb"),
                               wide(one(mix_dn, dm, 2 * MM_TK), "tn", F32, name="g_w_out_dn")], axis=0)
    shares = [g_w_out.reshape(N_CHIPS, -1, D), g_w_up, g_w_down.reshape(N_CHIPS, -1, D)]
    carried = {}

    def swap_on(arrays, plan):
        (carried["do_sb"], carried["g_sb_gain"]), out = _headnorm_bwd(o_sb, wt["sb_out_gain"], dmix, bt, "sb_norm_bwd",
                                                                    _Comm(arrays, plan))
        return out

    def scatter_on(arrays, plan):
        carried["dn"], out = _dn_core_bwd(qn, kn, vv, bx, gx, pdn, wt["dn_out_gain"], o_dn, dmix, ss, tms, W,
                                          "dn_core_bwd", _Comm(arrays, plan))
        return out

    early = _reduce_to_chips(shares, place, late_names, swap_on, scatter_on)
    g_sb_gain = carried["g_sb_gain"]
    (dq, dk, dv), _ = _sb_bwd(psb, carried["do_sb"], lt, swept, blk, "sb_bwd", _Comm())
    ddq, ddk, ddv, dbx, dgx, dz, g_dn_gain = carried["dn"]
    dconv, dba, g_dn_conv, g_a_log, g_dt_bias = _dn_pre_bwd(pdn, pba, wt["dn_conv_w"], a_log, dt_bias,
                                                            ddq, ddk, ddv, dbx, dgx, bt, "dn_pre_bwd")
    pieces = [(dq, 0), (dk, W), (dv, 2 * W), (dconv, 3 * W), (dz, 6 * W), (dba, 7 * W)]
    g_w_in = [wide(one(xn, d, 2 * MM_TK), "tn", F32, name=f"g_w_in_{i}") for i, (d, _) in enumerate(pieces)]
    g_w_in[-1] = g_w_in[-1][:, :2 * H]
    g_in = jnp.concatenate(g_w_in, axis=1)

    def with_d_xn(arrays, plan):
        carried["dxn"], out = mm([(d, w_in, 2 * MM_TK, 0, k0) for d, k0 in pieces], "nt", F32, name="d_xn",
                                 comm=_Comm(arrays, plan))
        return out

    last = _reduce_to_chips([g_in.reshape(D, N_CHIPS, -1).transpose(1, 0, 2)], place, ["w_in"],
                            lambda arrays, plan: _run_plan(arrays, plan, "grad_swap_cores"), with_d_xn)
    dx, g_mix_pre = _rms_bwd(x, wt["ln_mix_pre"], carried["dxn"], dh, F32, bt, "rms_mix_pre_bwd")
    exchanged = dict(zip(late_names, zip(*early)))
    exchanged["w_in"] = (last[0][0], last[1][0])

    grads = dict(
        sb_out_gain=g_sb_gain, dn_conv_w=g_dn_conv, dn_a_log=g_a_log[:, H:2 * H],
        dn_dt_bias=g_dt_bias[:, H:2 * H], dn_out_gain=jnp.sum(g_dn_gain, axis=0),
        ln_mix_pre=g_mix_pre, ln_mix_post=g_mix_post,
        ffn_conv_w=jnp.concatenate([dwg, dwv], axis=1), ffn_conv_b=jnp.concatenate([dbg, dbv], axis=1),
        ln_ffn_pre=g_ffn_pre, ln_ffn_post=g_ffn_post)
    return loss, dx, grads, exchanged


def kernel(x, w_in, sb_out_gain, dn_conv_w, dn_a_log, dn_dt_bias, dn_out_gain, w_out, ln_mix_pre, ln_mix_post, w_up, ffn_conv_w, ffn_conv_b, w_down, ln_ffn_pre, ln_ffn_post, loss_target, m_w_in, m_sb_out_gain, m_dn_conv_w, m_dn_a_log, m_dn_dt_bias, m_dn_out_gain, m_w_out, m_ln_mix_pre, m_ln_mix_post, m_w_up, m_ffn_conv_w, m_ffn_conv_b, m_w_down, m_ln_ffn_pre, m_ln_ffn_post, v_w_in, v_sb_out_gain, v_dn_conv_w, v_dn_a_log, v_dn_dt_bias, v_dn_out_gain, v_w_out, v_ln_mix_pre, v_ln_mix_post, v_w_up, v_ffn_conv_w, v_ffn_conv_b, v_w_down, v_ln_ffn_pre, v_ln_ffn_post):
    given = dict(locals())
    wl = {n: given[n][0] for n in WEIGHTS}
    ml = {n: given["m_" + n][0] for n in WEIGHTS}
    vl = {n: given["v_" + n][0] for n in WEIGHTS}
    for d in (wl, ml, vl):
        for n in SMALL:
            if d[n].ndim == 1:
                d[n] = d[n][None]
    cx, cy, cc = lax.axis_index("x"), lax.axis_index("y"), lax.axis_index("c")
    chip = 2 * cx + cy
    D = x.shape[2]
    W = D // 2

    first = ("w_in",) + CONV_SHARDED
    mine = [wl["w_in"].astype(BF16)] + [wl[n] for n in CONV_SHARDED]
    theirs = _run_plan(mine, _gather_plan(mine, [True, False, False]), "gather_w_in")
    got = {n: lax.dynamic_update_index_in_dim(t, s, chip, 0) for n, t, s in zip(first, theirs, mine)}
    columns = lambda g: g.transpose(1, 0, 2).reshape(g.shape[1], N_CHIPS * g.shape[2])
    wt = {n: wl[n] for n in SMALL}
    w_in_all = columns(got["w_in"])
    wt["w_in"] = jnp.pad(w_in_all, ((0, 0), (0, 7 * W + LANES - w_in_all.shape[1])))
    for n in CONV_SHARDED:
        wt[n] = columns(got[n])
    late = {n: wl[n].astype(BF16) for n in ("w_out", "w_up", "w_down")}

    place = jnp.stack([cc, chip]).astype(jnp.int32)
    loss, dx, grads, exchanged = _step(x[0], loss_target[0], wt, late, chip, place)
    loss = lax.psum(loss, ("x", "y", "c"))

    names = list(MATRICES)
    reduced = [_add_chips(*exchanged[n], PAIR_ROWS, "grad_add_chips_" + n) for n in names]
    siblings = _run_plan(reduced, _join_plan(reduced), "grad_join_cores")
    gl = {}

    small, small_layout = _pack([grads[n] for n in SMALL], HALO)
    small = _allreduce_small(small, "grad_allreduce_small")
    for n, g in zip(SMALL, _unpack(small, small_layout)):
        if n in CONV_SHARDED:
            size = g.shape[1] // N_CHIPS
            g = lax.dynamic_slice_in_dim(g, chip * size, size, axis=1)
        gl[n] = g

    delta, new_m, new_v = {}, {}, {}
    for n, mine_half, sibling_half in zip(names, reduced, siblings):
        gl[n], delta[n], new_m[n], new_v[n] = _adamw_halves(wl[n], mine_half, sibling_half, place, ml[n], vl[n],
                                                            ADAM_ROWS, "adamw_" + n)
    packs = [_pack([d[n] for n in SMALL], HALO) for d in (wl, gl, ml, vl)]
    outs = _adamw(*[p[0] for p in packs], ADAM_ROWS, "adamw_small")
    for res, o in zip((delta, new_m, new_v), outs):
        res.update(zip(SMALL, _unpack(o, packs[0][1])))

    shaped = lambda d: [d[n].reshape(given[n].shape) for n in WEIGHTS]
    return (loss, dx[None], *shaped(gl), *shaped(delta), *shaped(new_m), *shaped(new_v))
```

```python
import functools

import numpy as np
import jax
import jax.numpy as jnp
from jax import lax
from jax.experimental import pallas as pl
from jax.experimental.pallas import tpu as pltpu

F32 = jnp.float32
BF16 = jnp.bfloat16
HEAD_DIM = 128
CHUNK = 64
ROWS = 2 * CHUNK
EPS = 1e-6
EXP_UNDERFLOW = 110.0
LANES = 128
HALO = 8
VMEM_LIMIT = 48 * 1024 * 1024
ADAM_LR, ADAM_B1, ADAM_B2, ADAM_EPS, ADAM_WD, ADAM_STEP = 0.001, 0.9, 0.999, 1e-08, 0.01, 10
MESH = pl.DeviceIdType.MESH

NN = (((1,), (0,)), ((), ()))
NT = (((1,), (1,)), ((), ()))
TN = (((0,), (0,)), ((), ()))


def _params(n_axes):
    return pltpu.CompilerParams(dimension_semantics=("arbitrary",) * n_axes, vmem_limit_bytes=VMEM_LIMIT)


def _bdot(a, b, dims=NN):
    return lax.dot_general(a.astype(BF16), b.astype(BF16), dims, preferred_element_type=F32)


def _split3(a):
    hi = a.astype(BF16)
    r1 = a - hi.astype(F32)
    mid = r1.astype(BF16)
    lo = (r1 - mid.astype(F32)).astype(BF16)
    return hi, mid, lo


def _dot3(a, sel, dims=NN):
    return sum(lax.dot_general(p, sel, dims, preferred_element_type=F32) for p in _split3(a))


def _dot3r(sel, a, dims=NN):
    return sum(lax.dot_general(sel, p, dims, preferred_element_type=F32) for p in _split3(a))


def _iota2(n, m):
    return lax.broadcasted_iota(jnp.int32, (n, m), 0), lax.broadcasted_iota(jnp.int32, (n, m), 1)


def _sigmoid(x):
    return 1.0 / (1.0 + jnp.exp(-x))


def _softplus(x):
    return jnp.maximum(x, 0.0) + jnp.log(1.0 + jnp.exp(-jnp.abs(x)))


def _fit(values, target):
    values = [v for v in (values if isinstance(values, (list, tuple)) else [values]) if v]
    best = None
    for t in range(LANES, min(min(values), target) + 1, LANES):
        if all(v % t == 0 for v in values):
            best = t
    assert best is not None, (values, target)
    return best


def _mm(parts, mode, out_dtype, tm, tn, name, n_window=None, out_shard=None, into=None, comm=None):
    dims = {"nn": NN, "nt": NT, "tn": TN}[mode]
    a0, b0 = parts[0][0], parts[0][1]
    b3 = b0.ndim == 3
    shard_c = b0.shape[2] if b3 else None
    M = a0.shape[1] if mode == "tn" else a0.shape[0]
    if mode == "nt":
        n_full = b0.shape[1] if b3 else b0.shape[0]
    else:
        n_full = b0.shape[0] * b0.shape[2] if b3 else b0.shape[1]
    n0, N = n_window if n_window is not None else (0, n_full)
    out_n0 = into[1] if into is not None else 0
    tm = _fit(M, tm)
    tn = _fit([N, n0, out_n0, out_shard, shard_c if mode != "nt" else None], tn)
    specs_a, specs_b, offs, nks = [], [], [], []
    off = 0
    for a, b, tk, a_k0, b_k0 in parts:
        K = a.shape[0] if mode == "tn" else a.shape[1]
        tk = _fit([K, a_k0, b_k0, shard_c if mode == "nt" else None], tk)
        nk = K // tk
        kk = lambda k, o=off, n=nk: jnp.clip(k - o, 0, n - 1)
        ao, bo, no = a_k0 // tk, b_k0 // tk, n0 // tn
        if mode == "tn":
            specs_a.append(pl.BlockSpec((tk, tm), lambda i, j, k, kk=kk, ao=ao: (kk(k) + ao, i)))
        else:
            specs_a.append(pl.BlockSpec((tm, tk), lambda i, j, k, kk=kk, ao=ao: (i, kk(k) + ao)))
        if mode == "nt":
            if b3:
                per = shard_c // tk
                specs_b.append(pl.BlockSpec((None, tn, tk), lambda i, j, k, kk=kk, bo=bo, per=per:
                                            ((kk(k) + bo) // per, j, (kk(k) + bo) % per)))
            else:
                specs_b.append(pl.BlockSpec((tn, tk), lambda i, j, k, kk=kk, bo=bo: (j, kk(k) + bo)))
        else:
            if b3:
                per = shard_c // tn
                specs_b.append(pl.BlockSpec((None, tk, tn), lambda i, j, k, kk=kk, bo=bo, no=no, per=per:
                                            ((j + no) // per, kk(k) + bo, (j + no) % per)))
            else:
                specs_b.append(pl.BlockSpec((tk, tn), lambda i, j, k, kk=kk, bo=bo, no=no: (kk(k) + bo, j + no)))
        offs.append(off)
        nks.append(nk)
        off += nk
    nk_total = off
    n_parts = len(parts)

    comm = comm if comm is not None else _Comm()
    grid = (M // tm, N // tn, nk_total)
    n_in = 2 * n_parts + (1 if into is not None else 0)

    def body(*refs):
        ins, (o_ref,), scratch, (first, mid, last) = comm.split(refs, n_in, 1, 0 if nk_total == 1 else 1)
        a_refs, b_refs = ins[:n_parts], ins[n_parts:2 * n_parts]
        at = lambda step: functools.reduce(lambda x, y: x & y, [pl.program_id(d) == step[d] for d in range(3)])
        pl.when(at((0, 0, 0)))(first)
        pl.when(at((grid[0] // 2, 0, 0)))(mid)
        if nk_total == 1:
            o_ref[...] = _bdot(a_refs[0][...], b_refs[0][...], dims).astype(out_dtype)
        else:
            acc = scratch[0]
            k = pl.program_id(2)

            @pl.when(k == 0)
            def _():
                acc[...] = jnp.zeros_like(acc)

            for p in range(n_parts):
                @pl.when((k >= offs[p]) & (k < offs[p] + nks[p]))
                def _(p=p):
                    acc[...] += _bdot(a_refs[p][...], b_refs[p][...], dims)

            @pl.when(k == nk_total - 1)
            def _():
                o_ref[...] = acc[...].astype(out_dtype)
        pl.when(at(tuple(g - 1 for g in grid)))(last)

    jo = out_n0 // tn
    if out_shard is not None:
        per_o = out_shard // tn
        out_spec = pl.BlockSpec((None, tm, tn), lambda i, j, k: ((j + jo) // per_o, i, (j + jo) % per_o))
        out_shape = jax.ShapeDtypeStruct((N // out_shard, M, out_shard), out_dtype)
    else:
        out_spec = pl.BlockSpec((tm, tn), lambda i, j, k: (i, j + jo))
        out_shape = jax.ShapeDtypeStruct((M, N), out_dtype)
    ins = [p[0] for p in parts] + [p[1] for p in parts]
    in_specs = specs_a + specs_b
    aliases = {}
    if into is not None:
        out_shape = jax.ShapeDtypeStruct(into[0].shape, into[0].dtype)
        aliases = {len(ins): 0}
        ins.append(into[0])
        in_specs.append(pl.BlockSpec(memory_space=pl.ANY))
    (out,), carried = comm.call(body, name, grid, in_specs, (out_spec,), (out_shape,),
                                [] if nk_total == 1 else [pltpu.VMEM((tm, tn), F32)], ins, aliases)
    return (out, carried) if comm.phases is not None else out


def _rms_fwd(x, gain, resid, out_dtype, bt, name):
    T, D = x.shape
    row = pl.BlockSpec((bt, D), lambda i: (i, 0))
    vec = pl.BlockSpec((1, D), lambda i: (0, 0))

    def body(*refs):
        x_ref, g_ref = refs[0], refs[1]
        o_ref = refs[-1]
        xv = x_ref[...]
        y = xv * lax.rsqrt(jnp.mean(xv * xv, axis=-1, keepdims=True) + EPS) * g_ref[...]
        if resid is not None:
            y = refs[2][...] + y
        o_ref[...] = y.astype(out_dtype)

    ins = [x, gain] + ([resid] if resid is not None else [])
    return pl.pallas_call(
        body, name=name, grid=(T // bt,),
        in_specs=[row, vec] + ([row] if resid is not None else []),
        out_specs=row, out_shape=jax.ShapeDtypeStruct((T, D), out_dtype), compiler_params=_params(1),
    )(*ins)


def _rms_bwd_math(xv, g, dy):
    r = lax.rsqrt(jnp.mean(xv * xv, axis=-1, keepdims=True) + EPS)
    n = xv * r
    gy = dy * g
    dx = r * (gy - n * jnp.mean(gy * n, axis=-1, keepdims=True))
    return dx, dy * n


def _rms_bwd(x, gain, dy, resid, out_dtype, bt, name):
    T, D = x.shape
    row = pl.BlockSpec((bt, D), lambda i: (i, 0))
    vec = pl.BlockSpec((1, D), lambda i: (0, 0))

    def body(*refs):
        x_ref, g_ref, dy_ref = refs[0], refs[1], refs[2]
        dx_ref, dg_ref = refs[-2], refs[-1]
        dx, dgp = _rms_bwd_math(x_ref[...], g_ref[...], dy_ref[...].astype(F32))
        if resid is not None:
            dx = refs[3][...] + dx
        dx_ref[...] = dx.astype(out_dtype)

        @pl.when(pl.program_id(0) == 0)
        def _():
            dg_ref[...] = jnp.zeros_like(dg_ref)

        dg_ref[...] += jnp.sum(dgp, axis=0, keepdims=True)

    ins = [x, gain, dy] + ([resid] if resid is not None else [])
    return pl.pallas_call(
        body, name=name, grid=(T // bt,),
        in_specs=[row, vec, row] + ([row] if resid is not None else []),
        out_specs=(row, vec),
        out_shape=(jax.ShapeDtypeStruct((T, D), out_dtype), jax.ShapeDtypeStruct((1, D), F32)),
        compiler_params=_params(1),
    )(*ins)


def _loss_head(f, gain, h, target, bt, name):
    T, D = f.shape
    row = pl.BlockSpec((bt, D), lambda i: (i, 0))
    vec = pl.BlockSpec((1, D), lambda i: (0, 0))

    def body(f_ref, g_ref, h_ref, t_ref, dy_ref, df_ref, dg_ref, sq_ref):
        fv, g = f_ref[...], g_ref[...]
        r = lax.rsqrt(jnp.mean(fv * fv, axis=-1, keepdims=True) + EPS)
        n = fv * r
        err = (h_ref[...] + n * g) - t_ref[...]
        dy = err * (1.0 / D)
        gy = dy * g
        df = r * (gy - n * jnp.mean(gy * n, axis=-1, keepdims=True))
        dy_ref[...] = dy
        df_ref[...] = df.astype(BF16)

        @pl.when(pl.program_id(0) == 0)
        def _():
            dg_ref[...] = jnp.zeros_like(dg_ref)
            sq_ref[...] = jnp.zeros_like(sq_ref)

        dg_ref[...] += jnp.sum(dy * n, axis=0, keepdims=True)
        sq_ref[...] += jnp.sum(err * err, axis=0, keepdims=True)

    return pl.pallas_call(
        body, name=name, grid=(T // bt,), in_specs=[row, vec, row, row], out_specs=(row, row, vec, vec),
        out_shape=(jax.ShapeDtypeStruct((T, D), F32), jax.ShapeDtypeStruct((T, D), BF16),
                   jax.ShapeDtypeStruct((1, D), F32), jax.ShapeDtypeStruct((1, D), F32)),
        compiler_params=_params(1),
    )(f, gain, h, target)


def _sb_logits(q, k, valid):
    z = lax.dot_general(q, k, NT, preferred_element_type=F32) * (HEAD_DIM ** -0.5)
    sp = jnp.log(1.0 + jnp.exp(-jnp.abs(z)))
    lb = jnp.minimum(z, 0.0) - sp
    l1 = -(jnp.maximum(z, 0.0) + sp)
    return lb, (l1 if valid is None else jnp.where(valid, l1, 0.0))


def _masked(valid, x):
    return x if valid is None else jnp.where(valid, x, 0.0)


def _heads_per_step(n_heads):
    return 2 if n_heads % 2 == 0 else 1


def _dot2(a, sel):
    hi = a.astype(BF16)
    lo = (a - hi.astype(F32)).astype(BF16)
    return jnp.dot(hi, sel, preferred_element_type=F32) + jnp.dot(lo, sel, preferred_element_type=F32)


class _Comm:
    def __init__(self, arrays=(), plan=((), (), None)):
        self.arrays = list(arrays)
        self.out_shapes, self.sems, self.phases = list(plan[0]), list(plan[1]), plan[2]

    def split(self, refs, n_in, n_out, n_scratch):
        a, o = len(self.arrays), len(self.out_shapes)
        cuts = np.cumsum([0, n_in, a, n_out, o, n_scratch])
        ins, cin, outs, cout, scratch = (refs[cuts[t]:cuts[t + 1]] for t in range(5))
        if self.phases is None:
            return ins, outs, scratch, (lambda: None,) * 3
        return ins, outs, scratch, self.phases(cin, cout, refs[cuts[5]:])

    def call(self, body, name, grid, in_specs, out_specs, out_shape, scratch_shapes, operands, aliases=None):
        outs = pl.pallas_call(
            body, name=name, grid=grid, in_specs=list(in_specs) + [_HBM] * len(self.arrays),
            out_specs=tuple(out_specs) + (_HBM,) * len(self.out_shapes),
            out_shape=tuple(out_shape) + tuple(self.out_shapes),
            scratch_shapes=list(scratch_shapes) + self.sems, input_output_aliases=aliases or {},
            compiler_params=pltpu.CompilerParams(dimension_semantics=("arbitrary",) * len(grid),
                                                 vmem_limit_bytes=VMEM_LIMIT, has_side_effects=self.phases is not None),
        )(*operands, *self.arrays)
        return outs[:len(out_shape)], outs[len(out_shape):]


def _sb_fwd(qkv, gain, blk, name, comm):
    T, W = qkv.shape[0], qkv.shape[1] // 3
    H = W // HEAD_DIM
    nq = T // blk
    hp = _heads_per_step(H)
    ng = H // hp
    lanes = [slice(t * HEAD_DIM, (t + 1) * HEAD_DIM) for t in range(hp)]

    def body(*refs):
        (q_ref, k_ref, v_ref, g_ref), (o_ref, mix_ref, lt_ref, swept_ref), _, (first, mid, last) = comm.split(refs, 4, 4, 0)
        h, i = pl.program_id(0), pl.program_id(1)
        pl.when((h == 0) & (i == 0))(first)
        pl.when((h == ng // 2) & (i == 0))(mid)
        q = [q_ref[:, ln] for ln in lanes]
        row, col = _iota2(blk, blk)
        after = (row > col).astype(BF16)

        def step(kb, carry, valid):
            ks = pl.ds(pl.multiple_of(kb * blk, blk), blk)
            out = []
            for t, (run, acc) in enumerate(carry):
                lb, l1 = _sb_logits(q[t], k_ref[ks, lanes[t]], valid)
                att = _masked(valid, jnp.exp(lb + _dot2(l1, after) + run))
                out.append((run + jnp.sum(l1, axis=1, keepdims=True), acc + _bdot(att, v_ref[ks, lanes[t]])))
            return tuple(out)

        zero = (jnp.zeros((blk, 1), F32), jnp.zeros((blk, HEAD_DIM), F32))
        def alive(state):
            jj, c = state
            return (jj < i) & (functools.reduce(jnp.maximum, [jnp.max(run) for run, _ in c]) > -EXP_UNDERFLOW)

        swept, carry = lax.while_loop(alive, lambda st: (st[0] + 1, step(i - 1 - st[0], st[1], None)),
                                      (jnp.int32(0), step(i, (zero,) * hp, col < row)))
        swept_ref[h, i] = swept
        for t, (run, o) in enumerate(carry):
            o_ref[:, lanes[t]] = o
            r = lax.rsqrt(jnp.mean(o * o, axis=-1, keepdims=True) + EPS)
            mix_ref[:, lanes[t]] = (o * r * g_ref[...]).astype(BF16)
            lt_ref[:, lanes[t]] = jnp.broadcast_to(run, (blk, HEAD_DIM))
        pl.when((h == ng - 1) & (i == nq - 1))(last)

    wide = hp * HEAD_DIM
    qb = pl.BlockSpec((blk, wide), lambda h, i: (i, h))
    return comm.call(
        body, name, (ng, nq),
        [qb, pl.BlockSpec((T, wide), lambda h, i: (0, ng + h)),
         pl.BlockSpec((T, wide), lambda h, i: (0, 2 * ng + h)), pl.BlockSpec((1, HEAD_DIM), lambda h, i: (0, 0))],
        (qb, qb, qb, pl.BlockSpec(memory_space=pltpu.SMEM)),
        (jax.ShapeDtypeStruct((T, W), F32), jax.ShapeDtypeStruct((T, W), BF16), jax.ShapeDtypeStruct((T, W), F32),
         jax.ShapeDtypeStruct((ng, nq), jnp.int32)),
        [], (qkv, qkv, qkv, gain))


def _headnorm_bwd(o, gain, dmix, bt, name, comm):
    T, W = o.shape
    H = W // HEAD_DIM
    nt = T // bt
    blk = pl.BlockSpec((bt, HEAD_DIM), lambda i, h: (i, h))
    vec = pl.BlockSpec((1, HEAD_DIM), lambda i, h: (0, 0))

    def body(*refs):
        (o_ref, g_ref, d_ref), (do_ref, dg_ref), _, (first, mid, last) = comm.split(refs, 3, 2, 0)
        i, h = pl.program_id(0), pl.program_id(1)
        pl.when((i == 0) & (h == 0))(first)
        pl.when((i == nt // 2) & (h == 0))(mid)
        do, dgp = _rms_bwd_math(o_ref[...], g_ref[...], d_ref[...])
        do_ref[...] = do

        @pl.when((i == 0) & (h == 0))
        def _():
            dg_ref[...] = jnp.zeros_like(dg_ref)

        dg_ref[...] += jnp.sum(dgp, axis=0, keepdims=True)
        pl.when((i == nt - 1) & (h == H - 1))(last)

    return comm.call(body, name, (nt, H), [blk, vec, blk], (blk, vec),
                     (jax.ShapeDtypeStruct((T, W), F32), jax.ShapeDtypeStruct((1, HEAD_DIM), F32)), [], (o, gain, dmix))


def _sb_bwd(qkv, do, lt, swept, blk, name, comm):
    T, W = qkv.shape[0], qkv.shape[1] // 3
    H = W // HEAD_DIM
    nq = T // blk
    scale = HEAD_DIM ** -0.5
    hp = _heads_per_step(H)
    ng = H // hp
    lanes = [slice(t * HEAD_DIM, (t + 1) * HEAD_DIM) for t in range(hp)]

    def body(*refs):
        ((q_ref, k_ref, v_ref, do_ref, lt_ref, swept_ref), (dq_ref, dk_out, dv_out), (dk_ref, dv_ref),
         (first, mid, last)) = comm.split(refs, 6, 3, 2)
        h, i = pl.program_id(0), pl.program_id(1)
        pl.when((h == 0) & (i == 0))(first)
        pl.when((h == ng // 2) & (i == 0))(mid)

        @pl.when(i == 0)
        def _():
            dk_ref[...] = jnp.zeros_like(dk_ref)
            dv_ref[...] = jnp.zeros_like(dv_ref)

        q = [q_ref[:, ln] for ln in lanes]
        dob = [do_ref[:, ln].astype(BF16) for ln in lanes]
        total = [lt_ref[:, ln][:, :1] for ln in lanes]
        row, col = _iota2(blk, blk)
        upto = (row <= col).astype(BF16)
        before = (row < col).astype(BF16)

        def step(kb, carry, valid):
            ks = pl.ds(pl.multiple_of(kb * blk, blk), blk)
            out = []
            for t, (seen, psum, dq) in enumerate(carry):
                k, v = k_ref[ks, lanes[t]], v_ref[ks, lanes[t]]
                lb, l1 = _sb_logits(q[t], k, valid)
                later = total[t] - seen - _dot2(l1, upto)
                att = _masked(valid, jnp.exp(lb + later))
                p = att * lax.dot_general(dob[t], v, NT, preferred_element_type=F32)
                c = psum + _dot2(p, before)
                sig = jnp.exp(lb)
                dz = (_masked(valid, p * (1.0 - sig) - c * sig) * scale).astype(BF16)
                dq = dq + jnp.dot(dz, k, preferred_element_type=F32)
                dk_ref[ks, lanes[t]] += lax.dot_general(dz, q[t], TN, preferred_element_type=F32)
                dv_ref[ks, lanes[t]] += lax.dot_general(att.astype(BF16), dob[t], TN, preferred_element_type=F32)
                out.append((seen + jnp.sum(l1, axis=1, keepdims=True), psum + jnp.sum(p, axis=1, keepdims=True), dq))
            return tuple(out)

        zero = jnp.zeros((blk, 1), F32)
        start = ((zero, zero, jnp.zeros((blk, HEAD_DIM), F32)),) * hp
        carry = step(i, lax.fori_loop(i - swept_ref[h, i], i, lambda kb, c: step(kb, c, None), start), col < row)
        for t in range(hp):
            dq_ref[:, lanes[t]] = carry[t][2].astype(BF16)

        @pl.when(i == nq - 1)
        def _():
            dk_out[...] = dk_ref[...].astype(BF16)
            dv_out[...] = dv_ref[...].astype(BF16)

        pl.when((h == ng - 1) & (i == nq - 1))(last)

    wide = hp * HEAD_DIM
    qb = pl.BlockSpec((blk, wide), lambda h, i: (i, h))
    head = pl.BlockSpec((T, wide), lambda h, i: (0, h))
    out = jax.ShapeDtypeStruct((T, W), BF16)
    return comm.call(
        body, name, (ng, nq),
        [qb, pl.BlockSpec((T, wide), lambda h, i: (0, ng + h)),
         pl.BlockSpec((T, wide), lambda h, i: (0, 2 * ng + h)), qb, qb, pl.BlockSpec(memory_space=pltpu.SMEM)],
        (qb, head, head), (out, out, out), [pltpu.VMEM((T, wide), F32)] * 2, (qkv, qkv, qkv, do, lt, swept))


def _expanders(H):
    lane = np.arange(H * HEAD_DIM) // HEAD_DIM
    eb = np.zeros((LANES, H * HEAD_DIM), np.float32)
    eg = np.zeros((LANES, H * HEAD_DIM), np.float32)
    eb[lane, np.arange(H * HEAD_DIM)] = 1.0
    eg[H + lane, np.arange(H * HEAD_DIM)] = 1.0
    sb = np.zeros((H * HEAD_DIM, LANES), np.float32)
    sg = np.zeros((H * HEAD_DIM, LANES), np.float32)
    sb[np.arange(H) * HEAD_DIM, np.arange(H)] = 1.0
    sg[np.arange(H) * HEAD_DIM, H + np.arange(H)] = 1.0
    return [jnp.asarray(m, BF16) for m in (eb, eg, sb, sg)]


def _conv_taps(ext_ref, w, n_out, lead):
    K = w.shape[0]
    out = None
    for j in range(K):
        term = ext_ref[pl.ds(lead - (K - 1) + j, n_out), :] * w[j:j + 1, :]
        out = term if out is None else out + term
    return out


STRIP_ROWS = 64


def _strips(n_rows):
    return [(r0, min(STRIP_ROWS, n_rows - r0)) for r0 in range(0, n_rows, STRIP_ROWS)]


def _l2_heads(s, H, fn):
    return jnp.concatenate([fn(s[:, h * HEAD_DIM:(h + 1) * HEAD_DIM]) for h in range(H)], axis=1)


def _dn_pre_fwd(pdn, pba, conv_w, a_log, dt_bias, bt, name):
    T, W = pdn.shape[0], pdn.shape[1] // 4
    H = W // HEAD_DIM
    eb, eg, _, _ = _expanders(H)
    nb = T // bt

    def body(x_ref, prev_ref, ba_ref, w_ref, al_ref, dt_ref, eb_ref, eg_ref,
             q_ref, k_ref, v_ref, bx_ref, gx_ref, ext):
        i = pl.program_id(0)
        ext[pl.ds(0, HALO), :] = jnp.where(i > 0, prev_ref[...], 0.0)
        ext[pl.ds(HALO, bt), :] = x_ref[...]
        c = _conv_taps(ext, w_ref[...], bt, HALO)
        s = c * _sigmoid(c)
        q_ref[...] = _l2_heads(s[:, :W], H, lambda t: t * lax.rsqrt(jnp.sum(t * t, axis=-1, keepdims=True) + EPS)
                               * (HEAD_DIM ** -0.5))
        k_ref[...] = _l2_heads(s[:, W:2 * W], H, lambda t: t * lax.rsqrt(jnp.sum(t * t, axis=-1, keepdims=True) + EPS))
        v_ref[...] = s[:, 2 * W:]
        ba = ba_ref[...]
        beta = _sigmoid(ba)
        graw = -jnp.exp(al_ref[...]) * _softplus(ba + dt_ref[...])
        row, col = _iota2(bt, bt)
        tri = ((row // CHUNK == col // CHUNK) & (row >= col)).astype(BF16)
        gcum = _dot3r(tri, graw)
        bx_ref[...] = _dot3(beta, eb_ref[...])
        gx_ref[...] = _dot3(gcum, eg_ref[...])

    C = 3 * W
    rowb = lambda w: pl.BlockSpec((bt, w), lambda i: (i, 0))
    full = lambda a: pl.BlockSpec(a.shape, lambda i: (0,) * a.ndim)
    out = jax.ShapeDtypeStruct((T, W), F32)
    return pl.pallas_call(
        body, name=name, grid=(nb,),
        in_specs=[rowb(C), pl.BlockSpec((HALO, C), lambda i: (jnp.maximum(i * (bt // HALO) - 1, 0), 0)),
                  rowb(LANES), full(conv_w), full(a_log), full(dt_bias), full(eb), full(eg)],
        out_specs=(rowb(W),) * 5, out_shape=(out,) * 5,
        scratch_shapes=[pltpu.VMEM((bt + HALO, C), F32)], compiler_params=_params(1),
    )(pdn, pdn, pba, conv_w, a_log, dt_bias, eb, eg)


def _dn_pre_bwd(pdn, pba, conv_w, a_log, dt_bias, dq, dk, dv, dbx, dgx, bt, name):
    T, W = pdn.shape[0], pdn.shape[1] // 4
    H = W // HEAD_DIM
    C = 3 * W
    K = conv_w.shape[0]
    _, _, sb, sg = _expanders(H)
    nb = T // bt
    n_ext = bt + HALO

    def body(x_ref, prev_ref, next_ref, ba_ref, w_ref, al_ref, dt_ref, sb_ref, sg_ref,
             dq_ref, dqn_ref, dk_ref, dkn_ref, dv_ref, dvn_ref, dbx_ref, dgx_ref,
             dx_ref, dba_ref, dw_ref, dal_ref, ddt_ref, ext, dext, dcext):
        i = pl.program_id(0)
        last = i == nb - 1
        ext[pl.ds(0, HALO), :] = jnp.where(i > 0, prev_ref[...], 0.0)
        ext[pl.ds(HALO, bt), :] = x_ref[...]
        ext[pl.ds(HALO + bt, HALO), :] = jnp.where(last, 0.0, next_ref[...])
        dext[pl.ds(0, bt), pl.ds(0, W)] = dq_ref[...]
        dext[pl.ds(0, bt), pl.ds(W, W)] = dk_ref[...]
        dext[pl.ds(0, bt), pl.ds(2 * W, W)] = dv_ref[...]
        dext[pl.ds(bt, HALO), pl.ds(0, W)] = jnp.where(last, 0.0, dqn_ref[...])
        dext[pl.ds(bt, HALO), pl.ds(W, W)] = jnp.where(last, 0.0, dkn_ref[...])
        dext[pl.ds(bt, HALO), pl.ds(2 * W, W)] = jnp.where(last, 0.0, dvn_ref[...])
        w = w_ref[...]
        l2_scale = (HEAD_DIM ** -0.5, 1.0, None)

        @pl.when(i == 0)
        def _():
            dw_ref[...] = jnp.zeros_like(dw_ref)
            dal_ref[...] = jnp.zeros_like(dal_ref)
            ddt_ref[...] = jnp.zeros_like(ddt_ref)

        for g in range(C // HEAD_DIM):
            cols = pl.ds(g * HEAD_DIM, HEAD_DIM)
            wg = w[:, g * HEAD_DIM:(g + 1) * HEAD_DIM]
            scale = l2_scale[g // H]
            for r0, rows in _strips(bt) + [(bt, HALO)]:
                c = None
                for j in range(K):
                    term = ext[pl.ds(HALO + r0 - (K - 1) + j, rows), cols] * wg[j:j + 1, :]
                    c = term if c is None else c + term
                sg_c = _sigmoid(c)
                ds = dext[pl.ds(r0, rows), cols]
                if scale is not None:
                    s = c * sg_c
                    r = lax.rsqrt(jnp.sum(s * s, axis=-1, keepdims=True) + EPS)
                    ds = scale * r * (ds - s * (r * r) * jnp.sum(s * ds, axis=-1, keepdims=True))
                dcext[pl.ds(r0, rows), cols] = ds * (sg_c * (1.0 + c * (1.0 - sg_c)))
            dw = [0.0] * K
            for r0, rows in _strips(bt):
                dx = None
                for j in range(K):
                    term = dcext[pl.ds(r0 + K - 1 - j, rows), cols] * wg[j:j + 1, :]
                    dx = term if dx is None else dx + term
                dx_ref[pl.ds(r0, rows), cols] = dx.astype(BF16)
                dcur = dcext[pl.ds(r0, rows), cols]
                for j in range(K):
                    dw[j] = dw[j] + jnp.sum(dcur * ext[pl.ds(HALO + r0 - (K - 1) + j, rows), cols], axis=0, keepdims=True)
            dw_ref[:, cols] += jnp.concatenate(dw, axis=0)

        ba = ba_ref[...]
        beta = _sigmoid(ba)
        al, dtb = al_ref[...], dt_ref[...]
        dbeta = _dot3(dbx_ref[...], sb_ref[...])
        dg = _dot3(dgx_ref[...], sg_ref[...])
        sp = _softplus(ba + dtb)
        da = dg * (-jnp.exp(al)) * _sigmoid(ba + dtb)
        dba_ref[...] = dbeta * beta * (1.0 - beta) + da
        dal_ref[...] += jnp.sum(dg * (-jnp.exp(al)) * sp, axis=0, keepdims=True)
        ddt_ref[...] += jnp.sum(da, axis=0, keepdims=True)

    rowb = lambda w: pl.BlockSpec((bt, w), lambda i: (i, 0))
    full = lambda a: pl.BlockSpec(a.shape, lambda i: (0,) * a.ndim)
    nxt = lambda w: pl.BlockSpec((HALO, w), lambda i: (jnp.minimum((i + 1) * (bt // HALO), T // HALO - 1), 0))
    vec = pl.BlockSpec((1, LANES), lambda i: (0, 0))
    return pl.pallas_call(
        body, name=name, grid=(nb,),
        in_specs=[rowb(C), pl.BlockSpec((HALO, C), lambda i: (jnp.maximum(i * (bt // HALO) - 1, 0), 0)), nxt(C),
                  rowb(LANES), full(conv_w), full(a_log), full(dt_bias), full(sb), full(sg),
                  rowb(W), nxt(W), rowb(W), nxt(W), rowb(W), nxt(W), rowb(W), rowb(W)],
        out_specs=(rowb(C), rowb(LANES), pl.BlockSpec((K, C), lambda i: (0, 0)), vec, vec),
        out_shape=(jax.ShapeDtypeStruct((T, C), BF16), jax.ShapeDtypeStruct((T, LANES), F32),
                   jax.ShapeDtypeStruct((K, C), F32), jax.ShapeDtypeStruct((1, LANES), F32),
                   jax.ShapeDtypeStruct((1, LANES), F32)),
        scratch_shapes=[pltpu.VMEM((bt + 2 * HALO, C), F32), pltpu.VMEM((n_ext, C), F32), pltpu.VMEM((n_ext, C), F32)],
        compiler_params=_params(1),
    )(pdn, pdn, pdn, pba, conv_w, a_log, dt_bias, sb, sg, dq, dq, dk, dk, dv, dv, dbx, dgx)


def _dot_split(a, b):
    a_hi, b_hi = a.astype(BF16), b.astype(BF16)
    a_lo, b_lo = (a - a_hi.astype(F32)).astype(BF16), (b - b_hi.astype(F32)).astype(BF16)
    dot = functools.partial(jnp.dot, preferred_element_type=F32)
    return dot(a_hi, b_hi) + (dot(a_hi, b_lo) + dot(a_lo, b_hi))


def _dn_local(q, k, v, beta, g, tm=None):
    row, col = _iota2(ROWS, ROWS)
    same = (row // CHUNK) == (col // CHUNK)
    causal = same & (row >= col)
    strict = same & (row > col)
    eye = (row == col).astype(F32)
    last_of = (col == (row // CHUNK) * CHUNK + (CHUNK - 1)).astype(BF16)
    eg = jnp.exp(g)
    decay = jnp.where(causal, jnp.exp(jnp.where(causal, g - g.T, 0.0)), 0.0)
    kb = k * beta
    vb = v * beta
    kk = _bdot(kb, k, NT)
    low = jnp.where(strict, kk * decay, 0.0)
    if tm is None:
        pw = -low
        tm = eye + pw
        for _ in range(5):
            pw = _dot_split(pw, pw)
            tm = tm + _dot_split(tm, pw)
    kbg = kb * eg
    u = _bdot(tm, vb)
    w = _bdot(tm, kbg)
    qk = _bdot(q, k, NT)
    qa = jnp.where(causal, qk * decay, 0.0)
    glast = _dot3r(last_of, g)
    e2 = jnp.exp(glast - g)
    return dict(row=row, col=col, same=same, causal=causal, strict=strict, eg=eg, decay=decay, kb=kb, vb=vb, kk=kk,
                tm=tm, kbg=kbg, u=u, w=w, qk=qk, qa=qa, glast=glast, e2=e2, kte=k * e2, qd=q * eg)


def _dn_core_fwd(q, k, v, bx, gx, z, gain, name, comm):
    T, W = q.shape
    H = W // HEAD_DIM
    nb = T // ROWS
    hp = _heads_per_step(H)
    ng = H // hp

    def body(*refs):
        ((q_ref, k_ref, v_ref, b_ref, g_ref, z_ref, gain_ref), (o_ref, mix_ref, ss_ref, tm_ref), (state,),
         (first, mid, last)) = comm.split(refs, 7, 4, 1)
        h, b = pl.program_id(0), pl.program_id(1)
        pl.when((h == 0) & (b == 0))(first)
        pl.when((h == ng // 2) & (b == 0))(mid)

        @pl.when(b == 0)
        def _():
            state[...] = jnp.zeros_like(state)

        for t in range(hp):
            ln = slice(t * HEAD_DIM, (t + 1) * HEAD_DIM)
            L = _dn_local(q_ref[:, ln], k_ref[:, ln], v_ref[:, ln], b_ref[:, ln], g_ref[:, ln])
            s = state[t]
            vns, qds = [], []
            for c in range(2):
                rows = slice(c * CHUNK, (c + 1) * CHUNK)
                ss_ref[t, c] = s
                vn = L["u"][rows] - _bdot(L["w"][rows], s)
                qds.append(_bdot(L["qd"][rows], s))
                vns.append(vn)
                s = s * jnp.exp(L["glast"][c * CHUNK:c * CHUNK + 1, :]) + _bdot(L["kte"][rows], vn, TN)
            state[t] = s
            tm_ref[:, ln] = L["tm"]
            o = jnp.concatenate(qds, axis=0) + _bdot(L["qa"], jnp.concatenate(vns, axis=0))
            o_ref[:, ln] = o
            zz = z_ref[:, ln]
            r = lax.rsqrt(jnp.mean(o * o, axis=-1, keepdims=True) + EPS)
            mix_ref[:, ln] = ((o * r * gain_ref[...]) * (zz * _sigmoid(zz))).astype(BF16)
        pl.when((h == ng - 1) & (b == nb - 1))(last)

    wide = hp * HEAD_DIM
    blk = pl.BlockSpec((ROWS, wide), lambda h, b: (b, h))
    zblk = pl.BlockSpec((ROWS, wide), lambda h, b: (b, 3 * ng + h))
    return comm.call(
        body, name, (ng, nb), [blk] * 5 + [zblk, pl.BlockSpec((1, HEAD_DIM), lambda h, b: (0, 0))],
        (blk, blk, pl.BlockSpec((hp, 2, HEAD_DIM, HEAD_DIM), lambda h, b: (h, b, 0, 0)), blk),
        (jax.ShapeDtypeStruct((T, W), F32), jax.ShapeDtypeStruct((T, W), BF16),
         jax.ShapeDtypeStruct((H, T // CHUNK, HEAD_DIM, HEAD_DIM), F32), jax.ShapeDtypeStruct((T, W), F32)),
        [pltpu.VMEM((hp, HEAD_DIM, HEAD_DIM), F32)], (q, k, v, bx, gx, z, gain))


def _dn_core_bwd(q, k, v, bx, gx, z, gain, o, dmix, ss, tms, dmix_col0, name, comm):
    T, W = q.shape
    H = W // HEAD_DIM
    nb = T // ROWS
    hp = _heads_per_step(H)
    ng = H // hp
    wide = hp * HEAD_DIM
    c0 = dmix_col0 // wide

    def body(*refs):
        ((q_ref, k_ref, v_ref, b_ref, g_ref, z_ref, gain_ref, o_ref, dm_ref, ss_ref, tm_ref),
         (dq_ref, dk_ref, dv_ref, dbx_ref, dgx_ref, dz_ref, dgain_ref), (dstate,),
         (first, mid, last)) = comm.split(refs, 11, 7, 1)
        pl.when((pl.program_id(0) == 0) & (pl.program_id(1) == 0))(first)
        pl.when((pl.program_id(0) == ng // 2) & (pl.program_id(1) == 0))(mid)

        @pl.when(pl.program_id(1) == 0)
        def _():
            dstate[...] = jnp.zeros_like(dstate)
            dgain_ref[...] = jnp.zeros_like(dgain_ref)

        refs = (q_ref, k_ref, v_ref, b_ref, g_ref, z_ref, gain_ref, o_ref, dm_ref, ss_ref, tm_ref,
                dq_ref, dk_ref, dv_ref, dbx_ref, dgx_ref, dz_ref, dgain_ref, dstate)
        for t in range(hp):
            one_head(t, *refs)
        pl.when((pl.program_id(0) == ng - 1) & (pl.program_id(1) == nb - 1))(last)

    def one_head(t, q_ref, k_ref, v_ref, b_ref, g_ref, z_ref, gain_ref, o_ref, dm_ref, ss_ref, tm_ref,
                 dq_ref, dk_ref, dv_ref, dbx_ref, dgx_ref, dz_ref, dgain_ref, dstate):
        ln = slice(t * HEAD_DIM, (t + 1) * HEAD_DIM)
        qv, kv, vv, beta, g = q_ref[:, ln], k_ref[:, ln], v_ref[:, ln], b_ref[:, ln], g_ref[:, ln]
        gain_v = gain_ref[...]
        ov, zz, dm = o_ref[:, ln], z_ref[:, ln], dm_ref[:, ln]
        r = lax.rsqrt(jnp.mean(ov * ov, axis=-1, keepdims=True) + EPS)
        n = ov * r
        sgz = _sigmoid(zz)
        d_on = dm * (zz * sgz)
        dz_ref[:, ln] = (dm * (n * gain_v) * (sgz * (1.0 + zz * (1.0 - sgz)))).astype(BF16)
        dgain_ref[t] += jnp.sum(d_on * n, axis=0, keepdims=True)
        gy = d_on * gain_v
        do = r * (gy - n * jnp.mean(gy * n, axis=-1, keepdims=True))

        L = _dn_local(qv, kv, vv, beta, g, tm_ref[:, ln])
        row, causal, strict = L["row"], L["causal"], L["strict"]
        u, w, qa, qd, kte, tm = L["u"], L["w"], L["qa"], L["qd"], L["kte"], L["tm"]
        s_in = [ss_ref[t, 0], ss_ref[t, 1]]
        vn = [u[c * CHUNK:(c + 1) * CHUNK] - _bdot(w[c * CHUNK:(c + 1) * CHUNK], s_in[c]) for c in range(2)]
        vn_all = jnp.concatenate(vn, axis=0)
        qat_do = _bdot(qa, do, TN)
        d_qa = jnp.where(causal, _bdot(do, vn_all, NT), 0.0)
        ds = dstate[t]
        d_vn, d_kte, d_qd, d_w, d_gl = [None] * 2, [None] * 2, [None] * 2, [None] * 2, [None] * 2
        for c in (1, 0):
            rows = slice(c * CHUNK, (c + 1) * CHUNK)
            egl = jnp.exp(L["glast"][c * CHUNK:c * CHUNK + 1, :])
            d_vn[c] = qat_do[rows] + _bdot(kte[rows], ds)
            d_kte[c] = _bdot(vn[c], ds, NT)
            d_gl[c] = jnp.sum(jnp.sum(ds * s_in[c], axis=1, keepdims=True), axis=0, keepdims=True) * egl
            d_qd[c] = _bdot(do[rows], s_in[c], NT)
            d_w[c] = -_bdot(d_vn[c], s_in[c], NT)
            ds = ds * egl + _bdot(qd[rows], do[rows], TN) - _bdot(w[rows], d_vn[c], TN)
        dstate[t] = ds
        d_u = jnp.concatenate(d_vn, axis=0)
        d_w = jnp.concatenate(d_w, axis=0)
        d_qd = jnp.concatenate(d_qd, axis=0)
        d_kte = jnp.concatenate(d_kte, axis=0)

        d_tm = _bdot(d_u, L["vb"], NT) + _bdot(d_w, L["kbg"], NT)
        d_vb = _bdot(tm, d_u, TN)
        d_kbg = _bdot(tm, d_w, TN)
        d_low = jnp.where(strict, -_bdot(_bdot(tm, d_tm, TN), tm, NT), 0.0)
        decay = L["decay"]
        d_kk = d_low * decay
        d_qk = d_qa * decay
        d_decay = d_low * L["kk"] + d_qa * L["qk"]
        eg, e2 = L["eg"], L["e2"]
        d_kb = _bdot(d_kk, kv) + d_kbg * eg
        dk_ref[:, ln] = _bdot(d_kk, L["kb"], TN) + _bdot(d_qk, qv, TN) + d_kb * beta + d_kte * e2
        dq_ref[:, ln] = _bdot(d_qk, kv) + d_qd * eg
        dv_ref[:, ln] = d_vb * beta
        rsum = lambda a: jnp.sum(a, axis=-1, keepdims=True)
        dbx_ref[:, ln] = jnp.broadcast_to(rsum(d_kb * kv) + rsum(d_vb * vv), (ROWS, HEAD_DIM))
        d_eg = rsum(d_kbg * L["kb"]) + rsum(d_qd * qv)
        t2 = rsum(d_kte * kv) * e2
        ed = d_decay * decay
        d_g = d_eg * eg - t2 + rsum(ed) - rsum(ed.T)
        col = L["col"]
        chunk_sum = L["same"].astype(BF16)
        is_last = (row % CHUNK) == (CHUNK - 1)
        d_glast = _dot3r(chunk_sum, t2) + jnp.where(row < CHUNK, d_gl[0], d_gl[1])
        d_g = d_g + jnp.where(is_last, d_glast, 0.0)
        suffix = (L["same"] & (col >= row)).astype(BF16)
        dgx_ref[:, ln] = _dot3r(suffix, d_g)

    rev = lambda b: nb - 1 - b
    blk = pl.BlockSpec((ROWS, wide), lambda h, b: (rev(b), h))
    zblk = pl.BlockSpec((ROWS, wide), lambda h, b: (rev(b), 3 * ng + h))
    dmblk = pl.BlockSpec((ROWS, wide), lambda h, b: (rev(b), c0 + h))
    out = jax.ShapeDtypeStruct((T, W), F32)
    return comm.call(
        body, name, (ng, nb),
        [blk] * 5 + [zblk, pl.BlockSpec((1, HEAD_DIM), lambda h, b: (0, 0)), blk, dmblk,
                     pl.BlockSpec((hp, 2, HEAD_DIM, HEAD_DIM), lambda h, b: (h, rev(b), 0, 0)), blk],
        (blk,) * 6 + (pl.BlockSpec((hp, 1, HEAD_DIM), lambda h, b: (h, 0, 0)),),
        (out,) * 5 + (jax.ShapeDtypeStruct((T, W), BF16), jax.ShapeDtypeStruct((H, 1, HEAD_DIM), F32)),
        [pltpu.VMEM((hp, HEAD_DIM, HEAD_DIM), F32)], (q, k, v, bx, gx, z, gain, o, dmix, ss, tms))


_GELU_C = 0.7978845608028654
_GELU_A = 0.044715


def _gelu(x):
    x2 = x * x
    t = jnp.tanh(x * (x2 * (_GELU_C * _GELU_A) + _GELU_C))
    half = 0.5 * x
    return half + half * t, t, half, x2


FUSED_ROWS = 1024
FUSED_CHUNK = 256
BF16_ROWS = 16


def _ffn_up_fused(hn, w_up, conv_w, conv_b, tn, name):
    T, D = hn.shape
    S, _, C = w_up.shape
    F = S * C // 2
    K = conv_w.shape[0]
    tm = min(FUSED_ROWS, T)
    tn = _fit([C], tn)
    per, nj = C // tn, F // tn
    chunk = min(FUSED_CHUNK, tm)

    def body(a_ref, ap_ref, bg_ref, bv_ref, wg_ref, wv_ref, cg_ref, cv_ref, ug_ref, uv_ref, o_ref, gext, vext):
        keep = pl.program_id(0) > 0
        mats = ((bg_ref[...], wg_ref[...], cg_ref[...], ug_ref, gext), (bv_ref[...], wv_ref[...], cv_ref[...], uv_ref, vext))
        for b, _, _, _, ext in mats:
            ext[pl.ds(0, HALO), :] = jnp.where(keep, _bdot(ap_ref[...], b)[BF16_ROWS - HALO:], 0.0)
        for c0 in range(0, tm, chunk):
            a = a_ref[pl.ds(c0, chunk), :]
            for b, _, _, u_ref, ext in mats:
                u = _bdot(a, b)
                u_ref[pl.ds(c0, chunk), :] = u
                ext[pl.ds(HALO + c0, chunk), :] = u
            for r0, rows in _strips(chunk):
                gate, val = [_conv_taps(ext, w, rows, HALO + c0 + r0) + cb for _, w, cb, _, ext in mats]
                o_ref[pl.ds(c0 + r0, rows), :] = (_gelu(gate)[0] * val).astype(BF16)

    out = pl.BlockSpec((tm, tn), lambda i, j: (i, j))
    return pl.pallas_call(
        body, name=name, grid=(T // tm, nj),
        in_specs=[pl.BlockSpec((tm, D), lambda i, j: (i, 0)),
                  pl.BlockSpec((BF16_ROWS, D), lambda i, j: (jnp.maximum(i * (tm // BF16_ROWS) - 1, 0), 0)),
                  pl.BlockSpec((None, D, tn), lambda i, j: (j // per, 0, j % per)),
                  pl.BlockSpec((None, D, tn), lambda i, j: ((nj + j) // per, 0, (nj + j) % per)),
                  pl.BlockSpec((K, tn), lambda i, j: (0, j)), pl.BlockSpec((K, tn), lambda i, j: (0, nj + j)),
                  pl.BlockSpec((1, tn), lambda i, j: (0, j)), pl.BlockSpec((1, tn), lambda i, j: (0, nj + j))],
        out_specs=(out, out, out),
        out_shape=(jax.ShapeDtypeStruct((T, F), F32), jax.ShapeDtypeStruct((T, F), F32), jax.ShapeDtypeStruct((T, F), BF16)),
        scratch_shapes=[pltpu.VMEM((tm + HALO, tn), F32)] * 2, compiler_params=_params(2),
    )(hn, hn, w_up, w_up, conv_w, conv_w, conv_b, conv_b)


def _ffn_mid_bwd(up_g, up_v, conv_w, conv_b, da, bt, bc, name):
    T, F = up_g.shape
    nc = F // bc
    K = conv_w.shape[0]
    nb = T // bt
    n_ext = bt + HALO

    def body(g_ref, gp_ref, gn_ref, v_ref, vp_ref, vn_ref, da_ref, dan_ref, wg_ref, wv_ref, bg_ref, bv_ref,
             dg_ref, dv_ref, dwg_ref, dwv_ref, dbg_ref, dbv_ref, gext, vext, dgext, dvext):
        i = pl.program_id(1)
        last = i == nb - 1
        for ext, cur, prev, nxt in ((gext, g_ref, gp_ref, gn_ref), (vext, v_ref, vp_ref, vn_ref)):
            ext[pl.ds(0, HALO), :] = jnp.where(i > 0, prev[...], 0.0)
            ext[pl.ds(HALO, bt), :] = cur[...]
            ext[pl.ds(HALO + bt, HALO), :] = jnp.where(last, 0.0, nxt[...])
        wg, wv, bg, bv = wg_ref[...], wv_ref[...], bg_ref[...], bv_ref[...]
        for r0, rows in _strips(bt) + [(bt, HALO)]:
            gate = _conv_taps(gext, wg, rows, HALO + r0) + bg
            val = _conv_taps(vext, wv, rows, HALO + r0) + bv
            dact = da_ref[pl.ds(r0, rows), :] if r0 < bt else jnp.where(last, 0.0, dan_ref[...])
            ge, t, half, x2 = _gelu(gate)
            dgelu = (0.5 * t + 0.5) + (half * (1.0 - t * t)) * (x2 * (3.0 * _GELU_C * _GELU_A) + _GELU_C)
            dgext[pl.ds(r0, rows), :] = dact * val * dgelu
            dvext[pl.ds(r0, rows), :] = dact * ge

        @pl.when(i == 0)
        def _():
            for ref in (dwg_ref, dwv_ref, dbg_ref, dbv_ref):
                ref[...] = jnp.zeros_like(ref)

        for dext, ext, w, dx_ref, dw_ref, db_ref in ((dgext, gext, wg, dg_ref, dwg_ref, dbg_ref),
                                                     (dvext, vext, wv, dv_ref, dwv_ref, dbv_ref)):
            dw, db = [0.0] * K, 0.0
            for r0, rows in _strips(bt):
                dx = None
                for j in range(K):
                    term = dext[pl.ds(r0 + K - 1 - j, rows), :] * w[j:j + 1, :]
                    dx = term if dx is None else dx + term
                dx_ref[pl.ds(r0, rows), :] = dx.astype(BF16)
                dcur = dext[pl.ds(r0, rows), :]
                for j in range(K):
                    dw[j] = dw[j] + jnp.sum(dcur * ext[pl.ds(HALO + r0 - (K - 1) + j, rows), :], axis=0, keepdims=True)
                db = db + jnp.sum(dcur, axis=0, keepdims=True)
            dw_ref[...] += jnp.concatenate(dw, axis=0)
            db_ref[...] += db

    prev = lambda i: jnp.maximum(i * (bt // HALO) - 1, 0)
    nxt = lambda i: jnp.minimum((i + 1) * (bt // HALO), T // HALO - 1)
    cur_g = pl.BlockSpec((bt, bc), lambda j, i: (i, j))
    outs = pl.pallas_call(
        body, name=name, grid=(nc, nb),
        in_specs=[cur_g, pl.BlockSpec((HALO, bc), lambda j, i: (prev(i), j)),
                  pl.BlockSpec((HALO, bc), lambda j, i: (nxt(i), j)),
                  cur_g, pl.BlockSpec((HALO, bc), lambda j, i: (prev(i), j)),
                  pl.BlockSpec((HALO, bc), lambda j, i: (nxt(i), j)),
                  cur_g, pl.BlockSpec((HALO, bc), lambda j, i: (nxt(i), j)),
                  pl.BlockSpec((K, bc), lambda j, i: (0, j)), pl.BlockSpec((K, bc), lambda j, i: (0, nc + j)),
                  pl.BlockSpec((1, bc), lambda j, i: (0, j)), pl.BlockSpec((1, bc), lambda j, i: (0, nc + j))],
        out_specs=(cur_g, cur_g, pl.BlockSpec((K, bc), lambda j, i: (0, j)), pl.BlockSpec((K, bc), lambda j, i: (0, j)),
                   pl.BlockSpec((1, bc), lambda j, i: (0, j)), pl.BlockSpec((1, bc), lambda j, i: (0, j))),
        out_shape=(jax.ShapeDtypeStruct((T, F), BF16), jax.ShapeDtypeStruct((T, F), BF16),
                   jax.ShapeDtypeStruct((K, F), F32), jax.ShapeDtypeStruct((K, F), F32),
                   jax.ShapeDtypeStruct((1, F), F32), jax.ShapeDtypeStruct((1, F), F32)),
        scratch_shapes=[pltpu.VMEM((bt + 2 * HALO, bc), F32)] * 2 + [pltpu.VMEM((n_ext, bc), F32)] * 2,
        compiler_params=_params(2),
    )(up_g, up_g, up_g, up_v, up_v, up_v, da, da, conv_w, conv_w, conv_b, conv_b)
    return outs


def _adam_math(w, g, m, v):
    m2 = ADAM_B1 * m + (1.0 - ADAM_B1) * g
    v2 = ADAM_B2 * v + (1.0 - ADAM_B2) * (g * g)
    m_hat = m2 / (1.0 - ADAM_B1 ** ADAM_STEP)
    v_hat = v2 / (1.0 - ADAM_B2 ** ADAM_STEP)
    return -ADAM_LR * (m_hat / (jnp.sqrt(v_hat) + ADAM_EPS) + ADAM_WD * w), m2, v2


def _adamw_halves(w, mine, theirs, place, m, v, bt, name):
    R, C = w.shape
    h = R // 2
    bt = _fit_rows(h, bt)
    nh = h // bt

    def body(s_ref, w_ref, a_ref, b_ref, m_ref, v_ref, g_ref, d_ref, m2_ref, v2_ref):
        lower = pl.program_id(0) < nh
        gv = jnp.where(lower == (s_ref[0] == 0), a_ref[...], b_ref[...])
        g_ref[...] = gv
        d_ref[...], m2_ref[...], v2_ref[...] = _adam_math(w_ref[...], gv, m_ref[...], v_ref[...])

    full = pl.BlockSpec((bt, C), lambda i, s: (i, 0))
    half = pl.BlockSpec((bt, C), lambda i, s: (i % nh, 0))
    out = jax.ShapeDtypeStruct((R, C), F32)
    return pl.pallas_call(
        body, name=name,
        grid_spec=pltpu.PrefetchScalarGridSpec(num_scalar_prefetch=1, grid=(2 * nh,),
                                               in_specs=[full, half, half, full, full], out_specs=(full,) * 4),
        out_shape=(out,) * 4, compiler_params=_params(1),
    )(place, w, mine, theirs, m, v)


def _adamw(w, g, m, v, bt, name):
    R, C = w.shape
    bt = _fit_rows(R, bt)
    blk = pl.BlockSpec((bt, C), lambda i: (i, 0))

    def body(w_ref, g_ref, m_ref, v_ref, d_ref, m2_ref, v2_ref):
        gv = g_ref[...]
        m2 = ADAM_B1 * m_ref[...] + (1.0 - ADAM_B1) * gv
        v2 = ADAM_B2 * v_ref[...] + (1.0 - ADAM_B2) * (gv * gv)
        m_hat = m2 / (1.0 - ADAM_B1 ** ADAM_STEP)
        v_hat = v2 / (1.0 - ADAM_B2 ** ADAM_STEP)
        d_ref[...] = -ADAM_LR * (m_hat / (jnp.sqrt(v_hat) + ADAM_EPS) + ADAM_WD * w_ref[...])
        m2_ref[...] = m2
        v2_ref[...] = v2

    out = jax.ShapeDtypeStruct((R, C), F32)
    return pl.pallas_call(body, name=name, grid=(R // bt,), in_specs=[blk] * 4, out_specs=(blk,) * 3,
                          out_shape=(out,) * 3, compiler_params=_params(1))(w, g, m, v)


def _place():
    x, y, c = lax.axis_index("x"), lax.axis_index("y"), lax.axis_index("c")
    chips = [(1 - x, y), (x, 1 - y), (1 - x, 1 - y)]
    return x, y, c, chips


_HBM = pl.BlockSpec(memory_space=pltpu.HBM)


def _add_cores(buf, other, place, own_only, out_dtype, bt, name):
    n, _, h, cols = buf.shape
    bt = _fit_rows(h, bt)
    row = (lambda k, s: s[1]) if own_only else (lambda k, s: k)

    def body(s_ref, a_ref, b_ref, o_ref):
        o_ref[...] = (a_ref[...] + b_ref[...]).astype(out_dtype)

    return pl.pallas_call(
        body, name=name,
        grid_spec=pltpu.PrefetchScalarGridSpec(
            num_scalar_prefetch=1, grid=(1 if own_only else n, h // bt),
            in_specs=[pl.BlockSpec((None, None, bt, cols), lambda k, i, s: (row(k, s), s[0], i, 0)),
                      pl.BlockSpec((None, bt, cols), lambda k, i, s: (row(k, s), i, 0))],
            out_specs=(pl.BlockSpec((bt, cols), lambda k, i, s: (i, 0)) if own_only
                       else pl.BlockSpec((None, bt, cols), lambda k, i, s: (k, i, 0)))),
        out_shape=jax.ShapeDtypeStruct((h, cols) if own_only else (n, h, cols), out_dtype),
        compiler_params=_params(2),
    )(place, buf, other)


def _add_chips(own, others, bt, name):
    h, cols = own.shape
    bt = _fit_rows(h, bt)

    def body(a_ref, b_ref, o_ref):
        o_ref[...] = ((a_ref[...] + b_ref[0].astype(F32)) + b_ref[1].astype(F32)) + b_ref[2].astype(F32)

    return pl.pallas_call(
        body, name=name, grid=(h // bt,),
        in_specs=[pl.BlockSpec((bt, cols), lambda i: (i, 0)), pl.BlockSpec((3, bt, cols), lambda i: (0, i, 0))],
        out_specs=pl.BlockSpec((bt, cols), lambda i: (i, 0)),
        out_shape=jax.ShapeDtypeStruct((h, cols), F32), compiler_params=_params(1),
    )(own, others)


def _gather_plan(bufs, split):
    n = len(bufs)

    def phases(ins, outs, sems):
        send, recv = sems
        x, y, c, chips = _place()
        me = 2 * x + y

        def rows(b, core):
            h = bufs[b].shape[0] // 2
            return pl.ds(core * h, h) if split[b] else pl.ds(0, bufs[b].shape[0])

        def over_ici(b, j, block):
            px, py = chips[j]
            return pltpu.make_async_remote_copy(
                src_ref=ins[b].at[rows(b, c)], dst_ref=outs[b].at[block, rows(b, c)], send_sem=send.at[b, j],
                recv_sem=recv.at[b, j], device_id=(px, py, c), device_id_type=MESH)

        def over_d2d(b, j, block, core):
            return pltpu.make_async_remote_copy(
                src_ref=outs[b].at[block, rows(b, core)], dst_ref=outs[b].at[block, rows(b, core)],
                send_sem=send.at[b, 3 + j], recv_sem=recv.at[b, 3 + j], device_id=(x, y, 1 - c), device_id_type=MESH)

        pairs = [(b, j) for b in range(n) for j in range(3)]
        source = lambda j: 2 * chips[j][0] + chips[j][1]

        def first():
            for b, j in pairs:
                over_ici(b, j, me).start()

        def mid():
            for b, j in pairs:
                over_ici(b, j, source(j)).wait_recv()
                if split[b]:
                    over_d2d(b, j, source(j), c).start()

        def last():
            for b, j in pairs:
                if split[b]:
                    over_d2d(b, j, source(j), 1 - c).wait_recv()
            for b, j in pairs:
                over_ici(b, j, me).wait_send()
                if split[b]:
                    over_d2d(b, j, source(j), c).wait_send()

        return first, mid, last

    return ([jax.ShapeDtypeStruct((4,) + b.shape, b.dtype) for b in bufs],
            [pltpu.SemaphoreType.DMA((n, 6)), pltpu.SemaphoreType.DMA((n, 6))], phases)


def _exchange_plan(n, out_shapes, copy):
    def phases(ins, outs, sems):
        send, recv = sems
        place = _place()

        def first():
            for b in range(n):
                copy(b, ins, outs, send, recv, place).start()

        def last():
            for b in range(n):
                copy(b, ins, outs, send, recv, place).wait()

        return first, (lambda: None), last

    return out_shapes, [pltpu.SemaphoreType.DMA((n,)), pltpu.SemaphoreType.DMA((n,))], phases


def _swap_plan(bufs):
    def copy(b, ins, outs, send, recv, place):
        x, y, c, _ = place
        h = bufs[b].shape[1] // 2
        return pltpu.make_async_remote_copy(
            src_ref=ins[b].at[:, pl.ds((1 - c) * h, h)], dst_ref=outs[b], send_sem=send.at[b], recv_sem=recv.at[b],
            device_id=(x, y, 1 - c), device_id_type=MESH)

    shapes = [jax.ShapeDtypeStruct((b.shape[0], b.shape[1] // 2, b.shape[2]), b.dtype) for b in bufs]
    return _exchange_plan(len(bufs), shapes, copy)


def _scatter_plan(bufs):
    def copy(t, ins, outs, send, recv, place):
        x, y, c, chips = place
        b, j = divmod(t, 3)
        px, py = chips[j]
        return pltpu.make_async_remote_copy(
            src_ref=ins[b].at[2 * px + py], dst_ref=outs[b].at[j], send_sem=send.at[t], recv_sem=recv.at[t],
            device_id=(px, py, c), device_id_type=MESH)

    shapes = [jax.ShapeDtypeStruct((3,) + b.shape[1:], b.dtype) for b in bufs]
    return _exchange_plan(3 * len(bufs), shapes, copy)


def _join_plan(halves):
    def copy(b, ins, outs, send, recv, place):
        x, y, c, _ = place
        return pltpu.make_async_remote_copy(
            src_ref=ins[b], dst_ref=outs[b], send_sem=send.at[b], recv_sem=recv.at[b],
            device_id=(x, y, 1 - c), device_id_type=MESH)

    return _exchange_plan(len(halves), [jax.ShapeDtypeStruct(b.shape, b.dtype) for b in halves], copy)


def _run_plan(arrays, plan, name):
    out_shapes, sems, phases = plan
    n, m = len(arrays), len(out_shapes)

    def body(*refs):
        for phase in phases(refs[:n], refs[n:n + m], refs[n + m:]):
            phase()

    return pl.pallas_call(
        body, name=name, in_specs=[_HBM] * n, out_specs=[_HBM] * m, out_shape=out_shapes, scratch_shapes=sems,
        compiler_params=pltpu.CompilerParams(has_side_effects=True),
    )(*arrays)


def _fit_rows(n, target):
    for q in (2 * HALO, HALO):
        for t in range(min(n, target) // q * q, 0, -q):
            if n % t == 0:
                return t
    raise ValueError((n, target))


def _allreduce_small(buf, name):
    R, lanes = buf.shape

    def body(in_ref, out_ref, land, send, recv):
        x, y, c, _ = _place()
        me = 4 * x + 2 * y + c
        land[me] = in_ref[...]
        cps = []
        for r in range(1, 8):
            px, py, pc = x ^ (r >> 2), y ^ ((r >> 1) & 1), c ^ (r & 1)
            cp = pltpu.make_async_remote_copy(
                src_ref=in_ref, dst_ref=land.at[me], send_sem=send.at[r - 1], recv_sem=recv.at[me],
                device_id=(px, py, pc), device_id_type=MESH)
            cp.start()
            cps.append(cp)
        for r in range(1, 8):
            peer = 4 * (x ^ (r >> 2)) + 2 * (y ^ ((r >> 1) & 1)) + (c ^ (r & 1))
            pltpu.make_async_remote_copy(
                src_ref=in_ref, dst_ref=land.at[peer], send_sem=send.at[r - 1], recv_sem=recv.at[peer],
                device_id=(x, y, c), device_id_type=MESH).wait_recv()
        for cp in cps:
            cp.wait_send()
        acc = land[0]
        for d in range(1, 8):
            acc = acc + land[d]
        out_ref[...] = acc

    vm = pl.BlockSpec(memory_space=pltpu.VMEM)
    return pl.pallas_call(
        body, name=name, in_specs=[vm], out_specs=vm, out_shape=jax.ShapeDtypeStruct((R, lanes), buf.dtype),
        scratch_shapes=[pltpu.VMEM((8, R, lanes), buf.dtype), pltpu.SemaphoreType.DMA((7,)), pltpu.SemaphoreType.DMA((8,))],
        compiler_params=pltpu.CompilerParams(has_side_effects=True, vmem_limit_bytes=VMEM_LIMIT),
    )(buf)


ROW_BLOCK = 256
SB_BLOCK = 256
MM_TM, MM_TN, MM_TK = 1024, 512, 512
FFN_COLS = 512


def _lane_pad(vec, start):
    return jnp.pad(vec, ((0, 0), (start, LANES - start - vec.shape[1])))


WEIGHTS = ("w_in", "sb_out_gain", "dn_conv_w", "dn_a_log", "dn_dt_bias", "dn_out_gain", "w_out", "ln_mix_pre",
           "ln_mix_post", "w_up", "ffn_conv_w", "ffn_conv_b", "w_down", "ln_ffn_pre", "ln_ffn_post")
MATRICES = {"w_in": 1, "w_out": 0, "w_up": 1, "w_down": 0}
CONV_SHARDED = ("dn_conv_w", "ffn_conv_w")
SMALL = tuple(n for n in WEIGHTS if n not in MATRICES)
N_CHIPS = 4
ADAM_ROWS = 128


def _pack(arrs, quantum):
    rows, layout, off = [], [], 0
    for a in arrs:
        n = int(np.prod(a.shape))
        r = -(-n // LANES)
        r = -(-r // HALO) * HALO
        rows.append(jnp.pad(a.reshape(-1), (0, r * LANES - n)).reshape(r, LANES))
        layout.append((off, r, n, a.shape))
        off += r
    total = -(-off // quantum) * quantum
    if total > off:
        rows.append(jnp.zeros((total - off, LANES), rows[0].dtype))
    return jnp.concatenate(rows, axis=0), layout


def _unpack(packed, layout):
    return [packed[off:off + r].reshape(-1)[:n].reshape(shape) for off, r, n, shape in layout]


UP_TILE = 1408
PAIR_ROWS = 256


def _reduce_to_chips(shares, place, names, swap_on, scatter_on):
    from_sibling = swap_on(shares, _swap_plan(shares))
    halves = [s.reshape(N_CHIPS, 2, s.shape[1] // 2, s.shape[2]) for s in shares]
    to_chips = [_add_cores(hv, fs, place, False, BF16, PAIR_ROWS, "grad_add_cores_" + n)
                for hv, fs, n in zip(halves, from_sibling, names)]
    own = [_add_cores(hv, fs, place, True, F32, PAIR_ROWS, "grad_add_cores_own_" + n)
           for hv, fs, n in zip(halves, from_sibling, names)]
    return own, scatter_on(to_chips, _scatter_plan(to_chips))


def _step(x, target, wt, late, chip, place):
    T, D = x.shape
    W = D // 2
    H = W // HEAD_DIM
    bt = min(ROW_BLOCK, T)
    blk = min(SB_BLOCK, T)
    w_in = wt["w_in"]
    a_log, dt_bias = _lane_pad(wt["dn_a_log"], H), _lane_pad(wt["dn_dt_bias"], H)
    mm = functools.partial(_mm, tm=MM_TM, tn=MM_TN)
    wide = functools.partial(_mm, tm=MM_TM, tn=2 * MM_TN)
    mm_up = functools.partial(_mm, tm=MM_TM, tn=UP_TILE)
    one = lambda a, b, tk=MM_TK: [(a, b, tk, 0, 0)]

    xn = _rms_fwd(x, wt["ln_mix_pre"], None, BF16, bt, "rms_mix_pre")
    psb = mm(one(xn, w_in, D), "nn", BF16, name="proj_sb", n_window=(0, 3 * W))
    pdn = mm(one(xn, w_in, D), "nn", F32, name="proj_dn", n_window=(3 * W, 4 * W))
    pba = mm(one(xn, w_in, D), "nn", F32, name="proj_ba", n_window=(7 * W, LANES))
    late_names = ("w_out", "w_up", "w_down")
    gathered_with = lambda names: _Comm([late[n] for n in names], _gather_plan([late[n] for n in names], [True] * len(names)))
    own_block_in = lambda theirs, names: [lax.dynamic_update_index_in_dim(t, late[n], chip, 0) for t, n in zip(theirs, names)]
    (o_sb, mix_sb, lt, swept), theirs = _sb_fwd(psb, wt["sb_out_gain"], blk, "sb_fwd", gathered_with(("w_out", "w_down")))
    w_out, w_down = [w.reshape(-1, D) for w in own_block_in(theirs, ("w_out", "w_down"))]
    qn, kn, vv, bx, gx = _dn_pre_fwd(pdn, pba, wt["dn_conv_w"], a_log, dt_bias, bt, "dn_pre_fwd")
    (o_dn, mix_dn, ss, tms), theirs = _dn_core_fwd(qn, kn, vv, bx, gx, pdn, wt["dn_out_gain"], "dn_core_fwd",
                                                  gathered_with(("w_up",)))
    w_up, = own_block_in(theirs, ("w_up",))
    F = w_down.shape[0]
    m = wide([(mix_sb, w_out, 2 * MM_TK, 0, 0), (mix_dn, w_out, 2 * MM_TK, 0, W)], "nn", F32, name="out_proj")
    h = _rms_fwd(m, wt["ln_mix_post"], x, F32, bt, "rms_mix_post")
    hn = _rms_fwd(h, wt["ln_ffn_pre"], None, BF16, bt, "rms_ffn_pre")
    bc = min(FFN_COLS, F)
    up_g, up_v, act = _ffn_up_fused(hn, w_up, wt["ffn_conv_w"], wt["ffn_conv_b"], FFN_COLS // 2, "ffn_up")
    f = mm(one(act, w_down, UP_TILE), "nn", F32, name="ffn_down")
    dy, df, g_ffn_post, sq = _loss_head(f, wt["ln_ffn_post"], h, target, bt, "loss_head")
    loss = 0.5 * jnp.sum(sq) / D

    da = mm(one(df, w_down, D), "nt", F32, name="d_act")
    g_w_down = _mm(one(act, df, 2 * MM_TK), "tn", F32, tm=UP_TILE, tn=2 * MM_TN, name="g_w_down")
    dug, duv, dwg, dwv, dbg, dbv = _ffn_mid_bwd(up_g, up_v, wt["ffn_conv_w"], wt["ffn_conv_b"], da, bt, bc, "ffn_mid_bwd")
    dhn = wide([(dug, w_up, UP_TILE, 0, 0), (duv, w_up, UP_TILE, 0, F)], "nt", F32, name="d_hn")
    shard = w_up.shape[2]
    g_w_up = mm_up(one(hn, dug, 2 * MM_TK), "tn", F32, name="g_w_up_gate", out_shard=shard,
                   into=(lax.empty(w_up.shape, F32), 0))
    g_w_up = mm_up(one(hn, duv, 2 * MM_TK), "tn", F32, name="g_w_up_val", out_shard=shard, into=(g_w_up, F))
    dh, g_ffn_pre = _rms_bwd(h, wt["ln_ffn_pre"], dhn, dy, F32, bt, "rms_ffn_pre_bwd")
    dm, g_mix_post = _rms_bwd(m, wt["ln_mix_post"], dh, None, BF16, bt, "rms_mix_post_bwd")
    dmix = mm(one(dm, w_out, D), "nt", F32, name="d_mix")
    g_w_out = jnp.concatenate([wide(one(mix_sb, dm, 2 * MM_TK), "tn", F32, name="g_w_out_sb"),
                               wide(one(mix_dn, dm, 2 * MM_TK), "tn", F32, name="g_w_out_dn")], axis=0)
    shares = [g_w_out.reshape(N_CHIPS, -1, D), g_w_up, g_w_down.reshape(N_CHIPS, -1, D)]
    carried = {}

    def swap_on(arrays, plan):
        (carried["do_sb"], carried["g_sb_gain"]), out = _headnorm_bwd(o_sb, wt["sb_out_gain"], dmix, bt, "sb_norm_bwd",
                                                                    _Comm(arrays, plan))
        return out

    def scatter_on(arrays, plan):
        carried["dn"], out = _dn_core_bwd(qn, kn, vv, bx, gx, pdn, wt["dn_out_gain"], o_dn, dmix, ss, tms, W,
                                          "dn_core_bwd", _Comm(arrays, plan))
        return out

    early = _reduce_to_chips(shares, place, late_names, swap_on, scatter_on)
    g_sb_gain = carried["g_sb_gain"]
    (dq, dk, dv), _ = _sb_bwd(psb, carried["do_sb"], lt, swept, blk, "sb_bwd", _Comm())
    ddq, ddk, ddv, dbx, dgx, dz, g_dn_gain = carried["dn"]
    dconv, dba, g_dn_conv, g_a_log, g_dt_bias = _dn_pre_bwd(pdn, pba, wt["dn_conv_w"], a_log, dt_bias,
                                                            ddq, ddk, ddv, dbx, dgx, bt, "dn_pre_bwd")
    pieces = [(dq, 0), (dk, W), (dv, 2 * W), (dconv, 3 * W), (dz, 6 * W), (dba, 7 * W)]
    g_w_in = [wide(one(xn, d, 2 * MM_TK), "tn", F32, name=f"g_w_in_{i}") for i, (d, _) in enumerate(pieces)]
    g_w_in[-1] = g_w_in[-1][:, :2 * H]
    g_in = jnp.concatenate(g_w_in, axis=1)

    def with_d_xn(arrays, plan):
        carried["dxn"], out = mm([(d, w_in, 2 * MM_TK, 0, k0) for d, k0 in pieces], "nt", F32, name="d_xn",
                                 comm=_Comm(arrays, plan))
        return out

    last = _reduce_to_chips([g_in.reshape(D, N_CHIPS, -1).transpose(1, 0, 2)], place, ["w_in"],
                            lambda arrays, plan: _run_plan(arrays, plan, "grad_swap_cores"), with_d_xn)
    dx, g_mix_pre = _rms_bwd(x, wt["ln_mix_pre"], carried["dxn"], dh, F32, bt, "rms_mix_pre_bwd")
    exchanged = dict(zip(late_names, zip(*early)))
    exchanged["w_in"] = (last[0][0], last[1][0])

    grads = dict(
        sb_out_gain=g_sb_gain, dn_conv_w=g_dn_conv, dn_a_log=g_a_log[:, H:2 * H],
        dn_dt_bias=g_dt_bias[:, H:2 * H], dn_out_gain=jnp.sum(g_dn_gain, axis=0),
        ln_mix_pre=g_mix_pre, ln_mix_post=g_mix_post,
        ffn_conv_w=jnp.concatenate([dwg, dwv], axis=1), ffn_conv_b=jnp.concatenate([dbg, dbv], axis=1),
        ln_ffn_pre=g_ffn_pre, ln_ffn_post=g_ffn_post)
    return loss, dx, grads, exchanged


def kernel(x, w_in, sb_out_gain, dn_conv_w, dn_a_log, dn_dt_bias, dn_out_gain, w_out, ln_mix_pre, ln_mix_post, w_up, ffn_conv_w, ffn_conv_b, w_down, ln_ffn_pre, ln_ffn_post, loss_target, m_w_in, m_sb_out_gain, m_dn_conv_w, m_dn_a_log, m_dn_dt_bias, m_dn_out_gain, m_w_out, m_ln_mix_pre, m_ln_mix_post, m_w_up, m_ffn_conv_w, m_ffn_conv_b, m_w_down, m_ln_ffn_pre, m_ln_ffn_post, v_w_in, v_sb_out_gain, v_dn_conv_w, v_dn_a_log, v_dn_dt_bias, v_dn_out_gain, v_w_out, v_ln_mix_pre, v_ln_mix_post, v_w_up, v_ffn_conv_w, v_ffn_conv_b, v_w_down, v_ln_ffn_pre, v_ln_ffn_post):
    given = dict(locals())
    wl = {n: given[n][0] for n in WEIGHTS}
    ml = {n: given["m_" + n][0] for n in WEIGHTS}
    vl = {n: given["v_" + n][0] for n in WEIGHTS}
    for d in (wl, ml, vl):
        for n in SMALL:
            if d[n].ndim == 1:
                d[n] = d[n][None]
    cx, cy, cc = lax.axis_index("x"), lax.axis_index("y"), lax.axis_index("c")
    chip = 2 * cx + cy
    D = x.shape[2]
    W = D // 2

    first = ("w_in",) + CONV_SHARDED
    mine = [wl["w_in"].astype(BF16)] + [wl[n] for n in CONV_SHARDED]
    theirs = _run_plan(mine, _gather_plan(mine, [True, False, False]), "gather_w_in")
    got = {n: lax.dynamic_update_index_in_dim(t, s, chip, 0) for n, t, s in zip(first, theirs, mine)}
    columns = lambda g: g.transpose(1, 0, 2).reshape(g.shape[1], N_CHIPS * g.shape[2])
    wt = {n: wl[n] for n in SMALL}
    w_in_all = columns(got["w_in"])
    wt["w_in"] = jnp.pad(w_in_all, ((0, 0), (0, 7 * W + LANES - w_in_all.shape[1])))
    for n in CONV_SHARDED:
        wt[n] = columns(got[n])
    late = {n: wl[n].astype(BF16) for n in ("w_out", "w_up", "w_down")}

    place = jnp.stack([cc, chip]).astype(jnp.int32)
    loss, dx, grads, exchanged = _step(x[0], loss_target[0], wt, late, chip, place)
    loss = lax.psum(loss, ("x", "y", "c"))

    names = list(MATRICES)
    reduced = [_add_chips(*exchanged[n], PAIR_ROWS, "grad_add_chips_" + n) for n in names]
    siblings = _run_plan(reduced, _join_plan(reduced), "grad_join_cores")
    gl = {}

    small, small_layout = _pack([grads[n] for n in SMALL], HALO)
    small = _allreduce_small(small, "grad_allreduce_small")
    for n, g in zip(SMALL, _unpack(small, small_layout)):
        if n in CONV_SHARDED:
            size = g.shape[1] // N_CHIPS
            g = lax.dynamic_slice_in_dim(g, chip * size, size, axis=1)
        gl[n] = g

    delta, new_m, new_v = {}, {}, {}
    for n, mine_half, sibling_half in zip(names, reduced, siblings):
        gl[n], delta[n], new_m[n], new_v[n] = _adamw_halves(wl[n], mine_half, sibling_half, place, ml[n], vl[n],
                                                            ADAM_ROWS, "adamw_" + n)
    packs = [_pack([d[n] for n in SMALL], HALO) for d in (wl, gl, ml, vl)]
    outs = _adamw(*[p[0] for p in packs], ADAM_ROWS, "adamw_small")
    for res, o in zip((delta, new_m, new_v), outs):
        res.update(zip(SMALL, _unpack(o, packs[0][1])))

    shaped = lambda d: [d[n].reshape(given[n].shape) for n in WEIGHTS]
    return (loss, dx[None], *shaped(gl), *shaped(delta), *shaped(new_m), *shaped(new_v))
```

```python
import functools

import numpy as np
import jax
import jax.numpy as jnp
from jax import lax
from jax.experimental import pallas as pl
from jax.experimental.pallas import tpu as pltpu

F32 = jnp.float32
BF16 = jnp.bfloat16
HEAD_DIM = 128
CHUNK = 64
ROWS = 4 * CHUNK
N_CHUNKS = ROWS // CHUNK
EPS = 1e-6
EXP_UNDERFLOW = 110.0
LANES = 128
HALO = 8
VMEM_LIMIT = 48 * 1024 * 1024
ADAM_LR, ADAM_B1, ADAM_B2, ADAM_EPS, ADAM_WD, ADAM_STEP = 0.001, 0.9, 0.999, 1e-08, 0.01, 10
MESH = pl.DeviceIdType.MESH

NN = (((1,), (0,)), ((), ()))
NT = (((1,), (1,)), ((), ()))
TN = (((0,), (0,)), ((), ()))


def _params(n_axes):
    return pltpu.CompilerParams(dimension_semantics=("arbitrary",) * n_axes, vmem_limit_bytes=VMEM_LIMIT)


def _bdot(a, b, dims=NN):
    return lax.dot_general(a.astype(BF16), b.astype(BF16), dims, preferred_element_type=F32)


def _split3(a):
    hi = a.astype(BF16)
    r1 = a - hi.astype(F32)
    mid = r1.astype(BF16)
    lo = (r1 - mid.astype(F32)).astype(BF16)
    return hi, mid, lo


def _dot3(a, sel, dims=NN):
    return sum(lax.dot_general(p, sel, dims, preferred_element_type=F32) for p in _split3(a))


def _dot3r(sel, a, dims=NN):
    return sum(lax.dot_general(sel, p, dims, preferred_element_type=F32) for p in _split3(a))


def _iota2(n, m):
    return lax.broadcasted_iota(jnp.int32, (n, m), 0), lax.broadcasted_iota(jnp.int32, (n, m), 1)


def _sigmoid(x):
    return 1.0 / (1.0 + jnp.exp(-x))


def _softplus(x):
    return jnp.maximum(x, 0.0) + jnp.log(1.0 + jnp.exp(-jnp.abs(x)))


def _fit(values, target):
    values = [v for v in (values if isinstance(values, (list, tuple)) else [values]) if v]
    best = None
    for t in range(LANES, min(min(values), target) + 1, LANES):
        if all(v % t == 0 for v in values):
            best = t
    assert best is not None, (values, target)
    return best


def _mm(parts, mode, out_dtype, tm, tn, name, n_window=None, out_shard=None, into=None, comm=None):
    dims = {"nn": NN, "nt": NT, "tn": TN}[mode]
    a0, b0 = parts[0][0], parts[0][1]
    b3 = b0.ndim == 3
    shard_c = b0.shape[2] if b3 else None
    M = a0.shape[1] if mode == "tn" else a0.shape[0]
    if mode == "nt":
        n_full = b0.shape[1] if b3 else b0.shape[0]
    else:
        n_full = b0.shape[0] * b0.shape[2] if b3 else b0.shape[1]
    n0, N = n_window if n_window is not None else (0, n_full)
    out_n0 = into[1] if into is not None else 0
    tm = _fit(M, tm)
    tn = _fit([N, n0, out_n0, out_shard, shard_c if mode != "nt" else None], tn)
    specs_a, specs_b, offs, nks = [], [], [], []
    off = 0
    for a, b, tk, a_k0, b_k0 in parts:
        K = a.shape[0] if mode == "tn" else a.shape[1]
        tk = _fit([K, a_k0, b_k0, shard_c if mode == "nt" else None], tk)
        nk = K // tk
        kk = lambda k, o=off, n=nk: jnp.clip(k - o, 0, n - 1)
        ao, bo, no = a_k0 // tk, b_k0 // tk, n0 // tn
        if mode == "tn":
            specs_a.append(pl.BlockSpec((tk, tm), lambda i, j, k, kk=kk, ao=ao: (kk(k) + ao, i)))
        else:
            specs_a.append(pl.BlockSpec((tm, tk), lambda i, j, k, kk=kk, ao=ao: (i, kk(k) + ao)))
        if mode == "nt":
            if b3:
                per = shard_c // tk
                specs_b.append(pl.BlockSpec((None, tn, tk), lambda i, j, k, kk=kk, bo=bo, per=per:
                                            ((kk(k) + bo) // per, j, (kk(k) + bo) % per)))
            else:
                specs_b.append(pl.BlockSpec((tn, tk), lambda i, j, k, kk=kk, bo=bo: (j, kk(k) + bo)))
        else:
            if b3:
                per = shard_c // tn
                specs_b.append(pl.BlockSpec((None, tk, tn), lambda i, j, k, kk=kk, bo=bo, no=no, per=per:
                                            ((j + no) // per, kk(k) + bo, (j + no) % per)))
            else:
                specs_b.append(pl.BlockSpec((tk, tn), lambda i, j, k, kk=kk, bo=bo, no=no: (kk(k) + bo, j + no)))
        offs.append(off)
        nks.append(nk)
        off += nk
    nk_total = off
    n_parts = len(parts)

    comm = comm if comm is not None else _Comm()
    grid = (M // tm, N // tn, nk_total)
    n_in = 2 * n_parts + (1 if into is not None else 0)

    def body(*refs):
        ins, (o_ref,), scratch, (first, mid, last) = comm.split(refs, n_in, 1, 0 if nk_total == 1 else 1)
        a_refs, b_refs = ins[:n_parts], ins[n_parts:2 * n_parts]
        at = lambda step: functools.reduce(lambda x, y: x & y, [pl.program_id(d) == step[d] for d in range(3)])
        pl.when(at((0, 0, 0)))(first)
        pl.when(at((grid[0] // 2, 0, 0)))(mid)
        if nk_total == 1:
            o_ref[...] = _bdot(a_refs[0][...], b_refs[0][...], dims).astype(out_dtype)
        else:
            acc = scratch[0]
            k = pl.program_id(2)

            @pl.when(k == 0)
            def _():
                acc[...] = jnp.zeros_like(acc)

            for p in range(n_parts):
                @pl.when((k >= offs[p]) & (k < offs[p] + nks[p]))
                def _(p=p):
                    acc[...] += _bdot(a_refs[p][...], b_refs[p][...], dims)

            @pl.when(k == nk_total - 1)
            def _():
                o_ref[...] = acc[...].astype(out_dtype)
        pl.when(at(tuple(g - 1 for g in grid)))(last)

    jo = out_n0 // tn
    if out_shard is not None:
        per_o = out_shard // tn
        out_spec = pl.BlockSpec((None, tm, tn), lambda i, j, k: ((j + jo) // per_o, i, (j + jo) % per_o))
        out_shape = jax.ShapeDtypeStruct((N // out_shard, M, out_shard), out_dtype)
    else:
        out_spec = pl.BlockSpec((tm, tn), lambda i, j, k: (i, j + jo))
        out_shape = jax.ShapeDtypeStruct((M, N), out_dtype)
    ins = [p[0] for p in parts] + [p[1] for p in parts]
    in_specs = specs_a + specs_b
    aliases = {}
    if into is not None:
        out_shape = jax.ShapeDtypeStruct(into[0].shape, into[0].dtype)
        aliases = {len(ins): 0}
        ins.append(into[0])
        in_specs.append(pl.BlockSpec(memory_space=pl.ANY))
    (out,), carried = comm.call(body, name, grid, in_specs, (out_spec,), (out_shape,),
                                [] if nk_total == 1 else [pltpu.VMEM((tm, tn), F32)], ins, aliases)
    return (out, carried) if comm.phases is not None else out


def _rms_fwd(x, gain, resid, out_dtype, bt, name):
    T, D = x.shape
    row = pl.BlockSpec((bt, D), lambda i: (i, 0))
    vec = pl.BlockSpec((1, D), lambda i: (0, 0))

    def body(*refs):
        x_ref, g_ref = refs[0], refs[1]
        o_ref = refs[-1]
        xv = x_ref[...]
        y = xv * lax.rsqrt(jnp.mean(xv * xv, axis=-1, keepdims=True) + EPS) * g_ref[...]
        if resid is not None:
            y = refs[2][...] + y
        o_ref[...] = y.astype(out_dtype)

    ins = [x, gain] + ([resid] if resid is not None else [])
    return pl.pallas_call(
        body, name=name, grid=(T // bt,),
        in_specs=[row, vec] + ([row] if resid is not None else []),
        out_specs=row, out_shape=jax.ShapeDtypeStruct((T, D), out_dtype), compiler_params=_params(1),
    )(*ins)


def _rms_bwd_math(xv, g, dy):
    r = lax.rsqrt(jnp.mean(xv * xv, axis=-1, keepdims=True) + EPS)
    n = xv * r
    gy = dy * g
    dx = r * (gy - n * jnp.mean(gy * n, axis=-1, keepdims=True))
    return dx, dy * n


def _rms_bwd(x, gain, dy, resid, out_dtype, bt, name):
    T, D = x.shape
    row = pl.BlockSpec((bt, D), lambda i: (i, 0))
    vec = pl.BlockSpec((1, D), lambda i: (0, 0))

    def body(*refs):
        x_ref, g_ref, dy_ref = refs[0], refs[1], refs[2]
        dx_ref, dg_ref = refs[-2], refs[-1]
        dx, dgp = _rms_bwd_math(x_ref[...], g_ref[...], dy_ref[...].astype(F32))
        if resid is not None:
            dx = refs[3][...] + dx
        dx_ref[...] = dx.astype(out_dtype)

        @pl.when(pl.program_id(0) == 0)
        def _():
            dg_ref[...] = jnp.zeros_like(dg_ref)

        dg_ref[...] += jnp.sum(dgp, axis=0, keepdims=True)

    ins = [x, gain, dy] + ([resid] if resid is not None else [])
    return pl.pallas_call(
        body, name=name, grid=(T // bt,),
        in_specs=[row, vec, row] + ([row] if resid is not None else []),
        out_specs=(row, vec),
        out_shape=(jax.ShapeDtypeStruct((T, D), out_dtype), jax.ShapeDtypeStruct((1, D), F32)),
        compiler_params=_params(1),
    )(*ins)


def _loss_head(f, gain, h, target, bt, name):
    T, D = f.shape
    row = pl.BlockSpec((bt, D), lambda i: (i, 0))
    vec = pl.BlockSpec((1, D), lambda i: (0, 0))

    def body(f_ref, g_ref, h_ref, t_ref, dy_ref, df_ref, dg_ref, sq_ref):
        fv, g = f_ref[...], g_ref[...]
        r = lax.rsqrt(jnp.mean(fv * fv, axis=-1, keepdims=True) + EPS)
        n = fv * r
        err = (h_ref[...] + n * g) - t_ref[...]
        dy = err * (1.0 / D)
        gy = dy * g
        df = r * (gy - n * jnp.mean(gy * n, axis=-1, keepdims=True))
        dy_ref[...] = dy
        df_ref[...] = df.astype(BF16)

        @pl.when(pl.program_id(0) == 0)
        def _():
            dg_ref[...] = jnp.zeros_like(dg_ref)
            sq_ref[...] = jnp.zeros_like(sq_ref)

        dg_ref[...] += jnp.sum(dy * n, axis=0, keepdims=True)
        sq_ref[...] += jnp.sum(err * err, axis=0, keepdims=True)

    return pl.pallas_call(
        body, name=name, grid=(T // bt,), in_specs=[row, vec, row, row], out_specs=(row, row, vec, vec),
        out_shape=(jax.ShapeDtypeStruct((T, D), F32), jax.ShapeDtypeStruct((T, D), BF16),
                   jax.ShapeDtypeStruct((1, D), F32), jax.ShapeDtypeStruct((1, D), F32)),
        compiler_params=_params(1),
    )(f, gain, h, target)


def _sb_logits(q, k, valid):
    z = lax.dot_general(q, k, NT, preferred_element_type=F32) * (HEAD_DIM ** -0.5)
    sp = jnp.log(1.0 + jnp.exp(-jnp.abs(z)))
    lb = jnp.minimum(z, 0.0) - sp
    l1 = -(jnp.maximum(z, 0.0) + sp)
    return lb, (l1 if valid is None else jnp.where(valid, l1, 0.0))


def _masked(valid, x):
    return x if valid is None else jnp.where(valid, x, 0.0)


def _heads_per_step(n_heads):
    return 2 if n_heads % 2 == 0 else 1


def _dot2(a, sel):
    hi = a.astype(BF16)
    lo = (a - hi.astype(F32)).astype(BF16)
    return jnp.dot(hi, sel, preferred_element_type=F32) + jnp.dot(lo, sel, preferred_element_type=F32)


class _Comm:
    def __init__(self, arrays=(), plan=((), (), None)):
        self.arrays = list(arrays)
        self.out_shapes, self.sems, self.phases = list(plan[0]), list(plan[1]), plan[2]

    def split(self, refs, n_in, n_out, n_scratch):
        a, o = len(self.arrays), len(self.out_shapes)
        cuts = np.cumsum([0, n_in, a, n_out, o, n_scratch])
        ins, cin, outs, cout, scratch = (refs[cuts[t]:cuts[t + 1]] for t in range(5))
        if self.phases is None:
            return ins, outs, scratch, (lambda: None,) * 3
        return ins, outs, scratch, self.phases(cin, cout, refs[cuts[5]:])

    def call(self, body, name, grid, in_specs, out_specs, out_shape, scratch_shapes, operands, aliases=None):
        outs = pl.pallas_call(
            body, name=name, grid=grid, in_specs=list(in_specs) + [_HBM] * len(self.arrays),
            out_specs=tuple(out_specs) + (_HBM,) * len(self.out_shapes),
            out_shape=tuple(out_shape) + tuple(self.out_shapes),
            scratch_shapes=list(scratch_shapes) + self.sems, input_output_aliases=aliases or {},
            compiler_params=pltpu.CompilerParams(dimension_semantics=("arbitrary",) * len(grid),
                                                 vmem_limit_bytes=VMEM_LIMIT, has_side_effects=self.phases is not None),
        )(*operands, *self.arrays)
        return outs[:len(out_shape)], outs[len(out_shape):]


def _sb_fwd(qkv, gain, blk, name, comm):
    T, W = qkv.shape[0], qkv.shape[1] // 3
    H = W // HEAD_DIM
    nq = T // blk
    hp = _heads_per_step(H)
    ng = H // hp
    lanes = [slice(t * HEAD_DIM, (t + 1) * HEAD_DIM) for t in range(hp)]

    def body(*refs):
        (q_ref, k_ref, v_ref, g_ref), (o_ref, mix_ref, lt_ref, swept_ref), _, (first, mid, last) = comm.split(refs, 4, 4, 0)
        h, i = pl.program_id(0), pl.program_id(1)
        pl.when((h == 0) & (i == 0))(first)
        pl.when((h == 3 * ng // 4) & (i == 0))(mid)
        q = [q_ref[:, ln] for ln in lanes]
        row, col = _iota2(blk, blk)
        after =(row > col).astype(BF16)

        def step(kb, carry, valid):
            ks = pl.ds(pl.multiple_of(kb * blk, blk), blk)
            out = []
            for t, (run, acc) in enumerate(carry):
                lb, l1 = _sb_logits(q[t], k_ref[ks, lanes[t]], valid)
                att = _masked(valid, jnp.exp(lb + _dot2(l1, after) + run))
                out.append((run + jnp.sum(l1, axis=1, keepdims=True), acc + _bdot(att, v_ref[ks, lanes[t]])))
            return tuple(out)

        zero = (jnp.zeros((blk, 1), F32), jnp.zeros((blk, HEAD_DIM), F32))
        def alive(state):
            jj, c = state
            return (jj < i) & (functools.reduce(jnp.maximum, [jnp.max(run) for run, _ in c]) > -EXP_UNDERFLOW)

        swept, carry = lax.while_loop(alive, lambda st: (st[0] + 1, step(i - 1 - st[0], st[1], None)),
                                      (jnp.int32(0), step(i, (zero,) * hp, col < row)))
        swept_ref[h, i] = swept
        for t, (run, o) in enumerate(carry):
            o_ref[:, lanes[t]] = o
            r = lax.rsqrt(jnp.mean(o * o, axis=-1, keepdims=True) + EPS)
            mix_ref[:, lanes[t]] = (o * r * g_ref[...]).astype(BF16)
            lt_ref[:, lanes[t]] = jnp.broadcast_to(run, (blk, HEAD_DIM))
        pl.when((h == ng - 1) & (i == nq - 1))(last)

    wide = hp * HEAD_DIM
    qb = pl.BlockSpec((blk, wide), lambda h, i: (i, h))
    return comm.call(
        body, name, (ng, nq),
        [qb, pl.BlockSpec((T, wide), lambda h, i: (0, ng + h)),
         pl.BlockSpec((T, wide), lambda h, i: (0, 2 * ng + h)), pl.BlockSpec((1, HEAD_DIM), lambda h, i: (0, 0))],
        (qb, qb, qb, pl.BlockSpec(memory_space=pltpu.SMEM)),
        (jax.ShapeDtypeStruct((T, W), F32), jax.ShapeDtypeStruct((T, W), BF16), jax.ShapeDtypeStruct((T, W), F32),
         jax.ShapeDtypeStruct((ng, nq), jnp.int32)),
        [], (qkv, qkv, qkv, gain))


def _headnorm_bwd(o, gain, dmix, bt, name, comm):
    T, W = o.shape
    H = W // HEAD_DIM
    nt = T // bt
    blk = pl.BlockSpec((bt, HEAD_DIM), lambda i, h: (i, h))
    vec = pl.BlockSpec((1, HEAD_DIM), lambda i, h: (0, 0))

    def body(*refs):
        (o_ref, g_ref, d_ref), (do_ref, dg_ref), _, (first, mid, last) = comm.split(refs, 3, 2, 0)
        i, h = pl.program_id(0), pl.program_id(1)
        pl.when((i == 0) & (h == 0))(first)
        pl.when((i == nt // 2) & (h == 0))(mid)
        do, dgp = _rms_bwd_math(o_ref[...], g_ref[...], d_ref[...])
        do_ref[...] = do

        @pl.when((i == 0) & (h == 0))
        def _():
            dg_ref[...] = jnp.zeros_like(dg_ref)

        dg_ref[...] += jnp.sum(dgp, axis=0, keepdims=True)
        pl.when((i == nt - 1) & (h == H - 1))(last)

    return comm.call(body, name, (nt, H), [blk, vec, blk], (blk, vec),
                     (jax.ShapeDtypeStruct((T, W), F32), jax.ShapeDtypeStruct((1, HEAD_DIM), F32)), [], (o, gain, dmix))


def _sb_bwd(qkv, do, lt, swept, blk, name, comm):
    T, W = qkv.shape[0], qkv.shape[1] // 3
    H = W // HEAD_DIM
    nq = T // blk
    scale = HEAD_DIM ** -0.5
    hp = _heads_per_step(H)
    ng = H // hp
    lanes = [slice(t * HEAD_DIM, (t + 1) * HEAD_DIM) for t in range(hp)]

    def body(*refs):
        ((q_ref, k_ref, v_ref, do_ref, lt_ref, swept_ref), (dq_ref, dk_out, dv_out), (dk_ref, dv_ref),
         (first, mid, last)) = comm.split(refs, 6, 3, 2)
        h, i = pl.program_id(0), pl.program_id(1)
        pl.when((h == 0) & (i == 0))(first)
        pl.when((h == ng // 2) & (i == 0))(mid)

        @pl.when(i == 0)
        def _():
            dk_ref[...] = jnp.zeros_like(dk_ref)
            dv_ref[...] = jnp.zeros_like(dv_ref)

        q = [q_ref[:, ln] for ln in lanes]
        dob = [do_ref[:, ln].astype(BF16) for ln in lanes]
        total = [lt_ref[:, ln][:, :1] for ln in lanes]
        row, col = _iota2(blk, blk)
        upto = (row <= col).astype(BF16)
        before = (row < col).astype(BF16)

        def step(kb, carry, valid):
            ks = pl.ds(pl.multiple_of(kb * blk, blk), blk)
            out = []
            for t, (seen, psum, dq) in enumerate(carry):
                k, v = k_ref[ks, lanes[t]], v_ref[ks, lanes[t]]
                lb, l1 = _sb_logits(q[t], k, valid)
                later = total[t] - seen - _dot2(l1, upto)
                att = _masked(valid, jnp.exp(lb + later))
                p = att * lax.dot_general(dob[t], v, NT, preferred_element_type=F32)
                c = psum + _dot2(p, before)
                sig = jnp.exp(lb)
                dz = (_masked(valid, p * (1.0 - sig) - c * sig) * scale).astype(BF16)
                dq = dq + jnp.dot(dz, k, preferred_element_type=F32)
                dk_ref[ks, lanes[t]] += lax.dot_general(dz, q[t], TN, preferred_element_type=F32)
                dv_ref[ks, lanes[t]] += lax.dot_general(att.astype(BF16), dob[t], TN, preferred_element_type=F32)
                out.append((seen + jnp.sum(l1, axis=1, keepdims=True), psum + jnp.sum(p, axis=1, keepdims=True), dq))
            return tuple(out)

        zero = jnp.zeros((blk, 1), F32)
        start = ((zero, zero, jnp.zeros((blk, HEAD_DIM), F32)),) * hp
        carry = step(i, lax.fori_loop(i - swept_ref[h, i], i, lambda kb, c: step(kb, c, None), start), col < row)
        for t in range(hp):
            dq_ref[:, lanes[t]] = carry[t][2].astype(BF16)

        @pl.when(i == nq - 1)
        def _():
            dk_out[...] = dk_ref[...].astype(BF16)
            dv_out[...] = dv_ref[...].astype(BF16)

        pl.when((h == ng - 1) & (i == nq - 1))(last)

    wide = hp * HEAD_DIM
    qb = pl.BlockSpec((blk, wide), lambda h, i: (i, h))
    head = pl.BlockSpec((T, wide), lambda h, i: (0, h))
    out = jax.ShapeDtypeStruct((T, W), BF16)
    return comm.call(
        body, name, (ng, nq),
        [qb, pl.BlockSpec((T, wide), lambda h, i: (0, ng + h)),
         pl.BlockSpec((T, wide), lambda h, i: (0, 2 * ng + h)), qb, qb, pl.BlockSpec(memory_space=pltpu.SMEM)],
        (qb, head, head), (out, out, out), [pltpu.VMEM((T, wide), F32)] * 2, (qkv, qkv, qkv, do, lt, swept))


def _expanders(H):
    lane = np.arange(H * HEAD_DIM) // HEAD_DIM
    eb = np.zeros((LANES, H * HEAD_DIM), np.float32)
    eg = np.zeros((LANES, H * HEAD_DIM), np.float32)
    eb[lane, np.arange(H * HEAD_DIM)] = 1.0
    eg[H + lane, np.arange(H * HEAD_DIM)] = 1.0
    sb = np.zeros((H * HEAD_DIM, LANES), np.float32)
    sg = np.zeros((H * HEAD_DIM, LANES), np.float32)
    sb[np.arange(H) * HEAD_DIM, np.arange(H)] = 1.0
    sg[np.arange(H) * HEAD_DIM, H + np.arange(H)] = 1.0
    return [jnp.asarray(m, BF16) for m in (eb, eg, sb, sg)]


def _conv_taps(ext_ref, w, n_out, lead):
    K = w.shape[0]
    out = None
    for j in range(K):
        term = ext_ref[pl.ds(lead - (K - 1) + j, n_out), :] * w[j:j + 1, :]
        out = term if out is None else out + term
    return out


STRIP_ROWS = 64


def _strips(n_rows):
    return [(r0, min(STRIP_ROWS, n_rows - r0)) for r0 in range(0, n_rows, STRIP_ROWS)]


def _l2_heads(s, H, fn):
    return jnp.concatenate([fn(s[:, h * HEAD_DIM:(h + 1) * HEAD_DIM]) for h in range(H)], axis=1)


def _dn_pre_fwd(pdn, pba, conv_w, a_log, dt_bias, bt, name):
    T, W = pdn.shape[0], pdn.shape[1] // 4
    H = W // HEAD_DIM
    eb, eg, _, _ = _expanders(H)
    nb = T // bt

    def body(x_ref, prev_ref, ba_ref, w_ref, al_ref, dt_ref, eb_ref, eg_ref,
             q_ref, k_ref, v_ref, bx_ref, gx_ref, ext):
        i = pl.program_id(0)
        ext[pl.ds(0, HALO), :] = jnp.where(i > 0, prev_ref[...], 0.0)
        ext[pl.ds(HALO, bt), :] = x_ref[...]
        c = _conv_taps(ext, w_ref[...], bt, HALO)
        s = c * _sigmoid(c)
        q_ref[...] = _l2_heads(s[:, :W], H, lambda t: t * lax.rsqrt(jnp.sum(t * t, axis=-1, keepdims=True) + EPS)
                               * (HEAD_DIM ** -0.5))
        k_ref[...] = _l2_heads(s[:, W:2 * W], H, lambda t: t * lax.rsqrt(jnp.sum(t * t, axis=-1, keepdims=True) + EPS))
        v_ref[...] = s[:, 2 * W:]
        ba = ba_ref[...]
        beta = _sigmoid(ba)
        graw = -jnp.exp(al_ref[...]) * _softplus(ba + dt_ref[...])
        row, col = _iota2(bt, bt)
        tri = ((row // CHUNK == col // CHUNK) & (row >= col)).astype(BF16)
        gcum = _dot3r(tri, graw)
        bx_ref[...] = _dot3(beta, eb_ref[...])
        gx_ref[...] = _dot3(gcum, eg_ref[...])

    C = 3 * W
    rowb = lambda w: pl.BlockSpec((bt, w), lambda i: (i, 0))
    full = lambda a: pl.BlockSpec(a.shape, lambda i: (0,) * a.ndim)
    out = jax.ShapeDtypeStruct((T, W), F32)
    return pl.pallas_call(
        body, name=name, grid=(nb,),
        in_specs=[rowb(C), pl.BlockSpec((HALO, C), lambda i: (jnp.maximum(i * (bt // HALO) - 1, 0), 0)),
                  rowb(LANES), full(conv_w), full(a_log), full(dt_bias), full(eb), full(eg)],
        out_specs=(rowb(W),) * 5, out_shape=(out,) * 5,
        scratch_shapes=[pltpu.VMEM((bt + HALO, C), F32)], compiler_params=_params(1),
    )(pdn, pdn, pba, conv_w, a_log, dt_bias, eb, eg)


def _dn_pre_bwd(pdn, pba, conv_w, a_log, dt_bias, dq, dk, dv, dbx, dgx, bt, name):
    T, W = pdn.shape[0], pdn.shape[1] // 4
    H = W // HEAD_DIM
    C = 3 * W
    K = conv_w.shape[0]
    _, _, sb, sg = _expanders(H)
    nb = T // bt
    n_ext = bt + HALO

    def body(x_ref, prev_ref, next_ref, ba_ref, w_ref, al_ref, dt_ref, sb_ref, sg_ref,
             dq_ref, dqn_ref, dk_ref, dkn_ref, dv_ref, dvn_ref, dbx_ref, dgx_ref,
             dx_ref, dba_ref, dw_ref, dal_ref, ddt_ref, ext, dext, dcext):
        i = pl.program_id(0)
        last = i == nb - 1
        ext[pl.ds(0, HALO), :] = jnp.where(i > 0, prev_ref[...], 0.0)
        ext[pl.ds(HALO, bt), :] = x_ref[...]
        ext[pl.ds(HALO + bt, HALO), :] = jnp.where(last, 0.0, next_ref[...])
        dext[pl.ds(0, bt), pl.ds(0, W)] = dq_ref[...]
        dext[pl.ds(0, bt), pl.ds(W, W)] = dk_ref[...]
        dext[pl.ds(0, bt), pl.ds(2 * W, W)] = dv_ref[...]
        dext[pl.ds(bt, HALO), pl.ds(0, W)] = jnp.where(last, 0.0, dqn_ref[...])
        dext[pl.ds(bt, HALO), pl.ds(W, W)] = jnp.where(last, 0.0, dkn_ref[...])
        dext[pl.ds(bt, HALO), pl.ds(2 * W, W)] = jnp.where(last, 0.0, dvn_ref[...])
        w = w_ref[...]
        l2_scale = (HEAD_DIM ** -0.5, 1.0, None)

        @pl.when(i == 0)
        def _():
            dw_ref[...] = jnp.zeros_like(dw_ref)
            dal_ref[...] = jnp.zeros_like(dal_ref)
            ddt_ref[...] = jnp.zeros_like(ddt_ref)

        for g in range(C // HEAD_DIM):
            cols = pl.ds(g * HEAD_DIM, HEAD_DIM)
            wg = w[:, g * HEAD_DIM:(g + 1) * HEAD_DIM]
            scale = l2_scale[g // H]
            for r0, rows in _strips(bt) + [(bt, HALO)]:
                c = None
                for j in range(K):
                    term = ext[pl.ds(HALO + r0 - (K - 1) + j, rows), cols] * wg[j:j + 1, :]
                    c = term if c is None else c + term
                sg_c = _sigmoid(c)
                ds = dext[pl.ds(r0, rows), cols]
                if scale is not None:
                    s = c * sg_c
                    r = lax.rsqrt(jnp.sum(s * s, axis=-1, keepdims=True) + EPS)
                    ds = scale * r * (ds - s * (r * r) * jnp.sum(s * ds, axis=-1, keepdims=True))
                dcext[pl.ds(r0, rows), cols] = ds * (sg_c * (1.0 + c * (1.0 - sg_c)))
            dw = [0.0] * K
            for r0, rows in _strips(bt):
                dx = None
                for j in range(K):
                    term = dcext[pl.ds(r0 + K - 1 - j, rows), cols] * wg[j:j + 1, :]
                    dx = term if dx is None else dx + term
                dx_ref[pl.ds(r0, rows), cols] = dx.astype(BF16)
                dcur = dcext[pl.ds(r0, rows), cols]
                for j in range(K):
                    dw[j] = dw[j] + jnp.sum(dcur * ext[pl.ds(HALO + r0 - (K - 1) + j, rows), cols], axis=0, keepdims=True)
            dw_ref[:, cols] += jnp.concatenate(dw, axis=0)

        ba = ba_ref[...]
        beta = _sigmoid(ba)
        al, dtb = al_ref[...], dt_ref[...]
        dbeta = _dot3(dbx_ref[...], sb_ref[...])
        dg = _dot3(dgx_ref[...], sg_ref[...])
        sp = _softplus(ba + dtb)
        da = dg * (-jnp.exp(al)) * _sigmoid(ba + dtb)
        dba_ref[...] = dbeta * beta * (1.0 - beta) + da
        dal_ref[...] += jnp.sum(dg * (-jnp.exp(al)) * sp, axis=0, keepdims=True)
        ddt_ref[...] += jnp.sum(da, axis=0, keepdims=True)

    rowb = lambda w: pl.BlockSpec((bt, w), lambda i: (i, 0))
    full = lambda a: pl.BlockSpec(a.shape, lambda i: (0,) * a.ndim)
    nxt = lambda w: pl.BlockSpec((HALO, w), lambda i: (jnp.minimum((i + 1) * (bt // HALO), T // HALO - 1), 0))
    vec = pl.BlockSpec((1, LANES), lambda i: (0, 0))
    return pl.pallas_call(
        body, name=name, grid=(nb,),
        in_specs=[rowb(C), pl.BlockSpec((HALO, C), lambda i: (jnp.maximum(i * (bt // HALO) - 1, 0), 0)), nxt(C),
                  rowb(LANES), full(conv_w), full(a_log), full(dt_bias), full(sb), full(sg),
                  rowb(W), nxt(W), rowb(W), nxt(W), rowb(W), nxt(W), rowb(W), rowb(W)],
        out_specs=(rowb(C), rowb(LANES), pl.BlockSpec((K, C), lambda i: (0, 0)), vec, vec),
        out_shape=(jax.ShapeDtypeStruct((T, C), BF16), jax.ShapeDtypeStruct((T, LANES), F32),
                   jax.ShapeDtypeStruct((K, C), F32), jax.ShapeDtypeStruct((1, LANES), F32),
                   jax.ShapeDtypeStruct((1, LANES), F32)),
        scratch_shapes=[pltpu.VMEM((bt + 2 * HALO, C), F32), pltpu.VMEM((n_ext, C), F32), pltpu.VMEM((n_ext, C), F32)],
        compiler_params=_params(1),
    )(pdn, pdn, pdn, pba, conv_w, a_log, dt_bias, sb, sg, dq, dq, dk, dk, dv, dv, dbx, dgx)


def _dot_split(a, b):
    a_hi, b_hi = a.astype(BF16), b.astype(BF16)
    a_lo, b_lo = (a - a_hi.astype(F32)).astype(BF16), (b - b_hi.astype(F32)).astype(BF16)
    dot = functools.partial(jnp.dot, preferred_element_type=F32)
    return dot(a_hi, b_hi) + (dot(a_hi, b_lo) + dot(a_lo, b_hi))


def _dn_local(q, k, v, beta, g, tm=None):
    row, col = _iota2(ROWS, ROWS)
    same = (row // CHUNK) == (col // CHUNK)
    causal = same & (row >= col)
    strict = same & (row > col)
    eye = (row == col).astype(F32)
    last_of = (col == (row // CHUNK) * CHUNK + (CHUNK - 1)).astype(BF16)
    eg = jnp.exp(g)
    g_rows = jnp.concatenate([g] * (ROWS // HEAD_DIM), axis=1)
    decay = jnp.where(causal, jnp.exp(jnp.where(causal, g_rows - g_rows.T, 0.0)), 0.0)
    kb = k * beta
    vb = v * beta
    kk = _bdot(kb, k, NT)
    low = jnp.where(strict, kk * decay, 0.0)
    if tm is None:
        pw = -low
        tm = eye + pw
        for _ in range(5):
            pw = _dot_split(pw, pw)
            tm = tm + _dot_split(tm, pw)
    kbg = kb * eg
    u = _bdot(tm, vb)
    w = _bdot(tm, kbg)
    qk = _bdot(q, k, NT)
    qa = jnp.where(causal, qk * decay, 0.0)
    glast = _dot3r(last_of, g)
    e2 = jnp.exp(glast - g)
    return dict(row=row, col=col, same=same, causal=causal, strict=strict, eg=eg, decay=decay, kb=kb, vb=vb, kk=kk,
                tm=tm, kbg=kbg, u=u, w=w, qk=qk, qa=qa, glast=glast, e2=e2, kte=k * e2, qd=q * eg)


def _dn_core_fwd(q, k, v, bx, gx, z, gain, name, comm):
    T, W = q.shape
    H = W // HEAD_DIM
    nb = T // ROWS
    hp = _heads_per_step(H)
    ng = H // hp

    def body(*refs):
        ((q_ref, k_ref, v_ref, b_ref, g_ref, z_ref, gain_ref), (o_ref, mix_ref, ss_ref, tm_ref), (state,),
         (first, mid, last)) = comm.split(refs, 7, 4, 1)
        h, b = pl.program_id(0), pl.program_id(1)
        pl.when((h == 0) & (b == 0))(first)
        pl.when((h == 3 * ng // 4) & (b == 0))(mid)

        @pl.when(b == 0)
        def _():
            state[...] = jnp.zeros_like(state)

        for t in range(hp):
            ln = slice(t * HEAD_DIM, (t + 1) * HEAD_DIM)
            L = _dn_local(q_ref[:, ln], k_ref[:, ln], v_ref[:, ln], b_ref[:, ln], g_ref[:, ln])
            s = state[t]
            vns, qds = [], []
            for c in range(N_CHUNKS):
                rows = slice(c * CHUNK, (c + 1) * CHUNK)
                ss_ref[t, c] = s
                vn = L["u"][rows] - _bdot(L["w"][rows], s)
                qds.append(_bdot(L["qd"][rows], s))
                vns.append(vn)
                s = s * jnp.exp(L["glast"][c * CHUNK:c * CHUNK + 1, :]) + _bdot(L["kte"][rows], vn, TN)
            state[t] = s
            tm_ref[:, t * ROWS:(t + 1) * ROWS] = L["tm"]
            o = jnp.concatenate(qds, axis=0) + _bdot(L["qa"], jnp.concatenate(vns, axis=0))
            o_ref[:, ln] = o
            zz = z_ref[:, ln]
            r = lax.rsqrt(jnp.mean(o * o, axis=-1, keepdims=True) + EPS)
            mix_ref[:, ln] = ((o * r * gain_ref[...]) * (zz * _sigmoid(zz))).astype(BF16)
        pl.when((h == ng - 1) & (b == nb - 1))(last)

    wide = hp * HEAD_DIM
    blk = pl.BlockSpec((ROWS, wide), lambda h, b: (b, h))
    zblk = pl.BlockSpec((ROWS, wide), lambda h, b: (b, 3 * ng + h))
    return comm.call(
        body, name, (ng, nb), [blk] * 5 + [zblk, pl.BlockSpec((1, HEAD_DIM), lambda h, b: (0, 0))],
        (blk, blk, pl.BlockSpec((hp, N_CHUNKS, HEAD_DIM, HEAD_DIM), lambda h, b: (h, b, 0, 0)),
         pl.BlockSpec((ROWS, hp * ROWS), lambda h, b: (b, h))),
        (jax.ShapeDtypeStruct((T, W), F32), jax.ShapeDtypeStruct((T, W), BF16),
         jax.ShapeDtypeStruct((H, T // CHUNK, HEAD_DIM, HEAD_DIM), F32), jax.ShapeDtypeStruct((T, H * ROWS), F32)),
        [pltpu.VMEM((hp, HEAD_DIM, HEAD_DIM), F32)], (q, k, v, bx, gx, z, gain))


def _dn_core_bwd(q, k, v, bx, gx, z, gain, o, dmix, ss, tms, dmix_col0, name, comm):
    T, W = q.shape
    H = W // HEAD_DIM
    nb = T // ROWS
    hp = _heads_per_step(H)
    ng = H // hp
    wide = hp * HEAD_DIM
    c0 = dmix_col0 // wide

    def body(*refs):
        ((q_ref, k_ref, v_ref, b_ref, g_ref, z_ref, gain_ref, o_ref, dm_ref, ss_ref, tm_ref),
         (dq_ref, dk_ref, dv_ref, dbx_ref, dgx_ref, dz_ref, dgain_ref), (dstate,),
         (first, mid, last)) = comm.split(refs, 11, 7, 1)
        pl.when((pl.program_id(0) == 0) & (pl.program_id(1) == 0))(first)
        pl.when((pl.program_id(0) == ng // 2) & (pl.program_id(1) == 0))(mid)

        @pl.when(pl.program_id(1) == 0)
        def _():
            dstate[...] = jnp.zeros_like(dstate)
            dgain_ref[...] = jnp.zeros_like(dgain_ref)

        refs = (q_ref, k_ref, v_ref, b_ref, g_ref, z_ref, gain_ref, o_ref, dm_ref, ss_ref, tm_ref,
                dq_ref, dk_ref, dv_ref, dbx_ref, dgx_ref, dz_ref, dgain_ref, dstate)
        for t in range(hp):
            one_head(t, *refs)
        pl.when((pl.program_id(0) == ng - 1) & (pl.program_id(1) == nb - 1))(last)

    def one_head(t, q_ref, k_ref, v_ref, b_ref, g_ref, z_ref, gain_ref, o_ref, dm_ref, ss_ref, tm_ref,
                 dq_ref, dk_ref, dv_ref, dbx_ref, dgx_ref, dz_ref, dgain_ref, dstate):
        ln = slice(t * HEAD_DIM, (t + 1) * HEAD_DIM)
        qv, kv, vv, beta, g = q_ref[:, ln], k_ref[:, ln], v_ref[:, ln], b_ref[:, ln], g_ref[:, ln]
        gain_v = gain_ref[...]
        ov, zz, dm = o_ref[:, ln], z_ref[:, ln], dm_ref[:, ln]
        r = lax.rsqrt(jnp.mean(ov * ov, axis=-1, keepdims=True) + EPS)
        n = ov * r
        sgz = _sigmoid(zz)
        d_on = dm * (zz * sgz)
        dz_ref[:, ln] = (dm * (n * gain_v) * (sgz * (1.0 + zz * (1.0 - sgz)))).astype(BF16)
        dgain_ref[t] += jnp.sum(d_on * n, axis=0, keepdims=True)
        gy = d_on * gain_v
        do = r * (gy - n * jnp.mean(gy * n, axis=-1, keepdims=True))

        L = _dn_local(qv, kv, vv, beta, g, tm_ref[:, t * ROWS:(t + 1) * ROWS])
        causal, strict = L["causal"], L["strict"]
        u, w, qa, qd, kte, tm = L["u"], L["w"], L["qa"], L["qd"], L["kte"], L["tm"]
        s_in = [ss_ref[t, c] for c in range(N_CHUNKS)]
        vn = [u[c * CHUNK:(c + 1) * CHUNK] - _bdot(w[c * CHUNK:(c + 1) * CHUNK], s_in[c]) for c in range(N_CHUNKS)]
        vn_all = jnp.concatenate(vn, axis=0)
        qat_do = _bdot(qa, do, TN)
        d_qa = jnp.where(causal, _bdot(do, vn_all, NT), 0.0)
        ds = dstate[t]
        d_vn, d_kte, d_qd, d_w, d_gl = ([None] * N_CHUNKS for _ in range(5))
        for c in reversed(range(N_CHUNKS)):
            rows = slice(c * CHUNK, (c + 1) * CHUNK)
            egl = jnp.exp(L["glast"][c * CHUNK:c * CHUNK + 1, :])
            d_vn[c] = qat_do[rows] + _bdot(kte[rows], ds)
            d_kte[c] = _bdot(vn[c], ds, NT)
            d_gl[c] = jnp.sum(jnp.sum(ds * s_in[c], axis=1, keepdims=True), axis=0, keepdims=True) * egl
            d_qd[c] = _bdot(do[rows], s_in[c], NT)
            d_w[c] = -_bdot(d_vn[c], s_in[c], NT)
            ds = ds * egl + _bdot(qd[rows], do[rows], TN) - _bdot(w[rows], d_vn[c], TN)
        dstate[t] = ds
        d_u = jnp.concatenate(d_vn, axis=0)
        d_w = jnp.concatenate(d_w, axis=0)
        d_qd = jnp.concatenate(d_qd, axis=0)
        d_kte = jnp.concatenate(d_kte, axis=0)

        d_tm = _bdot(d_u, L["vb"], NT) + _bdot(d_w, L["kbg"], NT)
        d_vb = _bdot(tm, d_u, TN)
        d_kbg = _bdot(tm, d_w, TN)
        d_low = jnp.where(strict, -_bdot(_bdot(tm, d_tm, TN), tm, NT), 0.0)
        decay = L["decay"]
        d_kk = d_low * decay
        d_qk = d_qa * decay
        d_decay = d_low * L["kk"] + d_qa * L["qk"]
        eg, e2 = L["eg"], L["e2"]
        d_kb = _bdot(d_kk, kv) + d_kbg * eg
        dk_ref[:, ln] = _bdot(d_kk, L["kb"], TN) + _bdot(d_qk, qv, TN) + d_kb * beta + d_kte * e2
        dq_ref[:, ln] = _bdot(d_qk, kv) + d_qd * eg
        dv_ref[:, ln] = d_vb * beta
        rsum = lambda a: jnp.sum(a, axis=-1, keepdims=True)
        dbx_ref[:, ln] = jnp.broadcast_to(rsum(d_kb * kv) + rsum(d_vb * vv), (ROWS, HEAD_DIM))
        d_eg = rsum(d_kbg * L["kb"]) + rsum(d_qd * qv)
        t2 = rsum(d_kte * kv) * e2
        ed = d_decay * decay
        d_g = d_eg * eg - t2 + rsum(ed) - rsum(ed.T)
        row, col = L["row"], L["col"]
        chunk_sum = L["same"].astype(BF16)
        lane_row = lax.broadcasted_iota(jnp.int32, (ROWS, HEAD_DIM), 0)
        is_last = (lane_row % CHUNK) == (CHUNK - 1)
        d_glast = _dot3r(chunk_sum, t2)
        for c in range(N_CHUNKS):
            d_glast = d_glast + jnp.where(lane_row // CHUNK == c, d_gl[c], 0.0)
        d_g = d_g + jnp.where(is_last, d_glast, 0.0)
        suffix = (L["same"] & (col >= row)).astype(BF16)
        dgx_ref[:, ln] = _dot3r(suffix, d_g)

    rev = lambda b: nb - 1 - b
    blk = pl.BlockSpec((ROWS, wide), lambda h, b: (rev(b), h))
    zblk = pl.BlockSpec((ROWS, wide), lambda h, b: (rev(b), 3 * ng + h))
    dmblk = pl.BlockSpec((ROWS, wide), lambda h, b: (rev(b), c0 + h))
    out = jax.ShapeDtypeStruct((T, W), F32)
    return comm.call(
        body, name, (ng, nb),
        [blk] * 5 + [zblk, pl.BlockSpec((1, HEAD_DIM), lambda h, b: (0, 0)), blk, dmblk,
                     pl.BlockSpec((hp, N_CHUNKS, HEAD_DIM, HEAD_DIM), lambda h, b: (h, rev(b), 0, 0)),
                     pl.BlockSpec((ROWS, hp * ROWS), lambda h, b: (rev(b), h))],
        (blk,) * 6 + (pl.BlockSpec((hp, 1, HEAD_DIM), lambda h, b: (h, 0, 0)),),
        (out,) * 5 + (jax.ShapeDtypeStruct((T, W), BF16), jax.ShapeDtypeStruct((H, 1, HEAD_DIM), F32)),
        [pltpu.VMEM((hp, HEAD_DIM, HEAD_DIM), F32)], (q, k, v, bx, gx, z, gain, o, dmix, ss, tms))


_GELU_C = 0.7978845608028654
_GELU_A = 0.044715


def _gelu(x):
    x2 = x * x
    t = jnp.tanh(x * (x2 * (_GELU_C * _GELU_A) + _GELU_C))
    half = 0.5 * x
    return half + half * t, t, half, x2


FUSED_ROWS = 1024
FUSED_CHUNK = 256
BF16_ROWS = 16


def _ffn_up_fused(hn, w_up, conv_w, conv_b, tn, name):
    T, D = hn.shape
    S, _, C = w_up.shape
    F = S * C // 2
    K = conv_w.shape[0]
    tm = min(FUSED_ROWS, T)
    tn = _fit([C], tn)
    per, nj = C // tn, F // tn
    chunk = min(FUSED_CHUNK, tm)

    def body(a_ref, ap_ref, bg_ref, bv_ref, wg_ref, wv_ref, cg_ref, cv_ref, ug_ref, uv_ref, o_ref, gext, vext):
        keep = pl.program_id(0) > 0
        mats = ((bg_ref[...], wg_ref[...], cg_ref[...], ug_ref, gext), (bv_ref[...], wv_ref[...], cv_ref[...], uv_ref, vext))
        for b, _, _, _, ext in mats:
            ext[pl.ds(0, HALO), :] = jnp.where(keep, _bdot(ap_ref[...], b)[BF16_ROWS - HALO:], 0.0)
        for c0 in range(0, tm, chunk):
            a = a_ref[pl.ds(c0, chunk), :]
            for b, _, _, u_ref, ext in mats:
                u = _bdot(a, b)
                u_ref[pl.ds(c0, chunk), :] = u
                ext[pl.ds(HALO + c0, chunk), :] = u
            for r0, rows in _strips(chunk):
                gate, val = [_conv_taps(ext, w, rows, HALO + c0 + r0) + cb for _, w, cb, _, ext in mats]
                o_ref[pl.ds(c0 + r0, rows), :] = (_gelu(gate)[0] * val).astype(BF16)

    out = pl.BlockSpec((tm, tn), lambda i, j: (i, j))
    return pl.pallas_call(
        body, name=name, grid=(T // tm, nj),
        in_specs=[pl.BlockSpec((tm, D), lambda i, j: (i, 0)),
                  pl.BlockSpec((BF16_ROWS, D), lambda i, j: (jnp.maximum(i * (tm // BF16_ROWS) - 1, 0), 0)),
                  pl.BlockSpec((None, D, tn), lambda i, j: (j // per, 0, j % per)),
                  pl.BlockSpec((None, D, tn), lambda i, j: ((nj + j) // per, 0, (nj + j) % per)),
                  pl.BlockSpec((K, tn), lambda i, j: (0, j)), pl.BlockSpec((K, tn), lambda i, j: (0, nj + j)),
                  pl.BlockSpec((1, tn), lambda i, j: (0, j)), pl.BlockSpec((1, tn), lambda i, j: (0, nj + j))],
        out_specs=(out, out, out),
        out_shape=(jax.ShapeDtypeStruct((T, F), F32), jax.ShapeDtypeStruct((T, F), F32), jax.ShapeDtypeStruct((T, F), BF16)),
        scratch_shapes=[pltpu.VMEM((tm + HALO, tn), F32)] * 2, compiler_params=_params(2),
    )(hn, hn, w_up, w_up, conv_w, conv_w, conv_b, conv_b)


def _ffn_mid_bwd(up_g, up_v, conv_w, conv_b, da, bt, bc, name):
    T, F = up_g.shape
    nc = F // bc
    K = conv_w.shape[0]
    nb = T // bt
    n_ext = bt + HALO

    def body(g_ref, gp_ref, gn_ref, v_ref, vp_ref, vn_ref, da_ref, dan_ref, wg_ref, wv_ref, bg_ref, bv_ref,
             dg_ref, dv_ref, dwg_ref, dwv_ref, dbg_ref, dbv_ref, gext, vext, dgext, dvext):
        i = pl.program_id(1)
        last = i == nb - 1
        for ext, cur, prev, nxt in ((gext, g_ref, gp_ref, gn_ref), (vext, v_ref, vp_ref, vn_ref)):
            ext[pl.ds(0, HALO), :] = jnp.where(i > 0, prev[...], 0.0)
            ext[pl.ds(HALO, bt), :] = cur[...]
            ext[pl.ds(HALO + bt, HALO), :] = jnp.where(last, 0.0, nxt[...])
        wg, wv, bg, bv = wg_ref[...], wv_ref[...], bg_ref[...], bv_ref[...]
        for r0, rows in _strips(bt) + [(bt, HALO)]:
            gate = _conv_taps(gext, wg, rows, HALO + r0) + bg
            val = _conv_taps(vext, wv, rows, HALO + r0) + bv
            dact = da_ref[pl.ds(r0, rows), :] if r0 < bt else jnp.where(last, 0.0, dan_ref[...])
            ge, t, half, x2 = _gelu(gate)
            dgelu = (0.5 * t + 0.5) + (half * (1.0 - t * t)) * (x2 * (3.0 * _GELU_C * _GELU_A) + _GELU_C)
            dgext[pl.ds(r0, rows), :] = dact * val * dgelu
            dvext[pl.ds(r0, rows), :] = dact * ge

        @pl.when(i == 0)
        def _():
            for ref in (dwg_ref, dwv_ref, dbg_ref, dbv_ref):
                ref[...] = jnp.zeros_like(ref)

        for dext, ext, w, dx_ref, dw_ref, db_ref in ((dgext, gext, wg, dg_ref, dwg_ref, dbg_ref),
                                                     (dvext, vext, wv, dv_ref, dwv_ref, dbv_ref)):
            dw, db = [0.0] * K, 0.0
            for r0, rows in _strips(bt):
                dx = None
                for j in range(K):
                    term = dext[pl.ds(r0 + K - 1 - j, rows), :] * w[j:j + 1, :]
                    dx = term if dx is None else dx + term
                dx_ref[pl.ds(r0, rows), :] = dx.astype(BF16)
                dcur = dext[pl.ds(r0, rows), :]
                for j in range(K):
                    dw[j] = dw[j] + jnp.sum(dcur * ext[pl.ds(HALO + r0 - (K - 1) + j, rows), :], axis=0, keepdims=True)
                db = db + jnp.sum(dcur, axis=0, keepdims=True)
            dw_ref[...] += jnp.concatenate(dw, axis=0)
            db_ref[...] += db

    prev = lambda i: jnp.maximum(i * (bt // HALO) - 1, 0)
    nxt = lambda i: jnp.minimum((i + 1) * (bt // HALO), T // HALO - 1)
    cur_g = pl.BlockSpec((bt, bc), lambda j, i: (i, j))
    outs = pl.pallas_call(
        body, name=name, grid=(nc, nb),
        in_specs=[cur_g, pl.BlockSpec((HALO, bc), lambda j, i: (prev(i), j)),
                  pl.BlockSpec((HALO, bc), lambda j, i: (nxt(i), j)),
                  cur_g, pl.BlockSpec((HALO, bc), lambda j, i: (prev(i), j)),
                  pl.BlockSpec((HALO, bc), lambda j, i: (nxt(i), j)),
                  cur_g, pl.BlockSpec((HALO, bc), lambda j, i: (nxt(i), j)),
                  pl.BlockSpec((K, bc), lambda j, i: (0, j)), pl.BlockSpec((K, bc), lambda j, i: (0, nc + j)),
                  pl.BlockSpec((1, bc), lambda j, i: (0, j)), pl.BlockSpec((1, bc), lambda j, i: (0, nc + j))],
        out_specs=(cur_g, cur_g, pl.BlockSpec((K, bc), lambda j, i: (0, j)), pl.BlockSpec((K, bc), lambda j, i: (0, j)),
                   pl.BlockSpec((1, bc), lambda j, i: (0, j)), pl.BlockSpec((1, bc), lambda j, i: (0, j))),
        out_shape=(jax.ShapeDtypeStruct((T, F), BF16), jax.ShapeDtypeStruct((T, F), BF16),
                   jax.ShapeDtypeStruct((K, F), F32), jax.ShapeDtypeStruct((K, F), F32),
                   jax.ShapeDtypeStruct((1, F), F32), jax.ShapeDtypeStruct((1, F), F32)),
        scratch_shapes=[pltpu.VMEM((bt + 2 * HALO, bc), F32)] * 2 + [pltpu.VMEM((n_ext, bc), F32)] * 2,
        compiler_params=_params(2),
    )(up_g, up_g, up_g, up_v, up_v, up_v, da, da, conv_w, conv_w, conv_b, conv_b)
    return outs


def _adam_math(w, g, m, v):
    m2 = ADAM_B1 * m + (1.0 - ADAM_B1) * g
    v2 = ADAM_B2 * v + (1.0 - ADAM_B2) * (g * g)
    m_hat = m2 / (1.0 - ADAM_B1 ** ADAM_STEP)
    v_hat = v2 / (1.0 - ADAM_B2 ** ADAM_STEP)
    return -ADAM_LR * (m_hat / (jnp.sqrt(v_hat) + ADAM_EPS) + ADAM_WD * w), m2, v2


def _adamw_halves(w, mine, theirs, place, m, v, bt, name):
    R, C = w.shape
    h = R // 2
    bt = _fit_rows(h, bt)
    nh = h // bt

    def body(s_ref, w_ref, a_ref, b_ref, m_ref, v_ref, g_ref, d_ref, m2_ref, v2_ref):
        lower = pl.program_id(0) < nh
        gv = jnp.where(lower == (s_ref[0] == 0), a_ref[...], b_ref[...])
        g_ref[...] = gv
        d_ref[...], m2_ref[...], v2_ref[...] = _adam_math(w_ref[...], gv, m_ref[...], v_ref[...])

    full = pl.BlockSpec((bt, C), lambda i, s: (i, 0))

    def half(is_mine):
        def index(i, s):
            used = ((i < nh) == (s[0] == 0)) == is_mine
            return jnp.where(used, i % nh, jnp.where(i < nh, 0, nh - 1)), 0
        return pl.BlockSpec((bt, C), index)

    out = jax.ShapeDtypeStruct((R, C), F32)
    return pl.pallas_call(
        body, name=name,
        grid_spec=pltpu.PrefetchScalarGridSpec(num_scalar_prefetch=1, grid=(2 * nh,),
                                               in_specs=[full, half(True), half(False), full, full],
                                               out_specs=(full,) * 4),
        out_shape=(out,) * 4, compiler_params=_params(1),
    )(place, w, mine, theirs, m, v)


def _adamw(w, g, m, v, bt, name):
    R, C = w.shape
    bt = _fit_rows(R, bt)
    blk = pl.BlockSpec((bt, C), lambda i: (i, 0))

    def body(w_ref, g_ref, m_ref, v_ref, d_ref, m2_ref, v2_ref):
        gv = g_ref[...]
        m2 = ADAM_B1 * m_ref[...] + (1.0 - ADAM_B1) * gv
        v2 = ADAM_B2 * v_ref[...] + (1.0 - ADAM_B2) * (gv * gv)
        m_hat = m2 / (1.0 - ADAM_B1 ** ADAM_STEP)
        v_hat = v2 / (1.0 - ADAM_B2 ** ADAM_STEP)
        d_ref[...] = -ADAM_LR * (m_hat / (jnp.sqrt(v_hat) + ADAM_EPS) + ADAM_WD * w_ref[...])
        m2_ref[...] = m2
        v2_ref[...] = v2

    out = jax.ShapeDtypeStruct((R, C), F32)
    return pl.pallas_call(body, name=name, grid=(R // bt,), in_specs=[blk] * 4, out_specs=(blk,) * 3,
                          out_shape=(out,) * 3, compiler_params=_params(1))(w, g, m, v)


def _place():
    x, y, c = lax.axis_index("x"), lax.axis_index("y"), lax.axis_index("c")
    chips = [(1 - x, y), (x, 1 - y), (1 - x, 1 - y)]
    return x, y, c, chips


_HBM = pl.BlockSpec(memory_space=pltpu.HBM)


def _add_cores(buf, other, place, own_only, out_dtype, bt, name):
    n, _, h, cols = buf.shape
    bt = _fit_rows(h, bt)
    row = (lambda k, s: s[1]) if own_only else (lambda k, s: k)

    def body(s_ref, a_ref, b_ref, o_ref):
        o_ref[...] = (a_ref[...] + b_ref[...]).astype(out_dtype)

    return pl.pallas_call(
        body, name=name,
        grid_spec=pltpu.PrefetchScalarGridSpec(
            num_scalar_prefetch=1, grid=(1 if own_only else n, h // bt),
            in_specs=[pl.BlockSpec((None, None, bt, cols), lambda k, i, s: (row(k, s), s[0], i, 0)),
                      pl.BlockSpec((None, bt, cols), lambda k, i, s: (row(k, s), i, 0))],
            out_specs=(pl.BlockSpec((bt, cols), lambda k, i, s: (i, 0)) if own_only
                       else pl.BlockSpec((None, bt, cols), lambda k, i, s: (k, i, 0)))),
        out_shape=jax.ShapeDtypeStruct((h, cols) if own_only else (n, h, cols), out_dtype),
        compiler_params=_params(2),
    )(place, buf, other)


def _add_chips(own, others, bt, name):
    h, cols = own.shape
    bt = _fit_rows(h, bt)

    def body(a_ref, b_ref, o_ref):
        o_ref[...] = ((a_ref[...] + b_ref[0].astype(F32)) + b_ref[1].astype(F32)) + b_ref[2].astype(F32)

    return pl.pallas_call(
        body, name=name, grid=(h // bt,),
        in_specs=[pl.BlockSpec((bt, cols), lambda i: (i, 0)), pl.BlockSpec((3, bt, cols), lambda i: (0, i, 0))],
        out_specs=pl.BlockSpec((bt, cols), lambda i: (i, 0)),
        out_shape=jax.ShapeDtypeStruct((h, cols), F32), compiler_params=_params(1),
    )(own, others)


def _gather_plan(bufs, split):
    n = len(bufs)

    def phases(ins, outs, sems):
        send, recv = sems
        x, y, c, chips = _place()
        me = 2 * x + y

        def rows(b, core):
            h = bufs[b].shape[0] // 2
            return pl.ds(core * h, h) if split[b] else pl.ds(0, bufs[b].shape[0])

        def over_ici(b, j, block):
            px, py = chips[j]
            return pltpu.make_async_remote_copy(
                src_ref=ins[b].at[rows(b, c)], dst_ref=outs[b].at[block, rows(b, c)], send_sem=send.at[b, j],
                recv_sem=recv.at[b, j], device_id=(px, py, c), device_id_type=MESH)

        def over_d2d(b, j, block, core):
            return pltpu.make_async_remote_copy(
                src_ref=outs[b].at[block, rows(b, core)], dst_ref=outs[b].at[block, rows(b, core)],
                send_sem=send.at[b, 3 + j], recv_sem=recv.at[b, 3 + j], device_id=(x, y, 1 - c), device_id_type=MESH)

        pairs = [(b, j) for b in range(n) for j in range(3)]
        source = lambda j: 2 * chips[j][0] + chips[j][1]

        def first():
            for b, j in pairs:
                over_ici(b, j, me).start()

        def mid():
            for b, j in pairs:
                over_ici(b, j, source(j)).wait_recv()
                if split[b]:
                    over_d2d(b, j, source(j), c).start()

        def last():
            for b, j in pairs:
                if split[b]:
                    over_d2d(b, j, source(j), 1 - c).wait_recv()
            for b, j in pairs:
                over_ici(b, j, me).wait_send()
                if split[b]:
                    over_d2d(b, j, source(j), c).wait_send()

        return first, mid, last

    return ([jax.ShapeDtypeStruct((4,) + b.shape, b.dtype) for b in bufs],
            [pltpu.SemaphoreType.DMA((n, 6)), pltpu.SemaphoreType.DMA((n, 6))], phases)


def _exchange_plan(n, out_shapes, copy):
    def phases(ins, outs, sems):
        send, recv = sems
        place = _place()

        def first():
            for b in range(n):
                copy(b, ins, outs, send, recv, place).start()

        def last():
            for b in range(n):
                copy(b, ins, outs, send, recv, place).wait()

        return first, (lambda: None), last

    return out_shapes, [pltpu.SemaphoreType.DMA((n,)), pltpu.SemaphoreType.DMA((n,))], phases


def _swap_plan(bufs):
    def copy(b, ins, outs, send, recv, place):
        x, y, c, _ = place
        h = bufs[b].shape[1] // 2
        return pltpu.make_async_remote_copy(
            src_ref=ins[b].at[:, pl.ds((1 - c) * h, h)], dst_ref=outs[b], send_sem=send.at[b], recv_sem=recv.at[b],
            device_id=(x, y, 1 - c), device_id_type=MESH)

    shapes = [jax.ShapeDtypeStruct((b.shape[0], b.shape[1] // 2, b.shape[2]), b.dtype) for b in bufs]
    return _exchange_plan(len(bufs), shapes, copy)


def _scatter_plan(bufs):
    def copy(t, ins, outs, send, recv, place):
        x, y, c, chips = place
        b, j = divmod(t, 3)
        px, py = chips[j]
        return pltpu.make_async_remote_copy(
            src_ref=ins[b].at[2 * px + py], dst_ref=outs[b].at[j], send_sem=send.at[t], recv_sem=recv.at[t],
            device_id=(px, py, c), device_id_type=MESH)

    shapes = [jax.ShapeDtypeStruct((3,) + b.shape[1:], b.dtype) for b in bufs]
    return _exchange_plan(3 * len(bufs), shapes, copy)


def _join_plan(halves):
    def copy(b, ins, outs, send, recv, place):
        x, y, c, _ = place
        return pltpu.make_async_remote_copy(
            src_ref=ins[b], dst_ref=outs[b], send_sem=send.at[b], recv_sem=recv.at[b],
            device_id=(x, y, 1 - c), device_id_type=MESH)

    return _exchange_plan(len(halves), [jax.ShapeDtypeStruct(b.shape, b.dtype) for b in halves], copy)


def _run_plan(arrays, plan, name):
    out_shapes, sems, phases = plan
    n, m = len(arrays), len(out_shapes)

    def body(*refs):
        for phase in phases(refs[:n], refs[n:n + m], refs[n + m:]):
            phase()

    return pl.pallas_call(
        body, name=name, in_specs=[_HBM] * n, out_specs=[_HBM] * m, out_shape=out_shapes, scratch_shapes=sems,
        compiler_params=pltpu.CompilerParams(has_side_effects=True),
    )(*arrays)


def _fit_rows(n, target):
    for q in (2 * HALO, HALO):
        for t in range(min(n, target) // q * q, 0, -q):
            if n % t == 0:
                return t
    raise ValueError((n, target))


def _allreduce_small(buf, name):
    R, lanes = buf.shape

    def body(in_ref, out_ref, land, send, recv):
        x, y, c, _ = _place()
        me = 4 * x + 2 * y + c
        land[me] = in_ref[...]
        cps = []
        for r in range(1, 8):
            px, py, pc = x ^ (r >> 2), y ^ ((r >> 1) & 1), c ^ (r & 1)
            cp = pltpu.make_async_remote_copy(
                src_ref=in_ref, dst_ref=land.at[me], send_sem=send.at[r - 1], recv_sem=recv.at[me],
                device_id=(px, py, pc), device_id_type=MESH)
            cp.start()
            cps.append(cp)
        for r in range(1, 8):
            peer = 4 * (x ^ (r >> 2)) + 2 * (y ^ ((r >> 1) & 1)) + (c ^ (r & 1))
            pltpu.make_async_remote_copy(
                src_ref=in_ref, dst_ref=land.at[peer], send_sem=send.at[r - 1], recv_sem=recv.at[peer],
                device_id=(x, y, c), device_id_type=MESH).wait_recv()
        for cp in cps:
            cp.wait_send()
        acc = land[0]
        for d in range(1, 8):
            acc = acc + land[d]
        out_ref[...] = acc

    vm = pl.BlockSpec(memory_space=pltpu.VMEM)
    return pl.pallas_call(
        body, name=name, in_specs=[vm], out_specs=vm, out_shape=jax.ShapeDtypeStruct((R, lanes), buf.dtype),
        scratch_shapes=[pltpu.VMEM((8, R, lanes), buf.dtype), pltpu.SemaphoreType.DMA((7,)), pltpu.SemaphoreType.DMA((8,))],
        compiler_params=pltpu.CompilerParams(has_side_effects=True, vmem_limit_bytes=VMEM_LIMIT),
    )(buf)


ROW_BLOCK = 256
SB_BLOCK = 256
MM_TM, MM_TN, MM_TK = 1024, 512, 512
FFN_COLS = 512


def _lane_pad(vec, start):
    return jnp.pad(vec, ((0, 0), (start, LANES - start - vec.shape[1])))


WEIGHTS = ("w_in", "sb_out_gain", "dn_conv_w", "dn_a_log", "dn_dt_bias", "dn_out_gain", "w_out", "ln_mix_pre",
           "ln_mix_post", "w_up", "ffn_conv_w", "ffn_conv_b", "w_down", "ln_ffn_pre", "ln_ffn_post")
MATRICES = {"w_in": 1, "w_out": 0, "w_up": 1, "w_down": 0}
CONV_SHARDED = ("dn_conv_w", "ffn_conv_w")
SMALL = tuple(n for n in WEIGHTS if n not in MATRICES)
N_CHIPS = 4
ADAM_ROWS = 128


def _pack(arrs, quantum):
    rows, layout, off = [], [], 0
    for a in arrs:
        n = int(np.prod(a.shape))
        r = -(-n // LANES)
        r = -(-r // HALO) * HALO
        rows.append(jnp.pad(a.reshape(-1), (0, r * LANES - n)).reshape(r, LANES))
        layout.append((off, r, n, a.shape))
        off += r
    total = -(-off // quantum) * quantum
    if total > off:
        rows.append(jnp.zeros((total - off, LANES), rows[0].dtype))
    return jnp.concatenate(rows, axis=0), layout


def _unpack(packed, layout):
    return [packed[off:off + r].reshape(-1)[:n].reshape(shape) for off, r, n, shape in layout]


UP_TILE = 1408
PAIR_ROWS = 256


def _reduce_to_chips(shares, place, names, swap_on, scatter_on):
    from_sibling = swap_on(shares)
    halves = [s.reshape(N_CHIPS, 2, s.shape[1] // 2, s.shape[2]) for s in shares]
    to_chips = [_add_cores(hv, fs, place, False, BF16, PAIR_ROWS, "grad_add_cores_" + n)
                for hv, fs, n in zip(halves, from_sibling, names)]
    own = [_add_cores(hv, fs, place, True, F32, PAIR_ROWS, "grad_add_cores_own_" + n)
           for hv, fs, n in zip(halves, from_sibling, names)]
    return own, scatter_on(to_chips)


def _step(x, target, wt, late, chip, place):
    T, D = x.shape
    W = D // 2
    H = W // HEAD_DIM
    bt = min(ROW_BLOCK, T)
    blk = min(SB_BLOCK, T)
    w_in = wt["w_in"]
    a_log, dt_bias = _lane_pad(wt["dn_a_log"], H), _lane_pad(wt["dn_dt_bias"], H)
    mm = functools.partial(_mm, tm=MM_TM, tn=MM_TN)
    wide = functools.partial(_mm, tm=MM_TM, tn=2 * MM_TN)
    mm_up = functools.partial(_mm, tm=MM_TM, tn=UP_TILE)
    one = lambda a, b, tk=MM_TK: [(a, b, tk, 0, 0)]

    xn = _rms_fwd(x, wt["ln_mix_pre"], None, BF16, bt, "rms_mix_pre")
    psb = mm(one(xn, w_in, D), "nn", BF16, name="proj_sb", n_window=(0, 3 * W))
    pdn = mm(one(xn, w_in, D), "nn", F32, name="proj_dn", n_window=(3 * W, 4 * W))
    pba = mm(one(xn, w_in, D), "nn", F32, name="proj_ba", n_window=(7 * W, LANES))
    late_names = ("w_out", "w_up", "w_down")
    gathered_with = lambda names: _Comm([late[n] for n in names], _gather_plan([late[n] for n in names], [True] * len(names)))
    own_block_in = lambda theirs, names: [lax.dynamic_update_index_in_dim(t, late[n], chip, 0) for t, n in zip(theirs, names)]
    (o_sb, mix_sb, lt, swept), theirs = _sb_fwd(psb, wt["sb_out_gain"], blk, "sb_fwd", gathered_with(("w_out",)))
    w_out = own_block_in(theirs, ("w_out",))[0].reshape(-1, D)
    qn, kn, vv, bx, gx = _dn_pre_fwd(pdn, pba, wt["dn_conv_w"], a_log, dt_bias, bt, "dn_pre_fwd")
    (o_dn, mix_dn, ss, tms), theirs = _dn_core_fwd(qn, kn, vv, bx, gx, pdn, wt["dn_out_gain"], "dn_core_fwd",
                                                  gathered_with(("w_up", "w_down")))
    w_up, w_down = own_block_in(theirs, ("w_up", "w_down"))
    w_down = w_down.reshape(-1, D)
    F = w_down.shape[0]
    m = wide([(mix_sb, w_out, 2 * MM_TK, 0, 0), (mix_dn, w_out, 2 * MM_TK, 0, W)], "nn", F32, name="out_proj")
    h = _rms_fwd(m, wt["ln_mix_post"], x, F32, bt, "rms_mix_post")
    hn = _rms_fwd(h, wt["ln_ffn_pre"], None, BF16, bt, "rms_ffn_pre")
    bc = min(FFN_COLS, F)
    up_g, up_v, act = _ffn_up_fused(hn, w_up, wt["ffn_conv_w"], wt["ffn_conv_b"], FFN_COLS // 2, "ffn_up")
    f = mm(one(act, w_down, UP_TILE), "nn", F32, name="ffn_down")
    dy, df, g_ffn_post, sq = _loss_head(f, wt["ln_ffn_post"], h, target, bt, "loss_head")
    loss = 0.5 * jnp.sum(sq) / D

    da = mm(one(df, w_down, D), "nt", F32, name="d_act")
    g_w_down = _mm(one(act, df, 2 * MM_TK), "tn", F32, tm=UP_TILE, tn=2 * MM_TN, name="g_w_down")
    dug, duv, dwg, dwv, dbg, dbv = _ffn_mid_bwd(up_g, up_v, wt["ffn_conv_w"], wt["ffn_conv_b"], da, bt, bc, "ffn_mid_bwd")
    shard = w_up.shape[2]
    g_w_up = mm_up(one(hn, dug, 2 * MM_TK), "tn", F32, name="g_w_up_gate", out_shard=shard,
                   into=(lax.empty(w_up.shape, F32), 0))
    g_w_up = mm_up(one(hn, duv, 2 * MM_TK), "tn", F32, name="g_w_up_val", out_shard=shard, into=(g_w_up, F))
    ffn_shares = [g_w_up, g_w_down.reshape(N_CHIPS, -1, D)]
    dhn, ffn_swapped = wide([(dug, w_up, UP_TILE, 0, 0), (duv, w_up, UP_TILE, 0, F)], "nt", F32, name="d_hn",
                            comm=_Comm(ffn_shares, _swap_plan(ffn_shares)))
    dh, g_ffn_pre = _rms_bwd(h, wt["ln_ffn_pre"], dhn, dy, F32, bt, "rms_ffn_pre_bwd")
    dm, g_mix_post = _rms_bwd(m, wt["ln_mix_post"], dh, None, BF16, bt, "rms_mix_post_bwd")
    dmix = mm(one(dm, w_out, D), "nt", F32, name="d_mix")
    g_w_out = jnp.concatenate([wide(one(mix_sb, dm, 2 * MM_TK), "tn", F32, name="g_w_out_sb"),
                               wide(one(mix_dn, dm, 2 * MM_TK), "tn", F32, name="g_w_out_dn")], axis=0)
    shares = [g_w_out.reshape(N_CHIPS, -1, D)] + ffn_shares
    carried = {}

    def swap_on(arrays):
        (carried["do_sb"], carried["g_sb_gain"]), out = _headnorm_bwd(
            o_sb, wt["sb_out_gain"], dmix, bt, "sb_norm_bwd", _Comm(arrays[:1], _swap_plan(arrays[:1])))
        return list(out) + list(ffn_swapped)

    def scatter_on(arrays):
        carried["dn"], out = _dn_core_bwd(qn, kn, vv, bx, gx, pdn, wt["dn_out_gain"], o_dn, dmix, ss, tms, W,
                                          "dn_core_bwd", _Comm(arrays, _scatter_plan(arrays)))
        return out

    early = _reduce_to_chips(shares, place, late_names, swap_on, scatter_on)
    g_sb_gain = carried["g_sb_gain"]
    (dq, dk, dv), _ = _sb_bwd(psb, carried["do_sb"], lt, swept, blk, "sb_bwd", _Comm())
    ddq, ddk, ddv, dbx, dgx, dz, g_dn_gain = carried["dn"]
    dconv, dba, g_dn_conv, g_a_log, g_dt_bias = _dn_pre_bwd(pdn, pba, wt["dn_conv_w"], a_log, dt_bias,
                                                            ddq, ddk, ddv, dbx, dgx, bt, "dn_pre_bwd")
    pieces = [(dq, 0), (dk, W), (dv, 2 * W), (dconv, 3 * W), (dz, 6 * W), (dba, 7 * W)]
    g_w_in = [wide(one(xn, d, 2 * MM_TK), "tn", F32, name=f"g_w_in_{i}") for i, (d, _) in enumerate(pieces)]
    g_w_in[-1] = g_w_in[-1][:, :2 * H]
    g_in = jnp.concatenate(g_w_in, axis=1)

    def with_d_xn(arrays):
        carried["dxn"], out = mm([(d, w_in, 2 * MM_TK, 0, k0) for d, k0 in pieces], "nt", F32, name="d_xn",
                                 comm=_Comm(arrays, _scatter_plan(arrays)))
        return out

    last = _reduce_to_chips([g_in.reshape(D, N_CHIPS, -1).transpose(1, 0, 2)], place, ["w_in"],
                            lambda arrays: _run_plan(arrays, _swap_plan(arrays), "grad_swap_cores"), with_d_xn)
    dx, g_mix_pre = _rms_bwd(x, wt["ln_mix_pre"], carried["dxn"], dh, F32, bt, "rms_mix_pre_bwd")
    exchanged = dict(zip(late_names, zip(*early)))
    exchanged["w_in"] = (last[0][0], last[1][0])

    grads = dict(
        sb_out_gain=g_sb_gain, dn_conv_w=g_dn_conv, dn_a_log=g_a_log[:, H:2 * H],
        dn_dt_bias=g_dt_bias[:, H:2 * H], dn_out_gain=jnp.sum(g_dn_gain, axis=0),
        ln_mix_pre=g_mix_pre, ln_mix_post=g_mix_post,
        ffn_conv_w=jnp.concatenate([dwg, dwv], axis=1), ffn_conv_b=jnp.concatenate([dbg, dbv], axis=1),
        ln_ffn_pre=g_ffn_pre, ln_ffn_post=g_ffn_post)
    return loss, dx, grads, exchanged


def kernel(x, w_in, sb_out_gain, dn_conv_w, dn_a_log, dn_dt_bias, dn_out_gain, w_out, ln_mix_pre, ln_mix_post, w_up, ffn_conv_w, ffn_conv_b, w_down, ln_ffn_pre, ln_ffn_post, loss_target, m_w_in, m_sb_out_gain, m_dn_conv_w, m_dn_a_log, m_dn_dt_bias, m_dn_out_gain, m_w_out, m_ln_mix_pre, m_ln_mix_post, m_w_up, m_ffn_conv_w, m_ffn_conv_b, m_w_down, m_ln_ffn_pre, m_ln_ffn_post, v_w_in, v_sb_out_gain, v_dn_conv_w, v_dn_a_log, v_dn_dt_bias, v_dn_out_gain, v_w_out, v_ln_mix_pre, v_ln_mix_post, v_w_up, v_ffn_conv_w, v_ffn_conv_b, v_w_down, v_ln_ffn_pre, v_ln_ffn_post):
    given = dict(locals())
    wl = {n: given[n][0] for n in WEIGHTS}
    ml = {n: given["m_" + n][0] for n in WEIGHTS}
    vl = {n: given["v_" + n][0] for n in WEIGHTS}
    for d in (wl, ml, vl):
        for n in SMALL:
            if d[n].ndim == 1:
                d[n] = d[n][None]
    cx, cy, cc = lax.axis_index("x"), lax.axis_index("y"), lax.axis_index("c")
    chip = 2 * cx + cy
    D = x.shape[2]
    W = D // 2

    first = ("w_in",) + CONV_SHARDED
    mine = [wl["w_in"].astype(BF16)] + [wl[n] for n in CONV_SHARDED]
    theirs = _run_plan(mine, _gather_plan(mine, [True, False, False]), "gather_w_in")
    got = {n: lax.dynamic_update_index_in_dim(t, s, chip, 0) for n, t, s in zip(first, theirs, mine)}
    columns = lambda g: g.transpose(1, 0, 2).reshape(g.shape[1], N_CHIPS * g.shape[2])
    wt = {n: wl[n] for n in SMALL}
    w_in_all = columns(got["w_in"])
    wt["w_in"] = jnp.pad(w_in_all, ((0, 0), (0, 7 * W + LANES - w_in_all.shape[1])))
    for n in CONV_SHARDED:
        wt[n] = columns(got[n])
    late = {n: wl[n].astype(BF16) for n in ("w_out", "w_up", "w_down")}

    place = jnp.stack([cc, chip]).astype(jnp.int32)
    loss, dx, grads, exchanged = _step(x[0], loss_target[0], wt, late, chip, place)
    loss = lax.psum(loss, ("x", "y", "c"))

    names = list(MATRICES)
    reduced = [_add_chips(*exchanged[n], PAIR_ROWS, "grad_add_chips_" + n) for n in names]
    siblings = _run_plan(reduced, _join_plan(reduced), "grad_join_cores")
    gl = {}

    small, small_layout = _pack([grads[n] for n in SMALL], HALO)
    small = _allreduce_small(small, "grad_allreduce_small")
    for n, g in zip(SMALL, _unpack(small, small_layout)):
        if n in CONV_SHARDED:
            size = g.shape[1] // N_CHIPS
            g = lax.dynamic_slice_in_dim(g, chip * size, size, axis=1)
        gl[n] = g

    delta, new_m, new_v = {}, {}, {}
    for n, mine_half, sibling_half in zip(names, reduced, siblings):
        gl[n], delta[n], new_m[n], new_v[n] = _adamw_halves(wl[n], mine_half, sibling_half, place, ml[n], vl[n],
                                                            ADAM_ROWS, "adamw_" + n)
    packs = [_pack([d[n] for n in SMALL], HALO) for d in (wl, gl, ml, vl)]
    outs = _adamw(*[p[0] for p in packs], ADAM_ROWS, "adamw_small")
    for res, o in zip((delta, new_m, new_v), outs):
        res.update(zip(SMALL, _unpack(o, packs[0][1])))

    shaped = lambda d: [d[n].reshape(given[n].shape) for n in WEIGHTS]
    return (loss, dx[None], *shaped(gl), *shaped(delta), *shaped(new_m), *shaped(new_v))
```

```python
import functools

import numpy as np
import jax
import jax.numpy as jnp
from jax import lax
from jax.experimental import pallas as pl
from jax.experimental.pallas import tpu as pltpu

F32 = jnp.float32
BF16 = jnp.bfloat16
HEAD_DIM = 128
CHUNK = 64
ROWS = 4 * CHUNK
N_CHUNKS = ROWS // CHUNK
EPS = 1e-6
EXP_UNDERFLOW = 110.0
LANES = 128
HALO = 8
VMEM_LIMIT = 48 * 1024 * 1024
ADAM_LR, ADAM_B1, ADAM_B2, ADAM_EPS, ADAM_WD, ADAM_STEP = 0.001, 0.9, 0.999, 1e-08, 0.01, 10
MESH = pl.DeviceIdType.MESH

NN = (((1,), (0,)), ((), ()))
NT = (((1,), (1,)), ((), ()))
TN = (((0,), (0,)), ((), ()))


def _params(n_axes):
    return pltpu.CompilerParams(dimension_semantics=("arbitrary",) * n_axes, vmem_limit_bytes=VMEM_LIMIT)


def _bdot(a, b, dims=NN):
    return lax.dot_general(a.astype(BF16), b.astype(BF16), dims, preferred_element_type=F32)


def _split3(a):
    hi = a.astype(BF16)
    r1 = a - hi.astype(F32)
    mid = r1.astype(BF16)
    lo = (r1 - mid.astype(F32)).astype(BF16)
    return hi, mid, lo


def _dot3(a, sel, dims=NN):
    return sum(lax.dot_general(p, sel, dims, preferred_element_type=F32) for p in _split3(a))


def _dot3r(sel, a, dims=NN):
    return sum(lax.dot_general(sel, p, dims, preferred_element_type=F32) for p in _split3(a))


def _iota2(n, m):
    return lax.broadcasted_iota(jnp.int32, (n, m), 0), lax.broadcasted_iota(jnp.int32, (n, m), 1)


def _sigmoid(x):
    return 1.0 / (1.0 + jnp.exp(-x))


def _softplus(x):
    return jnp.maximum(x, 0.0) + jnp.log(1.0 + jnp.exp(-jnp.abs(x)))


def _fit(values, target):
    values = [v for v in (values if isinstance(values, (list, tuple)) else [values]) if v]
    best = None
    for t in range(LANES, min(min(values), target) + 1, LANES):
        if all(v % t == 0 for v in values):
            best = t
    assert best is not None, (values, target)
    return best


def _mm(parts, mode, out_dtype, tm, tn, name, n_window=None, out_shard=None, into=None, comm=None):
    dims = {"nn": NN, "nt": NT, "tn": TN}[mode]
    a0, b0 = parts[0][0], parts[0][1]
    b3 = b0.ndim == 3
    shard_c = b0.shape[2] if b3 else None
    M = a0.shape[1] if mode == "tn" else a0.shape[0]
    if mode == "nt":
        n_full = b0.shape[1] if b3 else b0.shape[0]
    else:
        n_full = b0.shape[0] * b0.shape[2] if b3 else b0.shape[1]
    n0, N = n_window if n_window is not None else (0, n_full)
    out_n0 = into[1] if into is not None else 0
    tm = _fit(M, tm)
    tn = _fit([N, n0, out_n0, out_shard, shard_c if mode != "nt" else None], tn)
    specs_a, specs_b, offs, nks = [], [], [], []
    off = 0
    for a, b, tk, a_k0, b_k0 in parts:
        K = a.shape[0] if mode == "tn" else a.shape[1]
        tk = _fit([K, a_k0, b_k0, shard_c if mode == "nt" else None], tk)
        nk = K // tk
        kk = lambda k, o=off, n=nk: jnp.clip(k - o, 0, n - 1)
        ao, bo, no = a_k0 // tk, b_k0 // tk, n0 // tn
        if mode == "tn":
            specs_a.append(pl.BlockSpec((tk, tm), lambda i, j, k, kk=kk, ao=ao: (kk(k) + ao, i)))
        else:
            specs_a.append(pl.BlockSpec((tm, tk), lambda i, j, k, kk=kk, ao=ao: (i, kk(k) + ao)))
        if mode == "nt":
            if b3:
                per = shard_c // tk
                specs_b.append(pl.BlockSpec((None, tn, tk), lambda i, j, k, kk=kk, bo=bo, per=per:
                                            ((kk(k) + bo) // per, j, (kk(k) + bo) % per)))
            else:
                specs_b.append(pl.BlockSpec((tn, tk), lambda i, j, k, kk=kk, bo=bo: (j, kk(k) + bo)))
        else:
            if b3:
                per = shard_c // tn
                specs_b.append(pl.BlockSpec((None, tk, tn), lambda i, j, k, kk=kk, bo=bo, no=no, per=per:
                                            ((j + no) // per, kk(k) + bo, (j + no) % per)))
            else:
                specs_b.append(pl.BlockSpec((tk, tn), lambda i, j, k, kk=kk, bo=bo, no=no: (kk(k) + bo, j + no)))
        offs.append(off)
        nks.append(nk)
        off += nk
    nk_total = off
    n_parts = len(parts)

    comm = comm if comm is not None else _Comm()
    grid = (M // tm, N // tn, nk_total)
    n_in = 2 * n_parts + (1 if into is not None else 0)

    def body(*refs):
        ins, (o_ref,), scratch, (first, mid, last) = comm.split(refs, n_in, 1, 0 if nk_total == 1 else 1)
        a_refs, b_refs = ins[:n_parts], ins[n_parts:2 * n_parts]
        at = lambda step: functools.reduce(lambda x, y: x & y, [pl.program_id(d) == step[d] for d in range(3)])
        pl.when(at((0, 0, 0)))(first)
        pl.when(at((grid[0] // 2, 0, 0)))(mid)
        if nk_total == 1:
            o_ref[...] = _bdot(a_refs[0][...], b_refs[0][...], dims).astype(out_dtype)
        else:
            acc = scratch[0]
            k = pl.program_id(2)

            @pl.when(k == 0)
            def _():
                acc[...] = jnp.zeros_like(acc)

            for p in range(n_parts):
                @pl.when((k >= offs[p]) & (k < offs[p] + nks[p]))
                def _(p=p):
                    acc[...] += _bdot(a_refs[p][...], b_refs[p][...], dims)

            @pl.when(k == nk_total - 1)
            def _():
                o_ref[...] = acc[...].astype(out_dtype)
        pl.when(at(tuple(g - 1 for g in grid)))(last)

    jo = out_n0 // tn
    if out_shard is not None:
        per_o = out_shard // tn
        out_spec = pl.BlockSpec((None, tm, tn), lambda i, j, k: ((j + jo) // per_o, i, (j + jo) % per_o))
        out_shape = jax.ShapeDtypeStruct((N // out_shard, M, out_shard), out_dtype)
    else:
        out_spec = pl.BlockSpec((tm, tn), lambda i, j, k: (i, j + jo))
        out_shape = jax.ShapeDtypeStruct((M, N), out_dtype)
    ins = [p[0] for p in parts] + [p[1] for p in parts]
    in_specs = specs_a + specs_b
    aliases = {}
    if into is not None:
        out_shape = jax.ShapeDtypeStruct(into[0].shape, into[0].dtype)
        aliases = {len(ins): 0}
        ins.append(into[0])
        in_specs.append(pl.BlockSpec(memory_space=pl.ANY))
    (out,), carried = comm.call(body, name, grid, in_specs, (out_spec,), (out_shape,),
                                [] if nk_total == 1 else [pltpu.VMEM((tm, tn), F32)], ins, aliases)
    return (out, carried) if comm.phases is not None else out


def _rms_fwd(x, gain, resid, out_dtype, bt, name):
    T, D = x.shape
    row = pl.BlockSpec((bt, D), lambda i: (i, 0))
    vec = pl.BlockSpec((1, D), lambda i: (0, 0))

    def body(*refs):
        x_ref, g_ref = refs[0], refs[1]
        o_ref = refs[-1]
        xv = x_ref[...]
        y = xv * lax.rsqrt(jnp.mean(xv * xv, axis=-1, keepdims=True) + EPS) * g_ref[...]
        if resid is not None:
            y = refs[2][...] + y
        o_ref[...] = y.astype(out_dtype)

    ins = [x, gain] + ([resid] if resid is not None else [])
    return pl.pallas_call(
        body, name=name, grid=(T // bt,),
        in_specs=[row, vec] + ([row] if resid is not None else []),
        out_specs=row, out_shape=jax.ShapeDtypeStruct((T, D), out_dtype), compiler_params=_params(1),
    )(*ins)


def _rms_bwd_math(xv, g, dy):
    r = lax.rsqrt(jnp.mean(xv * xv, axis=-1, keepdims=True) + EPS)
    n = xv * r
    gy = dy * g
    dx = r * (gy - n * jnp.mean(gy * n, axis=-1, keepdims=True))
    return dx, dy * n


def _rms_bwd(x, gain, dy, resid, out_dtype, bt, name):
    T, D = x.shape
    row = pl.BlockSpec((bt, D), lambda i: (i, 0))
    vec = pl.BlockSpec((1, D), lambda i: (0, 0))

    def body(*refs):
        x_ref, g_ref, dy_ref = refs[0], refs[1], refs[2]
        dx_ref, dg_ref = refs[-2], refs[-1]
        dx, dgp = _rms_bwd_math(x_ref[...], g_ref[...], dy_ref[...].astype(F32))
        if resid is not None:
            dx = refs[3][...] + dx
        dx_ref[...] = dx.astype(out_dtype)

        @pl.when(pl.program_id(0) == 0)
        def _():
            dg_ref[...] = jnp.zeros_like(dg_ref)

        dg_ref[...] += jnp.sum(dgp, axis=0, keepdims=True)

    ins = [x, gain, dy] + ([resid] if resid is not None else [])
    return pl.pallas_call(
        body, name=name, grid=(T // bt,),
        in_specs=[row, vec, row] + ([row] if resid is not None else []),
        out_specs=(row, vec),
        out_shape=(jax.ShapeDtypeStruct((T, D), out_dtype), jax.ShapeDtypeStruct((1, D), F32)),
        compiler_params=_params(1),
    )(*ins)


def _loss_head(f, gain, h, target, bt, name):
    T, D = f.shape
    row = pl.BlockSpec((bt, D), lambda i: (i, 0))
    vec = pl.BlockSpec((1, D), lambda i: (0, 0))

    def body(f_ref, g_ref, h_ref, t_ref, dy_ref, df_ref, dg_ref, sq_ref):
        fv, g = f_ref[...], g_ref[...]
        r = lax.rsqrt(jnp.mean(fv * fv, axis=-1, keepdims=True) + EPS)
        n = fv * r
        err = (h_ref[...] + n * g) - t_ref[...]
        dy = err * (1.0 / D)
        gy = dy * g
        df = r * (gy - n * jnp.mean(gy * n, axis=-1, keepdims=True))
        dy_ref[...] = dy
        df_ref[...] = df.astype(BF16)

        @pl.when(pl.program_id(0) == 0)
        def _():
            dg_ref[...] = jnp.zeros_like(dg_ref)
            sq_ref[...] = jnp.zeros_like(sq_ref)

        dg_ref[...] += jnp.sum(dy * n, axis=0, keepdims=True)
        sq_ref[...] += jnp.sum(err * err, axis=0, keepdims=True)

    return pl.pallas_call(
        body, name=name, grid=(T // bt,), in_specs=[row, vec, row, row], out_specs=(row, row, vec, vec),
        out_shape=(jax.ShapeDtypeStruct((T, D), F32), jax.ShapeDtypeStruct((T, D), BF16),
                   jax.ShapeDtypeStruct((1, D), F32), jax.ShapeDtypeStruct((1, D), F32)),
        compiler_params=_params(1),
    )(f, gain, h, target)


def _sb_logits(q, k, valid):
    z = lax.dot_general(q, k, NT, preferred_element_type=F32) * (HEAD_DIM ** -0.5)
    sp = jnp.log(1.0 + jnp.exp(-jnp.abs(z)))
    lb = jnp.minimum(z, 0.0) - sp
    l1 = -(jnp.maximum(z, 0.0) + sp)
    return lb, (l1 if valid is None else jnp.where(valid, l1, 0.0))


def _masked(valid, x):
    return x if valid is None else jnp.where(valid, x, 0.0)


def _heads_per_step(n_heads):
    return 2 if n_heads % 2 == 0 else 1


def _dot2(a, sel):
    hi = a.astype(BF16)
    lo = (a - hi.astype(F32)).astype(BF16)
    return jnp.dot(hi, sel, preferred_element_type=F32) + jnp.dot(lo, sel, preferred_element_type=F32)


class _Comm:
    def __init__(self, arrays=(), plan=((), (), None)):
        self.arrays = list(arrays)
        self.out_shapes, self.sems, self.phases = list(plan[0]), list(plan[1]), plan[2]

    def split(self, refs, n_in, n_out, n_scratch):
        a, o = len(self.arrays), len(self.out_shapes)
        cuts = np.cumsum([0, n_in, a, n_out, o, n_scratch])
        ins, cin, outs, cout, scratch = (refs[cuts[t]:cuts[t + 1]] for t in range(5))
        if self.phases is None:
            return ins, outs, scratch, (lambda: None,) * 3
        return ins, outs, scratch, self.phases(cin, cout, refs[cuts[5]:])

    def call(self, body, name, grid, in_specs, out_specs, out_shape, scratch_shapes, operands, aliases=None):
        outs = pl.pallas_call(
            body, name=name, grid=grid, in_specs=list(in_specs) + [_HBM] * len(self.arrays),
            out_specs=tuple(out_specs) + (_HBM,) * len(self.out_shapes),
            out_shape=tuple(out_shape) + tuple(self.out_shapes),
            scratch_shapes=list(scratch_shapes) + self.sems, input_output_aliases=aliases or {},
            compiler_params=pltpu.CompilerParams(dimension_semantics=("arbitrary",) * len(grid),
                                                 vmem_limit_bytes=VMEM_LIMIT, has_side_effects=self.phases is not None),
        )(*operands, *self.arrays)
        return outs[:len(out_shape)], outs[len(out_shape):]


def _sb_fwd(qkv, gain, blk, name, comm):
    T, W = qkv.shape[0], qkv.shape[1] // 3
    H = W // HEAD_DIM
    nq = T // blk
    hp = _heads_per_step(H)
    ng = H // hp
    lanes = [slice(t * HEAD_DIM, (t + 1) * HEAD_DIM) for t in range(hp)]

    def body(*refs):
        (q_ref, k_ref, v_ref, g_ref), (o_ref, mix_ref, lt_ref, swept_ref), _, (first, mid, last) = comm.split(refs, 4, 4, 0)
        h, i = pl.program_id(0), pl.program_id(1)
        pl.when((h == 0) & (i == 0))(first)
        pl.when((h == 3 * ng // 4) & (i == 0))(mid)
        q = [q_ref[:, ln] for ln in lanes]
        row, col = _iota2(blk, blk)
        after =(row > col).astype(BF16)

        def step(kb, carry, valid):
            ks = pl.ds(pl.multiple_of(kb * blk, blk), blk)
            out = []
            for t, (run, acc) in enumerate(carry):
                lb, l1 = _sb_logits(q[t], k_ref[ks, lanes[t]], valid)
                att = _masked(valid, jnp.exp(lb + _dot2(l1, after) + run))
                out.append((run + jnp.sum(l1, axis=1, keepdims=True), acc + _bdot(att, v_ref[ks, lanes[t]])))
            return tuple(out)

        zero = (jnp.zeros((blk, 1), F32), jnp.zeros((blk, HEAD_DIM), F32))
        def alive(state):
            jj, c = state
            return (jj < i) & (functools.reduce(jnp.maximum, [jnp.max(run) for run, _ in c]) > -EXP_UNDERFLOW)

        swept, carry = lax.while_loop(alive, lambda st: (st[0] + 1, step(i - 1 - st[0], st[1], None)),
                                      (jnp.int32(0), step(i, (zero,) * hp, col < row)))
        swept_ref[h, i] = swept
        for t, (run, o) in enumerate(carry):
            o_ref[:, lanes[t]] = o
            r = lax.rsqrt(jnp.mean(o * o, axis=-1, keepdims=True) + EPS)
            mix_ref[:, lanes[t]] = (o * r * g_ref[...]).astype(BF16)
            lt_ref[:, lanes[t]] = jnp.broadcast_to(run, (blk, HEAD_DIM))
        pl.when((h == ng - 1) & (i == nq - 1))(last)

    wide = hp * HEAD_DIM
    qb = pl.BlockSpec((blk, wide), lambda h, i: (i, h))
    return comm.call(
        body, name, (ng, nq),
        [qb, pl.BlockSpec((T, wide), lambda h, i: (0, ng + h)),
         pl.BlockSpec((T, wide), lambda h, i: (0, 2 * ng + h)), pl.BlockSpec((1, HEAD_DIM), lambda h, i: (0, 0))],
        (qb, qb, qb, pl.BlockSpec(memory_space=pltpu.SMEM)),
        (jax.ShapeDtypeStruct((T, W), F32), jax.ShapeDtypeStruct((T, W), BF16), jax.ShapeDtypeStruct((T, W), F32),
         jax.ShapeDtypeStruct((ng, nq), jnp.int32)),
        [], (qkv, qkv, qkv, gain))


def _headnorm_bwd(o, gain, dmix, bt, name, comm):
    T, W = o.shape
    H = W // HEAD_DIM
    nt = T // bt
    blk = pl.BlockSpec((bt, HEAD_DIM), lambda i, h: (i, h))
    vec = pl.BlockSpec((1, HEAD_DIM), lambda i, h: (0, 0))

    def body(*refs):
        (o_ref, g_ref, d_ref), (do_ref, dg_ref), _, (first, mid, last) = comm.split(refs, 3, 2, 0)
        i, h = pl.program_id(0), pl.program_id(1)
        pl.when((i == 0) & (h == 0))(first)
        pl.when((i == nt // 2) & (h == 0))(mid)
        do, dgp = _rms_bwd_math(o_ref[...], g_ref[...], d_ref[...])
        do_ref[...] = do

        @pl.when((i == 0) & (h == 0))
        def _():
            dg_ref[...] = jnp.zeros_like(dg_ref)

        dg_ref[...] += jnp.sum(dgp, axis=0, keepdims=True)
        pl.when((i == nt - 1) & (h == H - 1))(last)

    return comm.call(body, name, (nt, H), [blk, vec, blk], (blk, vec),
                     (jax.ShapeDtypeStruct((T, W), F32), jax.ShapeDtypeStruct((1, HEAD_DIM), F32)), [], (o, gain, dmix))


def _sb_bwd(qkv, do, lt, swept, blk, name, comm):
    T, W = qkv.shape[0], qkv.shape[1] // 3
    H = W // HEAD_DIM
    nq = T // blk
    scale = HEAD_DIM ** -0.5
    hp = _heads_per_step(H)
    ng = H // hp
    lanes = [slice(t * HEAD_DIM, (t + 1) * HEAD_DIM) for t in range(hp)]

    def body(*refs):
        ((q_ref, k_ref, v_ref, do_ref, lt_ref, swept_ref), (dq_ref, dk_out, dv_out), (dk_ref, dv_ref),
         (first, mid, last)) = comm.split(refs, 6, 3, 2)
        h, i = pl.program_id(0), pl.program_id(1)
        pl.when((h == 0) & (i == 0))(first)
        pl.when((h == ng // 2) & (i == 0))(mid)

        @pl.when(i == 0)
        def _():
            dk_ref[...] = jnp.zeros_like(dk_ref)
            dv_ref[...] = jnp.zeros_like(dv_ref)

        q = [q_ref[:, ln] for ln in lanes]
        dob = [do_ref[:, ln].astype(BF16) for ln in lanes]
        total = [lt_ref[:, ln][:, :1] for ln in lanes]
        row, col = _iota2(blk, blk)
        upto = (row <= col).astype(BF16)
        before = (row < col).astype(BF16)

        def step(kb, carry, valid):
            ks = pl.ds(pl.multiple_of(kb * blk, blk), blk)
            out = []
            for t, (seen, psum, dq) in enumerate(carry):
                k, v = k_ref[ks, lanes[t]], v_ref[ks, lanes[t]]
                lb, l1 = _sb_logits(q[t], k, valid)
                later = total[t] - seen - _dot2(l1, upto)
                att = _masked(valid, jnp.exp(lb + later))
                p = att * lax.dot_general(dob[t], v, NT, preferred_element_type=F32)
                c = psum + _dot2(p, before)
                sig = jnp.exp(lb)
                dz = (_masked(valid, p * (1.0 - sig) - c * sig) * scale).astype(BF16)
                dq = dq + jnp.dot(dz, k, preferred_element_type=F32)
                dk_ref[ks, lanes[t]] += lax.dot_general(dz, q[t], TN, preferred_element_type=F32)
                dv_ref[ks, lanes[t]] += lax.dot_general(att.astype(BF16), dob[t], TN, preferred_element_type=F32)
                out.append((seen + jnp.sum(l1, axis=1, keepdims=True), psum + jnp.sum(p, axis=1, keepdims=True), dq))
            return tuple(out)

        zero = jnp.zeros((blk, 1), F32)
        start = ((zero, zero, jnp.zeros((blk, HEAD_DIM), F32)),) * hp
        carry = step(i, lax.fori_loop(i - swept_ref[h, i], i, lambda kb, c: step(kb, c, None), start), col < row)
        for t in range(hp):
            dq_ref[:, lanes[t]] = carry[t][2].astype(BF16)

        @pl.when(i == nq - 1)
        def _():
            dk_out[...] = dk_ref[...].astype(BF16)
            dv_out[...] = dv_ref[...].astype(BF16)

        pl.when((h == ng - 1) & (i == nq - 1))(last)

    wide = hp * HEAD_DIM
    qb = pl.BlockSpec((blk, wide), lambda h, i: (i, h))
    head = pl.BlockSpec((T, wide), lambda h, i: (0, h))
    out = jax.ShapeDtypeStruct((T, W), BF16)
    return comm.call(
        body, name, (ng, nq),
        [qb, pl.BlockSpec((T, wide), lambda h, i: (0, ng + h)),
         pl.BlockSpec((T, wide), lambda h, i: (0, 2 * ng + h)), qb, qb, pl.BlockSpec(memory_space=pltpu.SMEM)],
        (qb, head, head), (out, out, out), [pltpu.VMEM((T, wide), F32)] * 2, (qkv, qkv, qkv, do, lt, swept))


def _expanders(H):
    lane = np.arange(H * HEAD_DIM) // HEAD_DIM
    eb = np.zeros((LANES, H * HEAD_DIM), np.float32)
    eg = np.zeros((LANES, H * HEAD_DIM), np.float32)
    eb[lane, np.arange(H * HEAD_DIM)] = 1.0
    eg[H + lane, np.arange(H * HEAD_DIM)] = 1.0
    sb = np.zeros((H * HEAD_DIM, LANES), np.float32)
    sg = np.zeros((H * HEAD_DIM, LANES), np.float32)
    sb[np.arange(H) * HEAD_DIM, np.arange(H)] = 1.0
    sg[np.arange(H) * HEAD_DIM, H + np.arange(H)] = 1.0
    return [jnp.asarray(m, BF16) for m in (eb, eg, sb, sg)]


def _conv_taps(ext_ref, w, n_out, lead):
    K = w.shape[0]
    out = None
    for j in range(K):
        term = ext_ref[pl.ds(lead - (K - 1) + j, n_out), :] * w[j:j + 1, :]
        out = term if out is None else out + term
    return out


STRIP_ROWS = 64


def _strips(n_rows):
    return [(r0, min(STRIP_ROWS, n_rows - r0)) for r0 in range(0, n_rows, STRIP_ROWS)]


def _l2_heads(s, H, fn):
    return jnp.concatenate([fn(s[:, h * HEAD_DIM:(h + 1) * HEAD_DIM]) for h in range(H)], axis=1)


def _dn_pre_fwd(pdn, pba, conv_w, a_log, dt_bias, bt, name):
    T, W = pdn.shape[0], pdn.shape[1] // 4
    H = W // HEAD_DIM
    eb, eg, _, _ = _expanders(H)
    nb = T // bt

    def body(x_ref, prev_ref, ba_ref, w_ref, al_ref, dt_ref, eb_ref, eg_ref,
             q_ref, k_ref, v_ref, bx_ref, gx_ref, ext):
        i = pl.program_id(0)
        ext[pl.ds(0, HALO), :] = jnp.where(i > 0, prev_ref[...], 0.0)
        ext[pl.ds(HALO, bt), :] = x_ref[...]
        c = _conv_taps(ext, w_ref[...], bt, HALO)
        s = c * _sigmoid(c)
        q_ref[...] = _l2_heads(s[:, :W], H, lambda t: t * lax.rsqrt(jnp.sum(t * t, axis=-1, keepdims=True) + EPS)
                               * (HEAD_DIM ** -0.5))
        k_ref[...] = _l2_heads(s[:, W:2 * W], H, lambda t: t * lax.rsqrt(jnp.sum(t * t, axis=-1, keepdims=True) + EPS))
        v_ref[...] = s[:, 2 * W:]
        ba = ba_ref[...]
        beta = _sigmoid(ba)
        graw = -jnp.exp(al_ref[...]) * _softplus(ba + dt_ref[...])
        row, col = _iota2(bt, bt)
        tri = ((row // CHUNK == col // CHUNK) & (row >= col)).astype(BF16)
        gcum = _dot3r(tri, graw)
        bx_ref[...] = _dot3(beta, eb_ref[...])
        gx_ref[...] = _dot3(gcum, eg_ref[...])

    C = 3 * W
    rowb = lambda w: pl.BlockSpec((bt, w), lambda i: (i, 0))
    full = lambda a: pl.BlockSpec(a.shape, lambda i: (0,) * a.ndim)
    out = jax.ShapeDtypeStruct((T, W), F32)
    return pl.pallas_call(
        body, name=name, grid=(nb,),
        in_specs=[rowb(C), pl.BlockSpec((HALO, C), lambda i: (jnp.maximum(i * (bt // HALO) - 1, 0), 0)),
                  rowb(LANES), full(conv_w), full(a_log), full(dt_bias), full(eb), full(eg)],
        out_specs=(rowb(W),) * 5, out_shape=(out,) * 5,
        scratch_shapes=[pltpu.VMEM((bt + HALO, C), F32)], compiler_params=_params(1),
    )(pdn, pdn, pba, conv_w, a_log, dt_bias, eb, eg)


def _dn_pre_bwd(pdn, pba, conv_w, a_log, dt_bias, dq, dk, dv, dbx, dgx, bt, name):
    T, W = pdn.shape[0], pdn.shape[1] // 4
    H = W // HEAD_DIM
    C = 3 * W
    K = conv_w.shape[0]
    _, _, sb, sg = _expanders(H)
    nb = T // bt
    n_ext = bt + HALO

    def body(x_ref, prev_ref, next_ref, ba_ref, w_ref, al_ref, dt_ref, sb_ref, sg_ref,
             dq_ref, dqn_ref, dk_ref, dkn_ref, dv_ref, dvn_ref, dbx_ref, dgx_ref,
             dx_ref, dba_ref, dw_ref, dal_ref, ddt_ref, ext, dext, dcext):
        i = pl.program_id(0)
        last = i == nb - 1
        ext[pl.ds(0, HALO), :] = jnp.where(i > 0, prev_ref[...], 0.0)
        ext[pl.ds(HALO, bt), :] = x_ref[...]
        ext[pl.ds(HALO + bt, HALO), :] = jnp.where(last, 0.0, next_ref[...])
        dext[pl.ds(0, bt), pl.ds(0, W)] = dq_ref[...]
        dext[pl.ds(0, bt), pl.ds(W, W)] = dk_ref[...]
        dext[pl.ds(0, bt), pl.ds(2 * W, W)] = dv_ref[...]
        dext[pl.ds(bt, HALO), pl.ds(0, W)] = jnp.where(last, 0.0, dqn_ref[...])
        dext[pl.ds(bt, HALO), pl.ds(W, W)] = jnp.where(last, 0.0, dkn_ref[...])
        dext[pl.ds(bt, HALO), pl.ds(2 * W, W)] = jnp.where(last, 0.0, dvn_ref[...])
        w = w_ref[...]
        l2_scale = (HEAD_DIM ** -0.5, 1.0, None)

        @pl.when(i == 0)
        def _():
            dw_ref[...] = jnp.zeros_like(dw_ref)
            dal_ref[...] = jnp.zeros_like(dal_ref)
            ddt_ref[...] = jnp.zeros_like(ddt_ref)

        for g in range(C // HEAD_DIM):
            cols = pl.ds(g * HEAD_DIM, HEAD_DIM)
            wg = w[:, g * HEAD_DIM:(g + 1) * HEAD_DIM]
            scale = l2_scale[g // H]
            for r0, rows in _strips(bt) + [(bt, HALO)]:
                c = None
                for j in range(K):
                    term = ext[pl.ds(HALO + r0 - (K - 1) + j, rows), cols] * wg[j:j + 1, :]
                    c = term if c is None else c + term
                sg_c = _sigmoid(c)
                ds = dext[pl.ds(r0, rows), cols]
                if scale is not None:
                    s = c * sg_c
                    r = lax.rsqrt(jnp.sum(s * s, axis=-1, keepdims=True) + EPS)
                    ds = scale * r * (ds - s * (r * r) * jnp.sum(s * ds, axis=-1, keepdims=True))
                dcext[pl.ds(r0, rows), cols] = ds * (sg_c * (1.0 + c * (1.0 - sg_c)))
            dw = [0.0] * K
            for r0, rows in _strips(bt):
                dx = None
                for j in range(K):
                    term = dcext[pl.ds(r0 + K - 1 - j, rows), cols] * wg[j:j + 1, :]
                    dx = term if dx is None else dx + term
                dx_ref[pl.ds(r0, rows), cols] = dx.astype(BF16)
                dcur = dcext[pl.ds(r0, rows), cols]
                for j in range(K):
                    dw[j] = dw[j] + jnp.sum(dcur * ext[pl.ds(HALO + r0 - (K - 1) + j, rows), cols], axis=0, keepdims=True)
            dw_ref[:, cols] += jnp.concatenate(dw, axis=0)

        ba = ba_ref[...]
        beta = _sigmoid(ba)
        al, dtb = al_ref[...], dt_ref[...]
        dbeta = _dot3(dbx_ref[...], sb_ref[...])
        dg = _dot3(dgx_ref[...], sg_ref[...])
        sp = _softplus(ba + dtb)
        da = dg * (-jnp.exp(al)) * _sigmoid(ba + dtb)
        dba_ref[...] = dbeta * beta * (1.0 - beta) + da
        dal_ref[...] += jnp.sum(dg * (-jnp.exp(al)) * sp, axis=0, keepdims=True)
        ddt_ref[...] += jnp.sum(da, axis=0, keepdims=True)

    rowb = lambda w: pl.BlockSpec((bt, w), lambda i: (i, 0))
    full = lambda a: pl.BlockSpec(a.shape, lambda i: (0,) * a.ndim)
    nxt = lambda w: pl.BlockSpec((HALO, w), lambda i: (jnp.minimum((i + 1) * (bt // HALO), T // HALO - 1), 0))
    vec = pl.BlockSpec((1, LANES), lambda i: (0, 0))
    return pl.pallas_call(
        body, name=name, grid=(nb,),
        in_specs=[rowb(C), pl.BlockSpec((HALO, C), lambda i: (jnp.maximum(i * (bt // HALO) - 1, 0), 0)), nxt(C),
                  rowb(LANES), full(conv_w), full(a_log), full(dt_bias), full(sb), full(sg),
                  rowb(W), nxt(W), rowb(W), nxt(W), rowb(W), nxt(W), rowb(W), rowb(W)],
        out_specs=(rowb(C), rowb(LANES), pl.BlockSpec((K, C), lambda i: (0, 0)), vec, vec),
        out_shape=(jax.ShapeDtypeStruct((T, C), BF16), jax.ShapeDtypeStruct((T, LANES), F32),
                   jax.ShapeDtypeStruct((K, C), F32), jax.ShapeDtypeStruct((1, LANES), F32),
                   jax.ShapeDtypeStruct((1, LANES), F32)),
        scratch_shapes=[pltpu.VMEM((bt + 2 * HALO, C), F32), pltpu.VMEM((n_ext, C), F32), pltpu.VMEM((n_ext, C), F32)],
        compiler_params=_params(1),
    )(pdn, pdn, pdn, pba, conv_w, a_log, dt_bias, sb, sg, dq, dq, dk, dk, dv, dv, dbx, dgx)


def _dot_split(a, b):
    a_hi, b_hi = a.astype(BF16), b.astype(BF16)
    a_lo, b_lo = (a - a_hi.astype(F32)).astype(BF16), (b - b_hi.astype(F32)).astype(BF16)
    dot = functools.partial(jnp.dot, preferred_element_type=F32)
    return dot(a_hi, b_hi) + (dot(a_hi, b_lo) + dot(a_lo, b_hi))


def _dn_local(q, k, v, beta, g, tm=None):
    row, col = _iota2(ROWS, ROWS)
    same = (row // CHUNK) == (col // CHUNK)
    causal = same & (row >= col)
    strict = same & (row > col)
    eye = (row == col).astype(F32)
    last_of = (col == (row // CHUNK) * CHUNK + (CHUNK - 1)).astype(BF16)
    eg = jnp.exp(g)
    g_rows = jnp.concatenate([g] * (ROWS // HEAD_DIM), axis=1)
    decay = jnp.where(causal, jnp.exp(jnp.where(causal, g_rows - g_rows.T, 0.0)), 0.0)
    kb = k * beta
    vb = v * beta
    kk = _bdot(kb, k, NT)
    low = jnp.where(strict, kk * decay, 0.0)
    if tm is None:
        pw = -low
        tm = eye + pw
        for _ in range(5):
            pw = _dot_split(pw, pw)
            tm = tm + _dot_split(tm, pw)
    kbg = kb * eg
    u = _bdot(tm, vb)
    w = _bdot(tm, kbg)
    qk = _bdot(q, k, NT)
    qa = jnp.where(causal, qk * decay, 0.0)
    glast = _dot3r(last_of, g)
    e2 = jnp.exp(glast - g)
    return dict(row=row, col=col, same=same, causal=causal, strict=strict, eg=eg, decay=decay, kb=kb, vb=vb, kk=kk,
                tm=tm, kbg=kbg, u=u, w=w, qk=qk, qa=qa, glast=glast, e2=e2, kte=k * e2, qd=q * eg)


def _dn_core_fwd(q, k, v, bx, gx, z, gain, name, comm):
    T, W = q.shape
    H = W // HEAD_DIM
    nb = T // ROWS
    hp = _heads_per_step(H)
    ng = H // hp

    def body(*refs):
        ((q_ref, k_ref, v_ref, b_ref, g_ref, z_ref, gain_ref), (o_ref, mix_ref, ss_ref, tm_ref), (state,),
         (first, mid, last)) = comm.split(refs, 7, 4, 1)
        h, b = pl.program_id(0), pl.program_id(1)
        pl.when((h == 0) & (b == 0))(first)
        pl.when((h == 3 * ng // 4) & (b == 0))(mid)

        @pl.when(b == 0)
        def _():
            state[...] = jnp.zeros_like(state)

        for t in range(hp):
            ln = slice(t * HEAD_DIM, (t + 1) * HEAD_DIM)
            L = _dn_local(q_ref[:, ln], k_ref[:, ln], v_ref[:, ln], b_ref[:, ln], g_ref[:, ln])
            s = state[t]
            vns, qds = [], []
            for c in range(N_CHUNKS):
                rows = slice(c * CHUNK, (c + 1) * CHUNK)
                ss_ref[t, c] = s
                vn = L["u"][rows] - _bdot(L["w"][rows], s)
                qds.append(_bdot(L["qd"][rows], s))
                vns.append(vn)
                s = s * jnp.exp(L["glast"][c * CHUNK:c * CHUNK + 1, :]) + _bdot(L["kte"][rows], vn, TN)
            state[t] = s
            tm_ref[:, t * ROWS:(t + 1) * ROWS] = L["tm"]
            o = jnp.concatenate(qds, axis=0) + _bdot(L["qa"], jnp.concatenate(vns, axis=0))
            o_ref[:, ln] = o
            zz = z_ref[:, ln]
            r = lax.rsqrt(jnp.mean(o * o, axis=-1, keepdims=True) + EPS)
            mix_ref[:, ln] = ((o * r * gain_ref[...]) * (zz * _sigmoid(zz))).astype(BF16)
        pl.when((h == ng - 1) & (b == nb - 1))(last)

    wide = hp * HEAD_DIM
    blk = pl.BlockSpec((ROWS, wide), lambda h, b: (b, h))
    zblk = pl.BlockSpec((ROWS, wide), lambda h, b: (b, 3 * ng + h))
    return comm.call(
        body, name, (ng, nb), [blk] * 5 + [zblk, pl.BlockSpec((1, HEAD_DIM), lambda h, b: (0, 0))],
        (blk, blk, pl.BlockSpec((hp, N_CHUNKS, HEAD_DIM, HEAD_DIM), lambda h, b: (h, b, 0, 0)),
         pl.BlockSpec((ROWS, hp * ROWS), lambda h, b: (b, h))),
        (jax.ShapeDtypeStruct((T, W), F32), jax.ShapeDtypeStruct((T, W), BF16),
         jax.ShapeDtypeStruct((H, T // CHUNK, HEAD_DIM, HEAD_DIM), F32), jax.ShapeDtypeStruct((T, H * ROWS), F32)),
        [pltpu.VMEM((hp, HEAD_DIM, HEAD_DIM), F32)], (q, k, v, bx, gx, z, gain))


def _dn_core_bwd(q, k, v, bx, gx, z, gain, o, dmix, ss, tms, dmix_col0, name, comm):
    T, W = q.shape
    H = W // HEAD_DIM
    nb = T // ROWS
    hp = _heads_per_step(H)
    ng = H // hp
    wide = hp * HEAD_DIM
    c0 = dmix_col0 // wide

    def body(*refs):
        ((q_ref, k_ref, v_ref, b_ref, g_ref, z_ref, gain_ref, o_ref, dm_ref, ss_ref, tm_ref),
         (dq_ref, dk_ref, dv_ref, dbx_ref, dgx_ref, dz_ref, dgain_ref), (dstate,),
         (first, mid, last)) = comm.split(refs, 11, 7, 1)
        pl.when((pl.program_id(0) == 0) & (pl.program_id(1) == 0))(first)
        pl.when((pl.program_id(0) == ng // 2) & (pl.program_id(1) == 0))(mid)

        @pl.when(pl.program_id(1) == 0)
        def _():
            dstate[...] = jnp.zeros_like(dstate)
            dgain_ref[...] = jnp.zeros_like(dgain_ref)

        refs = (q_ref, k_ref, v_ref, b_ref, g_ref, z_ref, gain_ref, o_ref, dm_ref, ss_ref, tm_ref,
                dq_ref, dk_ref, dv_ref, dbx_ref, dgx_ref, dz_ref, dgain_ref, dstate)
        for t in range(hp):
            one_head(t, *refs)
        pl.when((pl.program_id(0) == ng - 1) & (pl.program_id(1) == nb - 1))(last)

    def one_head(t, q_ref, k_ref, v_ref, b_ref, g_ref, z_ref, gain_ref, o_ref, dm_ref, ss_ref, tm_ref,
                 dq_ref, dk_ref, dv_ref, dbx_ref, dgx_ref, dz_ref, dgain_ref, dstate):
        ln = slice(t * HEAD_DIM, (t + 1) * HEAD_DIM)
        qv, kv, vv, beta, g = q_ref[:, ln], k_ref[:, ln], v_ref[:, ln], b_ref[:, ln], g_ref[:, ln]
        gain_v = gain_ref[...]
        ov, zz, dm = o_ref[:, ln], z_ref[:, ln], dm_ref[:, ln]
        r = lax.rsqrt(jnp.mean(ov * ov, axis=-1, keepdims=True) + EPS)
        n = ov * r
        sgz = _sigmoid(zz)
        d_on = dm * (zz * sgz)
        dz_ref[:, ln] = (dm * (n * gain_v) * (sgz * (1.0 + zz * (1.0 - sgz)))).astype(BF16)
        dgain_ref[t] += jnp.sum(d_on * n, axis=0, keepdims=True)
        gy = d_on * gain_v
        do = r * (gy - n * jnp.mean(gy * n, axis=-1, keepdims=True))

        L = _dn_local(qv, kv, vv, beta, g, tm_ref[:, t * ROWS:(t + 1) * ROWS])
        causal, strict = L["causal"], L["strict"]
        u, w, qa, qd, kte, tm = L["u"], L["w"], L["qa"], L["qd"], L["kte"], L["tm"]
        s_in = [ss_ref[t, c] for c in range(N_CHUNKS)]
        vn = [u[c * CHUNK:(c + 1) * CHUNK] - _bdot(w[c * CHUNK:(c + 1) * CHUNK], s_in[c]) for c in range(N_CHUNKS)]
        vn_all = jnp.concatenate(vn, axis=0)
        qat_do = _bdot(qa, do, TN)
        d_qa = jnp.where(causal, _bdot(do, vn_all, NT), 0.0)
        ds = dstate[t]
        d_vn, d_kte, d_qd, d_w, d_gl = ([None] * N_CHUNKS for _ in range(5))
        for c in reversed(range(N_CHUNKS)):
            rows = slice(c * CHUNK, (c + 1) * CHUNK)
            egl = jnp.exp(L["glast"][c * CHUNK:c * CHUNK + 1, :])
            d_vn[c] = qat_do[rows] + _bdot(kte[rows], ds)
            d_kte[c] = _bdot(vn[c], ds, NT)
            d_gl[c] = jnp.sum(jnp.sum(ds * s_in[c], axis=1, keepdims=True), axis=0, keepdims=True) * egl
            d_qd[c] = _bdot(do[rows], s_in[c], NT)
            d_w[c] = -_bdot(d_vn[c], s_in[c], NT)
            ds = ds * egl + _bdot(qd[rows], do[rows], TN) - _bdot(w[rows], d_vn[c], TN)
        dstate[t] = ds
        d_u = jnp.concatenate(d_vn, axis=0)
        d_w = jnp.concatenate(d_w, axis=0)
        d_qd = jnp.concatenate(d_qd, axis=0)
        d_kte = jnp.concatenate(d_kte, axis=0)

        d_tm = _bdot(d_u, L["vb"], NT) + _bdot(d_w, L["kbg"], NT)
        d_vb = _bdot(tm, d_u, TN)
        d_kbg = _bdot(tm, d_w, TN)
        d_low = jnp.where(strict, -_bdot(_bdot(tm, d_tm, TN), tm, NT), 0.0)
        decay = L["decay"]
        d_kk = d_low * decay
        d_qk = d_qa * decay
        d_decay = d_low * L["kk"] + d_qa * L["qk"]
        eg, e2 = L["eg"], L["e2"]
        d_kb = _bdot(d_kk, kv) + d_kbg * eg
        dk_ref[:, ln] = _bdot(d_kk, L["kb"], TN) + _bdot(d_qk, qv, TN) + d_kb * beta + d_kte * e2
        dq_ref[:, ln] = _bdot(d_qk, kv) + d_qd * eg
        dv_ref[:, ln] = d_vb * beta
        rsum = lambda a: jnp.sum(a, axis=-1, keepdims=True)
        dbx_ref[:, ln] = jnp.broadcast_to(rsum(d_kb * kv) + rsum(d_vb * vv), (ROWS, HEAD_DIM))
        d_eg = rsum(d_kbg * L["kb"]) + rsum(d_qd * qv)
        t2 = rsum(d_kte * kv) * e2
        ed = d_decay * decay
        d_g = d_eg * eg - t2 + rsum(ed) - rsum(ed.T)
        row, col = L["row"], L["col"]
        chunk_sum = L["same"].astype(BF16)
        lane_row = lax.broadcasted_iota(jnp.int32, (ROWS, HEAD_DIM), 0)
        is_last = (lane_row % CHUNK) == (CHUNK - 1)
        d_glast = _dot3r(chunk_sum, t2)
        for c in range(N_CHUNKS):
            d_glast = d_glast + jnp.where(lane_row // CHUNK == c, d_gl[c], 0.0)
        d_g = d_g + jnp.where(is_last, d_glast, 0.0)
        suffix = (L["same"] & (col >= row)).astype(BF16)
        dgx_ref[:, ln] = _dot3r(suffix, d_g)

    rev = lambda b: nb - 1 - b
    blk = pl.BlockSpec((ROWS, wide), lambda h, b: (rev(b), h))
    zblk = pl.BlockSpec((ROWS, wide), lambda h, b: (rev(b), 3 * ng + h))
    dmblk = pl.BlockSpec((ROWS, wide), lambda h, b: (rev(b), c0 + h))
    out = jax.ShapeDtypeStruct((T, W), F32)
    return comm.call(
        body, name, (ng, nb),
        [blk] * 5 + [zblk, pl.BlockSpec((1, HEAD_DIM), lambda h, b: (0, 0)), blk, dmblk,
                     pl.BlockSpec((hp, N_CHUNKS, HEAD_DIM, HEAD_DIM), lambda h, b: (h, rev(b), 0, 0)),
                     pl.BlockSpec((ROWS, hp * ROWS), lambda h, b: (rev(b), h))],
        (blk,) * 6 + (pl.BlockSpec((hp, 1, HEAD_DIM), lambda h, b: (h, 0, 0)),),
        (out,) * 5 + (jax.ShapeDtypeStruct((T, W), BF16), jax.ShapeDtypeStruct((H, 1, HEAD_DIM), F32)),
        [pltpu.VMEM((hp, HEAD_DIM, HEAD_DIM), F32)], (q, k, v, bx, gx, z, gain, o, dmix, ss, tms))


_GELU_C = 0.7978845608028654
_GELU_A = 0.044715


def _gelu(x):
    x2 = x * x
    t = jnp.tanh(x * (x2 * (_GELU_C * _GELU_A) + _GELU_C))
    half = 0.5 * x
    return half + half * t, t, half, x2


FUSED_ROWS = 1024
FUSED_CHUNK = 256
BF16_ROWS = 16


def _ffn_up_fused(hn, w_up, conv_w, conv_b, tn, name):
    T, D = hn.shape
    S, _, C = w_up.shape
    F = S * C // 2
    K = conv_w.shape[0]
    tm = min(FUSED_ROWS, T)
    tn = _fit([C], tn)
    per, nj = C // tn, F // tn
    chunk = min(FUSED_CHUNK, tm)

    def body(a_ref, ap_ref, bg_ref, bv_ref, wg_ref, wv_ref, cg_ref, cv_ref, ug_ref, uv_ref, o_ref, gext, vext):
        keep = pl.program_id(0) > 0
        mats = ((bg_ref[...], wg_ref[...], cg_ref[...], ug_ref, gext), (bv_ref[...], wv_ref[...], cv_ref[...], uv_ref, vext))
        for b, _, _, _, ext in mats:
            ext[pl.ds(0, HALO), :] = jnp.where(keep, _bdot(ap_ref[...], b)[BF16_ROWS - HALO:], 0.0)
        for c0 in range(0, tm, chunk):
            a = a_ref[pl.ds(c0, chunk), :]
            for b, _, _, u_ref, ext in mats:
                u = _bdot(a, b)
                u_ref[pl.ds(c0, chunk), :] = u
                ext[pl.ds(HALO + c0, chunk), :] = u
            for r0, rows in _strips(chunk):
                gate, val = [_conv_taps(ext, w, rows, HALO + c0 + r0) + cb for _, w, cb, _, ext in mats]
                o_ref[pl.ds(c0 + r0, rows), :] = (_gelu(gate)[0] * val).astype(BF16)

    out = pl.BlockSpec((tm, tn), lambda i, j: (i, j))
    return pl.pallas_call(
        body, name=name, grid=(T // tm, nj),
        in_specs=[pl.BlockSpec((tm, D), lambda i, j: (i, 0)),
                  pl.BlockSpec((BF16_ROWS, D), lambda i, j: (jnp.maximum(i * (tm // BF16_ROWS) - 1, 0), 0)),
                  pl.BlockSpec((None, D, tn), lambda i, j: (j // per, 0, j % per)),
                  pl.BlockSpec((None, D, tn), lambda i, j: ((nj + j) // per, 0, (nj + j) % per)),
                  pl.BlockSpec((K, tn), lambda i, j: (0, j)), pl.BlockSpec((K, tn), lambda i, j: (0, nj + j)),
                  pl.BlockSpec((1, tn), lambda i, j: (0, j)), pl.BlockSpec((1, tn), lambda i, j: (0, nj + j))],
        out_specs=(out, out, out),
        out_shape=(jax.ShapeDtypeStruct((T, F), F32), jax.ShapeDtypeStruct((T, F), F32), jax.ShapeDtypeStruct((T, F), BF16)),
        scratch_shapes=[pltpu.VMEM((tm + HALO, tn), F32)] * 2, compiler_params=_params(2),
    )(hn, hn, w_up, w_up, conv_w, conv_w, conv_b, conv_b)


def _ffn_mid_bwd(up_g, up_v, conv_w, conv_b, da, bt, bc, name):
    T, F = up_g.shape
    nc = F // bc
    K = conv_w.shape[0]
    nb = T // bt
    n_ext = bt + HALO

    def body(g_ref, gp_ref, gn_ref, v_ref, vp_ref, vn_ref, da_ref, dan_ref, wg_ref, wv_ref, bg_ref, bv_ref,
             dg_ref, dv_ref, dwg_ref, dwv_ref, dbg_ref, dbv_ref, gext, vext, dgext, dvext):
        i = pl.program_id(1)
        last = i == nb - 1
        for ext, cur, prev, nxt in ((gext, g_ref, gp_ref, gn_ref), (vext, v_ref, vp_ref, vn_ref)):
            ext[pl.ds(0, HALO), :] = jnp.where(i > 0, prev[...], 0.0)
            ext[pl.ds(HALO, bt), :] = cur[...]
            ext[pl.ds(HALO + bt, HALO), :] = jnp.where(last, 0.0, nxt[...])
        wg, wv, bg, bv = wg_ref[...], wv_ref[...], bg_ref[...], bv_ref[...]
        for r0, rows in _strips(bt) + [(bt, HALO)]:
            gate = _conv_taps(gext, wg, rows, HALO + r0) + bg
            val = _conv_taps(vext, wv, rows, HALO + r0) + bv
            dact = da_ref[pl.ds(r0, rows), :] if r0 < bt else jnp.where(last, 0.0, dan_ref[...])
            ge, t, half, x2 = _gelu(gate)
            dgelu = (0.5 * t + 0.5) + (half * (1.0 - t * t)) * (x2 * (3.0 * _GELU_C * _GELU_A) + _GELU_C)
            dgext[pl.ds(r0, rows), :] = dact * val * dgelu
            dvext[pl.ds(r0, rows), :] = dact * ge

        @pl.when(i == 0)
        def _():
            for ref in (dwg_ref, dwv_ref, dbg_ref, dbv_ref):
                ref[...] = jnp.zeros_like(ref)

        for dext, ext, w, dx_ref, dw_ref, db_ref in ((dgext, gext, wg, dg_ref, dwg_ref, dbg_ref),
                                                     (dvext, vext, wv, dv_ref, dwv_ref, dbv_ref)):
            dw, db = [0.0] * K, 0.0
            for r0, rows in _strips(bt):
                dx = None
                for j in range(K):
                    term = dext[pl.ds(r0 + K - 1 - j, rows), :] * w[j:j + 1, :]
                    dx = term if dx is None else dx + term
                dx_ref[pl.ds(r0, rows), :] = dx.astype(BF16)
                dcur = dext[pl.ds(r0, rows), :]
                for j in range(K):
                    dw[j] = dw[j] + jnp.sum(dcur * ext[pl.ds(HALO + r0 - (K - 1) + j, rows), :], axis=0, keepdims=True)
                db = db + jnp.sum(dcur, axis=0, keepdims=True)
            dw_ref[...] += jnp.concatenate(dw, axis=0)
            db_ref[...] += db

    prev = lambda i: jnp.maximum(i * (bt // HALO) - 1, 0)
    nxt = lambda i: jnp.minimum((i + 1) * (bt // HALO), T // HALO - 1)
    cur_g = pl.BlockSpec((bt, bc), lambda j, i: (i, j))
    outs = pl.pallas_call(
        body, name=name, grid=(nc, nb),
        in_specs=[cur_g, pl.BlockSpec((HALO, bc), lambda j, i: (prev(i), j)),
                  pl.BlockSpec((HALO, bc), lambda j, i: (nxt(i), j)),
                  cur_g, pl.BlockSpec((HALO, bc), lambda j, i: (prev(i), j)),
                  pl.BlockSpec((HALO, bc), lambda j, i: (nxt(i), j)),
                  cur_g, pl.BlockSpec((HALO, bc), lambda j, i: (nxt(i), j)),
                  pl.BlockSpec((K, bc), lambda j, i: (0, j)), pl.BlockSpec((K, bc), lambda j, i: (0, nc + j)),
                  pl.BlockSpec((1, bc), lambda j, i: (0, j)), pl.BlockSpec((1, bc), lambda j, i: (0, nc + j))],
        out_specs=(cur_g, cur_g, pl.BlockSpec((K, bc), lambda j, i: (0, j)), pl.BlockSpec((K, bc), lambda j, i: (0, j)),
                   pl.BlockSpec((1, bc), lambda j, i: (0, j)), pl.BlockSpec((1, bc), lambda j, i: (0, j))),
        out_shape=(jax.ShapeDtypeStruct((T, F), BF16), jax.ShapeDtypeStruct((T, F), BF16),
                   jax.ShapeDtypeStruct((K, F), F32), jax.ShapeDtypeStruct((K, F), F32),
                   jax.ShapeDtypeStruct((1, F), F32), jax.ShapeDtypeStruct((1, F), F32)),
        scratch_shapes=[pltpu.VMEM((bt + 2 * HALO, bc), F32)] * 2 + [pltpu.VMEM((n_ext, bc), F32)] * 2,
        compiler_params=_params(2),
    )(up_g, up_g, up_g, up_v, up_v, up_v, da, da, conv_w, conv_w, conv_b, conv_b)
    return outs


def _adam_math(w, g, m, v):
    m2 = ADAM_B1 * m + (1.0 - ADAM_B1) * g
    v2 = ADAM_B2 * v + (1.0 - ADAM_B2) * (g * g)
    m_hat = m2 / (1.0 - ADAM_B1 ** ADAM_STEP)
    v_hat = v2 / (1.0 - ADAM_B2 ** ADAM_STEP)
    return -ADAM_LR * (m_hat / (jnp.sqrt(v_hat) + ADAM_EPS) + ADAM_WD * w), m2, v2


def _adamw_halves(w, mine, theirs, place, m, v, bt, name):
    R, C = w.shape
    h = R // 2
    bt = _fit_rows(h, bt)
    nh = h // bt

    def body(s_ref, w_ref, a_ref, b_ref, m_ref, v_ref, g_ref, d_ref, m2_ref, v2_ref):
        lower = pl.program_id(0) < nh
        gv = jnp.where(lower == (s_ref[0] == 0), a_ref[...], b_ref[...])
        g_ref[...] = gv
        d_ref[...], m2_ref[...], v2_ref[...] = _adam_math(w_ref[...], gv, m_ref[...], v_ref[...])

    full = pl.BlockSpec((bt, C), lambda i, s: (i, 0))

    def half(is_mine):
        def index(i, s):
            used = ((i < nh) == (s[0] == 0)) == is_mine
            return jnp.where(used, i % nh, jnp.where(i < nh, 0, nh - 1)), 0
        return pl.BlockSpec((bt, C), index)

    out = jax.ShapeDtypeStruct((R, C), F32)
    return pl.pallas_call(
        body, name=name,
        grid_spec=pltpu.PrefetchScalarGridSpec(num_scalar_prefetch=1, grid=(2 * nh,),
                                               in_specs=[full, half(True), half(False), full, full],
                                               out_specs=(full,) * 4),
        out_shape=(out,) * 4, compiler_params=_params(1),
    )(place, w, mine, theirs, m, v)


def _adamw(w, g, m, v, bt, name):
    R, C = w.shape
    bt = _fit_rows(R, bt)
    blk = pl.BlockSpec((bt, C), lambda i: (i, 0))

    def body(w_ref, g_ref, m_ref, v_ref, d_ref, m2_ref, v2_ref):
        gv = g_ref[...]
        m2 = ADAM_B1 * m_ref[...] + (1.0 - ADAM_B1) * gv
        v2 = ADAM_B2 * v_ref[...] + (1.0 - ADAM_B2) * (gv * gv)
        m_hat = m2 / (1.0 - ADAM_B1 ** ADAM_STEP)
        v_hat = v2 / (1.0 - ADAM_B2 ** ADAM_STEP)
        d_ref[...] = -ADAM_LR * (m_hat / (jnp.sqrt(v_hat) + ADAM_EPS) + ADAM_WD * w_ref[...])
        m2_ref[...] = m2
        v2_ref[...] = v2

    out = jax.ShapeDtypeStruct((R, C), F32)
    return pl.pallas_call(body, name=name, grid=(R // bt,), in_specs=[blk] * 4, out_specs=(blk,) * 3,
                          out_shape=(out,) * 3, compiler_params=_params(1))(w, g, m, v)


def _place():
    x, y, c = lax.axis_index("x"), lax.axis_index("y"), lax.axis_index("c")
    chips = [(1 - x, y), (x, 1 - y), (1 - x, 1 - y)]
    return x, y, c, chips


_HBM = pl.BlockSpec(memory_space=pltpu.HBM)


def _add_cores(buf, other, place, own_only, out_dtype, bt, name):
    n, _, h, cols = buf.shape
    bt = _fit_rows(h, bt)
    row = (lambda k, s: s[1]) if own_only else (lambda k, s: k)

    def body(s_ref, a_ref, b_ref, o_ref):
        o_ref[...] = (a_ref[...] + b_ref[...]).astype(out_dtype)

    return pl.pallas_call(
        body, name=name,
        grid_spec=pltpu.PrefetchScalarGridSpec(
            num_scalar_prefetch=1, grid=(1 if own_only else n, h // bt),
            in_specs=[pl.BlockSpec((None, None, bt, cols), lambda k, i, s: (row(k, s), s[0], i, 0)),
                      pl.BlockSpec((None, bt, cols), lambda k, i, s: (row(k, s), i, 0))],
            out_specs=(pl.BlockSpec((bt, cols), lambda k, i, s: (i, 0)) if own_only
                       else pl.BlockSpec((None, bt, cols), lambda k, i, s: (k, i, 0)))),
        out_shape=jax.ShapeDtypeStruct((h, cols) if own_only else (n, h, cols), out_dtype),
        compiler_params=_params(2),
    )(place, buf, other)


def _add_chips(own, others, bt, name):
    h, cols = own.shape
    bt = _fit_rows(h, bt)

    def body(a_ref, b_ref, o_ref):
        o_ref[...] = ((a_ref[...] + b_ref[0].astype(F32)) + b_ref[1].astype(F32)) + b_ref[2].astype(F32)

    return pl.pallas_call(
        body, name=name, grid=(h // bt,),
        in_specs=[pl.BlockSpec((bt, cols), lambda i: (i, 0)), pl.BlockSpec((3, bt, cols), lambda i: (0, i, 0))],
        out_specs=pl.BlockSpec((bt, cols), lambda i: (i, 0)),
        out_shape=jax.ShapeDtypeStruct((h, cols), F32), compiler_params=_params(1),
    )(own, others)


def _gather_plan(bufs, split):
    n = len(bufs)

    def phases(ins, outs, sems):
        send, recv = sems
        x, y, c, chips = _place()
        me = 2 * x + y

        def rows(b, core):
            h = bufs[b].shape[0] // 2
            return pl.ds(core * h, h) if split[b] else pl.ds(0, bufs[b].shape[0])

        def over_ici(b, j, block):
            px, py = chips[j]
            return pltpu.make_async_remote_copy(
                src_ref=ins[b].at[rows(b, c)], dst_ref=outs[b].at[block, rows(b, c)], send_sem=send.at[b, j],
                recv_sem=recv.at[b, j], device_id=(px, py, c), device_id_type=MESH)

        def over_d2d(b, j, block, core):
            return pltpu.make_async_remote_copy(
                src_ref=outs[b].at[block, rows(b, core)], dst_ref=outs[b].at[block, rows(b, core)],
                send_sem=send.at[b, 3 + j], recv_sem=recv.at[b, 3 + j], device_id=(x, y, 1 - c), device_id_type=MESH)

        pairs = [(b, j) for b in range(n) for j in range(3)]
        source = lambda j: 2 * chips[j][0] + chips[j][1]

        def first():
            for b, j in pairs:
                over_ici(b, j, me).start()

        def mid():
            for b, j in pairs:
                over_ici(b, j, source(j)).wait_recv()
                if split[b]:
                    over_d2d(b, j, source(j), c).start()

        def last():
            for b, j in pairs:
                if split[b]:
                    over_d2d(b, j, source(j), 1 - c).wait_recv()
            for b, j in pairs:
                over_ici(b, j, me).wait_send()
                if split[b]:
                    over_d2d(b, j, source(j), c).wait_send()

        return first, mid, last

    return ([jax.ShapeDtypeStruct((4,) + b.shape, b.dtype) for b in bufs],
            [pltpu.SemaphoreType.DMA((n, 6)), pltpu.SemaphoreType.DMA((n, 6))], phases)


def _exchange_plan(n, out_shapes, copy):
    def phases(ins, outs, sems):
        send, recv = sems
        place = _place()

        def first():
            for b in range(n):
                copy(b, ins, outs, send, recv, place).start()

        def last():
            for b in range(n):
                copy(b, ins, outs, send, recv, place).wait()

        return first, (lambda: None), last

    return out_shapes, [pltpu.SemaphoreType.DMA((n,)), pltpu.SemaphoreType.DMA((n,))], phases


def _swap_plan(bufs):
    def copy(b, ins, outs, send, recv, place):
        x, y, c, _ = place
        h = bufs[b].shape[1] // 2
        return pltpu.make_async_remote_copy(
            src_ref=ins[b].at[:, pl.ds((1 - c) * h, h)], dst_ref=outs[b], send_sem=send.at[b], recv_sem=recv.at[b],
            device_id=(x, y, 1 - c), device_id_type=MESH)

    shapes = [jax.ShapeDtypeStruct((b.shape[0], b.shape[1] // 2, b.shape[2]), b.dtype) for b in bufs]
    return _exchange_plan(len(bufs), shapes, copy)


def _scatter_plan(bufs):
    def copy(t, ins, outs, send, recv, place):
        x, y, c, chips = place
        b, j = divmod(t, 3)
        px, py = chips[j]
        return pltpu.make_async_remote_copy(
            src_ref=ins[b].at[2 * px + py], dst_ref=outs[b].at[j], send_sem=send.at[t], recv_sem=recv.at[t],
            device_id=(px, py, c), device_id_type=MESH)

    shapes = [jax.ShapeDtypeStruct((3,) + b.shape[1:], b.dtype) for b in bufs]
    return _exchange_plan(3 * len(bufs), shapes, copy)


def _join_plan(halves):
    def copy(b, ins, outs, send, recv, place):
        x, y, c, _ = place
        return pltpu.make_async_remote_copy(
            src_ref=ins[b], dst_ref=outs[b], send_sem=send.at[b], recv_sem=recv.at[b],
            device_id=(x, y, 1 - c), device_id_type=MESH)

    return _exchange_plan(len(halves), [jax.ShapeDtypeStruct(b.shape, b.dtype) for b in halves], copy)


def _run_plan(arrays, plan, name):
    out_shapes, sems, phases = plan
    n, m = len(arrays), len(out_shapes)

    def body(*refs):
        for phase in phases(refs[:n], refs[n:n + m], refs[n + m:]):
            phase()

    return pl.pallas_call(
        body, name=name, in_specs=[_HBM] * n, out_specs=[_HBM] * m, out_shape=out_shapes, scratch_shapes=sems,
        compiler_params=pltpu.CompilerParams(has_side_effects=True),
    )(*arrays)


def _fit_rows(n, target):
    for q in (2 * HALO, HALO):
        for t in range(min(n, target) // q * q, 0, -q):
            if n % t == 0:
                return t
    raise ValueError((n, target))


def _allreduce_small(buf, name):
    R, lanes = buf.shape

    def body(in_ref, out_ref, land, send, recv):
        x, y, c, _ = _place()
        me = 4 * x + 2 * y + c
        land[me] = in_ref[...]
        cps = []
        for r in range(1, 8):
            px, py, pc = x ^ (r >> 2), y ^ ((r >> 1) & 1), c ^ (r & 1)
            cp = pltpu.make_async_remote_copy(
                src_ref=in_ref, dst_ref=land.at[me], send_sem=send.at[r - 1], recv_sem=recv.at[me],
                device_id=(px, py, pc), device_id_type=MESH)
            cp.start()
            cps.append(cp)
        for r in range(1, 8):
            peer = 4 * (x ^ (r >> 2)) + 2 * (y ^ ((r >> 1) & 1)) + (c ^ (r & 1))
            pltpu.make_async_remote_copy(
                src_ref=in_ref, dst_ref=land.at[peer], send_sem=send.at[r - 1], recv_sem=recv.at[peer],
                device_id=(x, y, c), device_id_type=MESH).wait_recv()
        for cp in cps:
            cp.wait_send()
        acc = land[0]
        for d in range(1, 8):
            acc = acc + land[d]
        out_ref[...] = acc

    vm = pl.BlockSpec(memory_space=pltpu.VMEM)
    return pl.pallas_call(
        body, name=name, in_specs=[vm], out_specs=vm, out_shape=jax.ShapeDtypeStruct((R, lanes), buf.dtype),
        scratch_shapes=[pltpu.VMEM((8, R, lanes), buf.dtype), pltpu.SemaphoreType.DMA((7,)), pltpu.SemaphoreType.DMA((8,))],
        compiler_params=pltpu.CompilerParams(has_side_effects=True, vmem_limit_bytes=VMEM_LIMIT),
    )(buf)


ROW_BLOCK = 256
SB_BLOCK = 256
MM_TM, MM_TN, MM_TK = 1024, 512, 512
FFN_COLS = 512


def _lane_pad(vec, start):
    return jnp.pad(vec, ((0, 0), (start, LANES - start - vec.shape[1])))


WEIGHTS = ("w_in", "sb_out_gain", "dn_conv_w", "dn_a_log", "dn_dt_bias", "dn_out_gain", "w_out", "ln_mix_pre",
           "ln_mix_post", "w_up", "ffn_conv_w", "ffn_conv_b", "w_down", "ln_ffn_pre", "ln_ffn_post")
MATRICES = {"w_in": 1, "w_out": 0, "w_up": 1, "w_down": 0}
CONV_SHARDED = ("dn_conv_w", "ffn_conv_w")
SMALL = tuple(n for n in WEIGHTS if n not in MATRICES)
N_CHIPS = 4
ADAM_ROWS = 128


def _pack(arrs, quantum):
    rows, layout, off = [], [], 0
    for a in arrs:
        n = int(np.prod(a.shape))
        r = -(-n // LANES)
        r = -(-r // HALO) * HALO
        rows.append(jnp.pad(a.reshape(-1), (0, r * LANES - n)).reshape(r, LANES))
        layout.append((off, r, n, a.shape))
        off += r
    total = -(-off // quantum) * quantum
    if total > off:
        rows.append(jnp.zeros((total - off, LANES), rows[0].dtype))
    return jnp.concatenate(rows, axis=0), layout


def _unpack(packed, layout):
    return [packed[off:off + r].reshape(-1)[:n].reshape(shape) for off, r, n, shape in layout]


UP_TILE = 1408
PAIR_ROWS = 256


PROLOGUE_STEPS = 8


def _prologue(x, gain, shards, comm, name):
    n = len(shards)
    steps = PROLOGUE_STEPS
    while any(s.shape[0] % (steps * BF16_ROWS) for s in [x] + list(shards)):
        steps //= 2

    def body(*refs):
        ins, outs, _, (first, mid, last) = comm.split(refs, 2 + n, 1 + n, 0)
        i = pl.program_id(0)
        pl.when(i == 0)(first)
        pl.when(i == 3 * steps // 4)(mid)
        xv = ins[0][...]
        outs[0][...] = (xv * lax.rsqrt(jnp.mean(xv * xv, axis=-1, keepdims=True) + EPS) * ins[1][...]).astype(BF16)
        for s_ref, o_ref in zip(ins[2:], outs[1:]):
            o_ref[...] = s_ref[...].astype(BF16)
        pl.when(i == steps - 1)(last)

    rows = lambda a: pl.BlockSpec((a.shape[0] // steps, a.shape[1]), lambda i: (i, 0))
    arrays = [x] + list(shards)
    return comm.call(body, name, (steps,), [rows(x), pl.BlockSpec((1, x.shape[1]), lambda i: (0, 0))] + [rows(s) for s in shards],
                     [rows(a) for a in arrays], [jax.ShapeDtypeStruct(a.shape, BF16) for a in arrays], [],
                     [x, gain] + list(shards))


def _reduce_to_chips(shares, place, names, swap_on, scatter_on):
    from_sibling = swap_on(shares)
    halves = [s.reshape(N_CHIPS, 2, s.shape[1] // 2, s.shape[2]) for s in shares]
    to_chips = [_add_cores(hv, fs, place, False, BF16, PAIR_ROWS, "grad_add_cores_" + n)
                for hv, fs, n in zip(halves, from_sibling, names)]
    own = [_add_cores(hv, fs, place, True, F32, PAIR_ROWS, "grad_add_cores_own_" + n)
           for hv, fs, n in zip(halves, from_sibling, names)]
    return own, scatter_on(to_chips)


def _step(x, xn, target, wt, late, chip, place):
    T, D = x.shape
    W = D // 2
    H = W // HEAD_DIM
    bt = min(ROW_BLOCK, T)
    blk = min(SB_BLOCK, T)
    w_in = wt["w_in"]
    a_log, dt_bias = _lane_pad(wt["dn_a_log"], H), _lane_pad(wt["dn_dt_bias"], H)
    mm = functools.partial(_mm, tm=MM_TM, tn=MM_TN)
    wide = functools.partial(_mm, tm=MM_TM, tn=2 * MM_TN)
    mm_up = functools.partial(_mm, tm=MM_TM, tn=UP_TILE)
    one = lambda a, b, tk=MM_TK: [(a, b, tk, 0, 0)]

    psb = mm(one(xn, w_in, D), "nn", BF16, name="proj_sb", n_window=(0, 3 * W))
    pdn = mm(one(xn, w_in, D), "nn", F32, name="proj_dn", n_window=(3 * W, 4 * W))
    pba = mm(one(xn, w_in, D), "nn", F32, name="proj_ba", n_window=(7 * W, LANES))
    late_names = ("w_out", "w_up", "w_down")
    gathered_with = lambda names: _Comm([late[n] for n in names], _gather_plan([late[n] for n in names], [True] * len(names)))
    own_block_in = lambda theirs, names: [lax.dynamic_update_index_in_dim(t, late[n], chip, 0) for t, n in zip(theirs, names)]
    (o_sb, mix_sb, lt, swept), theirs = _sb_fwd(psb, wt["sb_out_gain"], blk, "sb_fwd", gathered_with(("w_out",)))
    w_out = own_block_in(theirs, ("w_out",))[0].reshape(-1, D)
    qn, kn, vv, bx, gx = _dn_pre_fwd(pdn, pba, wt["dn_conv_w"], a_log, dt_bias, bt, "dn_pre_fwd")
    (o_dn, mix_dn, ss, tms), theirs = _dn_core_fwd(qn, kn, vv, bx, gx, pdn, wt["dn_out_gain"], "dn_core_fwd",
                                                  gathered_with(("w_up", "w_down")))
    w_up, w_down = own_block_in(theirs, ("w_up", "w_down"))
    w_down = w_down.reshape(-1, D)
    F = w_down.shape[0]
    m = wide([(mix_sb, w_out, 2 * MM_TK, 0, 0), (mix_dn, w_out, 2 * MM_TK, 0, W)], "nn", F32, name="out_proj")
    h = _rms_fwd(m, wt["ln_mix_post"], x, F32, bt, "rms_mix_post")
    hn = _rms_fwd(h, wt["ln_ffn_pre"], None, BF16, bt, "rms_ffn_pre")
    bc = min(FFN_COLS, F)
    up_g, up_v, act = _ffn_up_fused(hn, w_up, wt["ffn_conv_w"], wt["ffn_conv_b"], FFN_COLS // 2, "ffn_up")
    f = mm(one(act, w_down, UP_TILE), "nn", F32, name="ffn_down")
    dy, df, g_ffn_post, sq = _loss_head(f, wt["ln_ffn_post"], h, target, bt, "loss_head")
    loss = 0.5 * jnp.sum(sq) / D

    da = mm(one(df, w_down, D), "nt", F32, name="d_act")
    g_w_down = _mm(one(act, df, 2 * MM_TK), "tn", F32, tm=UP_TILE, tn=2 * MM_TN, name="g_w_down")
    dug, duv, dwg, dwv, dbg, dbv = _ffn_mid_bwd(up_g, up_v, wt["ffn_conv_w"], wt["ffn_conv_b"], da, bt, bc, "ffn_mid_bwd")
    shard = w_up.shape[2]
    g_w_up = mm_up(one(hn, dug, 2 * MM_TK), "tn", F32, name="g_w_up_gate", out_shard=shard,
                   into=(lax.empty(w_up.shape, F32), 0))
    g_w_up = mm_up(one(hn, duv, 2 * MM_TK), "tn", F32, name="g_w_up_val", out_shard=shard, into=(g_w_up, F))
    ffn_shares = [g_w_up, g_w_down.reshape(N_CHIPS, -1, D)]
    dhn, ffn_swapped = wide([(dug, w_up, UP_TILE, 0, 0), (duv, w_up, UP_TILE, 0, F)], "nt", F32, name="d_hn",
                            comm=_Comm(ffn_shares, _swap_plan(ffn_shares)))
    dh, g_ffn_pre = _rms_bwd(h, wt["ln_ffn_pre"], dhn, dy, F32, bt, "rms_ffn_pre_bwd")
    dm, g_mix_post = _rms_bwd(m, wt["ln_mix_post"], dh, None, BF16, bt, "rms_mix_post_bwd")
    dmix = mm(one(dm, w_out, D), "nt", F32, name="d_mix")
    g_w_out = jnp.concatenate([wide(one(mix_sb, dm, 2 * MM_TK), "tn", F32, name="g_w_out_sb"),
                               wide(one(mix_dn, dm, 2 * MM_TK), "tn", F32, name="g_w_out_dn")], axis=0)
    shares = [g_w_out.reshape(N_CHIPS, -1, D)] + ffn_shares
    carried = {}

    def swap_on(arrays):
        (carried["do_sb"], carried["g_sb_gain"]), out = _headnorm_bwd(
            o_sb, wt["sb_out_gain"], dmix, bt, "sb_norm_bwd", _Comm(arrays[:1], _swap_plan(arrays[:1])))
        return list(out) + list(ffn_swapped)

    def scatter_on(arrays):
        carried["dn"], out = _dn_core_bwd(qn, kn, vv, bx, gx, pdn, wt["dn_out_gain"], o_dn, dmix, ss, tms, W,
                                          "dn_core_bwd", _Comm(arrays, _scatter_plan(arrays)))
        return out

    early = _reduce_to_chips(shares, place, late_names, swap_on, scatter_on)
    g_sb_gain = carried["g_sb_gain"]
    (dq, dk, dv), _ = _sb_bwd(psb, carried["do_sb"], lt, swept, blk, "sb_bwd", _Comm())
    ddq, ddk, ddv, dbx, dgx, dz, g_dn_gain = carried["dn"]
    dconv, dba, g_dn_conv, g_a_log, g_dt_bias = _dn_pre_bwd(pdn, pba, wt["dn_conv_w"], a_log, dt_bias,
                                                            ddq, ddk, ddv, dbx, dgx, bt, "dn_pre_bwd")
    pieces = [(dq, 0), (dk, W), (dv, 2 * W), (dconv, 3 * W), (dz, 6 * W), (dba, 7 * W)]
    g_w_in = [wide(one(xn, d, 2 * MM_TK), "tn", F32, name=f"g_w_in_{i}") for i, (d, _) in enumerate(pieces)]
    g_w_in[-1] = g_w_in[-1][:, :2 * H]
    g_in = jnp.concatenate(g_w_in, axis=1)

    def with_d_xn(arrays):
        carried["dxn"], out = mm([(d, w_in, 2 * MM_TK, 0, k0) for d, k0 in pieces], "nt", F32, name="d_xn",
                                 comm=_Comm(arrays, _scatter_plan(arrays)))
        return out

    last = _reduce_to_chips([g_in.reshape(D, N_CHIPS, -1).transpose(1, 0, 2)], place, ["w_in"],
                            lambda arrays: _run_plan(arrays, _swap_plan(arrays), "grad_swap_cores"), with_d_xn)
    dx, g_mix_pre = _rms_bwd(x, wt["ln_mix_pre"], carried["dxn"], dh, F32, bt, "rms_mix_pre_bwd")
    exchanged = dict(zip(late_names, zip(*early)))
    exchanged["w_in"] = (last[0][0], last[1][0])

    grads = dict(
        sb_out_gain=g_sb_gain, dn_conv_w=g_dn_conv, dn_a_log=g_a_log[:, H:2 * H],
        dn_dt_bias=g_dt_bias[:, H:2 * H], dn_out_gain=jnp.sum(g_dn_gain, axis=0),
        ln_mix_pre=g_mix_pre, ln_mix_post=g_mix_post,
        ffn_conv_w=jnp.concatenate([dwg, dwv], axis=1), ffn_conv_b=jnp.concatenate([dbg, dbv], axis=1),
        ln_ffn_pre=g_ffn_pre, ln_ffn_post=g_ffn_post)
    return loss, dx, grads, exchanged


def kernel(x, w_in, sb_out_gain, dn_conv_w, dn_a_log, dn_dt_bias, dn_out_gain, w_out, ln_mix_pre, ln_mix_post, w_up, ffn_conv_w, ffn_conv_b, w_down, ln_ffn_pre, ln_ffn_post, loss_target, m_w_in, m_sb_out_gain, m_dn_conv_w, m_dn_a_log, m_dn_dt_bias, m_dn_out_gain, m_w_out, m_ln_mix_pre, m_ln_mix_post, m_w_up, m_ffn_conv_w, m_ffn_conv_b, m_w_down, m_ln_ffn_pre, m_ln_ffn_post, v_w_in, v_sb_out_gain, v_dn_conv_w, v_dn_a_log, v_dn_dt_bias, v_dn_out_gain, v_w_out, v_ln_mix_pre, v_ln_mix_post, v_w_up, v_ffn_conv_w, v_ffn_conv_b, v_w_down, v_ln_ffn_pre, v_ln_ffn_post):
    given = dict(locals())
    wl = {n: given[n][0] for n in WEIGHTS}
    ml = {n: given["m_" + n][0] for n in WEIGHTS}
    vl = {n: given["v_" + n][0] for n in WEIGHTS}
    for d in (wl, ml, vl):
        for n in SMALL:
            if d[n].ndim == 1:
                d[n] = d[n][None]
    cx, cy, cc = lax.axis_index("x"), lax.axis_index("y"), lax.axis_index("c")
    chip = 2 * cx + cy
    D = x.shape[2]
    W = D // 2

    first = ("w_in",) + CONV_SHARDED
    mine = [wl["w_in"].astype(BF16)] + [wl[n] for n in CONV_SHARDED]
    late_names = ("w_out", "w_up", "w_down")
    (xn, *late_shards), theirs = _prologue(x[0], wl["ln_mix_pre"], [wl[n] for n in late_names],
                                           _Comm(mine, _gather_plan(mine, [True, False, False])), "gather_w_in")
    got = {n: lax.dynamic_update_index_in_dim(t, s, chip, 0) for n, t, s in zip(first, theirs, mine)}
    columns = lambda g: g.transpose(1, 0, 2).reshape(g.shape[1], N_CHIPS * g.shape[2])
    wt = {n: wl[n] for n in SMALL}
    w_in_all = columns(got["w_in"])
    wt["w_in"] = jnp.pad(w_in_all, ((0, 0), (0, 7 * W + LANES - w_in_all.shape[1])))
    for n in CONV_SHARDED:
        wt[n] = columns(got[n])
    late = dict(zip(late_names, late_shards))

    place = jnp.stack([cc, chip]).astype(jnp.int32)
    loss, dx, grads, exchanged = _step(x[0], xn, loss_target[0], wt, late, chip, place)
    loss = lax.psum(loss, ("x", "y", "c"))

    names = list(MATRICES)
    reduced = [_add_chips(*exchanged[n], PAIR_ROWS, "grad_add_chips_" + n) for n in names]
    siblings = _run_plan(reduced, _join_plan(reduced), "grad_join_cores")
    gl = {}

    small, small_layout = _pack([grads[n] for n in SMALL], HALO)
    small = _allreduce_small(small, "grad_allreduce_small")
    for n, g in zip(SMALL, _unpack(small, small_layout)):
        if n in CONV_SHARDED:
            size = g.shape[1] // N_CHIPS
            g = lax.dynamic_slice_in_dim(g, chip * size, size, axis=1)
        gl[n] = g

    delta, new_m, new_v = {}, {}, {}
    for n, mine_half, sibling_half in zip(names, reduced, siblings):
        gl[n], delta[n], new_m[n], new_v[n] = _adamw_halves(wl[n], mine_half, sibling_half, place, ml[n], vl[n],
                                                            ADAM_ROWS, "adamw_" + n)
    packs = [_pack([d[n] for n in SMALL], HALO) for d in (wl, gl, ml, vl)]
    outs = _adamw(*[p[0] for p in packs], ADAM_ROWS, "adamw_small")
    for res, o in zip((delta, new_m, new_v), outs):
        res.update(zip(SMALL, _unpack(o, packs[0][1])))

    shaped = lambda d: [d[n].reshape(given[n].shape) for n in WEIGHTS]
    return (loss, dx[None], *shaped(gl), *shaped(delta), *shaped(new_m), *shaped(new_v))
```

```python
import functools

import numpy as np
import jax
import jax.numpy as jnp
from jax import lax
from jax.experimental import pallas as pl
from jax.experimental.pallas import tpu as pltpu

F32 = jnp.float32
BF16 = jnp.bfloat16
HEAD_DIM = 128
CHUNK = 64
ROWS = 4 * CHUNK
N_CHUNKS = ROWS // CHUNK
EPS = 1e-6
EXP_UNDERFLOW = 110.0
LANES = 128
HALO = 8
VMEM_LIMIT = 48 * 1024 * 1024
ADAM_LR, ADAM_B1, ADAM_B2, ADAM_EPS, ADAM_WD, ADAM_STEP = 0.001, 0.9, 0.999, 1e-08, 0.01, 10
MESH = pl.DeviceIdType.MESH

NN = (((1,), (0,)), ((), ()))
NT = (((1,), (1,)), ((), ()))
TN = (((0,), (0,)), ((), ()))


def _params(n_axes):
    return pltpu.CompilerParams(dimension_semantics=("arbitrary",) * n_axes, vmem_limit_bytes=VMEM_LIMIT)


def _bdot(a, b, dims=NN):
    return lax.dot_general(a.astype(BF16), b.astype(BF16), dims, preferred_element_type=F32)


def _split3(a):
    hi = a.astype(BF16)
    r1 = a - hi.astype(F32)
    mid = r1.astype(BF16)
    lo = (r1 - mid.astype(F32)).astype(BF16)
    return hi, mid, lo


def _dot3(a, sel, dims=NN):
    return sum(lax.dot_general(p, sel, dims, preferred_element_type=F32) for p in _split3(a))


def _dot3r(sel, a, dims=NN):
    return sum(lax.dot_general(sel, p, dims, preferred_element_type=F32) for p in _split3(a))


def _iota2(n, m):
    return lax.broadcasted_iota(jnp.int32, (n, m), 0), lax.broadcasted_iota(jnp.int32, (n, m), 1)


def _sigmoid(x):
    return 1.0 / (1.0 + jnp.exp(-x))


def _softplus(x):
    return jnp.maximum(x, 0.0) + jnp.log(1.0 + jnp.exp(-jnp.abs(x)))


def _fit(values, target):
    values = [v for v in (values if isinstance(values, (list, tuple)) else [values]) if v]
    best = None
    for t in range(LANES, min(min(values), target) + 1, LANES):
        if all(v % t == 0 for v in values):
            best = t
    assert best is not None, (values, target)
    return best


def _mm(parts, mode, out_dtype, tm, tn, name, n_window=None, out_shard=None, into=None, comm=None):
    dims = {"nn": NN, "nt": NT, "tn": TN}[mode]
    a0, b0 = parts[0][0], parts[0][1]
    b3 = b0.ndim == 3
    shard_c = b0.shape[2] if b3 else None
    M = a0.shape[1] if mode == "tn" else a0.shape[0]
    if mode == "nt":
        n_full = b0.shape[1] if b3 else b0.shape[0]
    else:
        n_full = b0.shape[0] * b0.shape[2] if b3 else b0.shape[1]
    n0, N = n_window if n_window is not None else (0, n_full)
    out_n0 = into[1] if into is not None else 0
    tm = _fit(M, tm)
    tn = _fit([N, n0, out_n0, out_shard, shard_c if mode != "nt" else None], tn)
    specs_a, specs_b, offs, nks = [], [], [], []
    off = 0
    for a, b, tk, a_k0, b_k0 in parts:
        K = a.shape[0] if mode == "tn" else a.shape[1]
        tk = _fit([K, a_k0, b_k0, shard_c if mode == "nt" else None], tk)
        nk = K // tk
        kk = lambda k, o=off, n=nk: jnp.clip(k - o, 0, n - 1)
        ao, bo, no = a_k0 // tk, b_k0 // tk, n0 // tn
        if mode == "tn":
            specs_a.append(pl.BlockSpec((tk, tm), lambda i, j, k, kk=kk, ao=ao: (kk(k) + ao, i)))
        else:
            specs_a.append(pl.BlockSpec((tm, tk), lambda i, j, k, kk=kk, ao=ao: (i, kk(k) + ao)))
        if mode == "nt":
            if b3:
                per = shard_c // tk
                specs_b.append(pl.BlockSpec((None, tn, tk), lambda i, j, k, kk=kk, bo=bo, per=per:
                                            ((kk(k) + bo) // per, j, (kk(k) + bo) % per)))
            else:
                specs_b.append(pl.BlockSpec((tn, tk), lambda i, j, k, kk=kk, bo=bo: (j, kk(k) + bo)))
        else:
            if b3:
                per = shard_c // tn
                specs_b.append(pl.BlockSpec((None, tk, tn), lambda i, j, k, kk=kk, bo=bo, no=no, per=per:
                                            ((j + no) // per, kk(k) + bo, (j + no) % per)))
            else:
                specs_b.append(pl.BlockSpec((tk, tn), lambda i, j, k, kk=kk, bo=bo, no=no: (kk(k) + bo, j + no)))
        offs.append(off)
        nks.append(nk)
        off += nk
    nk_total = off
    n_parts = len(parts)

    comm = comm if comm is not None else _Comm()
    grid = (M // tm, N // tn, nk_total)
    n_in = 2 * n_parts + (1 if into is not None else 0)

    def body(*refs):
        ins, (o_ref,), scratch, (first, mid, last) = comm.split(refs, n_in, 1, 0 if nk_total == 1 else 1)
        a_refs, b_refs = ins[:n_parts], ins[n_parts:2 * n_parts]
        at = lambda step: functools.reduce(lambda x, y: x & y, [pl.program_id(d) == step[d] for d in range(3)])
        pl.when(at((0, 0, 0)))(first)
        pl.when(at((grid[0] // 2, 0, 0)))(mid)
        if nk_total == 1:
            o_ref[...] = _bdot(a_refs[0][...], b_refs[0][...], dims).astype(out_dtype)
        else:
            acc = scratch[0]
            k = pl.program_id(2)

            @pl.when(k == 0)
            def _():
                acc[...] = jnp.zeros_like(acc)

            for p in range(n_parts):
                @pl.when((k >= offs[p]) & (k < offs[p] + nks[p]))
                def _(p=p):
                    acc[...] += _bdot(a_refs[p][...], b_refs[p][...], dims)

            @pl.when(k == nk_total - 1)
            def _():
                o_ref[...] = acc[...].astype(out_dtype)
        pl.when(at(tuple(g - 1 for g in grid)))(last)

    jo = out_n0 // tn
    if out_shard is not None:
        per_o = out_shard // tn
        out_spec = pl.BlockSpec((None, tm, tn), lambda i, j, k: ((j + jo) // per_o, i, (j + jo) % per_o))
        out_shape = jax.ShapeDtypeStruct((N // out_shard, M, out_shard), out_dtype)
    else:
        out_spec = pl.BlockSpec((tm, tn), lambda i, j, k: (i, j + jo))
        out_shape = jax.ShapeDtypeStruct((M, N), out_dtype)
    ins = [p[0] for p in parts] + [p[1] for p in parts]
    in_specs = specs_a + specs_b
    aliases = {}
    if into is not None:
        out_shape = jax.ShapeDtypeStruct(into[0].shape, into[0].dtype)
        aliases = {len(ins): 0}
        ins.append(into[0])
        in_specs.append(pl.BlockSpec(memory_space=pl.ANY))
    (out,), carried = comm.call(body, name, grid, in_specs, (out_spec,), (out_shape,),
                                [] if nk_total == 1 else [pltpu.VMEM((tm, tn), F32)], ins, aliases)
    return (out, carried) if comm.phases is not None else out


def _rms_bwd_math(xv, g, dy):
    r = lax.rsqrt(jnp.mean(xv * xv, axis=-1, keepdims=True) + EPS)
    n = xv * r
    gy = dy * g
    dx = r * (gy - n * jnp.mean(gy * n, axis=-1, keepdims=True))
    return dx, dy * n


def _rms_bwd(x, gain, dy, resid, out_dtype, bt, name):
    T, D = x.shape
    row = pl.BlockSpec((bt, D), lambda i: (i, 0))
    vec = pl.BlockSpec((1, D), lambda i: (0, 0))

    def body(*refs):
        x_ref, g_ref, dy_ref = refs[0], refs[1], refs[2]
        dx_ref, dg_ref = refs[-2], refs[-1]
        dx, dgp = _rms_bwd_math(x_ref[...], g_ref[...], dy_ref[...].astype(F32))
        if resid is not None:
            dx = refs[3][...] + dx
        dx_ref[...] = dx.astype(out_dtype)

        @pl.when(pl.program_id(0) == 0)
        def _():
            dg_ref[...] = jnp.zeros_like(dg_ref)

        dg_ref[...] += jnp.sum(dgp, axis=0, keepdims=True)

    ins = [x, gain, dy] + ([resid] if resid is not None else [])
    return pl.pallas_call(
        body, name=name, grid=(T // bt,),
        in_specs=[row, vec, row] + ([row] if resid is not None else []),
        out_specs=(row, vec),
        out_shape=(jax.ShapeDtypeStruct((T, D), out_dtype), jax.ShapeDtypeStruct((1, D), F32)),
        compiler_params=_params(1),
    )(*ins)


def _rms_chain_fwd(m, g_post, x, g_pre, bt, name):
    T, D = m.shape
    row = pl.BlockSpec((bt, D), lambda i: (i, 0))
    vec = pl.BlockSpec((1, D), lambda i: (0, 0))

    def body(m_ref, gp_ref, x_ref, gn_ref, h_ref, hn_ref):
        mv = m_ref[...]
        h = x_ref[...] + mv * lax.rsqrt(jnp.mean(mv * mv, axis=-1, keepdims=True) + EPS) * gp_ref[...]
        h_ref[...] = h
        hn_ref[...] = (h * lax.rsqrt(jnp.mean(h * h, axis=-1, keepdims=True) + EPS) * gn_ref[...]).astype(BF16)

    return pl.pallas_call(
        body, name=name, grid=(T // bt,), in_specs=[row, vec, row, vec], out_specs=(row, row),
        out_shape=(jax.ShapeDtypeStruct((T, D), F32), jax.ShapeDtypeStruct((T, D), BF16)), compiler_params=_params(1),
    )(m, g_post, x, g_pre)


def _rms_chain_bwd(h, g_pre, dhn, dy, m, g_post, bt, name):
    T, D = h.shape
    row = pl.BlockSpec((bt, D), lambda i: (i, 0))
    vec = pl.BlockSpec((1, D), lambda i: (0, 0))

    def body(h_ref, gn_ref, dhn_ref, dy_ref, m_ref, gp_ref, dh_ref, dm_ref, dgn_ref, dgp_ref):
        dh, dgn = _rms_bwd_math(h_ref[...], gn_ref[...], dhn_ref[...])
        dh = dy_ref[...] + dh
        dm, dgp = _rms_bwd_math(m_ref[...], gp_ref[...], dh)
        dh_ref[...] = dh
        dm_ref[...] = dm.astype(BF16)

        @pl.when(pl.program_id(0) == 0)
        def _():
            dgn_ref[...] = jnp.zeros_like(dgn_ref)
            dgp_ref[...] = jnp.zeros_like(dgp_ref)

        dgn_ref[...] += jnp.sum(dgn, axis=0, keepdims=True)
        dgp_ref[...] += jnp.sum(dgp, axis=0, keepdims=True)

    return pl.pallas_call(
        body, name=name, grid=(T // bt,), in_specs=[row, vec, row, row, row, vec], out_specs=(row, row, vec, vec),
        out_shape=(jax.ShapeDtypeStruct((T, D), F32), jax.ShapeDtypeStruct((T, D), BF16),
                   jax.ShapeDtypeStruct((1, D), F32), jax.ShapeDtypeStruct((1, D), F32)),
        compiler_params=_params(1),
    )(h, g_pre, dhn, dy, m, g_post)


def _loss_head(f, gain, h, target, bt, name):
    T, D = f.shape
    row = pl.BlockSpec((bt, D), lambda i: (i, 0))
    vec = pl.BlockSpec((1, D), lambda i: (0, 0))

    def body(f_ref, g_ref, h_ref, t_ref, dy_ref, df_ref, dg_ref, sq_ref):
        fv, g = f_ref[...], g_ref[...]
        r = lax.rsqrt(jnp.mean(fv * fv, axis=-1, keepdims=True) + EPS)
        n = fv * r
        err = (h_ref[...] + n * g) - t_ref[...]
        dy = err * (1.0 / D)
        gy = dy * g
        df = r * (gy - n * jnp.mean(gy * n, axis=-1, keepdims=True))
        dy_ref[...] = dy
        df_ref[...] = df.astype(BF16)

        @pl.when(pl.program_id(0) == 0)
        def _():
            dg_ref[...] = jnp.zeros_like(dg_ref)
            sq_ref[...] = jnp.zeros_like(sq_ref)

        dg_ref[...] += jnp.sum(dy * n, axis=0, keepdims=True)
        sq_ref[...] += jnp.sum(err * err, axis=0, keepdims=True)

    return pl.pallas_call(
        body, name=name, grid=(T // bt,), in_specs=[row, vec, row, row], out_specs=(row, row, vec, vec),
        out_shape=(jax.ShapeDtypeStruct((T, D), F32), jax.ShapeDtypeStruct((T, D), BF16),
                   jax.ShapeDtypeStruct((1, D), F32), jax.ShapeDtypeStruct((1, D), F32)),
        compiler_params=_params(1),
    )(f, gain, h, target)


def _sb_logits(q, k, valid):
    z = lax.dot_general(q, k, NT, preferred_element_type=F32) * (HEAD_DIM ** -0.5)
    sp = jnp.log(1.0 + jnp.exp(-jnp.abs(z)))
    lb = jnp.minimum(z, 0.0) - sp
    l1 = -(jnp.maximum(z, 0.0) + sp)
    return lb, (l1 if valid is None else jnp.where(valid, l1, 0.0))


def _masked(valid, x):
    return x if valid is None else jnp.where(valid, x, 0.0)


def _heads_per_step(n_heads):
    return 2 if n_heads % 2 == 0 else 1


def _dot2(a, sel):
    hi = a.astype(BF16)
    lo = (a - hi.astype(F32)).astype(BF16)
    return jnp.dot(hi, sel, preferred_element_type=F32) + jnp.dot(lo, sel, preferred_element_type=F32)


class _Comm:
    def __init__(self, arrays=(), plan=((), (), None)):
        self.arrays = list(arrays)
        self.out_shapes, self.sems, self.phases = list(plan[0]), list(plan[1]), plan[2]

    def split(self, refs, n_in, n_out, n_scratch):
        a, o = len(self.arrays), len(self.out_shapes)
        cuts = np.cumsum([0, n_in, a, n_out, o, n_scratch])
        ins, cin, outs, cout, scratch = (refs[cuts[t]:cuts[t + 1]] for t in range(5))
        if self.phases is None:
            return ins, outs, scratch, (lambda: None,) * 3
        return ins, outs, scratch, self.phases(cin, cout, refs[cuts[5]:])

    def call(self, body, name, grid, in_specs, out_specs, out_shape, scratch_shapes, operands, aliases=None):
        outs = pl.pallas_call(
            body, name=name, grid=grid, in_specs=list(in_specs) + [_HBM] * len(self.arrays),
            out_specs=tuple(out_specs) + (_HBM,) * len(self.out_shapes),
            out_shape=tuple(out_shape) + tuple(self.out_shapes),
            scratch_shapes=list(scratch_shapes) + self.sems, input_output_aliases=aliases or {},
            compiler_params=pltpu.CompilerParams(dimension_semantics=("arbitrary",) * len(grid),
                                                 vmem_limit_bytes=VMEM_LIMIT, has_side_effects=self.phases is not None),
        )(*operands, *self.arrays)
        return outs[:len(out_shape)], outs[len(out_shape):]


def _sb_fwd(qkv, gain, blk, name, comm):
    T, W = qkv.shape[0], qkv.shape[1] // 3
    H = W // HEAD_DIM
    nq = T // blk
    hp = _heads_per_step(H)
    ng = H // hp
    lanes = [slice(t * HEAD_DIM, (t + 1) * HEAD_DIM) for t in range(hp)]

    def body(*refs):
        (q_ref, k_ref, v_ref, g_ref), (o_ref, mix_ref, lt_ref, swept_ref), _, (first, mid, last) = comm.split(refs, 4, 4, 0)
        h, i = pl.program_id(0), pl.program_id(1)
        pl.when((h == 0) & (i == 0))(first)
        pl.when((h == 3 * ng // 4) & (i == 0))(mid)
        q = [q_ref[:, ln] for ln in lanes]
        row, col = _iota2(blk, blk)
        after =(row > col).astype(BF16)

        def step(kb, carry, valid):
            ks = pl.ds(pl.multiple_of(kb * blk, blk), blk)
            out = []
            for t, (run, acc) in enumerate(carry):
                lb, l1 = _sb_logits(q[t], k_ref[ks, lanes[t]], valid)
                att = _masked(valid, jnp.exp(lb + _dot2(l1, after) + run))
                out.append((run + jnp.sum(l1, axis=1, keepdims=True), acc + _bdot(att, v_ref[ks, lanes[t]])))
            return tuple(out)

        zero = (jnp.zeros((blk, 1), F32), jnp.zeros((blk, HEAD_DIM), F32))
        def alive(state):
            jj, c = state
            return (jj < i) & (functools.reduce(jnp.maximum, [jnp.max(run) for run, _ in c]) > -EXP_UNDERFLOW)

        swept, carry = lax.while_loop(alive, lambda st: (st[0] + 1, step(i - 1 - st[0], st[1], None)),
                                      (jnp.int32(0), step(i, (zero,) * hp, col < row)))
        swept_ref[h, i] = swept
        for t, (run, o) in enumerate(carry):
            o_ref[:, lanes[t]] = o
            r = lax.rsqrt(jnp.mean(o * o, axis=-1, keepdims=True) + EPS)
            mix_ref[:, lanes[t]] = (o * r * g_ref[...]).astype(BF16)
            lt_ref[:, lanes[t]] = jnp.broadcast_to(run, (blk, HEAD_DIM))
        pl.when((h == ng - 1) & (i == nq - 1))(last)

    wide = hp * HEAD_DIM
    qb = pl.BlockSpec((blk, wide), lambda h, i: (i, h))
    return comm.call(
        body, name, (ng, nq),
        [qb, pl.BlockSpec((T, wide), lambda h, i: (0, ng + h)),
         pl.BlockSpec((T, wide), lambda h, i: (0, 2 * ng + h)), pl.BlockSpec((1, HEAD_DIM), lambda h, i: (0, 0))],
        (qb, qb, qb, pl.BlockSpec(memory_space=pltpu.SMEM)),
        (jax.ShapeDtypeStruct((T, W), F32), jax.ShapeDtypeStruct((T, W), BF16), jax.ShapeDtypeStruct((T, W), F32),
         jax.ShapeDtypeStruct((ng, nq), jnp.int32)),
        [], (qkv, qkv, qkv, gain))


def _headnorm_bwd(o, gain, dmix, bt, name, comm):
    T, W = o.shape
    H = W // HEAD_DIM
    nt = T // bt
    blk = pl.BlockSpec((bt, HEAD_DIM), lambda i, h: (i, h))
    vec = pl.BlockSpec((1, HEAD_DIM), lambda i, h: (0, 0))

    def body(*refs):
        (o_ref, g_ref, d_ref), (do_ref, dg_ref), _, (first, mid, last) = comm.split(refs, 3, 2, 0)
        i, h = pl.program_id(0), pl.program_id(1)
        pl.when((i == 0) & (h == 0))(first)
        pl.when((i == nt // 2) & (h == 0))(mid)
        do, dgp = _rms_bwd_math(o_ref[...], g_ref[...], d_ref[...])
        do_ref[...] = do

        @pl.when((i == 0) & (h == 0))
        def _():
            dg_ref[...] = jnp.zeros_like(dg_ref)

        dg_ref[...] += jnp.sum(dgp, axis=0, keepdims=True)
        pl.when((i == nt - 1) & (h == H - 1))(last)

    return comm.call(body, name, (nt, H), [blk, vec, blk], (blk, vec),
                     (jax.ShapeDtypeStruct((T, W), F32), jax.ShapeDtypeStruct((1, HEAD_DIM), F32)), [], (o, gain, dmix))


def _sb_bwd(qkv, do, lt, swept, blk, name, comm):
    T, W = qkv.shape[0], qkv.shape[1] // 3
    H = W // HEAD_DIM
    nq = T // blk
    scale = HEAD_DIM ** -0.5
    hp = _heads_per_step(H)
    ng = H // hp
    lanes = [slice(t * HEAD_DIM, (t + 1) * HEAD_DIM) for t in range(hp)]

    def body(*refs):
        ((q_ref, k_ref, v_ref, do_ref, lt_ref, swept_ref), (dq_ref, dk_out, dv_out), (dk_ref, dv_ref),
         (first, mid, last)) = comm.split(refs, 6, 3, 2)
        h, i = pl.program_id(0), pl.program_id(1)
        pl.when((h == 0) & (i == 0))(first)
        pl.when((h == ng // 2) & (i == 0))(mid)

        @pl.when(i == 0)
        def _():
            dk_ref[...] = jnp.zeros_like(dk_ref)
            dv_ref[...] = jnp.zeros_like(dv_ref)

        q = [q_ref[:, ln] for ln in lanes]
        dob = [do_ref[:, ln].astype(BF16) for ln in lanes]
        total = [lt_ref[:, ln][:, :1] for ln in lanes]
        row, col = _iota2(blk, blk)
        upto = (row <= col).astype(BF16)
        before = (row < col).astype(BF16)

        def step(kb, carry, valid):
            ks = pl.ds(pl.multiple_of(kb * blk, blk), blk)
            out = []
            for t, (seen, psum, dq) in enumerate(carry):
                k, v = k_ref[ks, lanes[t]], v_ref[ks, lanes[t]]
                lb, l1 = _sb_logits(q[t], k, valid)
                later = total[t] - seen - _dot2(l1, upto)
                att = _masked(valid, jnp.exp(lb + later))
                p = att * lax.dot_general(dob[t], v, NT, preferred_element_type=F32)
                c = psum + _dot2(p, before)
                sig = jnp.exp(lb)
                dz = (_masked(valid, p * (1.0 - sig) - c * sig) * scale).astype(BF16)
                dq = dq + jnp.dot(dz, k, preferred_element_type=F32)
                dk_ref[ks, lanes[t]] += lax.dot_general(dz, q[t], TN, preferred_element_type=F32)
                dv_ref[ks, lanes[t]] += lax.dot_general(att.astype(BF16), dob[t], TN, preferred_element_type=F32)
                out.append((seen + jnp.sum(l1, axis=1, keepdims=True), psum + jnp.sum(p, axis=1, keepdims=True), dq))
            return tuple(out)

        zero = jnp.zeros((blk, 1), F32)
        start = ((zero, zero, jnp.zeros((blk, HEAD_DIM), F32)),) * hp
        carry = step(i, lax.fori_loop(i - swept_ref[h, i], i, lambda kb, c: step(kb, c, None), start), col < row)
        for t in range(hp):
            dq_ref[:, lanes[t]] = carry[t][2].astype(BF16)

        @pl.when(i == nq - 1)
        def _():
            dk_out[...] = dk_ref[...].astype(BF16)
            dv_out[...] = dv_ref[...].astype(BF16)

        pl.when((h == ng - 1) & (i == nq - 1))(last)

    wide = hp * HEAD_DIM
    qb = pl.BlockSpec((blk, wide), lambda h, i: (i, h))
    head = pl.BlockSpec((T, wide), lambda h, i: (0, h))
    out = jax.ShapeDtypeStruct((T, W), BF16)
    return comm.call(
        body, name, (ng, nq),
        [qb, pl.BlockSpec((T, wide), lambda h, i: (0, ng + h)),
         pl.BlockSpec((T, wide), lambda h, i: (0, 2 * ng + h)), qb, qb, pl.BlockSpec(memory_space=pltpu.SMEM)],
        (qb, head, head), (out, out, out), [pltpu.VMEM((T, wide), F32)] * 2, (qkv, qkv, qkv, do, lt, swept))


def _expanders(H):
    lane = np.arange(H * HEAD_DIM) // HEAD_DIM
    eb = np.zeros((LANES, H * HEAD_DIM), np.float32)
    eg = np.zeros((LANES, H * HEAD_DIM), np.float32)
    eb[lane, np.arange(H * HEAD_DIM)] = 1.0
    eg[H + lane, np.arange(H * HEAD_DIM)] = 1.0
    sb = np.zeros((H * HEAD_DIM, LANES), np.float32)
    sg = np.zeros((H * HEAD_DIM, LANES), np.float32)
    sb[np.arange(H) * HEAD_DIM, np.arange(H)] = 1.0
    sg[np.arange(H) * HEAD_DIM, H + np.arange(H)] = 1.0
    return [jnp.asarray(m, BF16) for m in (eb, eg, sb, sg)]


def _conv_taps(ext_ref, w, n_out, lead):
    K = w.shape[0]
    out = None
    for j in range(K):
        term = ext_ref[pl.ds(lead - (K - 1) + j, n_out), :] * w[j:j + 1, :]
        out = term if out is None else out + term
    return out


STRIP_ROWS = 64


def _strips(n_rows):
    return [(r0, min(STRIP_ROWS, n_rows - r0)) for r0 in range(0, n_rows, STRIP_ROWS)]


def _l2_heads(s, H, fn):
    return jnp.concatenate([fn(s[:, h * HEAD_DIM:(h + 1) * HEAD_DIM]) for h in range(H)], axis=1)


def _dn_pre_fwd(pdn, pba, conv_w, a_log, dt_bias, bt, name):
    T, W = pdn.shape[0], pdn.shape[1] // 4
    H = W // HEAD_DIM
    eb, eg, _, _ = _expanders(H)
    nb = T // bt

    def body(x_ref, prev_ref, ba_ref, w_ref, al_ref, dt_ref, eb_ref, eg_ref,
             q_ref, k_ref, v_ref, bx_ref, gx_ref, ext):
        i = pl.program_id(0)
        ext[pl.ds(0, HALO), :] = jnp.where(i > 0, prev_ref[...], 0.0)
        ext[pl.ds(HALO, bt), :] = x_ref[...]
        c = _conv_taps(ext, w_ref[...], bt, HALO)
        s = c * _sigmoid(c)
        q_ref[...] = _l2_heads(s[:, :W], H, lambda t: t * lax.rsqrt(jnp.sum(t * t, axis=-1, keepdims=True) + EPS)
                               * (HEAD_DIM ** -0.5))
        k_ref[...] = _l2_heads(s[:, W:2 * W], H, lambda t: t * lax.rsqrt(jnp.sum(t * t, axis=-1, keepdims=True) + EPS))
        v_ref[...] = s[:, 2 * W:]
        ba = ba_ref[...]
        beta = _sigmoid(ba)
        graw = -jnp.exp(al_ref[...]) * _softplus(ba + dt_ref[...])
        row, col = _iota2(bt, bt)
        tri = ((row // CHUNK == col // CHUNK) & (row >= col)).astype(BF16)
        gcum = _dot3r(tri, graw)
        bx_ref[...] = _dot3(beta, eb_ref[...])
        gx_ref[...] = _dot3(gcum, eg_ref[...])

    C = 3 * W
    rowb = lambda w: pl.BlockSpec((bt, w), lambda i: (i, 0))
    full = lambda a: pl.BlockSpec(a.shape, lambda i: (0,) * a.ndim)
    out = jax.ShapeDtypeStruct((T, W), F32)
    return pl.pallas_call(
        body, name=name, grid=(nb,),
        in_specs=[rowb(C), pl.BlockSpec((HALO, C), lambda i: (jnp.maximum(i * (bt // HALO) - 1, 0), 0)),
                  rowb(LANES), full(conv_w), full(a_log), full(dt_bias), full(eb), full(eg)],
        out_specs=(rowb(W),) * 5, out_shape=(out,) * 5,
        scratch_shapes=[pltpu.VMEM((bt + HALO, C), F32)], compiler_params=_params(1),
    )(pdn, pdn, pba, conv_w, a_log, dt_bias, eb, eg)


def _dn_pre_bwd(pdn, pba, conv_w, a_log, dt_bias, dq, dk, dv, dbx, dgx, bt, name):
    T, W = pdn.shape[0], pdn.shape[1] // 4
    H = W // HEAD_DIM
    C = 3 * W
    K = conv_w.shape[0]
    _, _, sb, sg = _expanders(H)
    nb = T // bt
    n_ext = bt + HALO

    def body(x_ref, prev_ref, next_ref, ba_ref, w_ref, al_ref, dt_ref, sb_ref, sg_ref,
             dq_ref, dqn_ref, dk_ref, dkn_ref, dv_ref, dvn_ref, dbx_ref, dgx_ref,
             dx_ref, dba_ref, dw_ref, dal_ref, ddt_ref, ext, dext, dcext):
        i = pl.program_id(0)
        last = i == nb - 1
        ext[pl.ds(0, HALO), :] = jnp.where(i > 0, prev_ref[...], 0.0)
        ext[pl.ds(HALO, bt), :] = x_ref[...]
        ext[pl.ds(HALO + bt, HALO), :] = jnp.where(last, 0.0, next_ref[...])
        dext[pl.ds(0, bt), pl.ds(0, W)] = dq_ref[...]
        dext[pl.ds(0, bt), pl.ds(W, W)] = dk_ref[...]
        dext[pl.ds(0, bt), pl.ds(2 * W, W)] = dv_ref[...]
        dext[pl.ds(bt, HALO), pl.ds(0, W)] = jnp.where(last, 0.0, dqn_ref[...])
        dext[pl.ds(bt, HALO), pl.ds(W, W)] = jnp.where(last, 0.0, dkn_ref[...])
        dext[pl.ds(bt, HALO), pl.ds(2 * W, W)] = jnp.where(last, 0.0, dvn_ref[...])
        w = w_ref[...]
        l2_scale = (HEAD_DIM ** -0.5, 1.0, None)

        @pl.when(i == 0)
        def _():
            dw_ref[...] = jnp.zeros_like(dw_ref)
            dal_ref[...] = jnp.zeros_like(dal_ref)
            ddt_ref[...] = jnp.zeros_like(ddt_ref)

        for g in range(C // HEAD_DIM):
            cols = pl.ds(g * HEAD_DIM, HEAD_DIM)
            wg = w[:, g * HEAD_DIM:(g + 1) * HEAD_DIM]
            scale = l2_scale[g // H]
            for r0, rows in _strips(bt) + [(bt, HALO)]:
                c = None
                for j in range(K):
                    term = ext[pl.ds(HALO + r0 - (K - 1) + j, rows), cols] * wg[j:j + 1, :]
                    c = term if c is None else c + term
                sg_c = _sigmoid(c)
                ds = dext[pl.ds(r0, rows), cols]
                if scale is not None:
                    s = c * sg_c
                    r = lax.rsqrt(jnp.sum(s * s, axis=-1, keepdims=True) + EPS)
                    ds = scale * r * (ds - s * (r * r) * jnp.sum(s * ds, axis=-1, keepdims=True))
                dcext[pl.ds(r0, rows), cols] = ds * (sg_c * (1.0 + c * (1.0 - sg_c)))
            dw = [0.0] * K
            for r0, rows in _strips(bt):
                dx = None
                for j in range(K):
                    term = dcext[pl.ds(r0 + K - 1 - j, rows), cols] * wg[j:j + 1, :]
                    dx = term if dx is None else dx + term
                dx_ref[pl.ds(r0, rows), cols] = dx.astype(BF16)
                dcur = dcext[pl.ds(r0, rows), cols]
                for j in range(K):
                    dw[j] = dw[j] + jnp.sum(dcur * ext[pl.ds(HALO + r0 - (K - 1) + j, rows), cols], axis=0, keepdims=True)
            dw_ref[:, cols] += jnp.concatenate(dw, axis=0)

        ba = ba_ref[...]
        beta = _sigmoid(ba)
        al, dtb = al_ref[...], dt_ref[...]
        dbeta = _dot3(dbx_ref[...], sb_ref[...])
        dg = _dot3(dgx_ref[...], sg_ref[...])
        sp = _softplus(ba + dtb)
        da = dg * (-jnp.exp(al)) * _sigmoid(ba + dtb)
        dba_ref[...] = dbeta * beta * (1.0 - beta) + da
        dal_ref[...] += jnp.sum(dg * (-jnp.exp(al)) * sp, axis=0, keepdims=True)
        ddt_ref[...] += jnp.sum(da, axis=0, keepdims=True)

    rowb = lambda w: pl.BlockSpec((bt, w), lambda i: (i, 0))
    full = lambda a: pl.BlockSpec(a.shape, lambda i: (0,) * a.ndim)
    nxt = lambda w: pl.BlockSpec((HALO, w), lambda i: (jnp.minimum((i + 1) * (bt // HALO), T // HALO - 1), 0))
    vec = pl.BlockSpec((1, LANES), lambda i: (0, 0))
    return pl.pallas_call(
        body, name=name, grid=(nb,),
        in_specs=[rowb(C), pl.BlockSpec((HALO, C), lambda i: (jnp.maximum(i * (bt // HALO) - 1, 0), 0)), nxt(C),
                  rowb(LANES), full(conv_w), full(a_log), full(dt_bias), full(sb), full(sg),
                  rowb(W), nxt(W), rowb(W), nxt(W), rowb(W), nxt(W), rowb(W), rowb(W)],
        out_specs=(rowb(C), rowb(LANES), pl.BlockSpec((K, C), lambda i: (0, 0)), vec, vec),
        out_shape=(jax.ShapeDtypeStruct((T, C), BF16), jax.ShapeDtypeStruct((T, LANES), F32),
                   jax.ShapeDtypeStruct((K, C), F32), jax.ShapeDtypeStruct((1, LANES), F32),
                   jax.ShapeDtypeStruct((1, LANES), F32)),
        scratch_shapes=[pltpu.VMEM((bt + 2 * HALO, C), F32), pltpu.VMEM((n_ext, C), F32), pltpu.VMEM((n_ext, C), F32)],
        compiler_params=_params(1),
    )(pdn, pdn, pdn, pba, conv_w, a_log, dt_bias, sb, sg, dq, dq, dk, dk, dv, dv, dbx, dgx)


def _dot_split(a, b):
    a_hi, b_hi = a.astype(BF16), b.astype(BF16)
    a_lo, b_lo = (a - a_hi.astype(F32)).astype(BF16), (b - b_hi.astype(F32)).astype(BF16)
    dot = functools.partial(jnp.dot, preferred_element_type=F32)
    return dot(a_hi, b_hi) + (dot(a_hi, b_lo) + dot(a_lo, b_hi))


def _dn_local(q, k, v, beta, g, tm=None):
    row, col = _iota2(ROWS, ROWS)
    same = (row // CHUNK) == (col // CHUNK)
    causal = same & (row >= col)
    strict = same & (row > col)
    eye = (row == col).astype(F32)
    last_of = (col == (row // CHUNK) * CHUNK + (CHUNK - 1)).astype(BF16)
    eg = jnp.exp(g)
    g_rows = jnp.concatenate([g] * (ROWS // HEAD_DIM), axis=1)
    decay = jnp.where(causal, jnp.exp(jnp.where(causal, g_rows - g_rows.T, 0.0)), 0.0)
    kb = k * beta
    vb = v * beta
    kk = _bdot(kb, k, NT)
    low = jnp.where(strict, kk * decay, 0.0)
    if tm is None:
        pw = -low
        tm = eye + pw
        for _ in range(5):
            pw = _dot_split(pw, pw)
            tm = tm + _dot_split(tm, pw)
    kbg = kb * eg
    u = _bdot(tm, vb)
    w = _bdot(tm, kbg)
    qk = _bdot(q, k, NT)
    qa = jnp.where(causal, qk * decay, 0.0)
    glast = _dot3r(last_of, g)
    e2 = jnp.exp(glast - g)
    return dict(row=row, col=col, same=same, causal=causal, strict=strict, eg=eg, decay=decay, kb=kb, vb=vb, kk=kk,
                tm=tm, kbg=kbg, u=u, w=w, qk=qk, qa=qa, glast=glast, e2=e2, kte=k * e2, qd=q * eg)


def _dn_core_fwd(q, k, v, bx, gx, z, gain, name, comm):
    T, W = q.shape
    H = W // HEAD_DIM
    nb = T // ROWS
    hp = _heads_per_step(H)
    ng = H // hp

    def body(*refs):
        ((q_ref, k_ref, v_ref, b_ref, g_ref, z_ref, gain_ref), (o_ref, mix_ref, ss_ref, tm_ref), (state,),
         (first, mid, last)) = comm.split(refs, 7, 4, 1)
        h, b = pl.program_id(0), pl.program_id(1)
        pl.when((h == 0) & (b == 0))(first)
        pl.when((h == 3 * ng // 4) & (b == 0))(mid)

        @pl.when(b == 0)
        def _():
            state[...] = jnp.zeros_like(state)

        for t in range(hp):
            ln = slice(t * HEAD_DIM, (t + 1) * HEAD_DIM)
            L = _dn_local(q_ref[:, ln], k_ref[:, ln], v_ref[:, ln], b_ref[:, ln], g_ref[:, ln])
            s = state[t]
            vns, qds = [], []
            for c in range(N_CHUNKS):
                rows = slice(c * CHUNK, (c + 1) * CHUNK)
                ss_ref[t, c] = s
                vn = L["u"][rows] - _bdot(L["w"][rows], s)
                qds.append(_bdot(L["qd"][rows], s))
                vns.append(vn)
                s = s * jnp.exp(L["glast"][c * CHUNK:c * CHUNK + 1, :]) + _bdot(L["kte"][rows], vn, TN)
            state[t] = s
            tm_ref[:, t * ROWS:(t + 1) * ROWS] = L["tm"]
            o = jnp.concatenate(qds, axis=0) + _bdot(L["qa"], jnp.concatenate(vns, axis=0))
            o_ref[:, ln] = o
            zz = z_ref[:, ln]
            r = lax.rsqrt(jnp.mean(o * o, axis=-1, keepdims=True) + EPS)
            mix_ref[:, ln] = ((o * r * gain_ref[...]) * (zz * _sigmoid(zz))).astype(BF16)
        pl.when((h == ng - 1) & (b == nb - 1))(last)

    wide = hp * HEAD_DIM
    blk = pl.BlockSpec((ROWS, wide), lambda h, b: (b, h))
    zblk = pl.BlockSpec((ROWS, wide), lambda h, b: (b, 3 * ng + h))
    return comm.call(
        body, name, (ng, nb), [blk] * 5 + [zblk, pl.BlockSpec((1, HEAD_DIM), lambda h, b: (0, 0))],
        (blk, blk, pl.BlockSpec((hp, N_CHUNKS, HEAD_DIM, HEAD_DIM), lambda h, b: (h, b, 0, 0)),
         pl.BlockSpec((ROWS, hp * ROWS), lambda h, b: (b, h))),
        (jax.ShapeDtypeStruct((T, W), F32), jax.ShapeDtypeStruct((T, W), BF16),
         jax.ShapeDtypeStruct((H, T // CHUNK, HEAD_DIM, HEAD_DIM), F32), jax.ShapeDtypeStruct((T, H * ROWS), F32)),
        [pltpu.VMEM((hp, HEAD_DIM, HEAD_DIM), F32)], (q, k, v, bx, gx, z, gain))


def _dn_core_bwd(q, k, v, bx, gx, z, gain, o, dmix, ss, tms, dmix_col0, name, comm):
    T, W = q.shape
    H = W // HEAD_DIM
    nb = T // ROWS
    hp = _heads_per_step(H)
    ng = H // hp
    wide = hp * HEAD_DIM
    c0 = dmix_col0 // wide

    def body(*refs):
        ((q_ref, k_ref, v_ref, b_ref, g_ref, z_ref, gain_ref, o_ref, dm_ref, ss_ref, tm_ref),
         (dq_ref, dk_ref, dv_ref, dbx_ref, dgx_ref, dz_ref, dgain_ref), (dstate,),
         (first, mid, last)) = comm.split(refs, 11, 7, 1)
        pl.when((pl.program_id(0) == 0) & (pl.program_id(1) == 0))(first)
        pl.when((pl.program_id(0) == ng // 2) & (pl.program_id(1) == 0))(mid)

        @pl.when(pl.program_id(1) == 0)
        def _():
            dstate[...] = jnp.zeros_like(dstate)
            dgain_ref[...] = jnp.zeros_like(dgain_ref)

        refs = (q_ref, k_ref, v_ref, b_ref, g_ref, z_ref, gain_ref, o_ref, dm_ref, ss_ref, tm_ref,
                dq_ref, dk_ref, dv_ref, dbx_ref, dgx_ref, dz_ref, dgain_ref, dstate)
        for t in range(hp):
            one_head(t, *refs)
        pl.when((pl.program_id(0) == ng - 1) & (pl.program_id(1) == nb - 1))(last)

    def one_head(t, q_ref, k_ref, v_ref, b_ref, g_ref, z_ref, gain_ref, o_ref, dm_ref, ss_ref, tm_ref,
                 dq_ref, dk_ref, dv_ref, dbx_ref, dgx_ref, dz_ref, dgain_ref, dstate):
        ln = slice(t * HEAD_DIM, (t + 1) * HEAD_DIM)
        qv, kv, vv, beta, g = q_ref[:, ln], k_ref[:, ln], v_ref[:, ln], b_ref[:, ln], g_ref[:, ln]
        gain_v = gain_ref[...]
        ov, zz, dm = o_ref[:, ln], z_ref[:, ln], dm_ref[:, ln]
        r = lax.rsqrt(jnp.mean(ov * ov, axis=-1, keepdims=True) + EPS)
        n = ov * r
        sgz = _sigmoid(zz)
        d_on = dm * (zz * sgz)
        dz_ref[:, ln] = (dm * (n * gain_v) * (sgz * (1.0 + zz * (1.0 - sgz)))).astype(BF16)
        dgain_ref[t] += jnp.sum(d_on * n, axis=0, keepdims=True)
        gy = d_on * gain_v
        do = r * (gy - n * jnp.mean(gy * n, axis=-1, keepdims=True))

        L = _dn_local(qv, kv, vv, beta, g, tm_ref[:, t * ROWS:(t + 1) * ROWS])
        causal, strict = L["causal"], L["strict"]
        u, w, qa, qd, kte, tm = L["u"], L["w"], L["qa"], L["qd"], L["kte"], L["tm"]
        s_in = [ss_ref[t, c] for c in range(N_CHUNKS)]
        vn = [u[c * CHUNK:(c + 1) * CHUNK] - _bdot(w[c * CHUNK:(c + 1) * CHUNK], s_in[c]) for c in range(N_CHUNKS)]
        vn_all = jnp.concatenate(vn, axis=0)
        qat_do = _bdot(qa, do, TN)
        d_qa = jnp.where(causal, _bdot(do, vn_all, NT), 0.0)
        ds = dstate[t]
        d_vn, d_kte, d_qd, d_w, d_gl = ([None] * N_CHUNKS for _ in range(5))
        for c in reversed(range(N_CHUNKS)):
            rows = slice(c * CHUNK, (c + 1) * CHUNK)
            egl = jnp.exp(L["glast"][c * CHUNK:c * CHUNK + 1, :])
            d_vn[c] = qat_do[rows] + _bdot(kte[rows], ds)
            d_kte[c] = _bdot(vn[c], ds, NT)
            d_gl[c] = jnp.sum(jnp.sum(ds * s_in[c], axis=1, keepdims=True), axis=0, keepdims=True) * egl
            d_qd[c] = _bdot(do[rows], s_in[c], NT)
            d_w[c] = -_bdot(d_vn[c], s_in[c], NT)
            ds = ds * egl + _bdot(qd[rows], do[rows], TN) - _bdot(w[rows], d_vn[c], TN)
        dstate[t] = ds
        d_u = jnp.concatenate(d_vn, axis=0)
        d_w = jnp.concatenate(d_w, axis=0)
        d_qd = jnp.concatenate(d_qd, axis=0)
        d_kte = jnp.concatenate(d_kte, axis=0)

        d_tm = _bdot(d_u, L["vb"], NT) + _bdot(d_w, L["kbg"], NT)
        d_vb = _bdot(tm, d_u, TN)
        d_kbg = _bdot(tm, d_w, TN)
        d_low = jnp.where(strict, -_bdot(_bdot(tm, d_tm, TN), tm, NT), 0.0)
        decay = L["decay"]
        d_kk = d_low * decay
        d_qk = d_qa * decay
        d_decay = d_low * L["kk"] + d_qa * L["qk"]
        eg, e2 = L["eg"], L["e2"]
        d_kb = _bdot(d_kk, kv) + d_kbg * eg
        dk_ref[:, ln] = _bdot(d_kk, L["kb"], TN) + _bdot(d_qk, qv, TN) + d_kb * beta + d_kte * e2
        dq_ref[:, ln] = _bdot(d_qk, kv) + d_qd * eg
        dv_ref[:, ln] = d_vb * beta
        rsum = lambda a: jnp.sum(a, axis=-1, keepdims=True)
        dbx_ref[:, ln] = jnp.broadcast_to(rsum(d_kb * kv) + rsum(d_vb * vv), (ROWS, HEAD_DIM))
        d_eg = rsum(d_kbg * L["kb"]) + rsum(d_qd * qv)
        t2 = rsum(d_kte * kv) * e2
        ed = d_decay * decay
        d_g = d_eg * eg - t2 + rsum(ed) - rsum(ed.T)
        row, col = L["row"], L["col"]
        chunk_sum = L["same"].astype(BF16)
        lane_row = lax.broadcasted_iota(jnp.int32, (ROWS, HEAD_DIM), 0)
        is_last = (lane_row % CHUNK) == (CHUNK - 1)
        d_glast = _dot3r(chunk_sum, t2)
        for c in range(N_CHUNKS):
            d_glast = d_glast + jnp.where(lane_row // CHUNK == c, d_gl[c], 0.0)
        d_g = d_g + jnp.where(is_last, d_glast, 0.0)
        suffix = (L["same"] & (col >= row)).astype(BF16)
        dgx_ref[:, ln] = _dot3r(suffix, d_g)

    rev = lambda b: nb - 1 - b
    blk = pl.BlockSpec((ROWS, wide), lambda h, b: (rev(b), h))
    zblk = pl.BlockSpec((ROWS, wide), lambda h, b: (rev(b), 3 * ng + h))
    dmblk = pl.BlockSpec((ROWS, wide), lambda h, b: (rev(b), c0 + h))
    out = jax.ShapeDtypeStruct((T, W), F32)
    return comm.call(
        body, name, (ng, nb),
        [blk] * 5 + [zblk, pl.BlockSpec((1, HEAD_DIM), lambda h, b: (0, 0)), blk, dmblk,
                     pl.BlockSpec((hp, N_CHUNKS, HEAD_DIM, HEAD_DIM), lambda h, b: (h, rev(b), 0, 0)),
                     pl.BlockSpec((ROWS, hp * ROWS), lambda h, b: (rev(b), h))],
        (blk,) * 6 + (pl.BlockSpec((hp, 1, HEAD_DIM), lambda h, b: (h, 0, 0)),),
        (out,) * 5 + (jax.ShapeDtypeStruct((T, W), BF16), jax.ShapeDtypeStruct((H, 1, HEAD_DIM), F32)),
        [pltpu.VMEM((hp, HEAD_DIM, HEAD_DIM), F32)], (q, k, v, bx, gx, z, gain, o, dmix, ss, tms))


_GELU_C = 0.7978845608028654
_GELU_A = 0.044715


def _gelu(x):
    x2 = x * x
    t = jnp.tanh(x * (x2 * (_GELU_C * _GELU_A) + _GELU_C))
    half = 0.5 * x
    return half + half * t, t, half, x2


FUSED_ROWS = 1024
FUSED_CHUNK = 256
BF16_ROWS = 16


def _ffn_up_fused(hn, w_up, conv_w, conv_b, tn, name):
    T, D = hn.shape
    S, _, C = w_up.shape
    F = S * C // 2
    K = conv_w.shape[0]
    tm = min(FUSED_ROWS, T)
    tn = _fit([C], tn)
    per, nj = C // tn, F // tn
    chunk = min(FUSED_CHUNK, tm)

    def body(a_ref, ap_ref, bg_ref, bv_ref, wg_ref, wv_ref, cg_ref, cv_ref, ug_ref, uv_ref, o_ref, gext, vext):
        keep = pl.program_id(0) > 0
        mats = ((bg_ref[...], wg_ref[...], cg_ref[...], ug_ref, gext), (bv_ref[...], wv_ref[...], cv_ref[...], uv_ref, vext))
        for b, _, _, _, ext in mats:
            ext[pl.ds(0, HALO), :] = jnp.where(keep, _bdot(ap_ref[...], b)[BF16_ROWS - HALO:], 0.0)
        for c0 in range(0, tm, chunk):
            a = a_ref[pl.ds(c0, chunk), :]
            for b, _, _, u_ref, ext in mats:
                u = _bdot(a, b)
                u_ref[pl.ds(c0, chunk), :] = u
                ext[pl.ds(HALO + c0, chunk), :] = u
            for r0, rows in _strips(chunk):
                gate, val = [_conv_taps(ext, w, rows, HALO + c0 + r0) + cb for _, w, cb, _, ext in mats]
                o_ref[pl.ds(c0 + r0, rows), :] = (_gelu(gate)[0] * val).astype(BF16)

    out = pl.BlockSpec((tm, tn), lambda i, j: (i, j))
    return pl.pallas_call(
        body, name=name, grid=(T // tm, nj),
        in_specs=[pl.BlockSpec((tm, D), lambda i, j: (i, 0)),
                  pl.BlockSpec((BF16_ROWS, D), lambda i, j: (jnp.maximum(i * (tm // BF16_ROWS) - 1, 0), 0)),
                  pl.BlockSpec((None, D, tn), lambda i, j: (j // per, 0, j % per)),
                  pl.BlockSpec((None, D, tn), lambda i, j: ((nj + j) // per, 0, (nj + j) % per)),
                  pl.BlockSpec((K, tn), lambda i, j: (0, j)), pl.BlockSpec((K, tn), lambda i, j: (0, nj + j)),
                  pl.BlockSpec((1, tn), lambda i, j: (0, j)), pl.BlockSpec((1, tn), lambda i, j: (0, nj + j))],
        out_specs=(out, out, out),
        out_shape=(jax.ShapeDtypeStruct((T, F), F32), jax.ShapeDtypeStruct((T, F), F32), jax.ShapeDtypeStruct((T, F), BF16)),
        scratch_shapes=[pltpu.VMEM((tm + HALO, tn), F32)] * 2, compiler_params=_params(2),
    )(hn, hn, w_up, w_up, conv_w, conv_w, conv_b, conv_b)


def _ffn_mid_bwd(up_g, up_v, conv_w, conv_b, da, bt, bc, name):
    T, F = up_g.shape
    nc = F // bc
    K = conv_w.shape[0]
    nb = T // bt
    n_ext = bt + HALO

    def body(g_ref, gp_ref, gn_ref, v_ref, vp_ref, vn_ref, da_ref, dan_ref, wg_ref, wv_ref, bg_ref, bv_ref,
             dg_ref, dv_ref, dwg_ref, dwv_ref, dbg_ref, dbv_ref, gext, vext, dgext, dvext):
        i = pl.program_id(1)
        last = i == nb - 1
        for ext, cur, prev, nxt in ((gext, g_ref, gp_ref, gn_ref), (vext, v_ref, vp_ref, vn_ref)):
            ext[pl.ds(0, HALO), :] = jnp.where(i > 0, prev[...], 0.0)
            ext[pl.ds(HALO, bt), :] = cur[...]
            ext[pl.ds(HALO + bt, HALO), :] = jnp.where(last, 0.0, nxt[...])
        wg, wv, bg, bv = wg_ref[...], wv_ref[...], bg_ref[...], bv_ref[...]
        for r0, rows in _strips(bt) + [(bt, HALO)]:
            gate = _conv_taps(gext, wg, rows, HALO + r0) + bg
            val = _conv_taps(vext, wv, rows, HALO + r0) + bv
            dact = da_ref[pl.ds(r0, rows), :] if r0 < bt else jnp.where(last, 0.0, dan_ref[...])
            ge, t, half, x2 = _gelu(gate)
            dgelu = (0.5 * t + 0.5) + (half * (1.0 - t * t)) * (x2 * (3.0 * _GELU_C * _GELU_A) + _GELU_C)
            dgext[pl.ds(r0, rows), :] = dact * val * dgelu
            dvext[pl.ds(r0, rows), :] = dact * ge

        @pl.when(i == 0)
        def _():
            for ref in (dwg_ref, dwv_ref, dbg_ref, dbv_ref):
                ref[...] = jnp.zeros_like(ref)

        for dext, ext, w, dx_ref, dw_ref, db_ref in ((dgext, gext, wg, dg_ref, dwg_ref, dbg_ref),
                                                     (dvext, vext, wv, dv_ref, dwv_ref, dbv_ref)):
            dw, db = [0.0] * K, 0.0
            for r0, rows in _strips(bt):
                dx = None
                for j in range(K):
                    term = dext[pl.ds(r0 + K - 1 - j, rows), :] * w[j:j + 1, :]
                    dx = term if dx is None else dx + term
                dx_ref[pl.ds(r0, rows), :] = dx.astype(BF16)
                dcur = dext[pl.ds(r0, rows), :]
                for j in range(K):
                    dw[j] = dw[j] + jnp.sum(dcur * ext[pl.ds(HALO + r0 - (K - 1) + j, rows), :], axis=0, keepdims=True)
                db = db + jnp.sum(dcur, axis=0, keepdims=True)
            dw_ref[...] += jnp.concatenate(dw, axis=0)
            db_ref[...] += db

    prev = lambda i: jnp.maximum(i * (bt // HALO) - 1, 0)
    nxt = lambda i: jnp.minimum((i + 1) * (bt // HALO), T // HALO - 1)
    cur_g = pl.BlockSpec((bt, bc), lambda j, i: (i, j))
    outs = pl.pallas_call(
        body, name=name, grid=(nc, nb),
        in_specs=[cur_g, pl.BlockSpec((HALO, bc), lambda j, i: (prev(i), j)),
                  pl.BlockSpec((HALO, bc), lambda j, i: (nxt(i), j)),
                  cur_g, pl.BlockSpec((HALO, bc), lambda j, i: (prev(i), j)),
                  pl.BlockSpec((HALO, bc), lambda j, i: (nxt(i), j)),
                  cur_g, pl.BlockSpec((HALO, bc), lambda j, i: (nxt(i), j)),
                  pl.BlockSpec((K, bc), lambda j, i: (0, j)), pl.BlockSpec((K, bc), lambda j, i: (0, nc + j)),
                  pl.BlockSpec((1, bc), lambda j, i: (0, j)), pl.BlockSpec((1, bc), lambda j, i: (0, nc + j))],
        out_specs=(cur_g, cur_g, pl.BlockSpec((K, bc), lambda j, i: (0, j)), pl.BlockSpec((K, bc), lambda j, i: (0, j)),
                   pl.BlockSpec((1, bc), lambda j, i: (0, j)), pl.BlockSpec((1, bc), lambda j, i: (0, j))),
        out_shape=(jax.ShapeDtypeStruct((T, F), BF16), jax.ShapeDtypeStruct((T, F), BF16),
                   jax.ShapeDtypeStruct((K, F), F32), jax.ShapeDtypeStruct((K, F), F32),
                   jax.ShapeDtypeStruct((1, F), F32), jax.ShapeDtypeStruct((1, F), F32)),
        scratch_shapes=[pltpu.VMEM((bt + 2 * HALO, bc), F32)] * 2 + [pltpu.VMEM((n_ext, bc), F32)] * 2,
        compiler_params=_params(2),
    )(up_g, up_g, up_g, up_v, up_v, up_v, da, da, conv_w, conv_w, conv_b, conv_b)
    return outs


def _adam_math(w, g, m, v):
    m2 = ADAM_B1 * m + (1.0 - ADAM_B1) * g
    v2 = ADAM_B2 * v + (1.0 - ADAM_B2) * (g * g)
    m_hat = m2 / (1.0 - ADAM_B1 ** ADAM_STEP)
    v_hat = v2 / (1.0 - ADAM_B2 ** ADAM_STEP)
    return -ADAM_LR * (m_hat / (jnp.sqrt(v_hat) + ADAM_EPS) + ADAM_WD * w), m2, v2


def _adamw_halves(w, mine, theirs, place, m, v, bt, name):
    R, C = w.shape
    h = R // 2
    bt = _fit_rows(h, bt)
    nh = h // bt

    def body(s_ref, w_ref, a_ref, b_ref, m_ref, v_ref, g_ref, d_ref, m2_ref, v2_ref):
        lower = pl.program_id(0) < nh
        gv = jnp.where(lower == (s_ref[0] == 0), a_ref[...], b_ref[...])
        g_ref[...] = gv
        d_ref[...], m2_ref[...], v2_ref[...] = _adam_math(w_ref[...], gv, m_ref[...], v_ref[...])

    full = pl.BlockSpec((bt, C), lambda i, s: (i, 0))

    def half(is_mine):
        def index(i, s):
            used = ((i < nh) == (s[0] == 0)) == is_mine
            return jnp.where(used, i % nh, jnp.where(i < nh, 0, nh - 1)), 0
        return pl.BlockSpec((bt, C), index)

    out = jax.ShapeDtypeStruct((R, C), F32)
    return pl.pallas_call(
        body, name=name,
        grid_spec=pltpu.PrefetchScalarGridSpec(num_scalar_prefetch=1, grid=(2 * nh,),
                                               in_specs=[full, half(True), half(False), full, full],
                                               out_specs=(full,) * 4),
        out_shape=(out,) * 4, compiler_params=_params(1),
    )(place, w, mine, theirs, m, v)


def _adamw(w, g, m, v, bt, name):
    R, C = w.shape
    bt = _fit_rows(R, bt)
    blk = pl.BlockSpec((bt, C), lambda i: (i, 0))

    def body(w_ref, g_ref, m_ref, v_ref, d_ref, m2_ref, v2_ref):
        gv = g_ref[...]
        m2 = ADAM_B1 * m_ref[...] + (1.0 - ADAM_B1) * gv
        v2 = ADAM_B2 * v_ref[...] + (1.0 - ADAM_B2) * (gv * gv)
        m_hat = m2 / (1.0 - ADAM_B1 ** ADAM_STEP)
        v_hat = v2 / (1.0 - ADAM_B2 ** ADAM_STEP)
        d_ref[...] = -ADAM_LR * (m_hat / (jnp.sqrt(v_hat) + ADAM_EPS) + ADAM_WD * w_ref[...])
        m2_ref[...] = m2
        v2_ref[...] = v2

    out = jax.ShapeDtypeStruct((R, C), F32)
    return pl.pallas_call(body, name=name, grid=(R // bt,), in_specs=[blk] * 4, out_specs=(blk,) * 3,
                          out_shape=(out,) * 3, compiler_params=_params(1))(w, g, m, v)


def _place():
    x, y, c = lax.axis_index("x"), lax.axis_index("y"), lax.axis_index("c")
    chips = [(1 - x, y), (x, 1 - y), (1 - x, 1 - y)]
    return x, y, c, chips


_HBM = pl.BlockSpec(memory_space=pltpu.HBM)


def _add_cores(buf, other, place, own_only, out_dtype, bt, name):
    n, _, h, cols = buf.shape
    bt = _fit_rows(h, bt)
    row = (lambda k, s: s[1]) if own_only else (lambda k, s: k)

    def body(s_ref, a_ref, b_ref, o_ref):
        o_ref[...] = (a_ref[...] + b_ref[...]).astype(out_dtype)

    return pl.pallas_call(
        body, name=name,
        grid_spec=pltpu.PrefetchScalarGridSpec(
            num_scalar_prefetch=1, grid=(1 if own_only else n, h // bt),
            in_specs=[pl.BlockSpec((None, None, bt, cols), lambda k, i, s: (row(k, s), s[0], i, 0)),
                      pl.BlockSpec((None, bt, cols), lambda k, i, s: (row(k, s), i, 0))],
            out_specs=(pl.BlockSpec((bt, cols), lambda k, i, s: (i, 0)) if own_only
                       else pl.BlockSpec((None, bt, cols), lambda k, i, s: (k, i, 0)))),
        out_shape=jax.ShapeDtypeStruct((h, cols) if own_only else (n, h, cols), out_dtype),
        compiler_params=_params(2),
    )(place, buf, other)


def _add_chips(own, others, bt, name):
    h, cols = own.shape
    bt = _fit_rows(h, bt)

    def body(a_ref, b_ref, o_ref):
        o_ref[...] = ((a_ref[...] + b_ref[0].astype(F32)) + b_ref[1].astype(F32)) + b_ref[2].astype(F32)

    return pl.pallas_call(
        body, name=name, grid=(h // bt,),
        in_specs=[pl.BlockSpec((bt, cols), lambda i: (i, 0)), pl.BlockSpec((3, bt, cols), lambda i: (0, i, 0))],
        out_specs=pl.BlockSpec((bt, cols), lambda i: (i, 0)),
        out_shape=jax.ShapeDtypeStruct((h, cols), F32), compiler_params=_params(1),
    )(own, others)


def _gather_plan(bufs, split):
    n = len(bufs)

    def phases(ins, outs, sems):
        send, recv = sems
        x, y, c, chips = _place()
        me = 2 * x + y

        def rows(b, core):
            h = bufs[b].shape[0] // 2
            return pl.ds(core * h, h) if split[b] else pl.ds(0, bufs[b].shape[0])

        def over_ici(b, j, block):
            px, py = chips[j]
            return pltpu.make_async_remote_copy(
                src_ref=ins[b].at[rows(b, c)], dst_ref=outs[b].at[block, rows(b, c)], send_sem=send.at[b, j],
                recv_sem=recv.at[b, j], device_id=(px, py, c), device_id_type=MESH)

        def over_d2d(b, j, block, core):
            return pltpu.make_async_remote_copy(
                src_ref=outs[b].at[block, rows(b, core)], dst_ref=outs[b].at[block, rows(b, core)],
                send_sem=send.at[b, 3 + j], recv_sem=recv.at[b, 3 + j], device_id=(x, y, 1 - c), device_id_type=MESH)

        pairs = [(b, j) for b in range(n) for j in range(3)]
        source = lambda j: 2 * chips[j][0] + chips[j][1]

        def first():
            for b, j in pairs:
                over_ici(b, j, me).start()

        def mid():
            for b, j in pairs:
                over_ici(b, j, source(j)).wait_recv()
                if split[b]:
                    over_d2d(b, j, source(j), c).start()

        def last():
            for b, j in pairs:
                if split[b]:
                    over_d2d(b, j, source(j), 1 - c).wait_recv()
            for b, j in pairs:
                over_ici(b, j, me).wait_send()
                if split[b]:
                    over_d2d(b, j, source(j), c).wait_send()

        return first, mid, last

    return ([jax.ShapeDtypeStruct((4,) + b.shape, b.dtype) for b in bufs],
            [pltpu.SemaphoreType.DMA((n, 6)), pltpu.SemaphoreType.DMA((n, 6))], phases)


def _exchange_plan(n, out_shapes, copy):
    def phases(ins, outs, sems):
        send, recv = sems
        place = _place()

        def first():
            for b in range(n):
                copy(b, ins, outs, send, recv, place).start()

        def last():
            for b in range(n):
                copy(b, ins, outs, send, recv, place).wait()

        return first, (lambda: None), last

    return out_shapes, [pltpu.SemaphoreType.DMA((n,)), pltpu.SemaphoreType.DMA((n,))], phases


def _swap_plan(bufs):
    def copy(b, ins, outs, send, recv, place):
        x, y, c, _ = place
        h = bufs[b].shape[1] // 2
        return pltpu.make_async_remote_copy(
            src_ref=ins[b].at[:, pl.ds((1 - c) * h, h)], dst_ref=outs[b], send_sem=send.at[b], recv_sem=recv.at[b],
            device_id=(x, y, 1 - c), device_id_type=MESH)

    shapes = [jax.ShapeDtypeStruct((b.shape[0], b.shape[1] // 2, b.shape[2]), b.dtype) for b in bufs]
    return _exchange_plan(len(bufs), shapes, copy)


def _scatter_plan(bufs):
    def copy(t, ins, outs, send, recv, place):
        x, y, c, chips = place
        b, j = divmod(t, 3)
        px, py = chips[j]
        return pltpu.make_async_remote_copy(
            src_ref=ins[b].at[2 * px + py], dst_ref=outs[b].at[j], send_sem=send.at[t], recv_sem=recv.at[t],
            device_id=(px, py, c), device_id_type=MESH)

    shapes = [jax.ShapeDtypeStruct((3,) + b.shape[1:], b.dtype) for b in bufs]
    return _exchange_plan(3 * len(bufs), shapes, copy)


def _join_plan(halves):
    def copy(b, ins, outs, send, recv, place):
        x, y, c, _ = place
        return pltpu.make_async_remote_copy(
            src_ref=ins[b], dst_ref=outs[b], send_sem=send.at[b], recv_sem=recv.at[b],
            device_id=(x, y, 1 - c), device_id_type=MESH)

    return _exchange_plan(len(halves), [jax.ShapeDtypeStruct(b.shape, b.dtype) for b in halves], copy)


def _run_plan(arrays, plan, name):
    out_shapes, sems, phases = plan
    n, m = len(arrays), len(out_shapes)

    def body(*refs):
        for phase in phases(refs[:n], refs[n:n + m], refs[n + m:]):
            phase()

    return pl.pallas_call(
        body, name=name, in_specs=[_HBM] * n, out_specs=[_HBM] * m, out_shape=out_shapes, scratch_shapes=sems,
        compiler_params=pltpu.CompilerParams(has_side_effects=True),
    )(*arrays)


def _fit_rows(n, target):
    for q in (2 * HALO, HALO):
        for t in range(min(n, target) // q * q, 0, -q):
            if n % t == 0:
                return t
    raise ValueError((n, target))


def _allreduce_small(buf, name):
    R, lanes = buf.shape

    def body(in_ref, out_ref, land, send, recv):
        x, y, c, _ = _place()
        me = 4 * x + 2 * y + c
        land[me] = in_ref[...]
        cps = []
        for r in range(1, 8):
            px, py, pc = x ^ (r >> 2), y ^ ((r >> 1) & 1), c ^ (r & 1)
            cp = pltpu.make_async_remote_copy(
                src_ref=in_ref, dst_ref=land.at[me], send_sem=send.at[r - 1], recv_sem=recv.at[me],
                device_id=(px, py, pc), device_id_type=MESH)
            cp.start()
            cps.append(cp)
        for r in range(1, 8):
            peer = 4 * (x ^ (r >> 2)) + 2 * (y ^ ((r >> 1) & 1)) + (c ^ (r & 1))
            pltpu.make_async_remote_copy(
                src_ref=in_ref, dst_ref=land.at[peer], send_sem=send.at[r - 1], recv_sem=recv.at[peer],
                device_id=(x, y, c), device_id_type=MESH).wait_recv()
        for cp in cps:
            cp.wait_send()
        acc = land[0]
        for d in range(1, 8):
            acc = acc + land[d]
        out_ref[...] = acc

    vm = pl.BlockSpec(memory_space=pltpu.VMEM)
    return pl.pallas_call(
        body, name=name, in_specs=[vm], out_specs=vm, out_shape=jax.ShapeDtypeStruct((R, lanes), buf.dtype),
        scratch_shapes=[pltpu.VMEM((8, R, lanes), buf.dtype), pltpu.SemaphoreType.DMA((7,)), pltpu.SemaphoreType.DMA((8,))],
        compiler_params=pltpu.CompilerParams(has_side_effects=True, vmem_limit_bytes=VMEM_LIMIT),
    )(buf)


ROW_BLOCK = 256
SB_BLOCK = 256
MM_TM, MM_TN, MM_TK = 1024, 512, 512
FFN_COLS = 512


def _lane_pad(vec, start):
    return jnp.pad(vec, ((0, 0), (start, LANES - start - vec.shape[1])))


WEIGHTS = ("w_in", "sb_out_gain", "dn_conv_w", "dn_a_log", "dn_dt_bias", "dn_out_gain", "w_out", "ln_mix_pre",
           "ln_mix_post", "w_up", "ffn_conv_w", "ffn_conv_b", "w_down", "ln_ffn_pre", "ln_ffn_post")
MATRICES = {"w_in": 1, "w_out": 0, "w_up": 1, "w_down": 0}
CONV_SHARDED = ("dn_conv_w", "ffn_conv_w")
SMALL = tuple(n for n in WEIGHTS if n not in MATRICES)
N_CHIPS = 4
ADAM_ROWS = 128


def _pack(arrs, quantum):
    rows, layout, off = [], [], 0
    for a in arrs:
        n = int(np.prod(a.shape))
        r = -(-n // LANES)
        r = -(-r // HALO) * HALO
        rows.append(jnp.pad(a.reshape(-1), (0, r * LANES - n)).reshape(r, LANES))
        layout.append((off, r, n, a.shape))
        off += r
    total = -(-off // quantum) * quantum
    if total > off:
        rows.append(jnp.zeros((total - off, LANES), rows[0].dtype))
    return jnp.concatenate(rows, axis=0), layout


def _unpack(packed, layout):
    return [packed[off:off + r].reshape(-1)[:n].reshape(shape) for off, r, n, shape in layout]


UP_TILE = 1408
PAIR_ROWS = 256


PROLOGUE_STEPS = 8


def _prologue(x, gain, shards, comm, name):
    n = len(shards)
    steps = PROLOGUE_STEPS
    while any(s.shape[0] % (steps * BF16_ROWS) for s in [x] + list(shards)):
        steps //= 2

    def body(*refs):
        ins, outs, _, (first, mid, last) = comm.split(refs, 2 + n, 1 + n, 0)
        i = pl.program_id(0)
        pl.when(i == 0)(first)
        pl.when(i == 3 * steps // 4)(mid)
        xv = ins[0][...]
        outs[0][...] = (xv * lax.rsqrt(jnp.mean(xv * xv, axis=-1, keepdims=True) + EPS) * ins[1][...]).astype(BF16)
        for s_ref, o_ref in zip(ins[2:], outs[1:]):
            o_ref[...] = s_ref[...].astype(BF16)
        pl.when(i == steps - 1)(last)

    rows = lambda a: pl.BlockSpec((a.shape[0] // steps, a.shape[1]), lambda i: (i, 0))
    arrays = [x] + list(shards)
    return comm.call(body, name, (steps,), [rows(x), pl.BlockSpec((1, x.shape[1]), lambda i: (0, 0))] + [rows(s) for s in shards],
                     [rows(a) for a in arrays], [jax.ShapeDtypeStruct(a.shape, BF16) for a in arrays], [],
                     [x, gain] + list(shards))


def _reduce_to_chips(shares, place, names, swap_on, scatter_on):
    from_sibling = swap_on(shares)
    halves = [s.reshape(N_CHIPS, 2, s.shape[1] // 2, s.shape[2]) for s in shares]
    to_chips = [_add_cores(hv, fs, place, False, BF16, PAIR_ROWS, "grad_add_cores_" + n)
                for hv, fs, n in zip(halves, from_sibling, names)]
    own = [_add_cores(hv, fs, place, True, F32, PAIR_ROWS, "grad_add_cores_own_" + n)
           for hv, fs, n in zip(halves, from_sibling, names)]
    return own, scatter_on(to_chips)


def _step(x, xn, target, wt, late, chip, place):
    T, D = x.shape
    W = D // 2
    H = W // HEAD_DIM
    bt = min(ROW_BLOCK, T)
    blk = min(SB_BLOCK, T)
    w_in = wt["w_in"]
    a_log, dt_bias = _lane_pad(wt["dn_a_log"], H), _lane_pad(wt["dn_dt_bias"], H)
    mm = functools.partial(_mm, tm=MM_TM, tn=MM_TN)
    wide = functools.partial(_mm, tm=MM_TM, tn=2 * MM_TN)
    mm_up = functools.partial(_mm, tm=MM_TM, tn=UP_TILE)
    one = lambda a, b, tk=MM_TK: [(a, b, tk, 0, 0)]

    psb = mm(one(xn, w_in, D), "nn", BF16, name="proj_sb", n_window=(0, 3 * W))
    pdn = mm(one(xn, w_in, D), "nn", F32, name="proj_dn", n_window=(3 * W, 4 * W))
    pba = mm(one(xn, w_in, D), "nn", F32, name="proj_ba", n_window=(7 * W, LANES))
    late_names = ("w_out", "w_up", "w_down")
    gathered_with = lambda names: _Comm([late[n] for n in names], _gather_plan([late[n] for n in names], [True] * len(names)))
    own_block_in = lambda theirs, names: [lax.dynamic_update_index_in_dim(t, late[n], chip, 0) for t, n in zip(theirs, names)]
    (o_sb, mix_sb, lt, swept), theirs = _sb_fwd(psb, wt["sb_out_gain"], blk, "sb_fwd", gathered_with(("w_out",)))
    w_out = own_block_in(theirs, ("w_out",))[0].reshape(-1, D)
    qn, kn, vv, bx, gx = _dn_pre_fwd(pdn, pba, wt["dn_conv_w"], a_log, dt_bias, bt, "dn_pre_fwd")
    (o_dn, mix_dn, ss, tms), theirs = _dn_core_fwd(qn, kn, vv, bx, gx, pdn, wt["dn_out_gain"], "dn_core_fwd",
                                                  gathered_with(("w_up", "w_down")))
    w_up, w_down = own_block_in(theirs, ("w_up", "w_down"))
    w_down = w_down.reshape(-1, D)
    F = w_down.shape[0]
    m = wide([(mix_sb, w_out, 2 * MM_TK, 0, 0), (mix_dn, w_out, 2 * MM_TK, 0, W)], "nn", F32, name="out_proj")
    h, hn = _rms_chain_fwd(m, wt["ln_mix_post"], x, wt["ln_ffn_pre"], bt, "rms_between")
    bc = min(FFN_COLS, F)
    up_g, up_v, act = _ffn_up_fused(hn, w_up, wt["ffn_conv_w"], wt["ffn_conv_b"], FFN_COLS // 2, "ffn_up")
    f = mm(one(act, w_down, UP_TILE), "nn", F32, name="ffn_down")
    dy, df, g_ffn_post, sq = _loss_head(f, wt["ln_ffn_post"], h, target, bt, "loss_head")
    loss = 0.5 * jnp.sum(sq) / D

    da = mm(one(df, w_down, D), "nt", F32, name="d_act")
    g_w_down = _mm(one(act, df, 2 * MM_TK), "tn", F32, tm=UP_TILE, tn=2 * MM_TN, name="g_w_down")
    dug, duv, dwg, dwv, dbg, dbv = _ffn_mid_bwd(up_g, up_v, wt["ffn_conv_w"], wt["ffn_conv_b"], da, bt, bc, "ffn_mid_bwd")
    shard = w_up.shape[2]
    g_w_up = mm_up(one(hn, dug, 2 * MM_TK), "tn", F32, name="g_w_up_gate", out_shard=shard,
                   into=(lax.empty(w_up.shape, F32), 0))
    g_w_up = mm_up(one(hn, duv, 2 * MM_TK), "tn", F32, name="g_w_up_val", out_shard=shard, into=(g_w_up, F))
    ffn_shares = [g_w_up, g_w_down.reshape(N_CHIPS, -1, D)]
    dhn, ffn_swapped = wide([(dug, w_up, UP_TILE, 0, 0), (duv, w_up, UP_TILE, 0, F)], "nt", F32, name="d_hn",
                            comm=_Comm(ffn_shares, _swap_plan(ffn_shares)))
    dh, dm, g_ffn_pre, g_mix_post = _rms_chain_bwd(h, wt["ln_ffn_pre"], dhn, dy, m, wt["ln_mix_post"], bt,
                                                   "rms_between_bwd")
    dmix = mm(one(dm, w_out, D), "nt", F32, name="d_mix")
    g_w_out = jnp.concatenate([wide(one(mix_sb, dm, 2 * MM_TK), "tn", F32, name="g_w_out_sb"),
                               wide(one(mix_dn, dm, 2 * MM_TK), "tn", F32, name="g_w_out_dn")], axis=0)
    shares = [g_w_out.reshape(N_CHIPS, -1, D)] + ffn_shares
    carried = {}

    def swap_on(arrays):
        (carried["do_sb"], carried["g_sb_gain"]), out = _headnorm_bwd(
            o_sb, wt["sb_out_gain"], dmix, bt, "sb_norm_bwd", _Comm(arrays[:1], _swap_plan(arrays[:1])))
        return list(out) + list(ffn_swapped)

    def scatter_on(arrays):
        carried["dn"], out = _dn_core_bwd(qn, kn, vv, bx, gx, pdn, wt["dn_out_gain"], o_dn, dmix, ss, tms, W,
                                          "dn_core_bwd", _Comm(arrays, _scatter_plan(arrays)))
        return out

    early = _reduce_to_chips(shares, place, late_names, swap_on, scatter_on)
    g_sb_gain = carried["g_sb_gain"]
    (dq, dk, dv), _ = _sb_bwd(psb, carried["do_sb"], lt, swept, blk, "sb_bwd", _Comm())
    ddq, ddk, ddv, dbx, dgx, dz, g_dn_gain = carried["dn"]
    dconv, dba, g_dn_conv, g_a_log, g_dt_bias = _dn_pre_bwd(pdn, pba, wt["dn_conv_w"], a_log, dt_bias,
                                                            ddq, ddk, ddv, dbx, dgx, bt, "dn_pre_bwd")
    pieces = [(dq, 0), (dk, W), (dv, 2 * W), (dconv, 3 * W), (dz, 6 * W), (dba, 7 * W)]
    g_w_in = [wide(one(xn, d, 2 * MM_TK), "tn", F32, name=f"g_w_in_{i}") for i, (d, _) in enumerate(pieces)]
    g_w_in[-1] = g_w_in[-1][:, :2 * H]
    g_in = jnp.concatenate(g_w_in, axis=1)

    def with_d_xn(arrays):
        carried["dxn"], out = mm([(d, w_in, 2 * MM_TK, 0, k0) for d, k0 in pieces], "nt", F32, name="d_xn",
                                 comm=_Comm(arrays, _scatter_plan(arrays)))
        return out

    last = _reduce_to_chips([g_in.reshape(D, N_CHIPS, -1).transpose(1, 0, 2)], place, ["w_in"],
                            lambda arrays: _run_plan(arrays, _swap_plan(arrays), "grad_swap_cores"), with_d_xn)
    dx, g_mix_pre = _rms_bwd(x, wt["ln_mix_pre"], carried["dxn"], dh, F32, bt, "rms_mix_pre_bwd")
    exchanged = dict(zip(late_names, zip(*early)))
    exchanged["w_in"] = (last[0][0], last[1][0])

    grads = dict(
        sb_out_gain=g_sb_gain, dn_conv_w=g_dn_conv, dn_a_log=g_a_log[:, H:2 * H],
        dn_dt_bias=g_dt_bias[:, H:2 * H], dn_out_gain=jnp.sum(g_dn_gain, axis=0),
        ln_mix_pre=g_mix_pre, ln_mix_post=g_mix_post,
        ffn_conv_w=jnp.concatenate([dwg, dwv], axis=1), ffn_conv_b=jnp.concatenate([dbg, dbv], axis=1),
        ln_ffn_pre=g_ffn_pre, ln_ffn_post=g_ffn_post)
    return loss, dx, grads, exchanged


def kernel(x, w_in, sb_out_gain, dn_conv_w, dn_a_log, dn_dt_bias, dn_out_gain, w_out, ln_mix_pre, ln_mix_post, w_up, ffn_conv_w, ffn_conv_b, w_down, ln_ffn_pre, ln_ffn_post, loss_target, m_w_in, m_sb_out_gain, m_dn_conv_w, m_dn_a_log, m_dn_dt_bias, m_dn_out_gain, m_w_out, m_ln_mix_pre, m_ln_mix_post, m_w_up, m_ffn_conv_w, m_ffn_conv_b, m_w_down, m_ln_ffn_pre, m_ln_ffn_post, v_w_in, v_sb_out_gain, v_dn_conv_w, v_dn_a_log, v_dn_dt_bias, v_dn_out_gain, v_w_out, v_ln_mix_pre, v_ln_mix_post, v_w_up, v_ffn_conv_w, v_ffn_conv_b, v_w_down, v_ln_ffn_pre, v_ln_ffn_post):
    given = dict(locals())
    wl = {n: given[n][0] for n in WEIGHTS}
    ml = {n: given["m_" + n][0] for n in WEIGHTS}
    vl = {n: given["v_" + n][0] for n in WEIGHTS}
    for d in (wl, ml, vl):
        for n in SMALL:
            if d[n].ndim == 1:
                d[n] = d[n][None]
    cx, cy, cc = lax.axis_index("x"), lax.axis_index("y"), lax.axis_index("c")
    chip = 2 * cx + cy
    D = x.shape[2]
    W = D // 2

    first = ("w_in",) + CONV_SHARDED
    mine = [wl["w_in"].astype(BF16)] + [wl[n] for n in CONV_SHARDED]
    late_names = ("w_out", "w_up", "w_down")
    (xn, *late_shards), theirs = _prologue(x[0], wl["ln_mix_pre"], [wl[n] for n in late_names],
                                           _Comm(mine, _gather_plan(mine, [True, False, False])), "gather_w_in")
    got = {n: lax.dynamic_update_index_in_dim(t, s, chip, 0) for n, t, s in zip(first, theirs, mine)}
    columns = lambda g: g.transpose(1, 0, 2).reshape(g.shape[1], N_CHIPS * g.shape[2])
    wt = {n: wl[n] for n in SMALL}
    w_in_all = columns(got["w_in"])
    wt["w_in"] = jnp.pad(w_in_all, ((0, 0), (0, 7 * W + LANES - w_in_all.shape[1])))
    for n in CONV_SHARDED:
        wt[n] = columns(got[n])
    late = dict(zip(late_names, late_shards))

    place = jnp.stack([cc, chip]).astype(jnp.int32)
    loss, dx, grads, exchanged = _step(x[0], xn, loss_target[0], wt, late, chip, place)
    loss = lax.psum(loss, ("x", "y", "c"))

    names = list(MATRICES)
    reduced = [_add_chips(*exchanged[n], PAIR_ROWS, "grad_add_chips_" + n) for n in names]
    siblings = _run_plan(reduced, _join_plan(reduced), "grad_join_cores")
    gl = {}

    small, small_layout = _pack([grads[n] for n in SMALL], HALO)
    small = _allreduce_small(small, "grad_allreduce_small")
    for n, g in zip(SMALL, _unpack(small, small_layout)):
        if n in CONV_SHARDED:
            size = g.shape[1] // N_CHIPS
            g = lax.dynamic_slice_in_dim(g, chip * size, size, axis=1)
        gl[n] = g

    delta, new_m, new_v = {}, {}, {}
    for n, mine_half, sibling_half in zip(names, reduced, siblings):
        gl[n], delta[n], new_m[n], new_v[n] = _adamw_halves(wl[n], mine_half, sibling_half, place, ml[n], vl[n],
                                                            ADAM_ROWS, "adamw_" + n)
    packs = [_pack([d[n] for n in SMALL], HALO) for d in (wl, gl, ml, vl)]
    outs = _adamw(*[p[0] for p in packs], ADAM_ROWS, "adamw_small")
    for res, o in zip((delta, new_m, new_v), outs):
        res.update(zip(SMALL, _unpack(o, packs[0][1])))

    shaped = lambda d: [d[n].reshape(given[n].shape) for n in WEIGHTS]
    return (loss, dx[None], *shaped(gl), *shaped(delta), *shaped(new_m), *shaped(new_v))
```

```python
import functools

import numpy as np
import jax
import jax.numpy as jnp
from jax import lax
from jax.experimental import pallas as pl
from jax.experimental.pallas import tpu as pltpu

F32 = jnp.float32
BF16 = jnp.bfloat16
HEAD_DIM = 128
CHUNK = 64
ROWS = 4 * CHUNK
N_CHUNKS = ROWS // CHUNK
EPS = 1e-6
EXP_UNDERFLOW = 110.0
LANES = 128
HALO = 8
VMEM_LIMIT = 48 * 1024 * 1024
ADAM_LR, ADAM_B1, ADAM_B2, ADAM_EPS, ADAM_WD, ADAM_STEP = 0.001, 0.9, 0.999, 1e-08, 0.01, 10
MESH = pl.DeviceIdType.MESH

NN = (((1,), (0,)), ((), ()))
NT = (((1,), (1,)), ((), ()))
TN = (((0,), (0,)), ((), ()))


def _params(n_axes):
    return pltpu.CompilerParams(dimension_semantics=("arbitrary",) * n_axes, vmem_limit_bytes=VMEM_LIMIT)


def _bdot(a, b, dims=NN):
    return lax.dot_general(a.astype(BF16), b.astype(BF16), dims, preferred_element_type=F32)


def _split3(a):
    hi = a.astype(BF16)
    r1 = a - hi.astype(F32)
    mid = r1.astype(BF16)
    lo = (r1 - mid.astype(F32)).astype(BF16)
    return hi, mid, lo


def _dot3(a, sel, dims=NN):
    return sum(lax.dot_general(p, sel, dims, preferred_element_type=F32) for p in _split3(a))


def _dot3r(sel, a, dims=NN):
    return sum(lax.dot_general(sel, p, dims, preferred_element_type=F32) for p in _split3(a))


def _iota2(n, m):
    return lax.broadcasted_iota(jnp.int32, (n, m), 0), lax.broadcasted_iota(jnp.int32, (n, m), 1)


def _sigmoid(x):
    return 1.0 / (1.0 + jnp.exp(-x))


def _softplus(x):
    return jnp.maximum(x, 0.0) + jnp.log(1.0 + jnp.exp(-jnp.abs(x)))


def _fit(values, target):
    values = [v for v in (values if isinstance(values, (list, tuple)) else [values]) if v]
    best = None
    for t in range(LANES, min(min(values), target) + 1, LANES):
        if all(v % t == 0 for v in values):
            best = t
    assert best is not None, (values, target)
    return best


def _mm(parts, mode, out_dtype, tm, tn, name, n_window=None, out_shard=None, into=None, comm=None):
    dims = {"nn": NN, "nt": NT, "tn": TN}[mode]
    a0, b0 = parts[0][0], parts[0][1]
    b3 = b0.ndim == 3
    shard_c = b0.shape[2] if b3 else None
    M = a0.shape[1] if mode == "tn" else a0.shape[0]
    if mode == "nt":
        n_full = b0.shape[1] if b3 else b0.shape[0]
    else:
        n_full = b0.shape[0] * b0.shape[2] if b3 else b0.shape[1]
    n0, N = n_window if n_window is not None else (0, n_full)
    out_n0 = into[1] if into is not None else 0
    tm = _fit(M, tm)
    tn = _fit([N, n0, out_n0, out_shard, shard_c if mode != "nt" else None], tn)
    specs_a, specs_b, offs, nks = [], [], [], []
    off = 0
    for a, b, tk, a_k0, b_k0 in parts:
        K = a.shape[0] if mode == "tn" else a.shape[1]
        tk = _fit([K, a_k0, b_k0, shard_c if mode == "nt" else None], tk)
        nk = K // tk
        kk = lambda k, o=off, n=nk: jnp.clip(k - o, 0, n - 1)
        ao, bo, no = a_k0 // tk, b_k0 // tk, n0 // tn
        if mode == "tn":
            specs_a.append(pl.BlockSpec((tk, tm), lambda i, j, k, kk=kk, ao=ao: (kk(k) + ao, i)))
        else:
            specs_a.append(pl.BlockSpec((tm, tk), lambda i, j, k, kk=kk, ao=ao: (i, kk(k) + ao)))
        if mode == "nt":
            if b3:
                per = shard_c // tk
                specs_b.append(pl.BlockSpec((None, tn, tk), lambda i, j, k, kk=kk, bo=bo, per=per:
                                            ((kk(k) + bo) // per, j, (kk(k) + bo) % per)))
            else:
                specs_b.append(pl.BlockSpec((tn, tk), lambda i, j, k, kk=kk, bo=bo: (j, kk(k) + bo)))
        else:
            if b3:
                per = shard_c // tn
                specs_b.append(pl.BlockSpec((None, tk, tn), lambda i, j, k, kk=kk, bo=bo, no=no, per=per:
                                            ((j + no) // per, kk(k) + bo, (j + no) % per)))
            else:
                specs_b.append(pl.BlockSpec((tk, tn), lambda i, j, k, kk=kk, bo=bo, no=no: (kk(k) + bo, j + no)))
        offs.append(off)
        nks.append(nk)
        off += nk
    nk_total = off
    n_parts = len(parts)

    comm = comm if comm is not None else _Comm()
    grid = (M // tm, N // tn, nk_total)
    n_in = 2 * n_parts + (1 if into is not None else 0)

    def body(*refs):
        ins, (o_ref,), scratch, (first, mid, last) = comm.split(refs, n_in, 1, 0 if nk_total == 1 else 1)
        a_refs, b_refs = ins[:n_parts], ins[n_parts:2 * n_parts]
        at = lambda step: functools.reduce(lambda x, y: x & y, [pl.program_id(d) == step[d] for d in range(3)])
        pl.when(at((0, 0, 0)))(first)
        pl.when(at((grid[0] // 2, 0, 0)))(mid)
        if nk_total == 1:
            o_ref[...] = _bdot(a_refs[0][...], b_refs[0][...], dims).astype(out_dtype)
        else:
            acc = scratch[0]
            k = pl.program_id(2)

            @pl.when(k == 0)
            def _():
                acc[...] = jnp.zeros_like(acc)

            for p in range(n_parts):
                @pl.when((k >= offs[p]) & (k < offs[p] + nks[p]))
                def _(p=p):
                    acc[...] += _bdot(a_refs[p][...], b_refs[p][...], dims)

            @pl.when(k == nk_total - 1)
            def _():
                o_ref[...] = acc[...].astype(out_dtype)
        pl.when(at(tuple(g - 1 for g in grid)))(last)

    jo = out_n0 // tn
    if out_shard is not None:
        per_o = out_shard // tn
        out_spec = pl.BlockSpec((None, tm, tn), lambda i, j, k: ((j + jo) // per_o, i, (j + jo) % per_o))
        out_shape = jax.ShapeDtypeStruct((N // out_shard, M, out_shard), out_dtype)
    else:
        out_spec = pl.BlockSpec((tm, tn), lambda i, j, k: (i, j + jo))
        out_shape = jax.ShapeDtypeStruct((M, N), out_dtype)
    ins = [p[0] for p in parts] + [p[1] for p in parts]
    in_specs = specs_a + specs_b
    aliases = {}
    if into is not None:
        out_shape = jax.ShapeDtypeStruct(into[0].shape, into[0].dtype)
        aliases = {len(ins): 0}
        ins.append(into[0])
        in_specs.append(pl.BlockSpec(memory_space=pl.ANY))
    (out,), carried = comm.call(body, name, grid, in_specs, (out_spec,), (out_shape,),
                                [] if nk_total == 1 else [pltpu.VMEM((tm, tn), F32)], ins, aliases)
    return (out, carried) if comm.phases is not None else out


def _rms_bwd_math(xv, g, dy):
    r = lax.rsqrt(jnp.mean(xv * xv, axis=-1, keepdims=True) + EPS)
    n = xv * r
    gy = dy * g
    dx = r * (gy - n * jnp.mean(gy * n, axis=-1, keepdims=True))
    return dx, dy * n


def _rms_bwd(x, gain, dy, resid, out_dtype, bt, name):
    T, D = x.shape
    row = pl.BlockSpec((bt, D), lambda i: (i, 0))
    vec = pl.BlockSpec((1, D), lambda i: (0, 0))

    def body(*refs):
        x_ref, g_ref, dy_ref = refs[0], refs[1], refs[2]
        dx_ref, dg_ref = refs[-2], refs[-1]
        dx, dgp = _rms_bwd_math(x_ref[...], g_ref[...], dy_ref[...].astype(F32))
        if resid is not None:
            dx = refs[3][...] + dx
        dx_ref[...] = dx.astype(out_dtype)

        @pl.when(pl.program_id(0) == 0)
        def _():
            dg_ref[...] = jnp.zeros_like(dg_ref)

        dg_ref[...] += jnp.sum(dgp, axis=0, keepdims=True)

    ins = [x, gain, dy] + ([resid] if resid is not None else [])
    return pl.pallas_call(
        body, name=name, grid=(T // bt,),
        in_specs=[row, vec, row] + ([row] if resid is not None else []),
        out_specs=(row, vec),
        out_shape=(jax.ShapeDtypeStruct((T, D), out_dtype), jax.ShapeDtypeStruct((1, D), F32)),
        compiler_params=_params(1),
    )(*ins)


def _rms_chain_fwd(m, g_post, x, g_pre, bt, name):
    T, D = m.shape
    row = pl.BlockSpec((bt, D), lambda i: (i, 0))
    vec = pl.BlockSpec((1, D), lambda i: (0, 0))

    def body(m_ref, gp_ref, x_ref, gn_ref, h_ref, hn_ref):
        mv = m_ref[...]
        h = x_ref[...] + mv * lax.rsqrt(jnp.mean(mv * mv, axis=-1, keepdims=True) + EPS) * gp_ref[...]
        h_ref[...] = h
        hn_ref[...] = (h * lax.rsqrt(jnp.mean(h * h, axis=-1, keepdims=True) + EPS) * gn_ref[...]).astype(BF16)

    return pl.pallas_call(
        body, name=name, grid=(T // bt,), in_specs=[row, vec, row, vec], out_specs=(row, row),
        out_shape=(jax.ShapeDtypeStruct((T, D), F32), jax.ShapeDtypeStruct((T, D), BF16)), compiler_params=_params(1),
    )(m, g_post, x, g_pre)


def _rms_chain_bwd(h, g_pre, dhn, dy, m, g_post, bt, name):
    T, D = h.shape
    row = pl.BlockSpec((bt, D), lambda i: (i, 0))
    vec = pl.BlockSpec((1, D), lambda i: (0, 0))

    def body(h_ref, gn_ref, dhn_ref, dy_ref, m_ref, gp_ref, dh_ref, dm_ref, dgn_ref, dgp_ref):
        dh, dgn = _rms_bwd_math(h_ref[...], gn_ref[...], dhn_ref[...])
        dh = dy_ref[...] + dh
        dm, dgp = _rms_bwd_math(m_ref[...], gp_ref[...], dh)
        dh_ref[...] = dh
        dm_ref[...] = dm.astype(BF16)

        @pl.when(pl.program_id(0) == 0)
        def _():
            dgn_ref[...] = jnp.zeros_like(dgn_ref)
            dgp_ref[...] = jnp.zeros_like(dgp_ref)

        dgn_ref[...] += jnp.sum(dgn, axis=0, keepdims=True)
        dgp_ref[...] += jnp.sum(dgp, axis=0, keepdims=True)

    return pl.pallas_call(
        body, name=name, grid=(T // bt,), in_specs=[row, vec, row, row, row, vec], out_specs=(row, row, vec, vec),
        out_shape=(jax.ShapeDtypeStruct((T, D), F32), jax.ShapeDtypeStruct((T, D), BF16),
                   jax.ShapeDtypeStruct((1, D), F32), jax.ShapeDtypeStruct((1, D), F32)),
        compiler_params=_params(1),
    )(h, g_pre, dhn, dy, m, g_post)


def _loss_head(f, gain, h, target, bt, name):
    T, D = f.shape
    row = pl.BlockSpec((bt, D), lambda i: (i, 0))
    vec = pl.BlockSpec((1, D), lambda i: (0, 0))

    def body(f_ref, g_ref, h_ref, t_ref, dy_ref, df_ref, dg_ref, sq_ref):
        fv, g = f_ref[...], g_ref[...]
        r = lax.rsqrt(jnp.mean(fv * fv, axis=-1, keepdims=True) + EPS)
        n = fv * r
        err = (h_ref[...] + n * g) - t_ref[...]
        dy = err * (1.0 / D)
        gy = dy * g
        df = r * (gy - n * jnp.mean(gy * n, axis=-1, keepdims=True))
        dy_ref[...] = dy
        df_ref[...] = df.astype(BF16)

        @pl.when(pl.program_id(0) == 0)
        def _():
            dg_ref[...] = jnp.zeros_like(dg_ref)
            sq_ref[...] = jnp.zeros_like(sq_ref)

        dg_ref[...] += jnp.sum(dy * n, axis=0, keepdims=True)
        sq_ref[...] += jnp.sum(err * err, axis=0, keepdims=True)

    return pl.pallas_call(
        body, name=name, grid=(T // bt,), in_specs=[row, vec, row, row], out_specs=(row, row, vec, vec),
        out_shape=(jax.ShapeDtypeStruct((T, D), F32), jax.ShapeDtypeStruct((T, D), BF16),
                   jax.ShapeDtypeStruct((1, D), F32), jax.ShapeDtypeStruct((1, D), F32)),
        compiler_params=_params(1),
    )(f, gain, h, target)


def _sb_logits(q, k, valid):
    z = lax.dot_general(q, k, NT, preferred_element_type=F32) * (HEAD_DIM ** -0.5)
    sp = jnp.log(1.0 + jnp.exp(-jnp.abs(z)))
    lb = jnp.minimum(z, 0.0) - sp
    l1 = -(jnp.maximum(z, 0.0) + sp)
    return lb, (l1 if valid is None else jnp.where(valid, l1, 0.0))


def _masked(valid, x):
    return x if valid is None else jnp.where(valid, x, 0.0)


def _heads_per_step(n_heads):
    return 2 if n_heads % 2 == 0 else 1


def _dot2(a, sel):
    hi = a.astype(BF16)
    lo = (a - hi.astype(F32)).astype(BF16)
    return jnp.dot(hi, sel, preferred_element_type=F32) + jnp.dot(lo, sel, preferred_element_type=F32)


class _Comm:
    def __init__(self, arrays=(), plan=((), (), None)):
        self.arrays = list(arrays)
        self.out_shapes, self.sems, self.phases = list(plan[0]), list(plan[1]), plan[2]

    def split(self, refs, n_in, n_out, n_scratch):
        a, o = len(self.arrays), len(self.out_shapes)
        cuts = np.cumsum([0, n_in, a, n_out, o, n_scratch])
        ins, cin, outs, cout, scratch = (refs[cuts[t]:cuts[t + 1]] for t in range(5))
        if self.phases is None:
            return ins, outs, scratch, (lambda: None,) * 3
        return ins, outs, scratch, self.phases(cin, cout, refs[cuts[5]:])

    def call(self, body, name, grid, in_specs, out_specs, out_shape, scratch_shapes, operands, aliases=None):
        outs = pl.pallas_call(
            body, name=name, grid=grid, in_specs=list(in_specs) + [_HBM] * len(self.arrays),
            out_specs=tuple(out_specs) + (_HBM,) * len(self.out_shapes),
            out_shape=tuple(out_shape) + tuple(self.out_shapes),
            scratch_shapes=list(scratch_shapes) + self.sems, input_output_aliases=aliases or {},
            compiler_params=pltpu.CompilerParams(dimension_semantics=("arbitrary",) * len(grid),
                                                 vmem_limit_bytes=VMEM_LIMIT, has_side_effects=self.phases is not None),
        )(*operands, *self.arrays)
        return outs[:len(out_shape)], outs[len(out_shape):]


def _sb_fwd(qkv, gain, blk, name, comm):
    T, W = qkv.shape[0], qkv.shape[1] // 3
    H = W // HEAD_DIM
    nq = T // blk
    hp = _heads_per_step(H)
    ng = H // hp
    lanes = [slice(t * HEAD_DIM, (t + 1) * HEAD_DIM) for t in range(hp)]

    def body(*refs):
        (q_ref, k_ref, v_ref, g_ref), (o_ref, mix_ref, lt_ref, swept_ref), _, (first, mid, last) = comm.split(refs, 4, 4, 0)
        h, i = pl.program_id(0), pl.program_id(1)
        pl.when((h == 0) & (i == 0))(first)
        pl.when((h == 3 * ng // 4) & (i == 0))(mid)
        q = [q_ref[:, ln] for ln in lanes]
        row, col = _iota2(blk, blk)
        after =(row > col).astype(BF16)

        def step(kb, carry, valid):
            ks = pl.ds(pl.multiple_of(kb * blk, blk), blk)
            out = []
            for t, (run, acc) in enumerate(carry):
                lb, l1 = _sb_logits(q[t], k_ref[ks, lanes[t]], valid)
                att = _masked(valid, jnp.exp(lb + _dot2(l1, after) + run))
                out.append((run + jnp.sum(l1, axis=1, keepdims=True), acc + _bdot(att, v_ref[ks, lanes[t]])))
            return tuple(out)

        zero = (jnp.zeros((blk, 1), F32), jnp.zeros((blk, HEAD_DIM), F32))
        def alive(state):
            jj, c = state
            return (jj < i) & (functools.reduce(jnp.maximum, [jnp.max(run) for run, _ in c]) > -EXP_UNDERFLOW)

        swept, carry = lax.while_loop(alive, lambda st: (st[0] + 1, step(i - 1 - st[0], st[1], None)),
                                      (jnp.int32(0), step(i, (zero,) * hp, col < row)))
        swept_ref[h, i] = swept
        for t, (run, o) in enumerate(carry):
            o_ref[:, lanes[t]] = o
            r = lax.rsqrt(jnp.mean(o * o, axis=-1, keepdims=True) + EPS)
            mix_ref[:, lanes[t]] = (o * r * g_ref[...]).astype(BF16)
            lt_ref[:, lanes[t]] = jnp.broadcast_to(run, (blk, HEAD_DIM))
        pl.when((h == ng - 1) & (i == nq - 1))(last)

    wide = hp * HEAD_DIM
    qb = pl.BlockSpec((blk, wide), lambda h, i: (i, h))
    return comm.call(
        body, name, (ng, nq),
        [qb, pl.BlockSpec((T, wide), lambda h, i: (0, ng + h)),
         pl.BlockSpec((T, wide), lambda h, i: (0, 2 * ng + h)), pl.BlockSpec((1, HEAD_DIM), lambda h, i: (0, 0))],
        (qb, qb, qb, pl.BlockSpec(memory_space=pltpu.SMEM)),
        (jax.ShapeDtypeStruct((T, W), F32), jax.ShapeDtypeStruct((T, W), BF16), jax.ShapeDtypeStruct((T, W), F32),
         jax.ShapeDtypeStruct((ng, nq), jnp.int32)),
        [], (qkv, qkv, qkv, gain))


def _headnorm_bwd(o, gain, dmix, bt, name, comm):
    T, W = o.shape
    H = W // HEAD_DIM
    nt = T // bt
    blk = pl.BlockSpec((bt, HEAD_DIM), lambda i, h: (i, h))
    vec = pl.BlockSpec((1, HEAD_DIM), lambda i, h: (0, 0))

    def body(*refs):
        (o_ref, g_ref, d_ref), (do_ref, dg_ref), _, (first, mid, last) = comm.split(refs, 3, 2, 0)
        i, h = pl.program_id(0), pl.program_id(1)
        pl.when((i == 0) & (h == 0))(first)
        pl.when((i == nt // 2) & (h == 0))(mid)
        do, dgp = _rms_bwd_math(o_ref[...], g_ref[...], d_ref[...])
        do_ref[...] = do

        @pl.when((i == 0) & (h == 0))
        def _():
            dg_ref[...] = jnp.zeros_like(dg_ref)

        dg_ref[...] += jnp.sum(dgp, axis=0, keepdims=True)
        pl.when((i == nt - 1) & (h == H - 1))(last)

    return comm.call(body, name, (nt, H), [blk, vec, blk], (blk, vec),
                     (jax.ShapeDtypeStruct((T, W), F32), jax.ShapeDtypeStruct((1, HEAD_DIM), F32)), [], (o, gain, dmix))


def _sb_bwd(qkv, do, lt, swept, blk, name, comm):
    T, W = qkv.shape[0], qkv.shape[1] // 3
    H = W // HEAD_DIM
    nq = T // blk
    scale = HEAD_DIM ** -0.5
    hp = _heads_per_step(H)
    ng = H // hp
    lanes = [slice(t * HEAD_DIM, (t + 1) * HEAD_DIM) for t in range(hp)]

    def body(*refs):
        ((q_ref, k_ref, v_ref, do_ref, lt_ref, swept_ref), (dq_ref, dk_out, dv_out), (dk_ref, dv_ref),
         (first, mid, last)) = comm.split(refs, 6, 3, 2)
        h, i = pl.program_id(0), pl.program_id(1)
        pl.when((h == 0) & (i == 0))(first)
        pl.when((h == ng // 2) & (i == 0))(mid)

        @pl.when(i == 0)
        def _():
            dk_ref[...] = jnp.zeros_like(dk_ref)
            dv_ref[...] = jnp.zeros_like(dv_ref)

        q = [q_ref[:, ln] for ln in lanes]
        dob = [do_ref[:, ln].astype(BF16) for ln in lanes]
        total = [lt_ref[:, ln][:, :1] for ln in lanes]
        row, col = _iota2(blk, blk)
        upto = (row <= col).astype(BF16)
        before = (row < col).astype(BF16)

        def step(kb, carry, valid):
            ks = pl.ds(pl.multiple_of(kb * blk, blk), blk)
            out = []
            for t, (seen, psum, dq) in enumerate(carry):
                k, v = k_ref[ks, lanes[t]], v_ref[ks, lanes[t]]
                lb, l1 = _sb_logits(q[t], k, valid)
                later = total[t] - seen - _dot2(l1, upto)
                att = _masked(valid, jnp.exp(lb + later))
                p = att * lax.dot_general(dob[t], v, NT, preferred_element_type=F32)
                c = psum + _dot2(p, before)
                sig = jnp.exp(lb)
                dz = (_masked(valid, p * (1.0 - sig) - c * sig) * scale).astype(BF16)
                dq = dq + jnp.dot(dz, k, preferred_element_type=F32)
                dk_ref[ks, lanes[t]] += lax.dot_general(dz, q[t], TN, preferred_element_type=F32)
                dv_ref[ks, lanes[t]] += lax.dot_general(att.astype(BF16), dob[t], TN, preferred_element_type=F32)
                out.append((seen + jnp.sum(l1, axis=1, keepdims=True), psum + jnp.sum(p, axis=1, keepdims=True), dq))
            return tuple(out)

        zero = jnp.zeros((blk, 1), F32)
        start = ((zero, zero, jnp.zeros((blk, HEAD_DIM), F32)),) * hp
        carry = step(i, lax.fori_loop(i - swept_ref[h, i], i, lambda kb, c: step(kb, c, None), start), col < row)
        for t in range(hp):
            dq_ref[:, lanes[t]] = carry[t][2].astype(BF16)

        @pl.when(i == nq - 1)
        def _():
            dk_out[...] = dk_ref[...].astype(BF16)
            dv_out[...] = dv_ref[...].astype(BF16)

        pl.when((h == ng - 1) & (i == nq - 1))(last)

    wide = hp * HEAD_DIM
    qb = pl.BlockSpec((blk, wide), lambda h, i: (i, h))
    head = pl.BlockSpec((T, wide), lambda h, i: (0, h))
    out = jax.ShapeDtypeStruct((T, W), BF16)
    return comm.call(
        body, name, (ng, nq),
        [qb, pl.BlockSpec((T, wide), lambda h, i: (0, ng + h)),
         pl.BlockSpec((T, wide), lambda h, i: (0, 2 * ng + h)), qb, qb, pl.BlockSpec(memory_space=pltpu.SMEM)],
        (qb, head, head), (out, out, out), [pltpu.VMEM((T, wide), F32)] * 2, (qkv, qkv, qkv, do, lt, swept))


def _expanders(H):
    lane = np.arange(H * HEAD_DIM) // HEAD_DIM
    eb = np.zeros((LANES, H * HEAD_DIM), np.float32)
    eg = np.zeros((LANES, H * HEAD_DIM), np.float32)
    eb[lane, np.arange(H * HEAD_DIM)] = 1.0
    eg[H + lane, np.arange(H * HEAD_DIM)] = 1.0
    sb = np.zeros((H * HEAD_DIM, LANES), np.float32)
    sg = np.zeros((H * HEAD_DIM, LANES), np.float32)
    sb[np.arange(H) * HEAD_DIM, np.arange(H)] = 1.0
    sg[np.arange(H) * HEAD_DIM, H + np.arange(H)] = 1.0
    return [jnp.asarray(m, BF16) for m in (eb, eg, sb, sg)]


def _conv_taps(ext_ref, w, n_out, lead):
    K = w.shape[0]
    out = None
    for j in range(K):
        term = ext_ref[pl.ds(lead - (K - 1) + j, n_out), :] * w[j:j + 1, :]
        out = term if out is None else out + term
    return out


STRIP_ROWS = 64


def _strips(n_rows):
    return [(r0, min(STRIP_ROWS, n_rows - r0)) for r0 in range(0, n_rows, STRIP_ROWS)]


def _dn_pre_fwd(pdn, pba, conv_w, a_log, dt_bias, bt, name):
    T, W = pdn.shape[0], pdn.shape[1] // 4
    H = W // HEAD_DIM
    eb, eg, _, _ = _expanders(H)
    nb = T // bt

    def body(x_ref, prev_ref, ba_ref, w_ref, al_ref, dt_ref, eb_ref, eg_ref,
             q_ref, k_ref, v_ref, bx_ref, gx_ref, ext):
        i = pl.program_id(0)
        ext[pl.ds(0, HALO), :] = jnp.where(i > 0, prev_ref[...], 0.0)
        ext[pl.ds(HALO, bt), :] = x_ref[...]
        w = w_ref[...]
        K = w.shape[0]
        for g in range(3 * H):
            out_ref, scale = ((q_ref, HEAD_DIM ** -0.5), (k_ref, 1.0), (v_ref, None))[g // H]
            cols = pl.ds(g * HEAD_DIM, HEAD_DIM)
            dst = pl.ds((g % H) * HEAD_DIM, HEAD_DIM)
            wg = w[:, g * HEAD_DIM:(g + 1) * HEAD_DIM]
            for r0, rows in _strips(bt):
                c = None
                for j in range(K):
                    term = ext[pl.ds(HALO + r0 - (K - 1) + j, rows), cols] * wg[j:j + 1, :]
                    c = term if c is None else c + term
                s = c * _sigmoid(c)
                if scale is not None:
                    s = s * lax.rsqrt(jnp.sum(s * s, axis=-1, keepdims=True) + EPS) * scale
                out_ref[pl.ds(r0, rows), dst] = s
        ba = ba_ref[...]
        beta = _sigmoid(ba)
        graw = -jnp.exp(al_ref[...]) * _softplus(ba + dt_ref[...])
        row, col = _iota2(bt, bt)
        tri = ((row // CHUNK == col // CHUNK) & (row >= col)).astype(BF16)
        gcum = _dot3r(tri, graw)
        bx_ref[...] = _dot3(beta, eb_ref[...])
        gx_ref[...] = _dot3(gcum, eg_ref[...])

    C = 3 * W
    rowb = lambda w: pl.BlockSpec((bt, w), lambda i: (i, 0))
    full = lambda a: pl.BlockSpec(a.shape, lambda i: (0,) * a.ndim)
    out = jax.ShapeDtypeStruct((T, W), F32)
    return pl.pallas_call(
        body, name=name, grid=(nb,),
        in_specs=[rowb(C), pl.BlockSpec((HALO, C), lambda i: (jnp.maximum(i * (bt // HALO) - 1, 0), 0)),
                  rowb(LANES), full(conv_w), full(a_log), full(dt_bias), full(eb), full(eg)],
        out_specs=(rowb(W),) * 5, out_shape=(out,) * 5,
        scratch_shapes=[pltpu.VMEM((bt + HALO, C), F32)], compiler_params=_params(1),
    )(pdn, pdn, pba, conv_w, a_log, dt_bias, eb, eg)


def _dn_pre_bwd(pdn, pba, conv_w, a_log, dt_bias, dq, dk, dv, dbx, dgx, bt, name):
    T, W = pdn.shape[0], pdn.shape[1] // 4
    H = W // HEAD_DIM
    C = 3 * W
    K = conv_w.shape[0]
    _, _, sb, sg = _expanders(H)
    nb = T // bt
    n_ext = bt + HALO

    def body(x_ref, prev_ref, next_ref, ba_ref, w_ref, al_ref, dt_ref, sb_ref, sg_ref,
             dq_ref, dqn_ref, dk_ref, dkn_ref, dv_ref, dvn_ref, dbx_ref, dgx_ref,
             dx_ref, dba_ref, dw_ref, dal_ref, ddt_ref, ext, dext, dcext):
        i = pl.program_id(0)
        last = i == nb - 1
        ext[pl.ds(0, HALO), :] = jnp.where(i > 0, prev_ref[...], 0.0)
        ext[pl.ds(HALO, bt), :] = x_ref[...]
        ext[pl.ds(HALO + bt, HALO), :] = jnp.where(last, 0.0, next_ref[...])
        dext[pl.ds(0, bt), pl.ds(0, W)] = dq_ref[...]
        dext[pl.ds(0, bt), pl.ds(W, W)] = dk_ref[...]
        dext[pl.ds(0, bt), pl.ds(2 * W, W)] = dv_ref[...]
        dext[pl.ds(bt, HALO), pl.ds(0, W)] = jnp.where(last, 0.0, dqn_ref[...])
        dext[pl.ds(bt, HALO), pl.ds(W, W)] = jnp.where(last, 0.0, dkn_ref[...])
        dext[pl.ds(bt, HALO), pl.ds(2 * W, W)] = jnp.where(last, 0.0, dvn_ref[...])
        w = w_ref[...]
        l2_scale = (HEAD_DIM ** -0.5, 1.0, None)

        @pl.when(i == 0)
        def _():
            dw_ref[...] = jnp.zeros_like(dw_ref)
            dal_ref[...] = jnp.zeros_like(dal_ref)
            ddt_ref[...] = jnp.zeros_like(ddt_ref)

        for g in range(C // HEAD_DIM):
            cols = pl.ds(g * HEAD_DIM, HEAD_DIM)
            wg = w[:, g * HEAD_DIM:(g + 1) * HEAD_DIM]
            scale = l2_scale[g // H]
            for r0, rows in _strips(bt) + [(bt, HALO)]:
                c = None
                for j in range(K):
                    term = ext[pl.ds(HALO + r0 - (K - 1) + j, rows), cols] * wg[j:j + 1, :]
                    c = term if c is None else c + term
                sg_c = _sigmoid(c)
                ds = dext[pl.ds(r0, rows), cols]
                if scale is not None:
                    s = c * sg_c
                    r = lax.rsqrt(jnp.sum(s * s, axis=-1, keepdims=True) + EPS)
                    ds = scale * r * (ds - s * (r * r) * jnp.sum(s * ds, axis=-1, keepdims=True))
                dcext[pl.ds(r0, rows), cols] = ds * (sg_c * (1.0 + c * (1.0 - sg_c)))
            dw = [0.0] * K
            for r0, rows in _strips(bt):
                dx = None
                for j in range(K):
                    term = dcext[pl.ds(r0 + K - 1 - j, rows), cols] * wg[j:j + 1, :]
                    dx = term if dx is None else dx + term
                dx_ref[pl.ds(r0, rows), cols] = dx.astype(BF16)
                dcur = dcext[pl.ds(r0, rows), cols]
                for j in range(K):
                    dw[j] = dw[j] + jnp.sum(dcur * ext[pl.ds(HALO + r0 - (K - 1) + j, rows), cols], axis=0, keepdims=True)
            dw_ref[:, cols] += jnp.concatenate(dw, axis=0)

        ba = ba_ref[...]
        beta = _sigmoid(ba)
        al, dtb = al_ref[...], dt_ref[...]
        dbeta = _dot3(dbx_ref[...], sb_ref[...])
        dg = _dot3(dgx_ref[...], sg_ref[...])
        sp = _softplus(ba + dtb)
        da = dg * (-jnp.exp(al)) * _sigmoid(ba + dtb)
        dba_ref[...] = dbeta * beta * (1.0 - beta) + da
        dal_ref[...] += jnp.sum(dg * (-jnp.exp(al)) * sp, axis=0, keepdims=True)
        ddt_ref[...] += jnp.sum(da, axis=0, keepdims=True)

    rowb = lambda w: pl.BlockSpec((bt, w), lambda i: (i, 0))
    full = lambda a: pl.BlockSpec(a.shape, lambda i: (0,) * a.ndim)
    nxt = lambda w: pl.BlockSpec((HALO, w), lambda i: (jnp.minimum((i + 1) * (bt // HALO), T // HALO - 1), 0))
    vec = pl.BlockSpec((1, LANES), lambda i: (0, 0))
    return pl.pallas_call(
        body, name=name, grid=(nb,),
        in_specs=[rowb(C), pl.BlockSpec((HALO, C), lambda i: (jnp.maximum(i * (bt // HALO) - 1, 0), 0)), nxt(C),
                  rowb(LANES), full(conv_w), full(a_log), full(dt_bias), full(sb), full(sg),
                  rowb(W), nxt(W), rowb(W), nxt(W), rowb(W), nxt(W), rowb(W), rowb(W)],
        out_specs=(rowb(C), rowb(LANES), pl.BlockSpec((K, C), lambda i: (0, 0)), vec, vec),
        out_shape=(jax.ShapeDtypeStruct((T, C), BF16), jax.ShapeDtypeStruct((T, LANES), F32),
                   jax.ShapeDtypeStruct((K, C), F32), jax.ShapeDtypeStruct((1, LANES), F32),
                   jax.ShapeDtypeStruct((1, LANES), F32)),
        scratch_shapes=[pltpu.VMEM((bt + 2 * HALO, C), F32), pltpu.VMEM((n_ext, C), F32), pltpu.VMEM((n_ext, C), F32)],
        compiler_params=_params(1),
    )(pdn, pdn, pdn, pba, conv_w, a_log, dt_bias, sb, sg, dq, dq, dk, dk, dv, dv, dbx, dgx)


def _dot_split(a, b):
    a_hi, b_hi = a.astype(BF16), b.astype(BF16)
    a_lo, b_lo = (a - a_hi.astype(F32)).astype(BF16), (b - b_hi.astype(F32)).astype(BF16)
    dot = functools.partial(jnp.dot, preferred_element_type=F32)
    return dot(a_hi, b_hi) + (dot(a_hi, b_lo) + dot(a_lo, b_hi))


def _dn_local(q, k, v, beta, g, tm=None):
    row, col = _iota2(ROWS, ROWS)
    same = (row // CHUNK) == (col // CHUNK)
    causal = same & (row >= col)
    strict = same & (row > col)
    eye = (row == col).astype(F32)
    last_of = (col == (row // CHUNK) * CHUNK + (CHUNK - 1)).astype(BF16)
    eg = jnp.exp(g)
    g_rows = jnp.concatenate([g] * (ROWS // HEAD_DIM), axis=1)
    decay = jnp.where(causal, jnp.exp(jnp.where(causal, g_rows - g_rows.T, 0.0)), 0.0)
    kb = k * beta
    vb = v * beta
    kk = _bdot(kb, k, NT)
    low = jnp.where(strict, kk * decay, 0.0)
    if tm is None:
        pw = -low
        tm = eye + pw
        for _ in range(5):
            pw = _dot_split(pw, pw)
            tm = tm + _dot_split(tm, pw)
    kbg = kb * eg
    u = _bdot(tm, vb)
    w = _bdot(tm, kbg)
    qk = _bdot(q, k, NT)
    qa = jnp.where(causal, qk * decay, 0.0)
    glast = _dot3r(last_of, g)
    e2 = jnp.exp(glast - g)
    return dict(row=row, col=col, same=same, causal=causal, strict=strict, eg=eg, decay=decay, kb=kb, vb=vb, kk=kk,
                tm=tm, kbg=kbg, u=u, w=w, qk=qk, qa=qa, glast=glast, e2=e2, kte=k * e2, qd=q * eg)


def _dn_core_fwd(q, k, v, bx, gx, z, gain, name, comm):
    T, W = q.shape
    H = W // HEAD_DIM
    nb = T // ROWS
    hp = _heads_per_step(H)
    ng = H // hp

    def body(*refs):
        ((q_ref, k_ref, v_ref, b_ref, g_ref, z_ref, gain_ref), (o_ref, mix_ref, ss_ref, tm_ref), (state,),
         (first, mid, last)) = comm.split(refs, 7, 4, 1)
        h, b = pl.program_id(0), pl.program_id(1)
        pl.when((h == 0) & (b == 0))(first)
        pl.when((h == 3 * ng // 4) & (b == 0))(mid)

        @pl.when(b == 0)
        def _():
            state[...] = jnp.zeros_like(state)

        for t in range(hp):
            ln = slice(t * HEAD_DIM, (t + 1) * HEAD_DIM)
            L = _dn_local(q_ref[:, ln], k_ref[:, ln], v_ref[:, ln], b_ref[:, ln], g_ref[:, ln])
            s = state[t]
            vns, qds = [], []
            for c in range(N_CHUNKS):
                rows = slice(c * CHUNK, (c + 1) * CHUNK)
                ss_ref[t, c] = s
                vn = L["u"][rows] - _bdot(L["w"][rows], s)
                qds.append(_bdot(L["qd"][rows], s))
                vns.append(vn)
                s = s * jnp.exp(L["glast"][c * CHUNK:c * CHUNK + 1, :]) + _bdot(L["kte"][rows], vn, TN)
            state[t] = s
            tm_ref[:, t * ROWS:(t + 1) * ROWS] = L["tm"]
            o = jnp.concatenate(qds, axis=0) + _bdot(L["qa"], jnp.concatenate(vns, axis=0))
            o_ref[:, ln] = o
            zz = z_ref[:, ln]
            r = lax.rsqrt(jnp.mean(o * o, axis=-1, keepdims=True) + EPS)
            mix_ref[:, ln] = ((o * r * gain_ref[...]) * (zz * _sigmoid(zz))).astype(BF16)
        pl.when((h == ng - 1) & (b == nb - 1))(last)

    wide = hp * HEAD_DIM
    blk = pl.BlockSpec((ROWS, wide), lambda h, b: (b, h))
    zblk = pl.BlockSpec((ROWS, wide), lambda h, b: (b, 3 * ng + h))
    return comm.call(
        body, name, (ng, nb), [blk] * 5 + [zblk, pl.BlockSpec((1, HEAD_DIM), lambda h, b: (0, 0))],
        (blk, blk, pl.BlockSpec((hp, N_CHUNKS, HEAD_DIM, HEAD_DIM), lambda h, b: (h, b, 0, 0)),
         pl.BlockSpec((ROWS, hp * ROWS), lambda h, b: (b, h))),
        (jax.ShapeDtypeStruct((T, W), F32), jax.ShapeDtypeStruct((T, W), BF16),
         jax.ShapeDtypeStruct((H, T // CHUNK, HEAD_DIM, HEAD_DIM), F32), jax.ShapeDtypeStruct((T, H * ROWS), F32)),
        [pltpu.VMEM((hp, HEAD_DIM, HEAD_DIM), F32)], (q, k, v, bx, gx, z, gain))


def _dn_core_bwd(q, k, v, bx, gx, z, gain, o, dmix, ss, tms, dmix_col0, name, comm):
    T, W = q.shape
    H = W // HEAD_DIM
    nb = T // ROWS
    hp = _heads_per_step(H)
    ng = H // hp
    wide = hp * HEAD_DIM
    c0 = dmix_col0 // wide

    def body(*refs):
        ((q_ref, k_ref, v_ref, b_ref, g_ref, z_ref, gain_ref, o_ref, dm_ref, ss_ref, tm_ref),
         (dq_ref, dk_ref, dv_ref, dbx_ref, dgx_ref, dz_ref, dgain_ref), (dstate,),
         (first, mid, last)) = comm.split(refs, 11, 7, 1)
        pl.when((pl.program_id(0) == 0) & (pl.program_id(1) == 0))(first)
        pl.when((pl.program_id(0) == ng // 2) & (pl.program_id(1) == 0))(mid)

        @pl.when(pl.program_id(1) == 0)
        def _():
            dstate[...] = jnp.zeros_like(dstate)
            dgain_ref[...] = jnp.zeros_like(dgain_ref)

        refs = (q_ref, k_ref, v_ref, b_ref, g_ref, z_ref, gain_ref, o_ref, dm_ref, ss_ref, tm_ref,
                dq_ref, dk_ref, dv_ref, dbx_ref, dgx_ref, dz_ref, dgain_ref, dstate)
        for t in range(hp):
            one_head(t, *refs)
        pl.when((pl.program_id(0) == ng - 1) & (pl.program_id(1) == nb - 1))(last)

    def one_head(t, q_ref, k_ref, v_ref, b_ref, g_ref, z_ref, gain_ref, o_ref, dm_ref, ss_ref, tm_ref,
                 dq_ref, dk_ref, dv_ref, dbx_ref, dgx_ref, dz_ref, dgain_ref, dstate):
        ln = slice(t * HEAD_DIM, (t + 1) * HEAD_DIM)
        qv, kv, vv, beta, g = q_ref[:, ln], k_ref[:, ln], v_ref[:, ln], b_ref[:, ln], g_ref[:, ln]
        gain_v = gain_ref[...]
        ov, zz, dm = o_ref[:, ln], z_ref[:, ln], dm_ref[:, ln]
        r = lax.rsqrt(jnp.mean(ov * ov, axis=-1, keepdims=True) + EPS)
        n = ov * r
        sgz = _sigmoid(zz)
        d_on = dm * (zz * sgz)
        dz_ref[:, ln] = (dm * (n * gain_v) * (sgz * (1.0 + zz * (1.0 - sgz)))).astype(BF16)
        dgain_ref[t] += jnp.sum(d_on * n, axis=0, keepdims=True)
        gy = d_on * gain_v
        do = r * (gy - n * jnp.mean(gy * n, axis=-1, keepdims=True))

        L = _dn_local(qv, kv, vv, beta, g, tm_ref[:, t * ROWS:(t + 1) * ROWS])
        causal, strict = L["causal"], L["strict"]
        u, w, qa, qd, kte, tm = L["u"], L["w"], L["qa"], L["qd"], L["kte"], L["tm"]
        s_in = [ss_ref[t, c] for c in range(N_CHUNKS)]
        vn = [u[c * CHUNK:(c + 1) * CHUNK] - _bdot(w[c * CHUNK:(c + 1) * CHUNK], s_in[c]) for c in range(N_CHUNKS)]
        vn_all = jnp.concatenate(vn, axis=0)
        qat_do = _bdot(qa, do, TN)
        d_qa = jnp.where(causal, _bdot(do, vn_all, NT), 0.0)
        ds = dstate[t]
        d_vn, d_kte, d_qd, d_w, d_gl = ([None] * N_CHUNKS for _ in range(5))
        for c in reversed(range(N_CHUNKS)):
            rows = slice(c * CHUNK, (c + 1) * CHUNK)
            egl = jnp.exp(L["glast"][c * CHUNK:c * CHUNK + 1, :])
            d_vn[c] = qat_do[rows] + _bdot(kte[rows], ds)
            d_kte[c] = _bdot(vn[c], ds, NT)
            d_gl[c] = jnp.sum(jnp.sum(ds * s_in[c], axis=1, keepdims=True), axis=0, keepdims=True) * egl
            d_qd[c] = _bdot(do[rows], s_in[c], NT)
            d_w[c] = -_bdot(d_vn[c], s_in[c], NT)
            ds = ds * egl + _bdot(qd[rows], do[rows], TN) - _bdot(w[rows], d_vn[c], TN)
        dstate[t] = ds
        d_u = jnp.concatenate(d_vn, axis=0)
        d_w = jnp.concatenate(d_w, axis=0)
        d_qd = jnp.concatenate(d_qd, axis=0)
        d_kte = jnp.concatenate(d_kte, axis=0)

        d_tm = _bdot(d_u, L["vb"], NT) + _bdot(d_w, L["kbg"], NT)
        d_vb = _bdot(tm, d_u, TN)
        d_kbg = _bdot(tm, d_w, TN)
        d_low = jnp.where(strict, -_bdot(_bdot(tm, d_tm, TN), tm, NT), 0.0)
        decay = L["decay"]
        d_kk = d_low * decay
        d_qk = d_qa * decay
        d_decay = d_low * L["kk"] + d_qa * L["qk"]
        eg, e2 = L["eg"], L["e2"]
        d_kb = _bdot(d_kk, kv) + d_kbg * eg
        dk_ref[:, ln] = _bdot(d_kk, L["kb"], TN) + _bdot(d_qk, qv, TN) + d_kb * beta + d_kte * e2
        dq_ref[:, ln] = _bdot(d_qk, kv) + d_qd * eg
        dv_ref[:, ln] = d_vb * beta
        rsum = lambda a: jnp.sum(a, axis=-1, keepdims=True)
        dbx_ref[:, ln] = jnp.broadcast_to(rsum(d_kb * kv) + rsum(d_vb * vv), (ROWS, HEAD_DIM))
        d_eg = rsum(d_kbg * L["kb"]) + rsum(d_qd * qv)
        t2 = rsum(d_kte * kv) * e2
        ed = d_decay * decay
        d_g = d_eg * eg - t2 + rsum(ed) - rsum(ed.T)
        row, col = L["row"], L["col"]
        chunk_sum = L["same"].astype(BF16)
        lane_row = lax.broadcasted_iota(jnp.int32, (ROWS, HEAD_DIM), 0)
        is_last = (lane_row % CHUNK) == (CHUNK - 1)
        d_glast = _dot3r(chunk_sum, t2)
        for c in range(N_CHUNKS):
            d_glast = d_glast + jnp.where(lane_row // CHUNK == c, d_gl[c], 0.0)
        d_g = d_g + jnp.where(is_last, d_glast, 0.0)
        suffix = (L["same"] & (col >= row)).astype(BF16)
        dgx_ref[:, ln] = _dot3r(suffix, d_g)

    rev = lambda b: nb - 1 - b
    blk = pl.BlockSpec((ROWS, wide), lambda h, b: (rev(b), h))
    zblk = pl.BlockSpec((ROWS, wide), lambda h, b: (rev(b), 3 * ng + h))
    dmblk = pl.BlockSpec((ROWS, wide), lambda h, b: (rev(b), c0 + h))
    out = jax.ShapeDtypeStruct((T, W), F32)
    return comm.call(
        body, name, (ng, nb),
        [blk] * 5 + [zblk, pl.BlockSpec((1, HEAD_DIM), lambda h, b: (0, 0)), blk, dmblk,
                     pl.BlockSpec((hp, N_CHUNKS, HEAD_DIM, HEAD_DIM), lambda h, b: (h, rev(b), 0, 0)),
                     pl.BlockSpec((ROWS, hp * ROWS), lambda h, b: (rev(b), h))],
        (blk,) * 6 + (pl.BlockSpec((hp, 1, HEAD_DIM), lambda h, b: (h, 0, 0)),),
        (out,) * 5 + (jax.ShapeDtypeStruct((T, W), BF16), jax.ShapeDtypeStruct((H, 1, HEAD_DIM), F32)),
        [pltpu.VMEM((hp, HEAD_DIM, HEAD_DIM), F32)], (q, k, v, bx, gx, z, gain, o, dmix, ss, tms))


_GELU_C = 0.7978845608028654
_GELU_A = 0.044715


def _gelu(x):
    x2 = x * x
    t = jnp.tanh(x * (x2 * (_GELU_C * _GELU_A) + _GELU_C))
    half = 0.5 * x
    return half + half * t, t, half, x2


FUSED_ROWS = 1024
FUSED_CHUNK = 256
BF16_ROWS = 16


def _ffn_up_fused(hn, w_up, conv_w, conv_b, tn, name):
    T, D = hn.shape
    S, _, C = w_up.shape
    F = S * C // 2
    K = conv_w.shape[0]
    tm = min(FUSED_ROWS, T)
    tn = _fit([C], tn)
    per, nj = C // tn, F // tn
    chunk = min(FUSED_CHUNK, tm)

    def body(a_ref, ap_ref, bg_ref, bv_ref, wg_ref, wv_ref, cg_ref, cv_ref, ug_ref, uv_ref, o_ref, gext, vext):
        keep = pl.program_id(0) > 0
        mats = ((bg_ref[...], wg_ref[...], cg_ref[...], ug_ref, gext), (bv_ref[...], wv_ref[...], cv_ref[...], uv_ref, vext))
        for b, _, _, _, ext in mats:
            ext[pl.ds(0, HALO), :] = jnp.where(keep, _bdot(ap_ref[...], b)[BF16_ROWS - HALO:], 0.0)
        for c0 in range(0, tm, chunk):
            a = a_ref[pl.ds(c0, chunk), :]
            for b, _, _, u_ref, ext in mats:
                u = _bdot(a, b)
                u_ref[pl.ds(c0, chunk), :] = u
                ext[pl.ds(HALO + c0, chunk), :] = u
            for r0, rows in _strips(chunk):
                gate, val = [_conv_taps(ext, w, rows, HALO + c0 + r0) + cb for _, w, cb, _, ext in mats]
                o_ref[pl.ds(c0 + r0, rows), :] = (_gelu(gate)[0] * val).astype(BF16)

    out = pl.BlockSpec((tm, tn), lambda i, j: (i, j))
    return pl.pallas_call(
        body, name=name, grid=(T // tm, nj),
        in_specs=[pl.BlockSpec((tm, D), lambda i, j: (i, 0)),
                  pl.BlockSpec((BF16_ROWS, D), lambda i, j: (jnp.maximum(i * (tm // BF16_ROWS) - 1, 0), 0)),
                  pl.BlockSpec((None, D, tn), lambda i, j: (j // per, 0, j % per)),
                  pl.BlockSpec((None, D, tn), lambda i, j: ((nj + j) // per, 0, (nj + j) % per)),
                  pl.BlockSpec((K, tn), lambda i, j: (0, j)), pl.BlockSpec((K, tn), lambda i, j: (0, nj + j)),
                  pl.BlockSpec((1, tn), lambda i, j: (0, j)), pl.BlockSpec((1, tn), lambda i, j: (0, nj + j))],
        out_specs=(out, out, out),
        out_shape=(jax.ShapeDtypeStruct((T, F), F32), jax.ShapeDtypeStruct((T, F), F32), jax.ShapeDtypeStruct((T, F), BF16)),
        scratch_shapes=[pltpu.VMEM((tm + HALO, tn), F32)] * 2, compiler_params=_params(2),
    )(hn, hn, w_up, w_up, conv_w, conv_w, conv_b, conv_b)


def _ffn_mid_bwd(up_g, up_v, conv_w, conv_b, da, bt, bc, name):
    T, F = up_g.shape
    nc = F // bc
    K = conv_w.shape[0]
    nb = T // bt
    n_ext = bt + HALO

    def body(g_ref, gp_ref, gn_ref, v_ref, vp_ref, vn_ref, da_ref, dan_ref, wg_ref, wv_ref, bg_ref, bv_ref,
             dg_ref, dv_ref, dwg_ref, dwv_ref, dbg_ref, dbv_ref, gext, vext, dgext, dvext):
        i = pl.program_id(1)
        last = i == nb - 1
        for ext, cur, prev, nxt in ((gext, g_ref, gp_ref, gn_ref), (vext, v_ref, vp_ref, vn_ref)):
            ext[pl.ds(0, HALO), :] = jnp.where(i > 0, prev[...], 0.0)
            ext[pl.ds(HALO, bt), :] = cur[...]
            ext[pl.ds(HALO + bt, HALO), :] = jnp.where(last, 0.0, nxt[...])
        wg, wv, bg, bv = wg_ref[...], wv_ref[...], bg_ref[...], bv_ref[...]
        for r0, rows in _strips(bt) + [(bt, HALO)]:
            gate = _conv_taps(gext, wg, rows, HALO + r0) + bg
            val = _conv_taps(vext, wv, rows, HALO + r0) + bv
            dact = da_ref[pl.ds(r0, rows), :] if r0 < bt else jnp.where(last, 0.0, dan_ref[...])
            ge, t, half, x2 = _gelu(gate)
            dgelu = (0.5 * t + 0.5) + (half * (1.0 - t * t)) * (x2 * (3.0 * _GELU_C * _GELU_A) + _GELU_C)
            dgext[pl.ds(r0, rows), :] = dact * val * dgelu
            dvext[pl.ds(r0, rows), :] = dact * ge

        @pl.when(i == 0)
        def _():
            for ref in (dwg_ref, dwv_ref, dbg_ref, dbv_ref):
                ref[...] = jnp.zeros_like(ref)

        for dext, ext, w, dx_ref, dw_ref, db_ref in ((dgext, gext, wg, dg_ref, dwg_ref, dbg_ref),
                                                     (dvext, vext, wv, dv_ref, dwv_ref, dbv_ref)):
            dw, db = [0.0] * K, 0.0
            for r0, rows in _strips(bt):
                dx = None
                for j in range(K):
                    term = dext[pl.ds(r0 + K - 1 - j, rows), :] * w[j:j + 1, :]
                    dx = term if dx is None else dx + term
                dx_ref[pl.ds(r0, rows), :] = dx.astype(BF16)
                dcur = dext[pl.ds(r0, rows), :]
                for j in range(K):
                    dw[j] = dw[j] + jnp.sum(dcur * ext[pl.ds(HALO + r0 - (K - 1) + j, rows), :], axis=0, keepdims=True)
                db = db + jnp.sum(dcur, axis=0, keepdims=True)
            dw_ref[...] += jnp.concatenate(dw, axis=0)
            db_ref[...] += db

    prev = lambda i: jnp.maximum(i * (bt // HALO) - 1, 0)
    nxt = lambda i: jnp.minimum((i + 1) * (bt // HALO), T // HALO - 1)
    cur_g = pl.BlockSpec((bt, bc), lambda j, i: (i, j))
    outs = pl.pallas_call(
        body, name=name, grid=(nc, nb),
        in_specs=[cur_g, pl.BlockSpec((HALO, bc), lambda j, i: (prev(i), j)),
                  pl.BlockSpec((HALO, bc), lambda j, i: (nxt(i), j)),
                  cur_g, pl.BlockSpec((HALO, bc), lambda j, i: (prev(i), j)),
                  pl.BlockSpec((HALO, bc), lambda j, i: (nxt(i), j)),
                  cur_g, pl.BlockSpec((HALO, bc), lambda j, i: (nxt(i), j)),
                  pl.BlockSpec((K, bc), lambda j, i: (0, j)), pl.BlockSpec((K, bc), lambda j, i: (0, nc + j)),
                  pl.BlockSpec((1, bc), lambda j, i: (0, j)), pl.BlockSpec((1, bc), lambda j, i: (0, nc + j))],
        out_specs=(cur_g, cur_g, pl.BlockSpec((K, bc), lambda j, i: (0, j)), pl.BlockSpec((K, bc), lambda j, i: (0, j)),
                   pl.BlockSpec((1, bc), lambda j, i: (0, j)), pl.BlockSpec((1, bc), lambda j, i: (0, j))),
        out_shape=(jax.ShapeDtypeStruct((T, F), BF16), jax.ShapeDtypeStruct((T, F), BF16),
                   jax.ShapeDtypeStruct((K, F), F32), jax.ShapeDtypeStruct((K, F), F32),
                   jax.ShapeDtypeStruct((1, F), F32), jax.ShapeDtypeStruct((1, F), F32)),
        scratch_shapes=[pltpu.VMEM((bt + 2 * HALO, bc), F32)] * 2 + [pltpu.VMEM((n_ext, bc), F32)] * 2,
        compiler_params=_params(2),
    )(up_g, up_g, up_g, up_v, up_v, up_v, da, da, conv_w, conv_w, conv_b, conv_b)
    return outs


def _adam_math(w, g, m, v):
    m2 = ADAM_B1 * m + (1.0 - ADAM_B1) * g
    v2 = ADAM_B2 * v + (1.0 - ADAM_B2) * (g * g)
    m_hat = m2 / (1.0 - ADAM_B1 ** ADAM_STEP)
    v_hat = v2 / (1.0 - ADAM_B2 ** ADAM_STEP)
    return -ADAM_LR * (m_hat / (jnp.sqrt(v_hat) + ADAM_EPS) + ADAM_WD * w), m2, v2


def _adamw_halves(w, mine, theirs, place, m, v, bt, name):
    R, C = w.shape
    h = R // 2
    bt = _fit_rows(h, bt)
    nh = h // bt

    def body(s_ref, w_ref, a_ref, b_ref, m_ref, v_ref, g_ref, d_ref, m2_ref, v2_ref):
        lower = pl.program_id(0) < nh
        gv = jnp.where(lower == (s_ref[0] == 0), a_ref[...], b_ref[...])
        g_ref[...] = gv
        d_ref[...], m2_ref[...], v2_ref[...] = _adam_math(w_ref[...], gv, m_ref[...], v_ref[...])

    full = pl.BlockSpec((bt, C), lambda i, s: (i, 0))

    def half(is_mine):
        def index(i, s):
            used = ((i < nh) == (s[0] == 0)) == is_mine
            return jnp.where(used, i % nh, jnp.where(i < nh, 0, nh - 1)), 0
        return pl.BlockSpec((bt, C), index)

    out = jax.ShapeDtypeStruct((R, C), F32)
    return pl.pallas_call(
        body, name=name,
        grid_spec=pltpu.PrefetchScalarGridSpec(num_scalar_prefetch=1, grid=(2 * nh,),
                                               in_specs=[full, half(True), half(False), full, full],
                                               out_specs=(full,) * 4),
        out_shape=(out,) * 4, compiler_params=_params(1),
    )(place, w, mine, theirs, m, v)


def _adamw(w, g, m, v, bt, name):
    R, C = w.shape
    bt = _fit_rows(R, bt)
    blk = pl.BlockSpec((bt, C), lambda i: (i, 0))

    def body(w_ref, g_ref, m_ref, v_ref, d_ref, m2_ref, v2_ref):
        gv = g_ref[...]
        m2 = ADAM_B1 * m_ref[...] + (1.0 - ADAM_B1) * gv
        v2 = ADAM_B2 * v_ref[...] + (1.0 - ADAM_B2) * (gv * gv)
        m_hat = m2 / (1.0 - ADAM_B1 ** ADAM_STEP)
        v_hat = v2 / (1.0 - ADAM_B2 ** ADAM_STEP)
        d_ref[...] = -ADAM_LR * (m_hat / (jnp.sqrt(v_hat) + ADAM_EPS) + ADAM_WD * w_ref[...])
        m2_ref[...] = m2
        v2_ref[...] = v2

    out = jax.ShapeDtypeStruct((R, C), F32)
    return pl.pallas_call(body, name=name, grid=(R // bt,), in_specs=[blk] * 4, out_specs=(blk,) * 3,
                          out_shape=(out,) * 3, compiler_params=_params(1))(w, g, m, v)


def _place():
    x, y, c = lax.axis_index("x"), lax.axis_index("y"), lax.axis_index("c")
    chips = [(1 - x, y), (x, 1 - y), (1 - x, 1 - y)]
    return x, y, c, chips


_HBM = pl.BlockSpec(memory_space=pltpu.HBM)


def _add_cores(buf, other, place, own_only, out_dtype, bt, name):
    n, _, h, cols = buf.shape
    bt = _fit_rows(h, bt)
    row = (lambda k, s: s[1]) if own_only else (lambda k, s: k)

    def body(s_ref, a_ref, b_ref, o_ref):
        o_ref[...] = (a_ref[...] + b_ref[...]).astype(out_dtype)

    return pl.pallas_call(
        body, name=name,
        grid_spec=pltpu.PrefetchScalarGridSpec(
            num_scalar_prefetch=1, grid=(1 if own_only else n, h // bt),
            in_specs=[pl.BlockSpec((None, None, bt, cols), lambda k, i, s: (row(k, s), s[0], i, 0)),
                      pl.BlockSpec((None, bt, cols), lambda k, i, s: (row(k, s), i, 0))],
            out_specs=(pl.BlockSpec((bt, cols), lambda k, i, s: (i, 0)) if own_only
                       else pl.BlockSpec((None, bt, cols), lambda k, i, s: (k, i, 0)))),
        out_shape=jax.ShapeDtypeStruct((h, cols) if own_only else (n, h, cols), out_dtype),
        compiler_params=_params(2),
    )(place, buf, other)


def _add_chips(own, others, bt, name):
    h, cols = own.shape
    bt = _fit_rows(h, bt)

    def body(a_ref, b_ref, o_ref):
        o_ref[...] = ((a_ref[...] + b_ref[0].astype(F32)) + b_ref[1].astype(F32)) + b_ref[2].astype(F32)

    return pl.pallas_call(
        body, name=name, grid=(h // bt,),
        in_specs=[pl.BlockSpec((bt, cols), lambda i: (i, 0)), pl.BlockSpec((3, bt, cols), lambda i: (0, i, 0))],
        out_specs=pl.BlockSpec((bt, cols), lambda i: (i, 0)),
        out_shape=jax.ShapeDtypeStruct((h, cols), F32), compiler_params=_params(1),
    )(own, others)


def _gather_plan(bufs, split):
    n = len(bufs)

    def phases(ins, outs, sems):
        send, recv = sems
        x, y, c, chips = _place()
        me = 2 * x + y

        def rows(b, core):
            h = bufs[b].shape[0] // 2
            return pl.ds(core * h, h) if split[b] else pl.ds(0, bufs[b].shape[0])

        def over_ici(b, j, block):
            px, py = chips[j]
            return pltpu.make_async_remote_copy(
                src_ref=ins[b].at[rows(b, c)], dst_ref=outs[b].at[block, rows(b, c)], send_sem=send.at[b, j],
                recv_sem=recv.at[b, j], device_id=(px, py, c), device_id_type=MESH)

        def over_d2d(b, j, block, core):
            return pltpu.make_async_remote_copy(
                src_ref=outs[b].at[block, rows(b, core)], dst_ref=outs[b].at[block, rows(b, core)],
                send_sem=send.at[b, 3 + j], recv_sem=recv.at[b, 3 + j], device_id=(x, y, 1 - c), device_id_type=MESH)

        pairs = [(b, j) for b in range(n) for j in range(3)]
        source = lambda j: 2 * chips[j][0] + chips[j][1]

        def first():
            for b, j in pairs:
                over_ici(b, j, me).start()

        def mid():
            for b, j in pairs:
                over_ici(b, j, source(j)).wait_recv()
                if split[b]:
                    over_d2d(b, j, source(j), c).start()

        def last():
            for b, j in pairs:
                if split[b]:
                    over_d2d(b, j, source(j), 1 - c).wait_recv()
            for b, j in pairs:
                over_ici(b, j, me).wait_send()
                if split[b]:
                    over_d2d(b, j, source(j), c).wait_send()

        return first, mid, last

    return ([jax.ShapeDtypeStruct((4,) + b.shape, b.dtype) for b in bufs],
            [pltpu.SemaphoreType.DMA((n, 6)), pltpu.SemaphoreType.DMA((n, 6))], phases)


def _exchange_plan(n, out_shapes, copy):
    def phases(ins, outs, sems):
        send, recv = sems
        place = _place()

        def first():
            for b in range(n):
                copy(b, ins, outs, send, recv, place).start()

        def last():
            for b in range(n):
                copy(b, ins, outs, send, recv, place).wait()

        return first, (lambda: None), last

    return out_shapes, [pltpu.SemaphoreType.DMA((n,)), pltpu.SemaphoreType.DMA((n,))], phases


def _swap_plan(bufs):
    def copy(b, ins, outs, send, recv, place):
        x, y, c, _ = place
        h = bufs[b].shape[1] // 2
        return pltpu.make_async_remote_copy(
            src_ref=ins[b].at[:, pl.ds((1 - c) * h, h)], dst_ref=outs[b], send_sem=send.at[b], recv_sem=recv.at[b],
            device_id=(x, y, 1 - c), device_id_type=MESH)

    shapes = [jax.ShapeDtypeStruct((b.shape[0], b.shape[1] // 2, b.shape[2]), b.dtype) for b in bufs]
    return _exchange_plan(len(bufs), shapes, copy)


def _scatter_plan(bufs):
    def copy(t, ins, outs, send, recv, place):
        x, y, c, chips = place
        b, j = divmod(t, 3)
        px, py = chips[j]
        return pltpu.make_async_remote_copy(
            src_ref=ins[b].at[2 * px + py], dst_ref=outs[b].at[j], send_sem=send.at[t], recv_sem=recv.at[t],
            device_id=(px, py, c), device_id_type=MESH)

    shapes = [jax.ShapeDtypeStruct((3,) + b.shape[1:], b.dtype) for b in bufs]
    return _exchange_plan(3 * len(bufs), shapes, copy)


def _join_plan(halves):
    def copy(b, ins, outs, send, recv, place):
        x, y, c, _ = place
        return pltpu.make_async_remote_copy(
            src_ref=ins[b], dst_ref=outs[b], send_sem=send.at[b], recv_sem=recv.at[b],
            device_id=(x, y, 1 - c), device_id_type=MESH)

    return _exchange_plan(len(halves), [jax.ShapeDtypeStruct(b.shape, b.dtype) for b in halves], copy)


def _run_plan(arrays, plan, name):
    out_shapes, sems, phases = plan
    n, m = len(arrays), len(out_shapes)

    def body(*refs):
        for phase in phases(refs[:n], refs[n:n + m], refs[n + m:]):
            phase()

    return pl.pallas_call(
        body, name=name, in_specs=[_HBM] * n, out_specs=[_HBM] * m, out_shape=out_shapes, scratch_shapes=sems,
        compiler_params=pltpu.CompilerParams(has_side_effects=True),
    )(*arrays)


def _fit_rows(n, target):
    for q in (2 * HALO, HALO):
        for t in range(min(n, target) // q * q, 0, -q):
            if n % t == 0:
                return t
    raise ValueError((n, target))


def _allreduce_small(buf, name):
    R, lanes = buf.shape

    def body(in_ref, out_ref, land, send, recv):
        x, y, c, _ = _place()
        me = 4 * x + 2 * y + c
        land[me] = in_ref[...]
        cps = []
        for r in range(1, 8):
            px, py, pc = x ^ (r >> 2), y ^ ((r >> 1) & 1), c ^ (r & 1)
            cp = pltpu.make_async_remote_copy(
                src_ref=in_ref, dst_ref=land.at[me], send_sem=send.at[r - 1], recv_sem=recv.at[me],
                device_id=(px, py, pc), device_id_type=MESH)
            cp.start()
            cps.append(cp)
        for r in range(1, 8):
            peer = 4 * (x ^ (r >> 2)) + 2 * (y ^ ((r >> 1) & 1)) + (c ^ (r & 1))
            pltpu.make_async_remote_copy(
                src_ref=in_ref, dst_ref=land.at[peer], send_sem=send.at[r - 1], recv_sem=recv.at[peer],
                device_id=(x, y, c), device_id_type=MESH).wait_recv()
        for cp in cps:
            cp.wait_send()
        acc = land[0]
        for d in range(1, 8):
            acc = acc + land[d]
        out_ref[...] = acc

    vm = pl.BlockSpec(memory_space=pltpu.VMEM)
    return pl.pallas_call(
        body, name=name, in_specs=[vm], out_specs=vm, out_shape=jax.ShapeDtypeStruct((R, lanes), buf.dtype),
        scratch_shapes=[pltpu.VMEM((8, R, lanes), buf.dtype), pltpu.SemaphoreType.DMA((7,)), pltpu.SemaphoreType.DMA((8,))],
        compiler_params=pltpu.CompilerParams(has_side_effects=True, vmem_limit_bytes=VMEM_LIMIT),
    )(buf)


ROW_BLOCK = 256
SB_BLOCK = 256
MM_TM, MM_TN, MM_TK = 1024, 512, 512
FFN_COLS = 512


def _lane_pad(vec, start):
    return jnp.pad(vec, ((0, 0), (start, LANES - start - vec.shape[1])))


WEIGHTS = ("w_in", "sb_out_gain", "dn_conv_w", "dn_a_log", "dn_dt_bias", "dn_out_gain", "w_out", "ln_mix_pre",
           "ln_mix_post", "w_up", "ffn_conv_w", "ffn_conv_b", "w_down", "ln_ffn_pre", "ln_ffn_post")
MATRICES = {"w_in": 1, "w_out": 0, "w_up": 1, "w_down": 0}
CONV_SHARDED = ("dn_conv_w", "ffn_conv_w")
SMALL = tuple(n for n in WEIGHTS if n not in MATRICES)
N_CHIPS = 4
ADAM_ROWS = 128


def _pack(arrs, quantum):
    rows, layout, off = [], [], 0
    for a in arrs:
        n = int(np.prod(a.shape))
        r = -(-n // LANES)
        r = -(-r // HALO) * HALO
        rows.append(jnp.pad(a.reshape(-1), (0, r * LANES - n)).reshape(r, LANES))
        layout.append((off, r, n, a.shape))
        off += r
    total = -(-off // quantum) * quantum
    if total > off:
        rows.append(jnp.zeros((total - off, LANES), rows[0].dtype))
    return jnp.concatenate(rows, axis=0), layout


def _unpack(packed, layout):
    return [packed[off:off + r].reshape(-1)[:n].reshape(shape) for off, r, n, shape in layout]


UP_TILE = 1408
PAIR_ROWS = 256


PROLOGUE_STEPS = 8


def _prologue(x, gain, shards, comm, name):
    n = len(shards)
    steps = PROLOGUE_STEPS
    while any(s.shape[0] % (steps * BF16_ROWS) for s in [x] + list(shards)):
        steps //= 2

    def body(*refs):
        ins, outs, _, (first, mid, last) = comm.split(refs, 2 + n, 1 + n, 0)
        i = pl.program_id(0)
        pl.when(i == 0)(first)
        pl.when(i == 3 * steps // 4)(mid)
        xv = ins[0][...]
        outs[0][...] = (xv * lax.rsqrt(jnp.mean(xv * xv, axis=-1, keepdims=True) + EPS) * ins[1][...]).astype(BF16)
        for s_ref, o_ref in zip(ins[2:], outs[1:]):
            o_ref[...] = s_ref[...].astype(BF16)
        pl.when(i == steps - 1)(last)

    rows = lambda a: pl.BlockSpec((a.shape[0] // steps, a.shape[1]), lambda i: (i, 0))
    arrays = [x] + list(shards)
    return comm.call(body, name, (steps,), [rows(x), pl.BlockSpec((1, x.shape[1]), lambda i: (0, 0))] + [rows(s) for s in shards],
                     [rows(a) for a in arrays], [jax.ShapeDtypeStruct(a.shape, BF16) for a in arrays], [],
                     [x, gain] + list(shards))


def _reduce_to_chips(shares, place, names, swap_on, scatter_on):
    from_sibling = swap_on(shares)
    halves = [s.reshape(N_CHIPS, 2, s.shape[1] // 2, s.shape[2]) for s in shares]
    to_chips = [_add_cores(hv, fs, place, False, BF16, PAIR_ROWS, "grad_add_cores_" + n)
                for hv, fs, n in zip(halves, from_sibling, names)]
    own = [_add_cores(hv, fs, place, True, F32, PAIR_ROWS, "grad_add_cores_own_" + n)
           for hv, fs, n in zip(halves, from_sibling, names)]
    return own, scatter_on(to_chips)


def _step(x, xn, target, wt, late, chip, place):
    T, D = x.shape
    W = D // 2
    H = W // HEAD_DIM
    bt = min(ROW_BLOCK, T)
    blk = min(SB_BLOCK, T)
    w_in = wt["w_in"]
    a_log, dt_bias = _lane_pad(wt["dn_a_log"], H), _lane_pad(wt["dn_dt_bias"], H)
    mm = functools.partial(_mm, tm=MM_TM, tn=MM_TN)
    wide = functools.partial(_mm, tm=MM_TM, tn=2 * MM_TN)
    mm_up = functools.partial(_mm, tm=MM_TM, tn=UP_TILE)
    one = lambda a, b, tk=MM_TK: [(a, b, tk, 0, 0)]

    psb = mm(one(xn, w_in, D), "nn", BF16, name="proj_sb", n_window=(0, 3 * W))
    pdn = mm(one(xn, w_in, D), "nn", F32, name="proj_dn", n_window=(3 * W, 4 * W))
    pba = mm(one(xn, w_in, D), "nn", F32, name="proj_ba", n_window=(7 * W, LANES))
    late_names = ("w_out", "w_up", "w_down")
    gathered_with = lambda names: _Comm([late[n] for n in names], _gather_plan([late[n] for n in names], [True] * len(names)))
    own_block_in = lambda theirs, names: [lax.dynamic_update_index_in_dim(t, late[n], chip, 0) for t, n in zip(theirs, names)]
    (o_sb, mix_sb, lt, swept), theirs = _sb_fwd(psb, wt["sb_out_gain"], blk, "sb_fwd", gathered_with(("w_out",)))
    w_out = own_block_in(theirs, ("w_out",))[0].reshape(-1, D)
    qn, kn, vv, bx, gx = _dn_pre_fwd(pdn, pba, wt["dn_conv_w"], a_log, dt_bias, bt, "dn_pre_fwd")
    (o_dn, mix_dn, ss, tms), theirs = _dn_core_fwd(qn, kn, vv, bx, gx, pdn, wt["dn_out_gain"], "dn_core_fwd",
                                                  gathered_with(("w_up", "w_down")))
    w_up, w_down = own_block_in(theirs, ("w_up", "w_down"))
    w_down = w_down.reshape(-1, D)
    F = w_down.shape[0]
    m = wide([(mix_sb, w_out, 2 * MM_TK, 0, 0), (mix_dn, w_out, 2 * MM_TK, 0, W)], "nn", F32, name="out_proj")
    h, hn = _rms_chain_fwd(m, wt["ln_mix_post"], x, wt["ln_ffn_pre"], bt, "rms_between")
    bc = min(FFN_COLS, F)
    up_g, up_v, act = _ffn_up_fused(hn, w_up, wt["ffn_conv_w"], wt["ffn_conv_b"], FFN_COLS // 2, "ffn_up")
    f = mm(one(act, w_down, UP_TILE), "nn", F32, name="ffn_down")
    dy, df, g_ffn_post, sq = _loss_head(f, wt["ln_ffn_post"], h, target, bt, "loss_head")
    loss = 0.5 * jnp.sum(sq) / D

    da = mm(one(df, w_down, D), "nt", F32, name="d_act")
    g_w_down = _mm(one(act, df, 2 * MM_TK), "tn", F32, tm=UP_TILE, tn=2 * MM_TN, name="g_w_down")
    dug, duv, dwg, dwv, dbg, dbv = _ffn_mid_bwd(up_g, up_v, wt["ffn_conv_w"], wt["ffn_conv_b"], da, bt, bc, "ffn_mid_bwd")
    shard = w_up.shape[2]
    g_w_up = mm_up(one(hn, dug, 2 * MM_TK), "tn", F32, name="g_w_up_gate", out_shard=shard,
                   into=(lax.empty(w_up.shape, F32), 0))
    g_w_up = mm_up(one(hn, duv, 2 * MM_TK), "tn", F32, name="g_w_up_val", out_shard=shard, into=(g_w_up, F))
    ffn_shares = [g_w_up, g_w_down.reshape(N_CHIPS, -1, D)]
    dhn, ffn_swapped = wide([(dug, w_up, UP_TILE, 0, 0), (duv, w_up, UP_TILE, 0, F)], "nt", F32, name="d_hn",
                            comm=_Comm(ffn_shares, _swap_plan(ffn_shares)))
    dh, dm, g_ffn_pre, g_mix_post = _rms_chain_bwd(h, wt["ln_ffn_pre"], dhn, dy, m, wt["ln_mix_post"], bt,
                                                   "rms_between_bwd")
    dmix = mm(one(dm, w_out, D), "nt", F32, name="d_mix")
    g_w_out = jnp.concatenate([wide(one(mix_sb, dm, 2 * MM_TK), "tn", F32, name="g_w_out_sb"),
                               wide(one(mix_dn, dm, 2 * MM_TK), "tn", F32, name="g_w_out_dn")], axis=0)
    shares = [g_w_out.reshape(N_CHIPS, -1, D)] + ffn_shares
    carried = {}

    def swap_on(arrays):
        (carried["do_sb"], carried["g_sb_gain"]), out = _headnorm_bwd(
            o_sb, wt["sb_out_gain"], dmix, bt, "sb_norm_bwd", _Comm(arrays[:1], _swap_plan(arrays[:1])))
        return list(out) + list(ffn_swapped)

    def scatter_on(arrays):
        carried["dn"], out = _dn_core_bwd(qn, kn, vv, bx, gx, pdn, wt["dn_out_gain"], o_dn, dmix, ss, tms, W,
                                          "dn_core_bwd", _Comm(arrays, _scatter_plan(arrays)))
        return out

    early = _reduce_to_chips(shares, place, late_names, swap_on, scatter_on)
    g_sb_gain = carried["g_sb_gain"]
    (dq, dk, dv), _ = _sb_bwd(psb, carried["do_sb"], lt, swept, blk, "sb_bwd", _Comm())
    ddq, ddk, ddv, dbx, dgx, dz, g_dn_gain = carried["dn"]
    dconv, dba, g_dn_conv, g_a_log, g_dt_bias = _dn_pre_bwd(pdn, pba, wt["dn_conv_w"], a_log, dt_bias,
                                                            ddq, ddk, ddv, dbx, dgx, bt, "dn_pre_bwd")
    pieces = [(dq, 0), (dk, W), (dv, 2 * W), (dconv, 3 * W), (dz, 6 * W), (dba, 7 * W)]
    g_w_in = [wide(one(xn, d, 2 * MM_TK), "tn", F32, name=f"g_w_in_{i}") for i, (d, _) in enumerate(pieces)]
    g_w_in[-1] = g_w_in[-1][:, :2 * H]
    g_in = jnp.concatenate(g_w_in, axis=1)

    def with_d_xn(arrays):
        carried["dxn"], out = mm([(d, w_in, 2 * MM_TK, 0, k0) for d, k0 in pieces], "nt", F32, name="d_xn",
                                 comm=_Comm(arrays, _scatter_plan(arrays)))
        return out

    last = _reduce_to_chips([g_in.reshape(D, N_CHIPS, -1).transpose(1, 0, 2)], place, ["w_in"],
                            lambda arrays: _run_plan(arrays, _swap_plan(arrays), "grad_swap_cores"), with_d_xn)
    dx, g_mix_pre = _rms_bwd(x, wt["ln_mix_pre"], carried["dxn"], dh, F32, bt, "rms_mix_pre_bwd")
    exchanged = dict(zip(late_names, zip(*early)))
    exchanged["w_in"] = (last[0][0], last[1][0])

    grads = dict(
        sb_out_gain=g_sb_gain, dn_conv_w=g_dn_conv, dn_a_log=g_a_log[:, H:2 * H],
        dn_dt_bias=g_dt_bias[:, H:2 * H], dn_out_gain=jnp.sum(g_dn_gain, axis=0),
        ln_mix_pre=g_mix_pre, ln_mix_post=g_mix_post,
        ffn_conv_w=jnp.concatenate([dwg, dwv], axis=1), ffn_conv_b=jnp.concatenate([dbg, dbv], axis=1),
        ln_ffn_pre=g_ffn_pre, ln_ffn_post=g_ffn_post)
    return loss, dx, grads, exchanged


def kernel(x, w_in, sb_out_gain, dn_conv_w, dn_a_log, dn_dt_bias, dn_out_gain, w_out, ln_mix_pre, ln_mix_post, w_up, ffn_conv_w, ffn_conv_b, w_down, ln_ffn_pre, ln_ffn_post, loss_target, m_w_in, m_sb_out_gain, m_dn_conv_w, m_dn_a_log, m_dn_dt_bias, m_dn_out_gain, m_w_out, m_ln_mix_pre, m_ln_mix_post, m_w_up, m_ffn_conv_w, m_ffn_conv_b, m_w_down, m_ln_ffn_pre, m_ln_ffn_post, v_w_in, v_sb_out_gain, v_dn_conv_w, v_dn_a_log, v_dn_dt_bias, v_dn_out_gain, v_w_out, v_ln_mix_pre, v_ln_mix_post, v_w_up, v_ffn_conv_w, v_ffn_conv_b, v_w_down, v_ln_ffn_pre, v_ln_ffn_post):
    given = dict(locals())
    wl = {n: given[n][0] for n in WEIGHTS}
    ml = {n: given["m_" + n][0] for n in WEIGHTS}
    vl = {n: given["v_" + n][0] for n in WEIGHTS}
    for d in (wl, ml, vl):
        for n in SMALL:
            if d[n].ndim == 1:
                d[n] = d[n][None]
    cx, cy, cc = lax.axis_index("x"), lax.axis_index("y"), lax.axis_index("c")
    chip = 2 * cx + cy
    D = x.shape[2]
    W = D // 2

    first = ("w_in",) + CONV_SHARDED
    mine = [wl["w_in"].astype(BF16)] + [wl[n] for n in CONV_SHARDED]
    late_names = ("w_out", "w_up", "w_down")
    (xn, *late_shards), theirs = _prologue(x[0], wl["ln_mix_pre"], [wl[n] for n in late_names],
                                           _Comm(mine, _gather_plan(mine, [True, False, False])), "gather_w_in")
    got = {n: lax.dynamic_update_index_in_dim(t, s, chip, 0) for n, t, s in zip(first, theirs, mine)}
    columns = lambda g: g.transpose(1, 0, 2).reshape(g.shape[1], N_CHIPS * g.shape[2])
    wt = {n: wl[n] for n in SMALL}
    w_in_all = columns(got["w_in"])
    wt["w_in"] = jnp.pad(w_in_all, ((0, 0), (0, 7 * W + LANES - w_in_all.shape[1])))
    for n in CONV_SHARDED:
        wt[n] = columns(got[n])
    late = dict(zip(late_names, late_shards))

    place = jnp.stack([cc, chip]).astype(jnp.int32)
    loss, dx, grads, exchanged = _step(x[0], xn, loss_target[0], wt, late, chip, place)
    loss = lax.psum(loss, ("x", "y", "c"))

    names = list(MATRICES)
    reduced = [_add_chips(*exchanged[n], PAIR_ROWS, "grad_add_chips_" + n) for n in names]
    siblings = _run_plan(reduced, _join_plan(reduced), "grad_join_cores")
    gl = {}

    small, small_layout = _pack([grads[n] for n in SMALL], HALO)
    small = _allreduce_small(small, "grad_allreduce_small")
    for n, g in zip(SMALL, _unpack(small, small_layout)):
        if n in CONV_SHARDED:
            size = g.shape[1] // N_CHIPS
            g = lax.dynamic_slice_in_dim(g, chip * size, size, axis=1)
        gl[n] = g

    delta, new_m, new_v = {}, {}, {}
    for n, mine_half, sibling_half in zip(names, reduced, siblings):
        gl[n], delta[n], new_m[n], new_v[n] = _adamw_halves(wl[n], mine_half, sibling_half, place, ml[n], vl[n],
                                                            ADAM_ROWS, "adamw_" + n)
    packs = [_pack([d[n] for n in SMALL], HALO) for d in (wl, gl, ml, vl)]
    outs = _adamw(*[p[0] for p in packs], ADAM_ROWS, "adamw_small")
    for res, o in zip((delta, new_m, new_v), outs):
        res.update(zip(SMALL, _unpack(o, packs[0][1])))

    shaped = lambda d: [d[n].reshape(given[n].shape) for n in WEIGHTS]
    return (loss, dx[None], *shaped(gl), *shaped(delta), *shaped(new_m), *shaped(new_v))
```
